```python
import math
import jax, jax.numpy as jnp
from jax import lax
import numpy as np


D_MODEL = 1024
BATCH = 1
SEQ = 16384
DEPTH = 2

GRID_W = 64
CTX_LEN = 256
HEAD_DIM = 64
BRANCH_W = D_MODEL // 2
N_BRANCH = 3
NA_HEADS = BRANCH_W // HEAD_DIM
NA_WIN_H = 8
NA_WIN_W = 16
S5_WIDTH = BRANCH_W
S5_GROUP = 16
S5_GROUPS = S5_WIDTH // S5_GROUP
S5_STATE = 64
S5_DT_MIN = 0.001
S5_DT_MAX = 0.1
SW_HEADS = BRANCH_W // HEAD_DIM
SW_KV_HEADS = SW_HEADS // 4
SW_WINDOW = 128
SW_BLOCK = 128
ROPE_BASE = 10000.0
IN_COLS = 3 * NA_HEADS * HEAD_DIM + S5_WIDTH + (SW_HEADS + 2 * SW_KV_HEADS) * HEAD_DIM + N_BRANCH * D_MODEL
PEER_HEADS = 8
PEER_NKEYS = 128
PEER_EXPERTS = PEER_NKEYS * PEER_NKEYS
PEER_QDIM = 256
PEER_TOPK = 16
PEER_BLOCK = 128
EPS = 1e-6
NEG_INF = -1e30

kernel_name = 'hybrid_na_s5_swa_peer_block'


def rms_norm(x, g):
    xf = x.astype(jnp.float32)
    y = xf * lax.rsqrt(jnp.mean(xf * xf, axis=-1, keepdims=True) + EPS)
    return (y * g.astype(jnp.float32)).astype(x.dtype)


def modulate(h, shift, scale):
    return h * (1 + scale) + shift


def to_heads(t, n):
    return t.reshape(t.shape[:-1] + (n, HEAD_DIM))


def split_columns(p):
    na = NA_HEADS * HEAD_DIM
    sizes = [na, na, na, S5_WIDTH, SW_HEADS * HEAD_DIM, SW_KV_HEADS * HEAD_DIM,
             SW_KV_HEADS * HEAD_DIM, D_MODEL, D_MODEL, D_MODEL]
    points = np.cumsum(sizes)[:-1].tolist()
    return jnp.split(p, points, axis=-1)


def _rope_axis(x, pos):
    half = x.shape[-1] // 2
    inv = ROPE_BASE ** (-jnp.arange(half, dtype=jnp.float32) / half)
    ang = pos.astype(jnp.float32)[:, None] * inv[None, :]
    cos = jnp.cos(ang)[:, None, :]
    sin = jnp.sin(ang)[:, None, :]
    xf = x.astype(jnp.float32)
    x1, x2 = xf[..., :half], xf[..., half:]
    return jnp.concatenate([x1 * cos - x2 * sin, x2 * cos + x1 * sin], axis=-1).astype(x.dtype)


def rope_2d(x, row, col):
    h = x.shape[-1] // 2
    return jnp.concatenate([_rope_axis(x[..., :h], row), _rope_axis(x[..., h:], col)], axis=-1)


def dense_ctx_attn(q, k, v, sink):
    scale = q.shape[-1] ** -0.5
    n_keys = k.shape[1]
    s = jnp.einsum('blgrd,bmgd->bgrlm', q, k).astype(jnp.float32) * scale
    if sink is not None:
        s = jnp.concatenate([s, jnp.broadcast_to(sink.astype(jnp.float32)[None, :, :, None, None],
                                                 s.shape[:-1] + (1,))], axis=-1)
    p = jax.nn.softmax(s, axis=-1)[..., :n_keys].astype(v.dtype)
    return jnp.einsum('bgrlm,bmgd->blgrd', p, v)


def neighborhood_attention(q, k, v, qc, kc, vc, rpb, need_ctx):
    B, S, H, d = q.shape
    L = kc.shape[1]
    rows = S // GRID_W
    kh = min(NA_WIN_H, rows)
    kw = NA_WIN_W
    scale = d ** -0.5
    cols = np.arange(GRID_W)
    col_start = np.clip(cols - kw // 2, 0, GRID_W - kw)
    col_idx = col_start[:, None] + np.arange(kw)[None, :]
    col_off = col_idx - cols[:, None] + NA_WIN_W - 1
    qg = q.reshape(B, rows, GRID_W, H, d)
    kg = k.reshape(B, rows, GRID_W, H, d)
    vg = v.reshape(B, rows, GRID_W, H, d)

    def row_step(r):
        r0 = jnp.clip(r - kh // 2, 0, rows - kh)
        kr = lax.dynamic_slice_in_dim(kg, r0, kh, axis=1)[:, :, col_idx]
        vr = lax.dynamic_slice_in_dim(vg, r0, kh, axis=1)[:, :, col_idx]
        qr = lax.dynamic_index_in_dim(qg, r, axis=1, keepdims=False)
        row_off = r0 + jnp.arange(kh) - r + NA_WIN_H - 1
        bias = rpb[:, row_off][:, :, col_off].astype(jnp.float32)
        s_loc = jnp.einsum('bqhd,biqjhd->bhqij', qr, kr).astype(jnp.float32) * scale
        s_loc = (s_loc + jnp.transpose(bias, (0, 2, 1, 3))[None]).reshape(B, H, GRID_W, kh * kw)
        s_ctx = jnp.einsum('bqhd,blhd->bhql', qr, kc).astype(jnp.float32) * scale
        p = jax.nn.softmax(jnp.concatenate([s_loc, s_ctx], axis=-1), axis=-1).astype(v.dtype)
        p_loc = p[..., :kh * kw].reshape(B, H, GRID_W, kh, kw)
        return (jnp.einsum('bhqij,biqjhd->bqhd', p_loc, vr)
                + jnp.einsum('bhql,blhd->bqhd', p[..., kh * kw:], vc))

    out = lax.map(row_step, jnp.arange(rows))
    out = jnp.moveaxis(out, 0, 1).reshape(B, S, H * d)
    out_c = dense_ctx_attn(qc[:, :, :, None], kc, vc, None).reshape(B, L, H * d) if need_ctx else None
    return out, out_c


def s5_discretize(lam_re, lam_im, b_re, b_im, log_step):
    lam = lax.complex(lam_re.astype(jnp.float32), lam_im.astype(jnp.float32))
    dt = jnp.exp(log_step.astype(jnp.float32))[:, None]
    lam_bar = jnp.exp(lam * dt)
    b_mat = lax.complex(b_re.astype(jnp.float32), b_im.astype(jnp.float32))
    b_bar = ((lam_bar - 1.0) / lam)[..., None] * b_mat
    return lam_bar, b_bar


def s5_scan(u, lam_bar, b_bar, s0, reverse):
    bu = jnp.einsum('btgh,gph->btgp', u.astype(jnp.complex64), b_bar)
    first = -1 if reverse else 0
    bu = bu.at[:, first].add(lam_bar * s0)
    a = jnp.broadcast_to(lam_bar, bu.shape)

    def combine(e1, e2):
        a1, b1 = e1
        a2, b2 = e2
        return a1 * a2, a2 * b1 + b2

    _, states = lax.associative_scan(combine, (a, bu), reverse=reverse, axis=1)
    return states


def s5_mixer(u, uc, lam_re, lam_im, b_re, b_im, c_re, c_im, log_step, d_skip, w_glu, need_ctx):
    B, S, W = u.shape
    L = uc.shape[1]
    ug = u.astype(jnp.float32).reshape(B, S, S5_GROUPS, S5_GROUP)
    ucg = uc.astype(jnp.float32).reshape(B, L, S5_GROUPS, S5_GROUP)
    dsk = d_skip.astype(jnp.float32)
    y = dsk * u.astype(jnp.float32)
    yc = dsk * uc.astype(jnp.float32) if need_ctx else None
    zero = jnp.zeros((B, S5_GROUPS, S5_STATE), jnp.complex64)
    for direction in range(2):
        rev = direction == 1
        lam_bar, b_bar = s5_discretize(lam_re[direction], lam_im[direction], b_re[direction],
                                       b_im[direction], log_step[direction])
        c_mat = lax.complex(c_re[direction].astype(jnp.float32), c_im[direction].astype(jnp.float32))
        st_c = s5_scan(ucg, lam_bar, b_bar, zero, rev)
        s_last = st_c[:, 0] if rev else st_c[:, -1]
        st_x = s5_scan(ug, lam_bar, b_bar, s_last, rev)
        y = y + jnp.einsum('btgp,ghp->btgh', st_x, c_mat).real.reshape(B, S, W)
        if need_ctx:
            yc = yc + jnp.einsum('btgp,ghp->btgh', st_c, c_mat).real.reshape(B, L, W)
    wg = w_glu.astype(jnp.float32)

    def glu(t):
        z = jax.nn.gelu(t, approximate=False)
        return z * jax.nn.sigmoid(z @ wg)

    out = glu(y).astype(u.dtype)
    out_c = glu(yc).astype(uc.dtype) if need_ctx else None
    return out, out_c


def sliding_window_attention(q, k, v, qc, kc, vc, sink, row, col, need_ctx):
    B, S, Hq, d = q.shape
    L = kc.shape[1]
    G = k.shape[2]
    R = Hq // G
    scale = d ** -0.5
    q = rope_2d(q, row, col)
    k = rope_2d(k, row, col)
    nb = S // SW_BLOCK
    qb = q.reshape(B, nb, SW_BLOCK, G, R, d)

    def band(t):
        tp = jnp.pad(t, ((0, 0), (SW_BLOCK, SW_BLOCK), (0, 0), (0, 0))).reshape(B, nb + 2, SW_BLOCK, G, d)
        return jnp.concatenate([tp[:, :-2], tp[:, 1:-1], tp[:, 2:]], axis=2)

    kb, vb = band(k), band(v)
    nk = 3 * SW_BLOCK
    qpos = np.arange(nb)[:, None] * SW_BLOCK + np.arange(SW_BLOCK)[None, :]
    kpos = (np.arange(nb)[:, None] - 1) * SW_BLOCK + np.arange(nk)[None, :]
    valid = ((np.abs(kpos[:, None, :] - qpos[:, :, None]) <= SW_WINDOW)
             & (kpos[:, None, :] >= 0) & (kpos[:, None, :] < S))
    s_loc = jnp.einsum('bnqgrd,bnkgd->bngrqk', qb, kb).astype(jnp.float32) * scale
    s_loc = jnp.where(valid[None, :, None, None], s_loc, NEG_INF)
    s_ctx = jnp.einsum('bnqgrd,blgd->bngrql', qb, kc).astype(jnp.float32) * scale
    sink_gr = sink.astype(jnp.float32).reshape(G, R)
    s_sink = jnp.broadcast_to(sink_gr[None, None, :, :, None, None], s_loc.shape[:-1] + (1,))
    p = jax.nn.softmax(jnp.concatenate([s_loc, s_ctx, s_sink], axis=-1), axis=-1).astype(v.dtype)
    out = (jnp.einsum('bngrqk,bnkgd->bnqgrd', p[..., :nk], vb)
           + jnp.einsum('bngrql,blgd->bnqgrd', p[..., nk:nk + L], vc))
    out = out.reshape(B, S, Hq * d)
    out_c = (dense_ctx_attn(qc.reshape(B, L, G, R, d), kc, vc, sink_gr).reshape(B, L, Hq * d)
             if need_ctx else None)
    return out, out_c


def merge_branches(ya, yb, ys, ga, gb, gs, w_branch, w_out):
    m = (jax.nn.sigmoid(ga) * (ya @ w_branch[0])
         + jax.nn.sigmoid(gb) * (yb @ w_branch[1])
         + jax.nn.sigmoid(gs) * (ys @ w_branch[2]))
    return m @ w_out


def peer(h, w_q, sub_keys, exp_u, exp_v):
    N, D = h.shape
    half = PEER_QDIM // 2
    q = (h @ w_q).reshape(N, PEER_HEADS, PEER_QDIM)
    s1 = jnp.einsum('nhd,kd->nhk', q[..., :half], sub_keys[0]).astype(jnp.float32)
    s2 = jnp.einsum('nhd,kd->nhk', q[..., half:], sub_keys[1]).astype(jnp.float32)
    v1, i1 = lax.top_k(s1, PEER_TOPK)
    v2, i2 = lax.top_k(s2, PEER_TOPK)
    cand = (v1[..., :, None] + v2[..., None, :]).reshape(N, PEER_HEADS, PEER_TOPK * PEER_TOPK)
    cidx = (i1[..., :, None] * PEER_NKEYS + i2[..., None, :]).reshape(N, PEER_HEADS, PEER_TOPK * PEER_TOPK)
    sc, pos = lax.top_k(cand, PEER_TOPK)
    eidx = jnp.take_along_axis(cidx, pos, axis=-1)
    g = jax.nn.softmax(sc, axis=-1)
    nblk = N // PEER_BLOCK

    def block(args):
        hb, eb, gb = args
        a = jnp.einsum('thkd,td->thk', exp_u[eb], hb).astype(jnp.float32)
        wgt = (jax.nn.gelu(a, approximate=False) * gb).astype(h.dtype)
        return jnp.einsum('thk,thkd->td', wgt, exp_v[eb])

    out = lax.map(block, (h.reshape(nblk, PEER_BLOCK, D),
                          eidx.reshape(nblk, PEER_BLOCK, PEER_HEADS, PEER_TOPK),
                          g.reshape(nblk, PEER_BLOCK, PEER_HEADS, PEER_TOPK)))
    return out.reshape(N, D)


def setup_inputs(seed: int = 0) -> dict:
    key = jax.random.key(seed)
    ks = jax.random.split(key, 32)
    D = D_MODEL

    def nrm(k, shape, std):
        return std * jax.random.normal(k, shape, jnp.float32)

    s5_shape = (DEPTH, 2, S5_GROUPS, S5_STATE)
    n_idx = jnp.arange(S5_STATE, dtype=jnp.float32)
    return {
        'x': nrm(ks[0], (BATCH, SEQ, D), 1.0),
        'c': nrm(ks[1], (BATCH, D), 1.0),
        'ctx': nrm(ks[2], (BATCH, CTX_LEN, D), 1.0),
        'c_ctx': nrm(ks[3], (D,), 1.0),
        'w_mod': nrm(ks[4], (DEPTH, D, 6 * D), 0.5 * D ** -0.5),
        'b_mod': nrm(ks[5], (DEPTH, 6 * D), 0.01),
        'g_mix': 1.0 + nrm(ks[6], (DEPTH, D), 0.01),
        'g_ffn': 1.0 + nrm(ks[7], (DEPTH, D), 0.01),
        'w_in': nrm(ks[8], (DEPTH, D, IN_COLS), D ** -0.5),
        'na_rpb': nrm(ks[9], (DEPTH, NA_HEADS, 2 * NA_WIN_H - 1, 2 * NA_WIN_W - 1), 0.05),
        's5_lam_re': -0.5 + nrm(ks[10], s5_shape, 0.01),
        's5_lam_im': jnp.broadcast_to(math.pi * n_idx, s5_shape) + nrm(ks[11], s5_shape, 0.01),
        's5_b_re': nrm(ks[12], s5_shape + (S5_GROUP,), (2.0 * S5_GROUP) ** -0.5),
        's5_b_im': nrm(ks[13], s5_shape + (S5_GROUP,), (2.0 * S5_GROUP) ** -0.5),
        's5_c_re': nrm(ks[14], (DEPTH, 2, S5_GROUPS, S5_GROUP, S5_STATE), (2.0 * S5_STATE) ** -0.5),
        's5_c_im': nrm(ks[15], (DEPTH, 2, S5_GROUPS, S5_GROUP, S5_STATE), (2.0 * S5_STATE) ** -0.5),
        's5_log_step': jax.random.uniform(ks[16], (DEPTH, 2, S5_GROUPS), jnp.float32,
                                          math.log(S5_DT_MIN), math.log(S5_DT_MAX)),
        's5_d': nrm(ks[17], (DEPTH, S5_WIDTH), 0.5),
        's5_w_glu': nrm(ks[18], (DEPTH, S5_WIDTH, S5_WIDTH), S5_WIDTH ** -0.5),
        'sw_sink': nrm(ks[19], (DEPTH, SW_HEADS), 0.5),
        'w_branch': nrm(ks[20], (DEPTH, N_BRANCH, BRANCH_W, D), BRANCH_W ** -0.5),
        'w_out': nrm(ks[21], (DEPTH, D, D), D ** -0.5),
        'peer_w_q': nrm(ks[22], (DEPTH, D, PEER_HEADS * PEER_QDIM), D ** -0.5),
        'peer_sub_keys': nrm(ks[23], (DEPTH, 2, PEER_NKEYS, PEER_QDIM // 2), (PEER_QDIM // 2) ** -0.5),
        'peer_u': nrm(ks[24], (DEPTH, PEER_EXPERTS, D), D ** -0.5),
        'peer_v': nrm(ks[25], (DEPTH, PEER_EXPERTS, D), 0.5),
        'g_final': 1.0 + nrm(ks[26], (D,), 0.01),
    }


def reference(x, c, ctx, c_ctx, w_mod, b_mod, g_mix, g_ffn, w_in, na_rpb, s5_lam_re, s5_lam_im,
              s5_b_re, s5_b_im, s5_c_re, s5_c_im, s5_log_step, s5_d, s5_w_glu, sw_sink, w_branch,
              w_out, peer_w_q, peer_sub_keys, peer_u, peer_v, g_final):
    B, S, D = x.shape
    L = ctx.shape[1]
    t = jnp.arange(S)
    row = t // GRID_W
    col = t % GRID_W
    c_act = jax.nn.silu(c)
    cc_act = jax.nn.silu(c_ctx)
    xx, xc = x, ctx
    for l in range(DEPTH):
        need_ctx = l < DEPTH - 1
        mod = c_act @ w_mod[l] + b_mod[l]
        mod_c = cc_act @ w_mod[l] + b_mod[l]
        sh1, sc1, gt1, sh2, sc2, gt2 = jnp.split(mod[:, None, :], 6, axis=-1)
        sh1c, sc1c, gt1c, sh2c, sc2c, gt2c = jnp.split(mod_c, 6, axis=-1)
        hx = modulate(rms_norm(xx, g_mix[l]), sh1, sc1)
        hc = modulate(rms_norm(xc, g_mix[l]), sh1c, sc1c)
        qa, ka, va, ub, qs, ks_, vs, ga, gb, gs = split_columns(hx @ w_in[l])
        qa_c, ka_c, va_c, ub_c, qs_c, ks_c, vs_c, ga_c, gb_c, gs_c = split_columns(hc @ w_in[l])
        ya, ya_c = neighborhood_attention(
            to_heads(qa, NA_HEADS), to_heads(ka, NA_HEADS), to_heads(va, NA_HEADS),
            to_heads(qa_c, NA_HEADS), to_heads(ka_c, NA_HEADS), to_heads(va_c, NA_HEADS),
            na_rpb[l], need_ctx)
        yb, yb_c = s5_mixer(ub, ub_c, s5_lam_re[l], s5_lam_im[l], s5_b_re[l], s5_b_im[l],
                            s5_c_re[l], s5_c_im[l], s5_log_step[l], s5_d[l], s5_w_glu[l], need_ctx)
        ys, ys_c = sliding_window_attention(
            to_heads(qs, SW_HEADS), to_heads(ks_, SW_KV_HEADS), to_heads(vs, SW_KV_HEADS),
            to_heads(qs_c, SW_HEADS), to_heads(ks_c, SW_KV_HEADS), to_heads(vs_c, SW_KV_HEADS),
            sw_sink[l], row, col, need_ctx)
        xx = xx + gt1 * merge_branches(ya, yb, ys, ga, gb, gs, w_branch[l], w_out[l])
        if need_ctx:
            xc = xc + gt1c * merge_branches(ya_c, yb_c, ys_c, ga_c, gb_c, gs_c, w_branch[l], w_out[l])
        hx2 = modulate(rms_norm(xx, g_ffn[l]), sh2, sc2)
        if need_ctx:
            hc2 = modulate(rms_norm(xc, g_ffn[l]), sh2c, sc2c)
            h_all = jnp.concatenate([hc2, hx2], axis=1).reshape(B * (L + S), D)
            f = peer(h_all, peer_w_q[l], peer_sub_keys[l], peer_u[l], peer_v[l]).reshape(B, L + S, D)
            xc = xc + gt2c * f[:, :L]
            xx = xx + gt2 * f[:, L:]
        else:
            f = peer(hx2.reshape(B * S, D), peer_w_q[l], peer_sub_keys[l], peer_u[l], peer_v[l])
            xx = xx + gt2 * f.reshape(B, S, D)
    return rms_norm(xx, g_final)
```

```python
import functools
import math

import numpy as np
import jax
import jax.numpy as jnp
from jax import lax
from jax.experimental import pallas as pl
from jax.experimental.pallas import tpu as pltpu

F32 = jnp.float32
BF16 = jnp.bfloat16

GRID_W = 64
HEAD_DIM = 64
NA_WIN_H = 8
NA_WIN_W = 16
S5_GROUP = 16
S5_STATE = 64
SW_WINDOW = 128
ROPE_BASE = 10000.0
PEER_HEADS = 8
PEER_NKEYS = 128
PEER_QDIM = 256
PEER_TOPK = 16
EPS = 1e-6
NEG_INF = -1e30

LANES = 128
SUBLANES = 8
VMEM_LIMIT_BYTES = 56 * 1024 * 1024

ROW_BLOCK = 256
NA_ROWS = 4
NA_KROWS = NA_ROWS + NA_WIN_H - 1
SW_BLOCK = 128
S5_CHUNK = 16
S5_GB = 4
PEER_ROUTE_BLOCK = 256
PEER_TOK_BLOCK = 512
PEER_EXP_BLOCK = 512


def _cparams(sem):
    return pltpu.CompilerParams(dimension_semantics=sem, vmem_limit_bytes=VMEM_LIMIT_BYTES)


def _dot(a, b):
    return jnp.dot(a, b, preferred_element_type=F32)


def _dot_nt(a, b):
    return lax.dot_general(a, b, (((1,), (1,)), ((), ())), preferred_element_type=F32)


def _gelu(x):
    return 0.5 * x * (1.0 + lax.erf(x * (1.0 / math.sqrt(2.0))))


def _mod_kernel(cc_ref, w_ref, b_ref, o_ref):
    a = cc_ref[...]
    a = a * jax.nn.sigmoid(a)
    o_ref[0] = _dot(a.astype(BF16), w_ref[0].astype(BF16)) + b_ref[0]


def _mod_vectors(cc, w_mod, b_mod):
    depth, d, n6 = w_mod.shape
    tn = 1024
    return pl.pallas_call(
        _mod_kernel,
        grid=(depth, n6 // tn),
        in_specs=[pl.BlockSpec((SUBLANES, d), lambda l, j: (0, 0)),
                  pl.BlockSpec((1, d, tn), lambda l, j: (l, 0, j)),
                  pl.BlockSpec((1, 1, tn), lambda l, j: (l, 0, j))],
        out_specs=pl.BlockSpec((1, SUBLANES, tn), lambda l, j: (l, 0, j)),
        out_shape=jax.ShapeDtypeStruct((depth, SUBLANES, n6), F32),
        compiler_params=_cparams(("arbitrary", "arbitrary")),
        name="mod_vectors",
    )(cc, w_mod, b_mod.reshape(depth, 1, n6))


def _norm_mod(x, g, shift, scale):
    y = x * lax.rsqrt(jnp.mean(x * x, axis=-1, keepdims=True) + EPS)
    return (y * g) * (1.0 + scale) + shift


def _rope(x, cos, sin, lane_lo):
    up = pltpu.roll(x, LANES - 16, 1)
    dn = pltpu.roll(x, 16, 1)
    return x * cos + jnp.where(lane_lo, up, dn) * sin


def _inproj_kernel(x_ref, g_ref, sh_ref, sc_ref, w_ref, *rest, bw, kvw, d_model, rope):
    if rope:
        cos_ref, sin_ref = rest[:2]
        rest = rest[2:]
    qa_ref, ka_ref, va_ref, ub_ref, qs_ref, ks_ref, vs_ref, ga_ref, gb_ref, gs_ref = rest
    h = _norm_mod(x_ref[...], g_ref[...], sh_ref[...], sc_ref[...]).astype(BF16)

    def proj(c0, width):
        return _dot(h, w_ref[:, c0:c0 + width])

    c = 0
    qa_ref[...] = proj(c, bw).astype(BF16); c += bw
    ka_ref[...] = proj(c, bw).astype(BF16); c += bw
    va_ref[...] = proj(c, bw).astype(BF16); c += bw
    ub_ref[...] = proj(c, bw); c += bw
    qs = proj(c, bw); c += bw
    ks = proj(c, kvw); c += kvw
    vs_ref[...] = proj(c, kvw).astype(BF16); c += kvw
    if rope:
        cos = cos_ref[...]
        sin = sin_ref[...]
        lane = lax.broadcasted_iota(jnp.int32, cos.shape, 1)
        lane_lo = (lane % 32) < 16
        for p in range(bw // LANES):
            sl = slice(p * LANES, (p + 1) * LANES)
            qs_ref[:, sl] = _rope(qs[:, sl], cos, sin, lane_lo).astype(BF16)
        for p in range(kvw // LANES):
            sl = slice(p * LANES, (p + 1) * LANES)
            ks_ref[:, sl] = _rope(ks[:, sl], cos, sin, lane_lo).astype(BF16)
    else:
        qs_ref[...] = qs.astype(BF16)
        ks_ref[...] = ks.astype(BF16)
    ga_ref[...] = jax.nn.sigmoid(proj(c, d_model)).astype(BF16); c += d_model
    gb_ref[...] = jax.nn.sigmoid(proj(c, d_model)).astype(BF16); c += d_model
    gs_ref[...] = jax.nn.sigmoid(proj(c, d_model)).astype(BF16)


def _inproj(x, g, shift, scale, w_in, rope_tabs):
    n, d = x.shape
    bw = d // 2
    kvw = bw // 4
    tm = min(ROW_BLOCK, n)
    rope = rope_tabs is not None
    row = lambda i: (i, 0)
    fixed = lambda i: (0, 0)
    in_specs = [pl.BlockSpec((tm, d), row), pl.BlockSpec((1, d), fixed), pl.BlockSpec((1, d), fixed),
                pl.BlockSpec((1, d), fixed), pl.BlockSpec(w_in.shape, fixed)]
    args = [x, g, shift, scale, w_in]
    if rope:
        in_specs += [pl.BlockSpec((tm, LANES), row)] * 2
        args += list(rope_tabs)
    widths = [bw, bw, bw, bw, bw, kvw, kvw, d, d, d]
    dtypes = [BF16, BF16, BF16, F32, BF16, BF16, BF16, BF16, BF16, BF16]
    return pl.pallas_call(
        functools.partial(_inproj_kernel, bw=bw, kvw=kvw, d_model=d, rope=rope),
        grid=(n // tm,),
        in_specs=in_specs,
        out_specs=[pl.BlockSpec((tm, w), row) for w in widths],
        out_shape=[jax.ShapeDtypeStruct((n, w), dt) for w, dt in zip(widths, dtypes)],
        compiler_params=_cparams(("arbitrary",)),
        name="inproj",
    )(*args)


def _rope_tables(seq):
    t = jnp.arange(seq)
    half = HEAD_DIM // 4
    inv = ROPE_BASE ** (-jnp.arange(half, dtype=F32) / half)
    sign = jnp.concatenate([-jnp.ones((half,), F32), jnp.ones((half,), F32)])

    def axis_tabs(pos):
        ang = pos.astype(F32)[:, None] * inv[None, :]
        c = jnp.cos(ang)
        s = jnp.sin(ang)
        return jnp.concatenate([c, c], -1), jnp.concatenate([s, s], -1) * sign

    cr, sr = axis_tabs(t // GRID_W)
    cc, sc = axis_tabs(t % GRID_W)
    cos = jnp.concatenate([cr, cc], -1)
    sin = jnp.concatenate([sr, sc], -1)
    return jnp.tile(cos, (1, LANES // HEAD_DIM)), jnp.tile(sin, (1, LANES // HEAD_DIM))


def _na_patterns(rows):
    kh = NA_WIN_H
    nb = rows // NA_ROWS
    kr0s, sigs = [], []
    for b in range(nb):
        r_lo = b * NA_ROWS
        kr0 = int(np.clip(r_lo - kh // 2, 0, rows - NA_KROWS))
        sig = tuple((int(np.clip(r - kh // 2, 0, rows - kh)) - kr0, r - kr0)
                    for r in range(r_lo, r_lo + NA_ROWS))
        kr0s.append(kr0)
        sigs.append(sig)
    uniq = sorted(set(sigs))
    ids = np.array([uniq.index(s) for s in sigs], np.int32)
    return np.array(kr0s, np.int32), ids, uniq


def _na_bias(rpb, uniq):
    kw = NA_WIN_W
    cols = np.arange(GRID_W)
    col_start = np.clip(cols - kw // 2, 0, GRID_W - kw)
    nq, nk = NA_ROWS * GRID_W, NA_KROWS * GRID_W
    ro = np.zeros((len(uniq), nq, nk), np.int32)
    co = np.zeros((len(uniq), nq, nk), np.int32)
    ok = np.zeros((len(uniq), nq, nk), bool)
    kk = np.arange(NA_KROWS)[:, None]
    kc = np.arange(GRID_W)[None, :]
    for p, sig in enumerate(uniq):
        for rq, (r0_rel, r_rel) in enumerate(sig):
            for c in range(GRID_W):
                q = rq * GRID_W + c
                v_r = (kk >= r0_rel) & (kk < r0_rel + NA_WIN_H)
                v_c = (kc >= col_start[c]) & (kc < col_start[c] + kw)
                ok[p, q] = (v_r & v_c).reshape(-1)
                ro[p, q] = np.broadcast_to(np.clip(kk - r_rel + NA_WIN_H - 1, 0, 2 * NA_WIN_H - 2),
                                           (NA_KROWS, GRID_W)).reshape(-1)
                co[p, q] = np.broadcast_to(np.clip(kc - c + kw - 1, 0, 2 * kw - 2),
                                           (NA_KROWS, GRID_W)).reshape(-1)
    bias = rpb[:, ro, co]
    bias = jnp.where(jnp.asarray(ok)[None], bias, NEG_INF)
    return jnp.transpose(bias, (1, 0, 2, 3))


def _na_kernel(kr0_ref, pat_ref, q_ref, k_ref, v_ref, kc_ref, vc_ref, bias_ref, o_ref):
    b = pl.program_id(1)
    start = pl.multiple_of(kr0_ref[b] * GRID_W, GRID_W)
    nk = NA_KROWS * GRID_W
    scale = HEAD_DIM ** -0.5
    q = q_ref[...]
    kw = k_ref[pl.ds(start, nk), :]
    vw = v_ref[pl.ds(start, nk), :]
    kc = kc_ref[...]
    vc = vc_ref[...]
    outs = []
    for h in range(LANES // HEAD_DIM):
        sl = slice(h * HEAD_DIM, (h + 1) * HEAD_DIM)
        s_loc = _dot_nt(q[:, sl], kw[:, sl]) * scale + bias_ref[0, h]
        s_ctx = _dot_nt(q[:, sl], kc[:, sl]) * scale
        m = jnp.maximum(jnp.max(s_loc, axis=-1, keepdims=True), jnp.max(s_ctx, axis=-1, keepdims=True))
        p_loc = jnp.exp(s_loc - m)
        p_ctx = jnp.exp(s_ctx - m)
        den = jnp.sum(p_loc, axis=-1, keepdims=True) + jnp.sum(p_ctx, axis=-1, keepdims=True)
        o = _dot(p_loc.astype(BF16), vw[:, sl]) + _dot(p_ctx.astype(BF16), vc[:, sl])
        outs.append(o / den)
    o_ref[...] = jnp.concatenate(outs, axis=-1).astype(o_ref.dtype)


def _na_attention(q, k, v, kc, vc, rpb):
    s, bw = q.shape
    l = kc.shape[0]
    rows = s // GRID_W
    kr0s, ids, uniq = _na_patterns(rows)
    bias = _na_bias(rpb, uniq)
    tq = NA_ROWS * GRID_W
    nk = NA_KROWS * GRID_W
    hp = LANES // HEAD_DIM
    grid_spec = pltpu.PrefetchScalarGridSpec(
        num_scalar_prefetch=2,
        grid=(bw // LANES, rows // NA_ROWS),
        in_specs=[pl.BlockSpec((tq, LANES), lambda p, b, kr, pt: (b, p)),
                  pl.BlockSpec((s, LANES), lambda p, b, kr, pt: (0, p)),
                  pl.BlockSpec((s, LANES), lambda p, b, kr, pt: (0, p)),
                  pl.BlockSpec((l, LANES), lambda p, b, kr, pt: (0, p)),
                  pl.BlockSpec((l, LANES), lambda p, b, kr, pt: (0, p)),
                  pl.BlockSpec((1, hp, tq, nk), lambda p, b, kr, pt: (pt[b], p, 0, 0))],
        out_specs=pl.BlockSpec((tq, LANES), lambda p, b, kr, pt: (b, p)),
    )
    return pl.pallas_call(
        _na_kernel,
        grid_spec=grid_spec,
        out_shape=jax.ShapeDtypeStruct((s, bw), BF16),
        compiler_params=_cparams(("arbitrary", "arbitrary")),
        name="na_attention",
    )(jnp.asarray(kr0s), jnp.asarray(ids), q, k, v, kc, vc, bias)


def _ctx_attn_kernel(*refs, n_heads, n_rep, has_sink):
    if has_sink:
        sink_ref, q_ref, k_ref, v_ref, o_ref = refs
    else:
        q_ref, k_ref, v_ref, o_ref = refs
    scale = HEAD_DIM ** -0.5
    q = q_ref[...]
    k = k_ref[...]
    v = v_ref[...]
    outs = []
    for h in range(n_heads):
        g = h // n_rep
        sl = slice(h * HEAD_DIM, (h + 1) * HEAD_DIM)
        gl = slice(g * HEAD_DIM, (g + 1) * HEAD_DIM)
        s = _dot_nt(q[:, sl], k[:, gl]) * scale
        m = jnp.max(s, axis=-1, keepdims=True)
        if has_sink:
            m = jnp.maximum(m, sink_ref[h])
        p = jnp.exp(s - m)
        den = jnp.sum(p, axis=-1, keepdims=True)
        if has_sink:
            den = den + jnp.exp(sink_ref[h] - m)
        outs.append(_dot(p.astype(BF16), v[:, gl]) / den)
    o_ref[...] = jnp.concatenate(outs, axis=-1).astype(o_ref.dtype)


def _ctx_attention(q, k, v, sink):
    l, bw = q.shape
    n_heads = bw // HEAD_DIM
    n_rep = n_heads // (k.shape[1] // HEAD_DIM)
    has_sink = sink is not None
    full = lambda a: pl.BlockSpec(a.shape, lambda i: (0, 0))
    in_specs = [full(q), full(k), full(v)]
    args = [q, k, v]
    if has_sink:
        in_specs = [pl.BlockSpec(memory_space=pltpu.SMEM)] + in_specs
        args = [sink] + args
    return pl.pallas_call(
        functools.partial(_ctx_attn_kernel, n_heads=n_heads, n_rep=n_rep, has_sink=has_sink),
        grid=(1,),
        in_specs=in_specs,
        out_specs=pl.BlockSpec((l, bw), lambda i: (0, 0)),
        out_shape=jax.ShapeDtypeStruct((l, bw), BF16),
        compiler_params=_cparams(("arbitrary",)),
        name="ctx_attention",
    )(*args)


def _swa_kernel(sink_ref, q_ref, k_ref, v_ref, kc_ref, vc_ref, o_ref, *, seq, n_kv, n_rep):
    n = pl.program_id(0)
    nk = 3 * SW_BLOCK
    scale = HEAD_DIM ** -0.5
    ws = jnp.clip((n - 1) * SW_BLOCK, 0, seq - nk)
    ws = pl.multiple_of(ws, SW_BLOCK)
    q = q_ref[...]
    kw = k_ref[pl.ds(ws, nk), :]
    vw = v_ref[pl.ds(ws, nk), :]
    kc = kc_ref[...]
    vc = vc_ref[...]
    qpos = n * SW_BLOCK + lax.broadcasted_iota(jnp.int32, (SW_BLOCK, nk), 0)
    kpos = ws + lax.broadcasted_iota(jnp.int32, (SW_BLOCK, nk), 1)
    maskb = jnp.where(jnp.abs(kpos - qpos) <= SW_WINDOW, 0.0, NEG_INF).astype(F32)
    outs = [None] * (n_kv * n_rep)
    for g in range(n_kv):
        gl = slice(g * HEAD_DIM, (g + 1) * HEAD_DIM)
        qg = jnp.concatenate([q[:, (g * n_rep + r) * HEAD_DIM:(g * n_rep + r + 1) * HEAD_DIM]
                              for r in range(n_rep)], axis=0)
        s_loc = _dot_nt(qg, kw[:, gl]) * scale
        s_loc = (s_loc.reshape(n_rep, SW_BLOCK, nk) + maskb[None]).reshape(n_rep * SW_BLOCK, nk)
        s_ctx = _dot_nt(qg, kc[:, gl]) * scale
        sink = jnp.concatenate([jnp.full((SW_BLOCK, 1), sink_ref[g * n_rep + r], F32)
                                for r in range(n_rep)], axis=0)
        m = jnp.maximum(jnp.maximum(jnp.max(s_loc, axis=-1, keepdims=True),
                                    jnp.max(s_ctx, axis=-1, keepdims=True)), sink)
        p_loc = jnp.exp(s_loc - m)
        p_ctx = jnp.exp(s_ctx - m)
        den = (jnp.sum(p_loc, axis=-1, keepdims=True) + jnp.sum(p_ctx, axis=-1, keepdims=True)
               + jnp.exp(sink - m))
        o = (_dot(p_loc.astype(BF16), vw[:, gl]) + _dot(p_ctx.astype(BF16), vc[:, gl])) / den
        for r in range(n_rep):
            outs[g * n_rep + r] = o[r * SW_BLOCK:(r + 1) * SW_BLOCK]
    o_ref[...] = jnp.concatenate(outs, axis=-1).astype(o_ref.dtype)


def _sw_attention(q, k, v, kc, vc, sink):
    s, bw = q.shape
    l = kc.shape[0]
    kvw = k.shape[1]
    n_kv = kvw // HEAD_DIM
    n_rep = (bw // HEAD_DIM) // n_kv
    return pl.pallas_call(
        functools.partial(_swa_kernel, seq=s, n_kv=n_kv, n_rep=n_rep),
        grid=(s // SW_BLOCK,),
        in_specs=[pl.BlockSpec(memory_space=pltpu.SMEM),
                  pl.BlockSpec((SW_BLOCK, bw), lambda n: (n, 0)),
                  pl.BlockSpec((s, kvw), lambda n: (0, 0)),
                  pl.BlockSpec((s, kvw), lambda n: (0, 0)),
                  pl.BlockSpec((l, kvw), lambda n: (0, 0)),
                  pl.BlockSpec((l, kvw), lambda n: (0, 0))],
        out_specs=pl.BlockSpec((SW_BLOCK, bw), lambda n: (n, 0)),
        out_shape=jax.ShapeDtypeStruct((s, bw), BF16),
        compiler_params=_cparams(("arbitrary",)),
        name="sw_attention",
    )(sink, q, k, v, kc, vc)


def _s5_prepare(lam_re, lam_im, b_re, b_im, c_re, c_im, log_step):
    hp = lax.Precision.HIGHEST
    tc = S5_CHUNK
    lam = lax.complex(lam_re.astype(F32), lam_im.astype(F32))
    dt = jnp.exp(log_step.astype(F32))[..., None]
    lam_dt = lam * dt
    lam_bar = jnp.exp(lam_dt)
    b_bar = ((lam_bar - 1.0) / lam)[..., None] * lax.complex(b_re.astype(F32), b_im.astype(F32))
    c_mat = lax.complex(c_re.astype(F32), c_im.astype(F32))
    kk = jnp.arange(tc + 1, dtype=F32)
    pw = jnp.exp(lam_dt[None] * kk[:, None, None, None])
    n_g, n_p, n_h = b_bar.shape[1:]

    cw = c_mat[None] * pw[:tc, :, :, None, :]
    kern = (jnp.einsum('kdghp,dgpj->kdghj', cw.real, b_bar.real, precision=hp)
            - jnp.einsum('kdghp,dgpj->kdghj', cw.imag, b_bar.imag, precision=hp))
    t_i = np.arange(tc)
    lag_f = t_i[None, :] - t_i[:, None]
    m_f = jnp.where((lag_f >= 0)[:, :, None, None, None],
                    kern[np.clip(lag_f, 0, tc - 1), 0], 0.0)
    m_r = jnp.where((lag_f <= 0)[:, :, None, None, None],
                    kern[np.clip(-lag_f, 0, tc - 1), 1], 0.0)
    m_intra = jnp.transpose(m_f + m_r, (2, 0, 4, 1, 3)).reshape(n_g, tc * n_h, tc * n_h)

    def cat_ri(z):
        return jnp.concatenate([z.real, z.imag], axis=-1)

    w_f = pw[tc - 1 - t_i, 0][:, :, None, :] * jnp.transpose(b_bar[0], (0, 2, 1))[None]
    w_r = pw[t_i, 1][:, :, None, :] * jnp.transpose(b_bar[1], (0, 2, 1))[None]
    w_f = jnp.transpose(cat_ri(w_f), (1, 0, 2, 3)).reshape(n_g, tc * n_h, 2 * n_p)
    w_r = jnp.transpose(cat_ri(w_r), (1, 0, 2, 3)).reshape(n_g, tc * n_h, 2 * n_p)
    m_cat = jnp.concatenate([m_intra, w_f, w_r], axis=-1).astype(BF16)

    def v_of(c_dir, pw_sel):
        cp = c_dir[None] * pw_sel[:, :, None, :]
        v = jnp.concatenate([cp.real, -cp.imag], axis=-1)
        return jnp.transpose(v, (1, 3, 0, 2)).reshape(n_g, 2 * n_p, tc * n_h)
    v_cat = jnp.concatenate([v_of(c_mat[0], pw[t_i + 1, 0]), v_of(c_mat[1], pw[tc - t_i, 1])],
                            axis=1).astype(BF16)

    def coef(k):
        a = jnp.exp(lam_dt * (tc * k))
        return jnp.stack([jnp.concatenate([a.real, a.real], -1),
                          jnp.concatenate([-a.imag, a.imag], -1)], axis=2)
    a_log = jnp.stack([coef(1.0), coef(2.0), coef(4.0)], axis=2)
    a_car = jnp.stack([coef(float(j)) for j in range(1, SUBLANES + 1)], axis=3)
    a_car = jnp.stack([a_car[0], a_car[1][:, :, ::-1]], axis=0)
    return m_cat, v_cat, jnp.transpose(a_log, (1, 0, 2, 3, 4)), jnp.transpose(a_car, (1, 0, 2, 3, 4))


def _s5_kernel(u_ref, d_ref, m_ref, v_ref, alog_ref, acar_ref, y_ref, z_scr, e_scr, *, nc, nc_ctx):
    n_state = 2 * S5_STATE
    width = S5_CHUNK * S5_GROUP
    nblk = nc // SUBLANES
    nblk_ctx = nc_ctx // SUBLANES
    row = lax.broadcasted_iota(jnp.int32, (SUBLANES, n_state), 0)

    for g in range(S5_GB):
        zz = _dot(u_ref[g].astype(BF16), m_ref[g])
        y_ref[g] = zz[:, :width] + u_ref[g] * d_ref[g]
        z_scr[g, 0] = zz[:, width:width + n_state]
        z_scr[g, 1] = zz[:, width + n_state:]

    def cmul(a1, a2, s):
        return a1 * s + a2 * pltpu.roll(s, S5_STATE, 1)

    def step(t, carry):
        blk = (t, jnp.where(t < nblk_ctx, nblk_ctx - 1 - t, nblk + nblk_ctx - 1 - t))
        new = []
        for g in range(S5_GB):
            for d in range(2):
                r0 = pl.multiple_of(blk[d] * SUBLANES, SUBLANES)
                z = z_scr[g, d, pl.ds(r0, SUBLANES), :]
                c = carry[g * 2 + d]
                for i, sh in enumerate((1, 2, 4)):
                    if d == 0:
                        zs = jnp.where(row >= sh, pltpu.roll(z, sh, 0), 0.0)
                    else:
                        zs = jnp.where(row < SUBLANES - sh, pltpu.roll(z, SUBLANES - sh, 0), 0.0)
                    z = z + cmul(alog_ref[g, d, i, 0:1, :], alog_ref[g, d, i, 1:2, :], zs)
                cb = jnp.broadcast_to(c, (SUBLANES, n_state))
                s = z + cmul(acar_ref[g, d, 0], acar_ref[g, d, 1], cb)
                if d == 0:
                    e = jnp.where(row >= 1, pltpu.roll(s, 1, 0), cb)
                    c_new = s[SUBLANES - 1:SUBLANES, :]
                else:
                    e = jnp.where(row < SUBLANES - 1, pltpu.roll(s, SUBLANES - 1, 0), cb)
                    c_new = s[0:1, :]
                e_scr[g, pl.ds(r0, SUBLANES), pl.ds(d * n_state, n_state)] = e
                new.append(c_new)
        return tuple(new)

    init = tuple(jnp.zeros((1, n_state), F32) for _ in range(2 * S5_GB))
    lax.fori_loop(0, nblk, step, init)

    for g in range(S5_GB):
        y_ref[g] = y_ref[g] + _dot(e_scr[g].astype(BF16), v_ref[g])


def _s5_scan(u_chunks, d_rows, m_cat, v_cat, a_log, a_car, nc_ctx):
    n_g, nc, width = u_chunks.shape
    n_state = 2 * S5_STATE
    blk = lambda *shape: pl.BlockSpec((S5_GB,) + shape, lambda i: (i,) + (0,) * len(shape))
    return pl.pallas_call(
        functools.partial(_s5_kernel, nc=nc, nc_ctx=nc_ctx),
        grid=(n_g // S5_GB,),
        in_specs=[blk(nc, width), blk(1, width), blk(width, width + 2 * n_state),
                  blk(2 * n_state, width), blk(2, 3, 2, n_state), blk(2, 2, SUBLANES, n_state)],
        out_specs=blk(nc, width),
        out_shape=jax.ShapeDtypeStruct((n_g, nc, width), F32),
        scratch_shapes=[pltpu.VMEM((S5_GB, 2, nc, n_state), F32),
                        pltpu.VMEM((S5_GB, nc, 2 * n_state), F32)],
        compiler_params=_cparams(("arbitrary",)),
        name="s5_scan",
    )(u_chunks, d_rows, m_cat, v_cat, a_log, a_car)


def _s5_mixer_pre_glu(u_ctx, u_lat, prep, d_skip):
    m_cat, v_cat, a_log, a_car = prep
    l, w = u_ctx.shape
    s = u_lat.shape[0]
    n_g = w // S5_GROUP
    u = jnp.concatenate([u_ctx, u_lat], axis=0)
    nc = (l + s) // S5_CHUNK
    u_chunks = jnp.transpose(u.reshape(nc, S5_CHUNK, n_g, S5_GROUP), (2, 0, 1, 3)).reshape(n_g, nc, -1)
    d_rows = jnp.tile(d_skip.astype(F32).reshape(n_g, 1, S5_GROUP), (1, 1, S5_CHUNK))
    y = _s5_scan(u_chunks, d_rows, m_cat, v_cat, a_log, a_car, l // S5_CHUNK)
    y = jnp.transpose(y.reshape(n_g, nc, S5_CHUNK, S5_GROUP), (1, 2, 0, 3)).reshape(l + s, w)
    return y[:l], y[l:]


def _merge_kernel(x_ref, ya_ref, y5_ref, ys_ref, ga_ref, gb_ref, gs_ref, wglu_ref, wb_ref, wo_ref,
                  gt_ref, g_ref, sh_ref, sc_ref, xo_ref, h_ref):
    z = _gelu(y5_ref[...])
    yb = z * jax.nn.sigmoid(_dot(z.astype(BF16), wglu_ref[...]))
    m = (ga_ref[...].astype(F32) * _dot(ya_ref[...], wb_ref[0])
         + gb_ref[...].astype(F32) * _dot(yb.astype(BF16), wb_ref[1])
         + gs_ref[...].astype(F32) * _dot(ys_ref[...], wb_ref[2]))
    x = x_ref[...] + gt_ref[...] * _dot(m.astype(BF16), wo_ref[...])
    xo_ref[...] = x
    h_ref[...] = _norm_mod(x, g_ref[...], sh_ref[...], sc_ref[...]).astype(BF16)


def _merge(x, ya, y5, ys, ga, gb, gs, w_glu, w_branch, w_out, gate, g, shift, scale):
    n, d = x.shape
    bw = ya.shape[1]
    tm = min(ROW_BLOCK, n)
    row = lambda w: pl.BlockSpec((tm, w), lambda i: (i, 0))
    full = lambda a: pl.BlockSpec(a.shape, lambda i: (0,) * a.ndim)
    vec = pl.BlockSpec((1, d), lambda i: (0, 0))
    return pl.pallas_call(
        _merge_kernel,
        grid=(n // tm,),
        in_specs=[row(d), row(bw), row(bw), row(bw), row(d), row(d), row(d),
                  full(w_glu), full(w_branch), full(w_out), vec, vec, vec, vec],
        out_specs=[row(d), row(d)],
        out_shape=[jax.ShapeDtypeStruct((n, d), F32), jax.ShapeDtypeStruct((n, d), BF16)],
        compiler_params=_cparams(("arbitrary",)),
        name="merge",
    )(x, ya, y5, ys, ga, gb, gs, w_glu, w_branch, w_out, gate, g, shift, scale)


def _top16(s, iota):
    rank = jnp.full(s.shape, float(PEER_TOPK), F32)
    vals = []
    n_rows = float(s.shape[0])
    for r in range(PEER_TOPK):
        m = jnp.max(s, axis=0, keepdims=True)
        idx = jnp.min(jnp.where(s == m, iota, n_rows), axis=0, keepdims=True)
        hit = iota == idx
        rank = jnp.where(hit, float(r), rank)
        s = jnp.where(hit, -jnp.inf, s)
        vals.append(m)
    return rank, jnp.concatenate(vals, axis=0)


def _pair_tiles():
    tiles = [(0, 0, 8), (0, 8, 8)]
    for a in range(1, 8):
        tiles.append((a, 0, PEER_TOPK // (a + 1)))
    return tiles


def _route_kernel(h_ref, wq_ref, k1_ref, k2_ref, cnt_ref, c1_ref, rk_ref, e2_ref, q_scr):
    tb = h_ref.shape[0]
    q_scr[...] = _dot(h_ref[...], wq_ref[...]).astype(BF16)
    half = PEER_QDIM // 2
    iota = lax.broadcasted_iota(jnp.int32, (PEER_NKEYS, tb), 0).astype(F32)
    row8 = lax.broadcasted_iota(jnp.int32, (SUBLANES, tb), 0).astype(F32)
    k1 = k1_ref[...]
    k2 = k2_ref[...]

    def head(hd, _):
        c0 = pl.multiple_of(hd * PEER_QDIM, PEER_QDIM)
        s1 = _dot_nt(k1, q_scr[:, pl.ds(c0, half)])
        s2 = _dot_nt(k2, q_scr[:, pl.ds(c0 + half, half)])
        rank1, v1 = _top16(s1, iota)
        rank2, v2 = _top16(s2, iota)
        e1v = jnp.exp(v1 - v1[0:1])
        e2v = jnp.exp(v2 - v2[0:1])
        tiles, flats, gates = [], [], []
        for a, b0, nv in _pair_tiles():
            c = v1[a:a + 1] + v2[b0:b0 + SUBLANES]
            tiles.append(jnp.where(row8 < nv, c, -jnp.inf))
            flats.append(a * PEER_TOPK + b0 + row8)
            gates.append(e1v[a:a + 1] * e2v[b0:b0 + SUBLANES])
        tiles.append(v1[SUBLANES:] + v2[0:1])
        flats.append((row8 + SUBLANES) * PEER_TOPK)
        gates.append(e1v[SUBLANES:] * e2v[0:1])
        cand = jnp.concatenate(tiles, axis=0)
        flat = jnp.concatenate(flats, axis=0)
        gate = jnp.concatenate(gates, axis=0)
        self_ = jnp.zeros(cand.shape, F32)
        for _ in range(PEER_TOPK):
            m = jnp.max(cand, axis=0, keepdims=True)
            idx = jnp.min(jnp.where(cand == m, flat, float(PEER_TOPK * PEER_TOPK)), axis=0, keepdims=True)
            hit = flat == idx
            self_ = jnp.where(hit, 1.0, self_)
            cand = jnp.where(hit, -jnp.inf, cand)
        z = jnp.sum(self_ * gate, axis=0, keepdims=True)
        cnt = [self_[0:8].sum(axis=0, keepdims=True) + self_[8:16].sum(axis=0, keepdims=True)]
        for t in range(2, 9):
            cnt.append(self_[t * SUBLANES:(t + 1) * SUBLANES].sum(axis=0, keepdims=True))
        cnt = jnp.concatenate(cnt + [self_[9 * SUBLANES:]], axis=0)
        cnt1 = jnp.zeros((PEER_NKEYS, tb), F32)
        for a in range(PEER_TOPK):
            cnt1 = jnp.where(rank1 == float(a), cnt[a:a + 1], cnt1)
        cnt_ref[hd] = cnt1
        c1_ref[hd] = jnp.exp(s1 - v1[0:1]) / z
        rk_ref[hd] = rank2.astype(BF16)
        e2_ref[hd] = jnp.exp(s2 - v2[0:1]).astype(BF16)
        return 0

    lax.fori_loop(0, PEER_HEADS, head, 0)


def _peer_route(h, w_q, k1, k2):
    n, d = h.shape
    tb = min(PEER_ROUTE_BLOCK, n)
    qw = w_q.shape[1]
    out_blk = pl.BlockSpec((PEER_HEADS, PEER_NKEYS, tb), lambda i: (0, 0, i))
    shp = lambda dt: jax.ShapeDtypeStruct((PEER_HEADS, PEER_NKEYS, n), dt)
    return pl.pallas_call(
        _route_kernel,
        grid=(n // tb,),
        in_specs=[pl.BlockSpec((tb, d), lambda i: (i, 0)),
                  pl.BlockSpec((d, qw), lambda i: (0, 0)),
                  pl.BlockSpec(k1.shape, lambda i: (0, 0)),
                  pl.BlockSpec(k2.shape, lambda i: (0, 0))],
        out_specs=[out_blk] * 4,
        out_shape=[shp(F32), shp(F32), shp(BF16), shp(BF16)],
        scratch_shapes=[pltpu.VMEM((tb, qw), BF16)],
        compiler_params=_cparams(("arbitrary",)),
        name="peer_route",
    )(h, w_q, k1, k2)


def _dense_kernel(h_ref, x_ref, gt_ref, u_ref, vt_ref, cnt_ref, c1_ref, rk_ref, e2_ref, o_ref,
                  acc_ref, w_scr):
    eb = pl.program_id(1)
    tb = h_ref.shape[0]
    n_i = u_ref.shape[0] // PEER_NKEYS

    @pl.when(eb == 0)
    def _():
        acc_ref[...] = jnp.zeros_like(acc_ref)

    a_t = _dot_nt(u_ref[...], h_ref[...])
    for il in range(n_i):
        i = eb * n_i + il
        gsum = jnp.zeros((PEER_NKEYS, tb), BF16)
        for hd in range(PEER_HEADS):
            cnt = cnt_ref[hd, pl.ds(i, 1), :].astype(BF16)
            c1 = c1_ref[hd, pl.ds(i, 1), :].astype(BF16)
            gsum = gsum + jnp.where(rk_ref[hd] < cnt, e2_ref[hd] * c1, jnp.zeros((), BF16))
        rows = slice(il * PEER_NKEYS, (il + 1) * PEER_NKEYS)
        w_scr[rows, :] = (_gelu(a_t[rows]) * gsum.astype(F32)).astype(BF16)
    acc_ref[...] += _dot(vt_ref[...], w_scr[...])

    @pl.when(eb == pl.num_programs(1) - 1)
    def _():
        o_ref[...] = x_ref[...] + gt_ref[...] * acc_ref[...].T


def _peer_dense(h, x, gate, u, v_t, cnt1, c1, rank2, e2):
    n, d = h.shape
    n_e = u.shape[0]
    tb = min(PEER_TOK_BLOCK, n)
    eb = PEER_EXP_BLOCK
    tab = pl.BlockSpec((PEER_HEADS, PEER_NKEYS, tb), lambda i, e: (0, 0, i))
    return pl.pallas_call(
        _dense_kernel,
        grid=(n // tb, n_e // eb),
        in_specs=[pl.BlockSpec((tb, d), lambda i, e: (i, 0)),
                  pl.BlockSpec((tb, d), lambda i, e: (i, 0)),
                  pl.BlockSpec((1, d), lambda i, e: (0, 0)),
                  pl.BlockSpec((eb, d), lambda i, e: (e, 0)),
                  pl.BlockSpec((d, eb), lambda i, e: (0, e)),
                  tab, tab, tab, tab],
        out_specs=pl.BlockSpec((tb, d), lambda i, e: (i, 0)),
        out_shape=jax.ShapeDtypeStruct((n, d), F32),
        scratch_shapes=[pltpu.VMEM((d, tb), F32), pltpu.VMEM((eb, tb), BF16)],
        compiler_params=_cparams(("arbitrary", "arbitrary")),
        name="peer_dense",
    )(h, x, gate, u, v_t, cnt1, c1, rank2, e2)


def _final_norm_kernel(x_ref, g_ref, o_ref):
    x = x_ref[...]
    o_ref[...] = x * lax.rsqrt(jnp.mean(x * x, axis=-1, keepdims=True) + EPS) * g_ref[...]


def _final_norm(x, g):
    n, d = x.shape
    tm = min(2 * ROW_BLOCK, n)
    return pl.pallas_call(
        _final_norm_kernel,
        grid=(n // tm,),
        in_specs=[pl.BlockSpec((tm, d), lambda i: (i, 0)), pl.BlockSpec((1, d), lambda i: (0, 0))],
        out_specs=pl.BlockSpec((tm, d), lambda i: (i, 0)),
        out_shape=jax.ShapeDtypeStruct((n, d), F32),
        compiler_params=_cparams(("arbitrary",)),
        name="final_norm",
    )(x, g)


def kernel(x, c, ctx, c_ctx, w_mod, b_mod, g_mix, g_ffn, w_in, na_rpb, s5_lam_re, s5_lam_im,
           s5_b_re, s5_b_im, s5_c_re, s5_c_im, s5_log_step, s5_d, s5_w_glu, sw_sink, w_branch,
           w_out, peer_w_q, peer_sub_keys, peer_u, peer_v, g_final):
    batch, seq, d = x.shape
    l_ctx = ctx.shape[1]
    depth = w_mod.shape[0]
    assert batch == 1 and seq % (NA_ROWS * GRID_W) == 0 and l_ctx % (S5_CHUNK * SUBLANES) == 0
    assert seq // GRID_W >= NA_KROWS and seq >= 3 * SW_BLOCK

    cc = jnp.zeros((SUBLANES, d), F32).at[0].set(c[0]).at[1].set(c_ctx)
    mod = _mod_vectors(cc, w_mod, b_mod).reshape(depth, SUBLANES, 6, d)
    rope_tabs = _rope_tables(seq)
    row = lambda v: v.reshape(1, d)

    xx, xc = x[0], ctx[0]
    for l in range(depth):
        need_ctx = l < depth - 1
        m_lat, m_ctx = mod[l, 0], mod[l, 1]
        w_in_l = w_in[l].astype(BF16)
        g_mix_l = row(g_mix[l])
        g_ffn_l = row(g_ffn[l])

        qa, ka, va, ub, qs, ks, vs, ga, gb, gs = _inproj(
            xx, g_mix_l, row(m_lat[0]), row(m_lat[1]), w_in_l, rope_tabs)
        qa_c, ka_c, va_c, ub_c, qs_c, ks_c, vs_c, ga_c, gb_c, gs_c = _inproj(
            xc, g_mix_l, row(m_ctx[0]), row(m_ctx[1]), w_in_l, None)

        ya = _na_attention(qa, ka, va, ka_c, va_c, na_rpb[l])
        ys = _sw_attention(qs, ks, vs, ks_c, vs_c, sw_sink[l])
        prep = _s5_prepare(s5_lam_re[l], s5_lam_im[l], s5_b_re[l], s5_b_im[l], s5_c_re[l],
                           s5_c_im[l], s5_log_step[l])
        y5_c, y5 = _s5_mixer_pre_glu(ub_c, ub, prep, s5_d[l])

        w_glu_l = s5_w_glu[l].astype(BF16)
        w_branch_l = w_branch[l].astype(BF16)
        w_out_l = w_out[l].astype(BF16)
        xx, hx2 = _merge(xx, ya, y5, ys, ga, gb, gs, w_glu_l, w_branch_l, w_out_l,
                         row(m_lat[2]), g_ffn_l, row(m_lat[3]), row(m_lat[4]))

        w_q_l = peer_w_q[l].astype(BF16)
        k1 = peer_sub_keys[l, 0].astype(BF16)
        k2 = peer_sub_keys[l, 1].astype(BF16)
        u_l = peer_u[l].astype(BF16)
        vt_l = peer_v[l].astype(BF16).T
        if need_ctx:
            ya_c = _ctx_attention(qa_c, ka_c, va_c, None)
            ys_c = _ctx_attention(qs_c, ks_c, vs_c, sw_sink[l])
            xc, hc2 = _merge(xc, ya_c, y5_c, ys_c, ga_c, gb_c, gs_c, w_glu_l, w_branch_l, w_out_l,
                             row(m_ctx[2]), g_ffn_l, row(m_ctx[3]), row(m_ctx[4]))
            xc = _peer_dense(hc2, xc, row(m_ctx[5]), u_l, vt_l, *_peer_route(hc2, w_q_l, k1, k2))
        xx = _peer_dense(hx2, xx, row(m_lat[5]), u_l, vt_l, *_peer_route(hx2, w_q_l, k1, k2))

    return _final_norm(xx, row(g_final))[None]
```

```python
import functools
import math

import numpy as np
import jax
import jax.numpy as jnp
from jax import lax
from jax.experimental import pallas as pl
from jax.experimental.pallas import tpu as pltpu

F32 = jnp.float32
BF16 = jnp.bfloat16

GRID_W = 64
HEAD_DIM = 64
NA_WIN_H = 8
NA_WIN_W = 16
S5_GROUP = 16
S5_STATE = 64
SW_WINDOW = 128
ROPE_BASE = 10000.0
PEER_HEADS = 8
PEER_NKEYS = 128
PEER_QDIM = 256
PEER_TOPK = 16
EPS = 1e-6
NEG_INF = -1e30

LANES = 128
SUBLANES = 8
VMEM_LIMIT_BYTES = 56 * 1024 * 1024

ROW_BLOCK = 256
NA_ROWS = 4
NA_KROWS = NA_ROWS + NA_WIN_H - 1
SW_BLOCK = 128
S5_CHUNK = 16
S5_GB = 4
PEER_ROUTE_BLOCK = 256
PEER_TOK_BLOCK = 512
PEER_EXP_BLOCK = 1024
PEER_SUB = 256
PEER_ACC_CHUNK = 512


def _cparams(sem):
    return pltpu.CompilerParams(dimension_semantics=sem, vmem_limit_bytes=VMEM_LIMIT_BYTES)


def _dot(a, b):
    return jnp.dot(a, b, preferred_element_type=F32)


def _dot_nt(a, b):
    return lax.dot_general(a, b, (((1,), (1,)), ((), ())), preferred_element_type=F32)


def _gelu(x):
    return 0.5 * x * (1.0 + lax.erf(x * (1.0 / math.sqrt(2.0))))


def _mod_kernel(cc_ref, w_ref, b_ref, o_ref):
    a = cc_ref[...]
    a = a * jax.nn.sigmoid(a)
    o_ref[0] = _dot(a.astype(BF16), w_ref[0].astype(BF16)) + b_ref[0]


def _mod_vectors(cc, w_mod, b_mod):
    depth, d, n6 = w_mod.shape
    tn = 1024
    return pl.pallas_call(
        _mod_kernel,
        grid=(depth, n6 // tn),
        in_specs=[pl.BlockSpec((SUBLANES, d), lambda l, j: (0, 0)),
                  pl.BlockSpec((1, d, tn), lambda l, j: (l, 0, j)),
                  pl.BlockSpec((1, 1, tn), lambda l, j: (l, 0, j))],
        out_specs=pl.BlockSpec((1, SUBLANES, tn), lambda l, j: (l, 0, j)),
        out_shape=jax.ShapeDtypeStruct((depth, SUBLANES, n6), F32),
        compiler_params=_cparams(("arbitrary", "arbitrary")),
        name="mod_vectors",
    )(cc, w_mod, b_mod.reshape(depth, 1, n6))


def _norm_mod(x, g, shift, scale):
    y = x * lax.rsqrt(jnp.mean(x * x, axis=-1, keepdims=True) + EPS)
    return (y * g) * (1.0 + scale) + shift


def _rope(x, cos, sin, lane_lo):
    up = pltpu.roll(x, LANES - 16, 1)
    dn = pltpu.roll(x, 16, 1)
    return x * cos + jnp.where(lane_lo, up, dn) * sin


def _inproj_kernel(x_ref, g_ref, sh_ref, sc_ref, w_ref, *rest, bw, kvw, d_model, rope):
    if rope:
        cos_ref, sin_ref = rest[:2]
        rest = rest[2:]
    qa_ref, ka_ref, va_ref, ub_ref, qs_ref, ks_ref, vs_ref, ga_ref, gb_ref, gs_ref = rest
    h = _norm_mod(x_ref[...], g_ref[...], sh_ref[...], sc_ref[...]).astype(BF16)

    def proj(c0, width):
        return _dot(h, w_ref[:, c0:c0 + width])

    c = 0
    qa_ref[...] = proj(c, bw).astype(BF16); c += bw
    ka_ref[...] = proj(c, bw).astype(BF16); c += bw
    va_ref[...] = proj(c, bw).astype(BF16); c += bw
    ub_ref[...] = proj(c, bw); c += bw
    qs = proj(c, bw); c += bw
    ks = proj(c, kvw); c += kvw
    vs_ref[...] = proj(c, kvw).astype(BF16); c += kvw
    if rope:
        cos = cos_ref[...]
        sin = sin_ref[...]
        lane = lax.broadcasted_iota(jnp.int32, cos.shape, 1)
        lane_lo = (lane % 32) < 16
        for p in range(bw // LANES):
            sl = slice(p * LANES, (p + 1) * LANES)
            qs_ref[:, sl] = _rope(qs[:, sl], cos, sin, lane_lo).astype(BF16)
        for p in range(kvw // LANES):
            sl = slice(p * LANES, (p + 1) * LANES)
            ks_ref[:, sl] = _rope(ks[:, sl], cos, sin, lane_lo).astype(BF16)
    else:
        qs_ref[...] = qs.astype(BF16)
        ks_ref[...] = ks.astype(BF16)
    ga_ref[...] = jax.nn.sigmoid(proj(c, d_model)).astype(BF16); c += d_model
    gb_ref[...] = jax.nn.sigmoid(proj(c, d_model)).astype(BF16); c += d_model
    gs_ref[...] = jax.nn.sigmoid(proj(c, d_model)).astype(BF16)


def _inproj(x, g, shift, scale, w_in, rope_tabs):
    n, d = x.shape
    bw = d // 2
    kvw = bw // 4
    tm = min(ROW_BLOCK, n)
    rope = rope_tabs is not None
    row = lambda i: (i, 0)
    fixed = lambda i: (0, 0)
    in_specs = [pl.BlockSpec((tm, d), row), pl.BlockSpec((1, d), fixed), pl.BlockSpec((1, d), fixed),
                pl.BlockSpec((1, d), fixed), pl.BlockSpec(w_in.shape, fixed)]
    args = [x, g, shift, scale, w_in]
    if rope:
        in_specs += [pl.BlockSpec((tm, LANES), row)] * 2
        args += list(rope_tabs)
    widths = [bw, bw, bw, bw, bw, kvw, kvw, d, d, d]
    dtypes = [BF16, BF16, BF16, F32, BF16, BF16, BF16, BF16, BF16, BF16]
    return pl.pallas_call(
        functools.partial(_inproj_kernel, bw=bw, kvw=kvw, d_model=d, rope=rope),
        grid=(n // tm,),
        in_specs=in_specs,
        out_specs=[pl.BlockSpec((tm, w), row) for w in widths],
        out_shape=[jax.ShapeDtypeStruct((n, w), dt) for w, dt in zip(widths, dtypes)],
        compiler_params=_cparams(("arbitrary",)),
        name="inproj",
    )(*args)


def _rope_tables(seq):
    t = jnp.arange(seq)
    half = HEAD_DIM // 4
    inv = ROPE_BASE ** (-jnp.arange(half, dtype=F32) / half)
    sign = jnp.concatenate([-jnp.ones((half,), F32), jnp.ones((half,), F32)])

    def axis_tabs(pos):
        ang = pos.astype(F32)[:, None] * inv[None, :]
        c = jnp.cos(ang)
        s = jnp.sin(ang)
        return jnp.concatenate([c, c], -1), jnp.concatenate([s, s], -1) * sign

    cr, sr = axis_tabs(t // GRID_W)
    cc, sc = axis_tabs(t % GRID_W)
    cos = jnp.concatenate([cr, cc], -1)
    sin = jnp.concatenate([sr, sc], -1)
    return jnp.tile(cos, (1, LANES // HEAD_DIM)), jnp.tile(sin, (1, LANES // HEAD_DIM))


def _na_patterns(rows):
    kh = NA_WIN_H
    nb = rows // NA_ROWS
    kr0s, sigs = [], []
    for b in range(nb):
        r_lo = b * NA_ROWS
        kr0 = int(np.clip(r_lo - kh // 2, 0, rows - NA_KROWS))
        sig = tuple((int(np.clip(r - kh // 2, 0, rows - kh)) - kr0, r - kr0)
                    for r in range(r_lo, r_lo + NA_ROWS))
        kr0s.append(kr0)
        sigs.append(sig)
    uniq = sorted(set(sigs))
    ids = np.array([uniq.index(s) for s in sigs], np.int32)
    return np.array(kr0s, np.int32), ids, uniq


def _na_bias(rpb, uniq):
    kw = NA_WIN_W
    cols = np.arange(GRID_W)
    col_start = np.clip(cols - kw // 2, 0, GRID_W - kw)
    n_pat = len(uniq)
    kk = np.arange(NA_KROWS)
    col_sel = (cols[None, None, :] - cols[None, :, None] + kw - 1
               == np.arange(2 * kw - 1)[:, None, None])
    col_ok = ((cols[None, :] >= col_start[:, None]) & (cols[None, :] < col_start[:, None] + kw))
    row_sel = np.zeros((n_pat, NA_ROWS, NA_KROWS, 2 * NA_WIN_H - 1), bool)
    row_ok = np.zeros((n_pat, NA_ROWS, NA_KROWS), bool)
    for p, sig in enumerate(uniq):
        for rq, (r0_rel, r_rel) in enumerate(sig):
            row_ok[p, rq] = (kk >= r0_rel) & (kk < r0_rel + NA_WIN_H)
            row_sel[p, rq] = (kk[:, None] - r_rel + NA_WIN_H - 1) == np.arange(2 * NA_WIN_H - 1)[None, :]
    hp = lax.Precision.HIGHEST
    by_col = jnp.einsum('hrv,vcj->hrcj', rpb.astype(F32), jnp.asarray(col_sel, F32), precision=hp)
    bias = jnp.einsum('pqkr,hrcj->phqckj', jnp.asarray(row_sel, F32), by_col, precision=hp)
    ok = row_ok[:, None, :, None, :, None] & col_ok[None, None, None, :, None, :]
    bias = jnp.where(jnp.asarray(ok), bias, NEG_INF)
    return bias.reshape(n_pat, rpb.shape[0], NA_ROWS * GRID_W, NA_KROWS * GRID_W)


def _na_kernel(kr0_ref, pat_ref, q_ref, k_ref, v_ref, kc_ref, vc_ref, bias_ref, o_ref):
    b = pl.program_id(1)
    start = pl.multiple_of(kr0_ref[b] * GRID_W, GRID_W)
    nk = NA_KROWS * GRID_W
    scale = HEAD_DIM ** -0.5
    q = q_ref[...]
    kw = k_ref[pl.ds(start, nk), :]
    vw = v_ref[pl.ds(start, nk), :]
    kc = kc_ref[...]
    vc = vc_ref[...]
    outs = []
    for h in range(LANES // HEAD_DIM):
        sl = slice(h * HEAD_DIM, (h + 1) * HEAD_DIM)
        s_loc = _dot_nt(q[:, sl], kw[:, sl]) * scale + bias_ref[0, h]
        s_ctx = _dot_nt(q[:, sl], kc[:, sl]) * scale
        m = jnp.maximum(jnp.max(s_loc, axis=-1, keepdims=True), jnp.max(s_ctx, axis=-1, keepdims=True))
        p_loc = jnp.exp(s_loc - m)
        p_ctx = jnp.exp(s_ctx - m)
        den = jnp.sum(p_loc, axis=-1, keepdims=True) + jnp.sum(p_ctx, axis=-1, keepdims=True)
        o = _dot(p_loc.astype(BF16), vw[:, sl]) + _dot(p_ctx.astype(BF16), vc[:, sl])
        outs.append(o / den)
    o_ref[...] = jnp.concatenate(outs, axis=-1).astype(o_ref.dtype)


def _na_attention(q, k, v, kc, vc, rpb):
    s, bw = q.shape
    l = kc.shape[0]
    rows = s // GRID_W
    kr0s, ids, uniq = _na_patterns(rows)
    bias = _na_bias(rpb, uniq)
    tq = NA_ROWS * GRID_W
    nk = NA_KROWS * GRID_W
    hp = LANES // HEAD_DIM
    grid_spec = pltpu.PrefetchScalarGridSpec(
        num_scalar_prefetch=2,
        grid=(bw // LANES, rows // NA_ROWS),
        in_specs=[pl.BlockSpec((tq, LANES), lambda p, b, kr, pt: (b, p)),
                  pl.BlockSpec((s, LANES), lambda p, b, kr, pt: (0, p)),
                  pl.BlockSpec((s, LANES), lambda p, b, kr, pt: (0, p)),
                  pl.BlockSpec((l, LANES), lambda p, b, kr, pt: (0, p)),
                  pl.BlockSpec((l, LANES), lambda p, b, kr, pt: (0, p)),
                  pl.BlockSpec((1, hp, tq, nk), lambda p, b, kr, pt: (pt[b], p, 0, 0))],
        out_specs=pl.BlockSpec((tq, LANES), lambda p, b, kr, pt: (b, p)),
    )
    return pl.pallas_call(
        _na_kernel,
        grid_spec=grid_spec,
        out_shape=jax.ShapeDtypeStruct((s, bw), BF16),
        compiler_params=_cparams(("arbitrary", "arbitrary")),
        name="na_attention",
    )(jnp.asarray(kr0s), jnp.asarray(ids), q, k, v, kc, vc, bias)


def _ctx_attn_kernel(*refs, n_heads, n_rep, has_sink):
    if has_sink:
        sink_ref, q_ref, k_ref, v_ref, o_ref = refs
    else:
        q_ref, k_ref, v_ref, o_ref = refs
    scale = HEAD_DIM ** -0.5
    q = q_ref[...]
    k = k_ref[...]
    v = v_ref[...]
    outs = []
    for h in range(n_heads):
        g = h // n_rep
        sl = slice(h * HEAD_DIM, (h + 1) * HEAD_DIM)
        gl = slice(g * HEAD_DIM, (g + 1) * HEAD_DIM)
        s = _dot_nt(q[:, sl], k[:, gl]) * scale
        m = jnp.max(s, axis=-1, keepdims=True)
        if has_sink:
            m = jnp.maximum(m, sink_ref[h])
        p = jnp.exp(s - m)
        den = jnp.sum(p, axis=-1, keepdims=True)
        if has_sink:
            den = den + jnp.exp(sink_ref[h] - m)
        outs.append(_dot(p.astype(BF16), v[:, gl]) / den)
    o_ref[...] = jnp.concatenate(outs, axis=-1).astype(o_ref.dtype)


def _ctx_attention(q, k, v, sink):
    l, bw = q.shape
    n_heads = bw // HEAD_DIM
    n_rep = n_heads // (k.shape[1] // HEAD_DIM)
    has_sink = sink is not None
    full = lambda a: pl.BlockSpec(a.shape, lambda i: (0, 0))
    in_specs = [full(q), full(k), full(v)]
    args = [q, k, v]
    if has_sink:
        in_specs = [pl.BlockSpec(memory_space=pltpu.SMEM)] + in_specs
        args = [sink] + args
    return pl.pallas_call(
        functools.partial(_ctx_attn_kernel, n_heads=n_heads, n_rep=n_rep, has_sink=has_sink),
        grid=(1,),
        in_specs=in_specs,
        out_specs=pl.BlockSpec((l, bw), lambda i: (0, 0)),
        out_shape=jax.ShapeDtypeStruct((l, bw), BF16),
        compiler_params=_cparams(("arbitrary",)),
        name="ctx_attention",
    )(*args)


def _swa_kernel(sink_ref, q_ref, k_ref, v_ref, kc_ref, vc_ref, o_ref, *, seq, n_kv, n_rep):
    n = pl.program_id(0)
    nk = 3 * SW_BLOCK
    scale = HEAD_DIM ** -0.5
    ws = jnp.clip((n - 1) * SW_BLOCK, 0, seq - nk)
    ws = pl.multiple_of(ws, SW_BLOCK)
    q = q_ref[...]
    kw = k_ref[pl.ds(ws, nk), :]
    vw = v_ref[pl.ds(ws, nk), :]
    kc = kc_ref[...]
    vc = vc_ref[...]
    qpos = n * SW_BLOCK + lax.broadcasted_iota(jnp.int32, (SW_BLOCK, nk), 0)
    kpos = ws + lax.broadcasted_iota(jnp.int32, (SW_BLOCK, nk), 1)
    maskb = jnp.where(jnp.abs(kpos - qpos) <= SW_WINDOW, 0.0, NEG_INF).astype(F32)
    outs = [None] * (n_kv * n_rep)
    for g in range(n_kv):
        gl = slice(g * HEAD_DIM, (g + 1) * HEAD_DIM)
        qg = jnp.concatenate([q[:, (g * n_rep + r) * HEAD_DIM:(g * n_rep + r + 1) * HEAD_DIM]
                              for r in range(n_rep)], axis=0)
        s_loc = _dot_nt(qg, kw[:, gl]) * scale
        s_loc = (s_loc.reshape(n_rep, SW_BLOCK, nk) + maskb[None]).reshape(n_rep * SW_BLOCK, nk)
        s_ctx = _dot_nt(qg, kc[:, gl]) * scale
        sink = jnp.concatenate([jnp.full((SW_BLOCK, 1), sink_ref[g * n_rep + r], F32)
                                for r in range(n_rep)], axis=0)
        m = jnp.maximum(jnp.maximum(jnp.max(s_loc, axis=-1, keepdims=True),
                                    jnp.max(s_ctx, axis=-1, keepdims=True)), sink)
        p_loc = jnp.exp(s_loc - m)
        p_ctx = jnp.exp(s_ctx - m)
        den = (jnp.sum(p_loc, axis=-1, keepdims=True) + jnp.sum(p_ctx, axis=-1, keepdims=True)
               + jnp.exp(sink - m))
        o = (_dot(p_loc.astype(BF16), vw[:, gl]) + _dot(p_ctx.astype(BF16), vc[:, gl])) / den
        for r in range(n_rep):
            outs[g * n_rep + r] = o[r * SW_BLOCK:(r + 1) * SW_BLOCK]
    o_ref[...] = jnp.concatenate(outs, axis=-1).astype(o_ref.dtype)


def _sw_attention(q, k, v, kc, vc, sink):
    s, bw = q.shape
    l = kc.shape[0]
    kvw = k.shape[1]
    n_kv = kvw // HEAD_DIM
    n_rep = (bw // HEAD_DIM) // n_kv
    return pl.pallas_call(
        functools.partial(_swa_kernel, seq=s, n_kv=n_kv, n_rep=n_rep),
        grid=(s // SW_BLOCK,),
        in_specs=[pl.BlockSpec(memory_space=pltpu.SMEM),
                  pl.BlockSpec((SW_BLOCK, bw), lambda n: (n, 0)),
                  pl.BlockSpec((s, kvw), lambda n: (0, 0)),
                  pl.BlockSpec((s, kvw), lambda n: (0, 0)),
                  pl.BlockSpec((l, kvw), lambda n: (0, 0)),
                  pl.BlockSpec((l, kvw), lambda n: (0, 0))],
        out_specs=pl.BlockSpec((SW_BLOCK, bw), lambda n: (n, 0)),
        out_shape=jax.ShapeDtypeStruct((s, bw), BF16),
        compiler_params=_cparams(("arbitrary",)),
        name="sw_attention",
    )(sink, q, k, v, kc, vc)


def _s5_prepare(lam_re, lam_im, b_re, b_im, c_re, c_im, log_step):
    hp = lax.Precision.HIGHEST
    tc = S5_CHUNK
    lam = lax.complex(lam_re.astype(F32), lam_im.astype(F32))
    dt = jnp.exp(log_step.astype(F32))[..., None]
    lam_dt = lam * dt
    lam_bar = jnp.exp(lam_dt)
    b_bar = ((lam_bar - 1.0) / lam)[..., None] * lax.complex(b_re.astype(F32), b_im.astype(F32))
    c_mat = lax.complex(c_re.astype(F32), c_im.astype(F32))
    kk = jnp.arange(tc + 1, dtype=F32)
    pw = jnp.exp(lam_dt[None] * kk[:, None, None, None])
    n_g, n_p, n_h = b_bar.shape[1:]

    cw = c_mat[None] * pw[:tc, :, :, None, :]
    kern = (jnp.einsum('kdghp,dgpj->kdghj', cw.real, b_bar.real, precision=hp)
            - jnp.einsum('kdghp,dgpj->kdghj', cw.imag, b_bar.imag, precision=hp))
    t_i = np.arange(tc)
    lag_f = t_i[None, :] - t_i[:, None]
    m_f = jnp.where((lag_f >= 0)[:, :, None, None, None],
                    kern[np.clip(lag_f, 0, tc - 1), 0], 0.0)
    m_r = jnp.where((lag_f <= 0)[:, :, None, None, None],
                    kern[np.clip(-lag_f, 0, tc - 1), 1], 0.0)
    m_intra = jnp.transpose(m_f + m_r, (2, 0, 4, 1, 3)).reshape(n_g, tc * n_h, tc * n_h)

    def cat_ri(z):
        return jnp.concatenate([z.real, z.imag], axis=-1)

    w_f = pw[tc - 1 - t_i, 0][:, :, None, :] * jnp.transpose(b_bar[0], (0, 2, 1))[None]
    w_r = pw[t_i, 1][:, :, None, :] * jnp.transpose(b_bar[1], (0, 2, 1))[None]
    w_f = jnp.transpose(cat_ri(w_f), (1, 0, 2, 3)).reshape(n_g, tc * n_h, 2 * n_p)
    w_r = jnp.transpose(cat_ri(w_r), (1, 0, 2, 3)).reshape(n_g, tc * n_h, 2 * n_p)
    m_cat = jnp.concatenate([m_intra, w_f, w_r], axis=-1).astype(BF16)

    def v_of(c_dir, pw_sel):
        cp = c_dir[None] * pw_sel[:, :, None, :]
        v = jnp.concatenate([cp.real, -cp.imag], axis=-1)
        return jnp.transpose(v, (1, 3, 0, 2)).reshape(n_g, 2 * n_p, tc * n_h)
    v_cat = jnp.concatenate([v_of(c_mat[0], pw[t_i + 1, 0]), v_of(c_mat[1], pw[tc - t_i, 1])],
                            axis=1).astype(BF16)

    def coef(k):
        a = jnp.exp(lam_dt * (tc * k))
        return jnp.stack([jnp.concatenate([a.real, a.real], -1),
                          jnp.concatenate([-a.imag, a.imag], -1)], axis=2)
    a_log = jnp.stack([coef(1.0), coef(2.0), coef(4.0)], axis=2)
    a_car = jnp.stack([coef(float(j)) for j in range(1, SUBLANES + 1)], axis=3)
    a_car = jnp.stack([a_car[0], a_car[1][:, :, ::-1]], axis=0)
    return m_cat, v_cat, jnp.transpose(a_log, (1, 0, 2, 3, 4)), jnp.transpose(a_car, (1, 0, 2, 3, 4))


def _s5_kernel(u_ref, d_ref, m_ref, v_ref, alog_ref, acar_ref, y_ref, z_scr, e_scr, *, nc, nc_ctx):
    n_state = 2 * S5_STATE
    width = S5_CHUNK * S5_GROUP
    nblk = nc // SUBLANES
    nblk_ctx = nc_ctx // SUBLANES
    row = lax.broadcasted_iota(jnp.int32, (SUBLANES, n_state), 0)

    for g in range(S5_GB):
        zz = _dot(u_ref[g].astype(BF16), m_ref[g])
        y_ref[g] = zz[:, :width] + u_ref[g] * d_ref[g]
        z_scr[g, 0] = zz[:, width:width + n_state]
        z_scr[g, 1] = zz[:, width + n_state:]

    def cmul(a1, a2, s):
        return a1 * s + a2 * pltpu.roll(s, S5_STATE, 1)

    def step(t, carry):
        blk = (t, jnp.where(t < nblk_ctx, nblk_ctx - 1 - t, nblk + nblk_ctx - 1 - t))
        new = []
        for g in range(S5_GB):
            for d in range(2):
                r0 = pl.multiple_of(blk[d] * SUBLANES, SUBLANES)
                z = z_scr[g, d, pl.ds(r0, SUBLANES), :]
                c = carry[g * 2 + d]
                for i, sh in enumerate((1, 2, 4)):
                    if d == 0:
                        zs = jnp.where(row >= sh, pltpu.roll(z, sh, 0), 0.0)
                    else:
                        zs = jnp.where(row < SUBLANES - sh, pltpu.roll(z, SUBLANES - sh, 0), 0.0)
                    z = z + cmul(alog_ref[g, d, i, 0:1, :], alog_ref[g, d, i, 1:2, :], zs)
                cb = jnp.broadcast_to(c, (SUBLANES, n_state))
                s = z + cmul(acar_ref[g, d, 0], acar_ref[g, d, 1], cb)
                if d == 0:
                    e = jnp.where(row >= 1, pltpu.roll(s, 1, 0), cb)
                    c_new = s[SUBLANES - 1:SUBLANES, :]
                else:
                    e = jnp.where(row < SUBLANES - 1, pltpu.roll(s, SUBLANES - 1, 0), cb)
                    c_new = s[0:1, :]
                e_scr[g, pl.ds(r0, SUBLANES), pl.ds(d * n_state, n_state)] = e
                new.append(c_new)
        return tuple(new)

    init = tuple(jnp.zeros((1, n_state), F32) for _ in range(2 * S5_GB))
    lax.fori_loop(0, nblk, step, init)

    for g in range(S5_GB):
        y_ref[g] = y_ref[g] + _dot(e_scr[g].astype(BF16), v_ref[g])


def _s5_scan(u_chunks, d_rows, m_cat, v_cat, a_log, a_car, nc_ctx):
    n_g, nc, width = u_chunks.shape
    n_state = 2 * S5_STATE
    blk = lambda *shape: pl.BlockSpec((S5_GB,) + shape, lambda i: (i,) + (0,) * len(shape))
    return pl.pallas_call(
        functools.partial(_s5_kernel, nc=nc, nc_ctx=nc_ctx),
        grid=(n_g // S5_GB,),
        in_specs=[blk(nc, width), blk(1, width), blk(width, width + 2 * n_state),
                  blk(2 * n_state, width), blk(2, 3, 2, n_state), blk(2, 2, SUBLANES, n_state)],
        out_specs=blk(nc, width),
        out_shape=jax.ShapeDtypeStruct((n_g, nc, width), F32),
        scratch_shapes=[pltpu.VMEM((S5_GB, 2, nc, n_state), F32),
                        pltpu.VMEM((S5_GB, nc, 2 * n_state), F32)],
        compiler_params=_cparams(("arbitrary",)),
        name="s5_scan",
    )(u_chunks, d_rows, m_cat, v_cat, a_log, a_car)


def _s5_mixer_pre_glu(u_ctx, u_lat, prep, d_skip):
    m_cat, v_cat, a_log, a_car = prep
    l, w = u_ctx.shape
    s = u_lat.shape[0]
    n_g = w // S5_GROUP
    u = jnp.concatenate([u_ctx, u_lat], axis=0)
    nc = (l + s) // S5_CHUNK
    u_chunks = jnp.transpose(u.reshape(nc, S5_CHUNK, n_g, S5_GROUP), (2, 0, 1, 3)).reshape(n_g, nc, -1)
    d_rows = jnp.tile(d_skip.astype(F32).reshape(n_g, 1, S5_GROUP), (1, 1, S5_CHUNK))
    y = _s5_scan(u_chunks, d_rows, m_cat, v_cat, a_log, a_car, l // S5_CHUNK)
    y = jnp.transpose(y.reshape(n_g, nc, S5_CHUNK, S5_GROUP), (1, 2, 0, 3)).reshape(l + s, w)
    return y[:l], y[l:]


def _merge_kernel(x_ref, ya_ref, y5_ref, ys_ref, ga_ref, gb_ref, gs_ref, wglu_ref, wb_ref, wo_ref,
                  gt_ref, g_ref, sh_ref, sc_ref, xo_ref, h_ref, ht_ref):
    z = _gelu(y5_ref[...])
    yb = z * jax.nn.sigmoid(_dot(z.astype(BF16), wglu_ref[...]))
    m = (ga_ref[...].astype(F32) * _dot(ya_ref[...], wb_ref[0])
         + gb_ref[...].astype(F32) * _dot(yb.astype(BF16), wb_ref[1])
         + gs_ref[...].astype(F32) * _dot(ys_ref[...], wb_ref[2]))
    x = x_ref[...] + gt_ref[...] * _dot(m.astype(BF16), wo_ref[...])
    xo_ref[...] = x
    h = _norm_mod(x, g_ref[...], sh_ref[...], sc_ref[...])
    h_ref[...] = h.astype(BF16)
    ht_ref[...] = h.T.astype(BF16)


def _merge(x, ya, y5, ys, ga, gb, gs, w_glu, w_branch, w_out, gate, g, shift, scale):
    n, d = x.shape
    bw = ya.shape[1]
    tm = min(ROW_BLOCK, n)
    row = lambda w: pl.BlockSpec((tm, w), lambda i: (i, 0))
    full = lambda a: pl.BlockSpec(a.shape, lambda i: (0,) * a.ndim)
    vec = pl.BlockSpec((1, d), lambda i: (0, 0))
    return pl.pallas_call(
        _merge_kernel,
        grid=(n // tm,),
        in_specs=[row(d), row(bw), row(bw), row(bw), row(d), row(d), row(d),
                  full(w_glu), full(w_branch), full(w_out), vec, vec, vec, vec],
        out_specs=[row(d), row(d), pl.BlockSpec((d, tm), lambda i: (0, i))],
        out_shape=[jax.ShapeDtypeStruct((n, d), F32), jax.ShapeDtypeStruct((n, d), BF16),
                   jax.ShapeDtypeStruct((d, n), BF16)],
        compiler_params=_cparams(("arbitrary",)),
        name="merge",
    )(x, ya, y5, ys, ga, gb, gs, w_glu, w_branch, w_out, gate, g, shift, scale)


def _top16(s, iota):
    rank = jnp.full(s.shape, float(PEER_TOPK), F32)
    vals = []
    n_rows = float(s.shape[0])
    for r in range(PEER_TOPK):
        m = jnp.max(s, axis=0, keepdims=True)
        idx = jnp.min(jnp.where(s == m, iota, n_rows), axis=0, keepdims=True)
        hit = iota == idx
        rank = jnp.where(hit, float(r), rank)
        s = jnp.where(hit, -jnp.inf, s)
        vals.append(m)
    return rank, jnp.concatenate(vals, axis=0)


def _bf16_pair_word(x):
    hi = pltpu.bitcast(x.astype(BF16).astype(F32), jnp.uint32)
    return hi | (hi >> 16)


def _pair_tiles():
    tiles = [(0, 0, 8), (0, 8, 8)]
    for a in range(1, 8):
        tiles.append((a, 0, PEER_TOPK // (a + 1)))
    return tiles


def _route_kernel(h_ref, wq_ref, k1_ref, k2_ref, cnt_ref, c1_ref, rk_ref, e2_ref, q_scr):
    tb = h_ref.shape[0]
    q_scr[...] = _dot(h_ref[...], wq_ref[...]).astype(BF16)
    half = PEER_QDIM // 2
    iota = lax.broadcasted_iota(jnp.int32, (PEER_NKEYS, tb), 0).astype(F32)
    row8 = lax.broadcasted_iota(jnp.int32, (SUBLANES, tb), 0).astype(F32)
    k1 = k1_ref[...]
    k2 = k2_ref[...]

    def head(hd, _):
        c0 = pl.multiple_of(hd * PEER_QDIM, PEER_QDIM)
        s1 = _dot_nt(k1, q_scr[:, pl.ds(c0, half)])
        s2 = _dot_nt(k2, q_scr[:, pl.ds(c0 + half, half)])
        rank1, v1 = _top16(s1, iota)
        rank2, v2 = _top16(s2, iota)
        e1v = jnp.exp(v1 - v1[0:1])
        e2v = jnp.exp(v2 - v2[0:1])
        tiles, flats, gates = [], [], []
        for a, b0, nv in _pair_tiles():
            c = v1[a:a + 1] + v2[b0:b0 + SUBLANES]
            tiles.append(jnp.where(row8 < nv, c, -jnp.inf))
            flats.append(a * PEER_TOPK + b0 + row8)
            gates.append(e1v[a:a + 1] * e2v[b0:b0 + SUBLANES])
        tiles.append(v1[SUBLANES:] + v2[0:1])
        flats.append((row8 + SUBLANES) * PEER_TOPK)
        gates.append(e1v[SUBLANES:] * e2v[0:1])
        cand = jnp.concatenate(tiles, axis=0)
        flat = jnp.concatenate(flats, axis=0)
        gate = jnp.concatenate(gates, axis=0)
        self_ = jnp.zeros(cand.shape, F32)
        for _ in range(PEER_TOPK):
            m = jnp.max(cand, axis=0, keepdims=True)
            idx = jnp.min(jnp.where(cand == m, flat, float(PEER_TOPK * PEER_TOPK)), axis=0, keepdims=True)
            hit = flat == idx
            self_ = jnp.where(hit, 1.0, self_)
            cand = jnp.where(hit, -jnp.inf, cand)
        z = jnp.sum(self_ * gate, axis=0, keepdims=True)
        cnt = [self_[0:8].sum(axis=0, keepdims=True) + self_[8:16].sum(axis=0, keepdims=True)]
        for t in range(2, 9):
            cnt.append(self_[t * SUBLANES:(t + 1) * SUBLANES].sum(axis=0, keepdims=True))
        cnt = jnp.concatenate(cnt + [self_[9 * SUBLANES:]], axis=0)
        cnt1 = jnp.zeros((PEER_NKEYS, tb), F32)
        for a in range(PEER_TOPK):
            cnt1 = jnp.where(rank1 == float(a), cnt[a:a + 1], cnt1)
        cnt_ref[hd] = _bf16_pair_word(cnt1)
        c1_ref[hd] = _bf16_pair_word(jnp.exp(s1 - v1[0:1]) * (0.5 / z))
        rk_ref[hd] = rank2.astype(BF16)
        e2_ref[hd] = jnp.exp(s2 - v2[0:1]).astype(BF16)
        return 0

    lax.fori_loop(0, PEER_HEADS, head, 0)


def _peer_route(h, w_q, k1, k2):
    n, d = h.shape
    tb = min(PEER_ROUTE_BLOCK, n)
    qw = w_q.shape[1]
    out_blk = pl.BlockSpec((PEER_HEADS, PEER_NKEYS, tb), lambda i: (0, 0, i))
    shp = lambda dt: jax.ShapeDtypeStruct((PEER_HEADS, PEER_NKEYS, n), dt)
    return pl.pallas_call(
        _route_kernel,
        grid=(n // tb,),
        in_specs=[pl.BlockSpec((tb, d), lambda i: (i, 0)),
                  pl.BlockSpec((d, qw), lambda i: (0, 0)),
                  pl.BlockSpec(k1.shape, lambda i: (0, 0)),
                  pl.BlockSpec(k2.shape, lambda i: (0, 0))],
        out_specs=[out_blk] * 4,
        out_shape=[shp(jnp.uint32), shp(jnp.uint32), shp(BF16), shp(BF16)],
        scratch_shapes=[pltpu.VMEM((tb, qw), BF16)],
        compiler_params=_cparams(("arbitrary",)),
        name="peer_route",
    )(h, w_q, k1, k2)


def _dense_kernel(ht_ref, x_ref, gt_ref, u_ref, vt_ref, cnt_ref, c1_ref, rk_ref, e2_ref, o_ref,
                  acc_ref, w_scr):
    eb = pl.program_id(1)
    tb = ht_ref.shape[1]
    n_exp = u_ref.shape[0]
    n_sub = n_exp // PEER_SUB
    i_per_sub = PEER_SUB // PEER_NKEYS

    @pl.when(eb == 0)
    def _():
        acc_ref[...] = jnp.zeros_like(acc_ref)

    def scores(sb):
        return _dot(u_ref[sb * PEER_SUB:(sb + 1) * PEER_SUB, :], ht_ref[...])

    pk_rows = 2 * SUBLANES
    n_pk = PEER_NKEYS // pk_rows

    def row_tile(ref, hd, i):
        words = jnp.broadcast_to(ref[hd, pl.ds(i, 1), :], (SUBLANES, tb))
        return pltpu.bitcast(words, BF16)[None]

    a_next = scores(0)
    for sb in range(n_sub):
        a_cur = a_next
        if sb + 1 < n_sub:
            a_next = scores(sb + 1)
        for il in range(i_per_sub):
            i = eb * (n_exp // PEER_NKEYS) + sb * i_per_sub + il
            gsum = jnp.zeros((n_pk, pk_rows, tb), BF16)
            for hd in range(PEER_HEADS):
                cnt = row_tile(cnt_ref, hd, i)
                c1 = row_tile(c1_ref, hd, i)
                rk = rk_ref[hd].reshape(n_pk, pk_rows, tb)
                e2 = e2_ref[hd].reshape(n_pk, pk_rows, tb)
                gsum = gsum + jnp.where(rk < cnt, e2 * c1, jnp.zeros((), BF16))
            r0 = sb * PEER_SUB + il * PEER_NKEYS
            a_i = a_cur[il * PEER_NKEYS:(il + 1) * PEER_NKEYS]
            half_gate = gsum.reshape(PEER_NKEYS, tb).astype(F32)
            w_i = a_i * (1.0 + lax.erf(a_i * (1.0 / math.sqrt(2.0)))) * half_gate
            w_scr[r0:r0 + PEER_NKEYS, :] = w_i.astype(BF16)
        done = (sb + 1) * PEER_SUB
        if done % PEER_ACC_CHUNK == 0:
            c0 = done - PEER_ACC_CHUNK
            acc_ref[...] += _dot(vt_ref[:, c0:done], w_scr[c0:done, :])

    @pl.when(eb == pl.num_programs(1) - 1)
    def _():
        o_ref[...] = x_ref[...] + gt_ref[...] * acc_ref[...].T


def _peer_dense(h_t, x, gate, u, v_t, cnt1, c1, rank2, e2):
    d, n = h_t.shape
    n_e = u.shape[0]
    tb = min(PEER_TOK_BLOCK, n)
    eb = PEER_EXP_BLOCK
    tab = pl.BlockSpec((PEER_HEADS, PEER_NKEYS, tb), lambda i, e: (0, 0, i))
    return pl.pallas_call(
        _dense_kernel,
        grid=(n // tb, n_e // eb),
        in_specs=[pl.BlockSpec((d, tb), lambda i, e: (0, i)),
                  pl.BlockSpec((tb, d), lambda i, e: (i, 0)),
                  pl.BlockSpec((1, d), lambda i, e: (0, 0)),
                  pl.BlockSpec((eb, d), lambda i, e: (e, 0)),
                  pl.BlockSpec((d, eb), lambda i, e: (0, e)),
                  tab, tab, tab, tab],
        out_specs=pl.BlockSpec((tb, d), lambda i, e: (i, 0)),
        out_shape=jax.ShapeDtypeStruct((n, d), F32),
        scratch_shapes=[pltpu.VMEM((d, tb), F32), pltpu.VMEM((eb, tb), BF16)],
        compiler_params=_cparams(("arbitrary", "arbitrary")),
        name="peer_dense",
    )(h_t, x, gate, u, v_t, cnt1, c1, rank2, e2)


def _final_norm_kernel(x_ref, g_ref, o_ref):
    x = x_ref[...]
    o_ref[...] = x * lax.rsqrt(jnp.mean(x * x, axis=-1, keepdims=True) + EPS) * g_ref[...]


def _final_norm(x, g):
    n, d = x.shape
    tm = min(2 * ROW_BLOCK, n)
    return pl.pallas_call(
        _final_norm_kernel,
        grid=(n // tm,),
        in_specs=[pl.BlockSpec((tm, d), lambda i: (i, 0)), pl.BlockSpec((1, d), lambda i: (0, 0))],
        out_specs=pl.BlockSpec((tm, d), lambda i: (i, 0)),
        out_shape=jax.ShapeDtypeStruct((n, d), F32),
        compiler_params=_cparams(("arbitrary",)),
        name="final_norm",
    )(x, g)


def kernel(x, c, ctx, c_ctx, w_mod, b_mod, g_mix, g_ffn, w_in, na_rpb, s5_lam_re, s5_lam_im,
           s5_b_re, s5_b_im, s5_c_re, s5_c_im, s5_log_step, s5_d, s5_w_glu, sw_sink, w_branch,
           w_out, peer_w_q, peer_sub_keys, peer_u, peer_v, g_final):
    batch, seq, d = x.shape
    l_ctx = ctx.shape[1]
    depth = w_mod.shape[0]
    assert batch == 1 and seq % (NA_ROWS * GRID_W) == 0 and l_ctx % (S5_CHUNK * SUBLANES) == 0
    assert seq // GRID_W >= NA_KROWS and seq >= 3 * SW_BLOCK

    cc = jnp.zeros((SUBLANES, d), F32).at[0].set(c[0]).at[1].set(c_ctx)
    mod = _mod_vectors(cc, w_mod, b_mod).reshape(depth, SUBLANES, 6, d)
    rope_tabs = _rope_tables(seq)
    row = lambda v: v.reshape(1, d)

    xx, xc = x[0], ctx[0]
    for l in range(depth):
        need_ctx = l < depth - 1
        m_lat, m_ctx = mod[l, 0], mod[l, 1]
        w_in_l = w_in[l].astype(BF16)
        g_mix_l = row(g_mix[l])
        g_ffn_l = row(g_ffn[l])

        qa, ka, va, ub, qs, ks, vs, ga, gb, gs = _inproj(
            xx, g_mix_l, row(m_lat[0]), row(m_lat[1]), w_in_l, rope_tabs)
        qa_c, ka_c, va_c, ub_c, qs_c, ks_c, vs_c, ga_c, gb_c, gs_c = _inproj(
            xc, g_mix_l, row(m_ctx[0]), row(m_ctx[1]), w_in_l, None)

        ya = _na_attention(qa, ka, va, ka_c, va_c, na_rpb[l])
        ys = _sw_attention(qs, ks, vs, ks_c, vs_c, sw_sink[l])
        prep = _s5_prepare(s5_lam_re[l], s5_lam_im[l], s5_b_re[l], s5_b_im[l], s5_c_re[l],
                           s5_c_im[l], s5_log_step[l])
        y5_c, y5 = _s5_mixer_pre_glu(ub_c, ub, prep, s5_d[l])

        w_glu_l = s5_w_glu[l].astype(BF16)
        w_branch_l = w_branch[l].astype(BF16)
        w_out_l = w_out[l].astype(BF16)
        xx, hx2, hx2_t = _merge(xx, ya, y5, ys, ga, gb, gs, w_glu_l, w_branch_l, w_out_l,
                                row(m_lat[2]), g_ffn_l, row(m_lat[3]), row(m_lat[4]))

        w_q_l = peer_w_q[l].astype(BF16)
        k1 = peer_sub_keys[l, 0].astype(BF16)
        k2 = peer_sub_keys[l, 1].astype(BF16)
        u_l = peer_u[l].astype(BF16)
        vt_l = peer_v[l].astype(BF16).T
        if need_ctx:
            ya_c = _ctx_attention(qa_c, ka_c, va_c, None)
            ys_c = _ctx_attention(qs_c, ks_c, vs_c, sw_sink[l])
            xc, hc2, hc2_t = _merge(xc, ya_c, y5_c, ys_c, ga_c, gb_c, gs_c, w_glu_l, w_branch_l, w_out_l,
                                    row(m_ctx[2]), g_ffn_l, row(m_ctx[3]), row(m_ctx[4]))
            xc = _peer_dense(hc2_t, xc, row(m_ctx[5]), u_l, vt_l, *_peer_route(hc2, w_q_l, k1, k2))
        xx = _peer_dense(hx2_t, xx, row(m_lat[5]), u_l, vt_l, *_peer_route(hx2, w_q_l, k1, k2))

    return _final_norm(xx, row(g_final))[None]
```

```python
import functools
import math

import numpy as np
import jax
import jax.numpy as jnp
from jax import lax
from jax.experimental import pallas as pl
from jax.experimental.pallas import tpu as pltpu

F32 = jnp.float32
BF16 = jnp.bfloat16

GRID_W = 64
HEAD_DIM = 64
NA_WIN_H = 8
NA_WIN_W = 16
S5_GROUP = 16
S5_STATE = 64
SW_WINDOW = 128
ROPE_BASE = 10000.0
PEER_HEADS = 8
PEER_NKEYS = 128
PEER_QDIM = 256
PEER_TOPK = 16
EPS = 1e-6
NEG_INF = -1e30

LANES = 128
SUBLANES = 8
VMEM_LIMIT_BYTES = 56 * 1024 * 1024

ROW_BLOCK = 256
NA_ROWS = 4
NA_KROWS = NA_ROWS + NA_WIN_H - 1
NA_LANES = 256
SW_BLOCK = 128
S5_CHUNK = 16
S5_GB = 4
PEER_ROUTE_BLOCK = 256
PEER_TOK_BLOCK = 512
PEER_EXP_BLOCK = 1024
PEER_SUB = 256
PEER_ACC_CHUNK = 512


def _cparams(sem):
    return pltpu.CompilerParams(dimension_semantics=sem, vmem_limit_bytes=VMEM_LIMIT_BYTES)


def _dot(a, b):
    return jnp.dot(a, b, preferred_element_type=F32)


def _dot_nt(a, b):
    return lax.dot_general(a, b, (((1,), (1,)), ((), ())), preferred_element_type=F32)


def _gelu(x):
    return 0.5 * x * (1.0 + lax.erf(x * (1.0 / math.sqrt(2.0))))


def _mod_kernel(cc_ref, w_ref, b_ref, o_ref):
    a = cc_ref[...]
    a = a * jax.nn.sigmoid(a)
    o_ref[0] = _dot(a.astype(BF16), w_ref[0].astype(BF16)) + b_ref[0]


def _mod_vectors(cc, w_mod, b_mod):
    depth, d, n6 = w_mod.shape
    tn = 1024
    return pl.pallas_call(
        _mod_kernel,
        grid=(depth, n6 // tn),
        in_specs=[pl.BlockSpec((SUBLANES, d), lambda l, j: (0, 0)),
                  pl.BlockSpec((1, d, tn), lambda l, j: (l, 0, j)),
                  pl.BlockSpec((1, 1, tn), lambda l, j: (l, 0, j))],
        out_specs=pl.BlockSpec((1, SUBLANES, tn), lambda l, j: (l, 0, j)),
        out_shape=jax.ShapeDtypeStruct((depth, SUBLANES, n6), F32),
        compiler_params=_cparams(("arbitrary", "arbitrary")),
        name="mod_vectors",
    )(cc, w_mod, b_mod.reshape(depth, 1, n6))


def _norm_mod(x, g, shift, scale):
    y = x * lax.rsqrt(jnp.mean(x * x, axis=-1, keepdims=True) + EPS)
    return (y * g) * (1.0 + scale) + shift


def _rope(x, cos, sin, lane_lo):
    up = pltpu.roll(x, LANES - 16, 1)
    dn = pltpu.roll(x, 16, 1)
    return x * cos + jnp.where(lane_lo, up, dn) * sin


def _inproj_kernel(x_ref, g_ref, sh_ref, sc_ref, w_ref, *rest, bw, kvw, d_model, rope):
    if rope:
        cos_ref, sin_ref = rest[:2]
        rest = rest[2:]
    qa_ref, ka_ref, va_ref, ub_ref, qs_ref, ks_ref, vs_ref, ga_ref, gb_ref, gs_ref = rest
    h = _norm_mod(x_ref[...], g_ref[...], sh_ref[...], sc_ref[...]).astype(BF16)

    def proj(c0, width):
        return _dot(h, w_ref[:, c0:c0 + width])

    c = 0
    qa_ref[...] = proj(c, bw).astype(BF16); c += bw
    ka_ref[...] = proj(c, bw).astype(BF16); c += bw
    va_ref[...] = proj(c, bw).astype(BF16); c += bw
    ub_ref[...] = proj(c, bw); c += bw
    qs = proj(c, bw); c += bw
    ks = proj(c, kvw); c += kvw
    vs_ref[...] = proj(c, kvw).astype(BF16); c += kvw
    if rope:
        cos = cos_ref[...]
        sin = sin_ref[...]
        lane = lax.broadcasted_iota(jnp.int32, cos.shape, 1)
        lane_lo = (lane % 32) < 16
        for p in range(bw // LANES):
            sl = slice(p * LANES, (p + 1) * LANES)
            qs_ref[:, sl] = _rope(qs[:, sl], cos, sin, lane_lo).astype(BF16)
        for p in range(kvw // LANES):
            sl = slice(p * LANES, (p + 1) * LANES)
            ks_ref[:, sl] = _rope(ks[:, sl], cos, sin, lane_lo).astype(BF16)
    else:
        qs_ref[...] = qs.astype(BF16)
        ks_ref[...] = ks.astype(BF16)
    ga_ref[...] = jax.nn.sigmoid(proj(c, d_model)).astype(BF16); c += d_model
    gb_ref[...] = jax.nn.sigmoid(proj(c, d_model)).astype(BF16); c += d_model
    gs_ref[...] = jax.nn.sigmoid(proj(c, d_model)).astype(BF16)


def _inproj(x, g, shift, scale, w_in, rope_tabs):
    n, d = x.shape
    bw = d // 2
    kvw = bw // 4
    tm = min(ROW_BLOCK, n)
    rope = rope_tabs is not None
    row = lambda i: (i, 0)
    fixed = lambda i: (0, 0)
    in_specs = [pl.BlockSpec((tm, d), row), pl.BlockSpec((1, d), fixed), pl.BlockSpec((1, d), fixed),
                pl.BlockSpec((1, d), fixed), pl.BlockSpec(w_in.shape, fixed)]
    args = [x, g, shift, scale, w_in]
    if rope:
        in_specs += [pl.BlockSpec((tm, LANES), row)] * 2
        args += list(rope_tabs)
    widths = [bw, bw, bw, bw, bw, kvw, kvw, d, d, d]
    dtypes = [BF16, BF16, BF16, F32, BF16, BF16, BF16, BF16, BF16, BF16]
    return pl.pallas_call(
        functools.partial(_inproj_kernel, bw=bw, kvw=kvw, d_model=d, rope=rope),
        grid=(n // tm,),
        in_specs=in_specs,
        out_specs=[pl.BlockSpec((tm, w), row) for w in widths],
        out_shape=[jax.ShapeDtypeStruct((n, w), dt) for w, dt in zip(widths, dtypes)],
        compiler_params=_cparams(("arbitrary",)),
        name="inproj",
    )(*args)


def _rope_tables(seq):
    t = jnp.arange(seq)
    half = HEAD_DIM // 4
    inv = ROPE_BASE ** (-jnp.arange(half, dtype=F32) / half)
    sign = jnp.concatenate([-jnp.ones((half,), F32), jnp.ones((half,), F32)])

    def axis_tabs(pos):
        ang = pos.astype(F32)[:, None] * inv[None, :]
        c = jnp.cos(ang)
        s = jnp.sin(ang)
        return jnp.concatenate([c, c], -1), jnp.concatenate([s, s], -1) * sign

    cr, sr = axis_tabs(t // GRID_W)
    cc, sc = axis_tabs(t % GRID_W)
    cos = jnp.concatenate([cr, cc], -1)
    sin = jnp.concatenate([sr, sc], -1)
    return jnp.tile(cos, (1, LANES // HEAD_DIM)), jnp.tile(sin, (1, LANES // HEAD_DIM))


def _na_patterns(rows):
    kh = NA_WIN_H
    nb = rows // NA_ROWS
    kr0s, sigs = [], []
    for b in range(nb):
        r_lo = b * NA_ROWS
        kr0 = int(np.clip(r_lo - kh // 2, 0, rows - NA_KROWS))
        sig = tuple((int(np.clip(r - kh // 2, 0, rows - kh)) - kr0, r - kr0)
                    for r in range(r_lo, r_lo + NA_ROWS))
        kr0s.append(kr0)
        sigs.append(sig)
    uniq = sorted(set(sigs))
    ids = np.array([uniq.index(s) for s in sigs], np.int32)
    return np.array(kr0s, np.int32), ids, uniq


def _na_bias(rpb, uniq, n_ctx):
    kw = NA_WIN_W
    cols = np.arange(GRID_W)
    col_start = np.clip(cols - kw // 2, 0, GRID_W - kw)
    n_pat = len(uniq)
    kk = np.arange(NA_KROWS)
    col_sel = (cols[None, None, :] - cols[None, :, None] + kw - 1
               == np.arange(2 * kw - 1)[:, None, None])
    col_ok = ((cols[None, :] >= col_start[:, None]) & (cols[None, :] < col_start[:, None] + kw))
    row_sel = np.zeros((n_pat, NA_ROWS, NA_KROWS, 2 * NA_WIN_H - 1), bool)
    row_ok = np.zeros((n_pat, NA_ROWS, NA_KROWS), bool)
    for p, sig in enumerate(uniq):
        for rq, (r0_rel, r_rel) in enumerate(sig):
            row_ok[p, rq] = (kk >= r0_rel) & (kk < r0_rel + NA_WIN_H)
            row_sel[p, rq] = (kk[:, None] - r_rel + NA_WIN_H - 1) == np.arange(2 * NA_WIN_H - 1)[None, :]
    hp = lax.Precision.HIGHEST
    by_col = jnp.einsum('hrv,vcj->hrcj', rpb.astype(F32), jnp.asarray(col_sel, F32), precision=hp)
    bias = jnp.einsum('pqkr,hrcj->phqckj', jnp.asarray(row_sel, F32), by_col, precision=hp)
    ok = row_ok[:, None, :, None, :, None] & col_ok[None, None, None, :, None, :]
    bias = jnp.where(jnp.asarray(ok), bias, NEG_INF)
    bias = bias.reshape(n_pat, rpb.shape[0], NA_ROWS * GRID_W, NA_KROWS * GRID_W)
    return jnp.concatenate([bias, jnp.zeros(bias.shape[:3] + (n_ctx,), F32)], axis=-1)


def _na_kernel(kr0_ref, pat_ref, q_ref, k_ref, v_ref, kc_ref, vc_ref, bias_ref, o_ref):
    b = pl.program_id(1)
    start = pl.multiple_of(kr0_ref[b] * GRID_W, GRID_W)
    nk = NA_KROWS * GRID_W
    scale = HEAD_DIM ** -0.5
    one = jnp.ones((), BF16)
    hp = LANES // HEAD_DIM
    for lt in range(q_ref.shape[1] // LANES):
        lanes = slice(lt * LANES, (lt + 1) * LANES)
        q = q_ref[:, lanes]
        k_all = jnp.concatenate([k_ref[pl.ds(start, nk), lanes], kc_ref[:, lanes]], axis=0)
        v_all = jnp.concatenate([v_ref[pl.ds(start, nk), lanes], vc_ref[:, lanes]], axis=0)
        lane = lax.broadcasted_iota(jnp.int32, v_all.shape, 1)
        ext = []
        for h in range(hp):
            sl = slice(h * HEAD_DIM, (h + 1) * HEAD_DIM)
            own = (lane >= h * HEAD_DIM) & (lane < (h + 1) * HEAD_DIM)
            s = _dot_nt(q[:, sl], k_all[:, sl]) * scale + bias_ref[0, lt * hp + h]
            p = jnp.exp(s - jnp.max(s, axis=-1, keepdims=True)).astype(BF16)
            ext.append(_dot(p, jnp.where(own, v_all, one)))
        out_lane = lax.broadcasted_iota(jnp.int32, ext[0].shape, 1)
        low = out_lane < HEAD_DIM
        num = jnp.where(low, ext[0], ext[1])
        den = pltpu.roll(jnp.where(low, ext[1], ext[0]), HEAD_DIM, 1)
        o_ref[:, lanes] = (num / den).astype(o_ref.dtype)


def _na_attention(q, k, v, kc, vc, rpb):
    s, bw = q.shape
    l = kc.shape[0]
    rows = s // GRID_W
    kr0s, ids, uniq = _na_patterns(rows)
    bias = _na_bias(rpb, uniq, l)
    tq = NA_ROWS * GRID_W
    nk = NA_KROWS * GRID_W
    hp = NA_LANES // HEAD_DIM
    grid_spec = pltpu.PrefetchScalarGridSpec(
        num_scalar_prefetch=2,
        grid=(bw // NA_LANES, rows // NA_ROWS),
        in_specs=[pl.BlockSpec((tq, NA_LANES), lambda p, b, kr, pt: (b, p)),
                  pl.BlockSpec((s, NA_LANES), lambda p, b, kr, pt: (0, p)),
                  pl.BlockSpec((s, NA_LANES), lambda p, b, kr, pt: (0, p)),
                  pl.BlockSpec((l, NA_LANES), lambda p, b, kr, pt: (0, p)),
                  pl.BlockSpec((l, NA_LANES), lambda p, b, kr, pt: (0, p)),
                  pl.BlockSpec((1, hp, tq, nk + l), lambda p, b, kr, pt: (pt[b], p, 0, 0))],
        out_specs=pl.BlockSpec((tq, NA_LANES), lambda p, b, kr, pt: (b, p)),
    )
    return pl.pallas_call(
        _na_kernel,
        grid_spec=grid_spec,
        out_shape=jax.ShapeDtypeStruct((s, bw), BF16),
        compiler_params=_cparams(("arbitrary", "arbitrary")),
        name="na_attention",
    )(jnp.asarray(kr0s), jnp.asarray(ids), q, k, v, kc, vc, bias)


def _ctx_attn_kernel(*refs, n_heads, n_rep, has_sink):
    if has_sink:
        sink_ref, q_ref, k_ref, v_ref, o_ref = refs
    else:
        q_ref, k_ref, v_ref, o_ref = refs
    scale = HEAD_DIM ** -0.5
    q = q_ref[...]
    k = k_ref[...]
    v = v_ref[...]
    outs = []
    for h in range(n_heads):
        g = h // n_rep
        sl = slice(h * HEAD_DIM, (h + 1) * HEAD_DIM)
        gl = slice(g * HEAD_DIM, (g + 1) * HEAD_DIM)
        s = _dot_nt(q[:, sl], k[:, gl]) * scale
        m = jnp.max(s, axis=-1, keepdims=True)
        if has_sink:
            m = jnp.maximum(m, sink_ref[h])
        p = jnp.exp(s - m)
        den = jnp.sum(p, axis=-1, keepdims=True)
        if has_sink:
            den = den + jnp.exp(sink_ref[h] - m)
        outs.append(_dot(p.astype(BF16), v[:, gl]) / den)
    o_ref[...] = jnp.concatenate(outs, axis=-1).astype(o_ref.dtype)


def _ctx_attention(q, k, v, sink):
    l, bw = q.shape
    n_heads = bw // HEAD_DIM
    n_rep = n_heads // (k.shape[1] // HEAD_DIM)
    has_sink = sink is not None
    full = lambda a: pl.BlockSpec(a.shape, lambda i: (0, 0))
    in_specs = [full(q), full(k), full(v)]
    args = [q, k, v]
    if has_sink:
        in_specs = [pl.BlockSpec(memory_space=pltpu.SMEM)] + in_specs
        args = [sink] + args
    return pl.pallas_call(
        functools.partial(_ctx_attn_kernel, n_heads=n_heads, n_rep=n_rep, has_sink=has_sink),
        grid=(1,),
        in_specs=in_specs,
        out_specs=pl.BlockSpec((l, bw), lambda i: (0, 0)),
        out_shape=jax.ShapeDtypeStruct((l, bw), BF16),
        compiler_params=_cparams(("arbitrary",)),
        name="ctx_attention",
    )(*args)


def _swa_kernel(sink_ref, q_ref, k_ref, v_ref, kc_ref, vc_ref, o_ref, *, seq, n_kv, n_rep):
    n = pl.program_id(0)
    nk = 3 * SW_BLOCK
    scale = HEAD_DIM ** -0.5
    ws = jnp.clip((n - 1) * SW_BLOCK, 0, seq - nk)
    ws = pl.multiple_of(ws, SW_BLOCK)
    q = q_ref[...]
    n_all = nk + kc_ref.shape[0]
    k_all = jnp.concatenate([k_ref[pl.ds(ws, nk), :], kc_ref[...]], axis=0)
    v_all = jnp.concatenate([v_ref[pl.ds(ws, nk), :], vc_ref[...]], axis=0)
    qpos = n * SW_BLOCK + lax.broadcasted_iota(jnp.int32, (SW_BLOCK, n_all), 0)
    col = lax.broadcasted_iota(jnp.int32, (SW_BLOCK, n_all), 1)
    visible = (col >= nk) | (jnp.abs(ws + col - qpos) <= SW_WINDOW)
    maskb = jnp.where(visible, 0.0, NEG_INF).astype(F32)
    lane = lax.broadcasted_iota(jnp.int32, v_all.shape, 1)
    one = jnp.ones((), BF16)
    outs = [None] * (n_kv * n_rep)
    for g in range(n_kv):
        gl = slice(g * HEAD_DIM, (g + 1) * HEAD_DIM)
        own = (lane >= g * HEAD_DIM) & (lane < (g + 1) * HEAD_DIM)
        qg = jnp.concatenate([q[:, (g * n_rep + r) * HEAD_DIM:(g * n_rep + r + 1) * HEAD_DIM]
                              for r in range(n_rep)], axis=0)
        s = _dot_nt(qg, k_all[:, gl]) * scale
        s = (s.reshape(n_rep, SW_BLOCK, n_all) + maskb[None]).reshape(n_rep * SW_BLOCK, n_all)
        sink = jnp.concatenate([jnp.full((SW_BLOCK, 1), sink_ref[g * n_rep + r], F32)
                                for r in range(n_rep)], axis=0)
        m = jnp.maximum(jnp.max(s, axis=-1, keepdims=True), sink)
        p = jnp.exp(s - m).astype(BF16)
        ext = _dot(p, jnp.where(own, v_all, one))
        den = pltpu.roll(ext, HEAD_DIM, 1) + jnp.exp(sink - m)
        o = ext / den
        for r in range(n_rep):
            outs[g * n_rep + r] = o[r * SW_BLOCK:(r + 1) * SW_BLOCK, gl]
    o_ref[...] = jnp.concatenate(outs, axis=-1).astype(o_ref.dtype)


def _sw_attention(q, k, v, kc, vc, sink):
    s, bw = q.shape
    l = kc.shape[0]
    kvw = k.shape[1]
    n_kv = kvw // HEAD_DIM
    n_rep = (bw // HEAD_DIM) // n_kv
    assert kvw == LANES
    return pl.pallas_call(
        functools.partial(_swa_kernel, seq=s, n_kv=n_kv, n_rep=n_rep),
        grid=(s // SW_BLOCK,),
        in_specs=[pl.BlockSpec(memory_space=pltpu.SMEM),
                  pl.BlockSpec((SW_BLOCK, bw), lambda n: (n, 0)),
                  pl.BlockSpec((s, kvw), lambda n: (0, 0)),
                  pl.BlockSpec((s, kvw), lambda n: (0, 0)),
                  pl.BlockSpec((l, kvw), lambda n: (0, 0)),
                  pl.BlockSpec((l, kvw), lambda n: (0, 0))],
        out_specs=pl.BlockSpec((SW_BLOCK, bw), lambda n: (n, 0)),
        out_shape=jax.ShapeDtypeStruct((s, bw), BF16),
        compiler_params=_cparams(("arbitrary",)),
        name="sw_attention",
    )(sink, q, k, v, kc, vc)


def _s5_prepare(lam_re, lam_im, b_re, b_im, c_re, c_im, log_step):
    hp = lax.Precision.HIGHEST
    tc = S5_CHUNK
    lam = lax.complex(lam_re.astype(F32), lam_im.astype(F32))
    dt = jnp.exp(log_step.astype(F32))[..., None]
    lam_dt = lam * dt
    lam_bar = jnp.exp(lam_dt)
    b_bar = ((lam_bar - 1.0) / lam)[..., None] * lax.complex(b_re.astype(F32), b_im.astype(F32))
    c_mat = lax.complex(c_re.astype(F32), c_im.astype(F32))
    kk = jnp.arange(tc + 1, dtype=F32)
    pw = jnp.exp(lam_dt[None] * kk[:, None, None, None])
    n_g, n_p, n_h = b_bar.shape[1:]

    cw = c_mat[None] * pw[:tc, :, :, None, :]
    kern = (jnp.einsum('kdghp,dgpj->kdghj', cw.real, b_bar.real, precision=hp)
            - jnp.einsum('kdghp,dgpj->kdghj', cw.imag, b_bar.imag, precision=hp))
    t_i = np.arange(tc)
    lag_f = t_i[None, :] - t_i[:, None]
    m_f = jnp.where((lag_f >= 0)[:, :, None, None, None],
                    kern[np.clip(lag_f, 0, tc - 1), 0], 0.0)
    m_r = jnp.where((lag_f <= 0)[:, :, None, None, None],
                    kern[np.clip(-lag_f, 0, tc - 1), 1], 0.0)
    m_intra = jnp.transpose(m_f + m_r, (2, 0, 4, 1, 3)).reshape(n_g, tc * n_h, tc * n_h)

    def cat_ri(z):
        return jnp.concatenate([z.real, z.imag], axis=-1)

    w_f = pw[tc - 1 - t_i, 0][:, :, None, :] * jnp.transpose(b_bar[0], (0, 2, 1))[None]
    w_r = pw[t_i, 1][:, :, None, :] * jnp.transpose(b_bar[1], (0, 2, 1))[None]
    w_f = jnp.transpose(cat_ri(w_f), (1, 0, 2, 3)).reshape(n_g, tc * n_h, 2 * n_p)
    w_r = jnp.transpose(cat_ri(w_r), (1, 0, 2, 3)).reshape(n_g, tc * n_h, 2 * n_p)
    m_cat = jnp.concatenate([m_intra, w_f, w_r], axis=-1).astype(BF16)

    def v_of(c_dir, pw_sel):
        cp = c_dir[None] * pw_sel[:, :, None, :]
        v = jnp.concatenate([cp.real, -cp.imag], axis=-1)
        return jnp.transpose(v, (1, 3, 0, 2)).reshape(n_g, 2 * n_p, tc * n_h)
    v_cat = jnp.concatenate([v_of(c_mat[0], pw[t_i + 1, 0]), v_of(c_mat[1], pw[tc - t_i, 1])],
                            axis=1).astype(BF16)

    def coef(k):
        a = jnp.exp(lam_dt * (tc * k))
        return jnp.stack([jnp.concatenate([a.real, a.real], -1),
                          jnp.concatenate([-a.imag, a.imag], -1)], axis=2)
    a_log = jnp.stack([coef(1.0), coef(2.0), coef(4.0)], axis=2)
    a_car = jnp.stack([coef(float(j)) for j in range(1, SUBLANES + 1)], axis=3)
    a_car = jnp.stack([a_car[0], a_car[1][:, :, ::-1]], axis=0)
    return m_cat, v_cat, jnp.transpose(a_log, (1, 0, 2, 3, 4)), jnp.transpose(a_car, (1, 0, 2, 3, 4))


def _s5_kernel(u_ref, d_ref, m_ref, v_ref, alog_ref, acar_ref, y_ref, z_scr, e_scr, *, nc, nc_ctx):
    n_state = 2 * S5_STATE
    width = S5_CHUNK * S5_GROUP
    nblk = nc // SUBLANES
    nblk_ctx = nc_ctx // SUBLANES
    row = lax.broadcasted_iota(jnp.int32, (SUBLANES, n_state), 0)

    for g in range(S5_GB):
        zz = _dot(u_ref[g].astype(BF16), m_ref[g])
        y_ref[g] = zz[:, :width] + u_ref[g] * d_ref[g]
        z_scr[g, 0] = zz[:, width:width + n_state]
        z_scr[g, 1] = zz[:, width + n_state:]

    def cmul(a1, a2, s):
        return a1 * s + a2 * pltpu.roll(s, S5_STATE, 1)

    def step(t, carry):
        blk = (t, jnp.where(t < nblk_ctx, nblk_ctx - 1 - t, nblk + nblk_ctx - 1 - t))
        new = []
        for g in range(S5_GB):
            for d in range(2):
                r0 = pl.multiple_of(blk[d] * SUBLANES, SUBLANES)
                z = z_scr[g, d, pl.ds(r0, SUBLANES), :]
                c = carry[g * 2 + d]
                for i, sh in enumerate((1, 2, 4)):
                    if d == 0:
                        zs = jnp.where(row >= sh, pltpu.roll(z, sh, 0), 0.0)
                    else:
                        zs = jnp.where(row < SUBLANES - sh, pltpu.roll(z, SUBLANES - sh, 0), 0.0)
                    z = z + cmul(alog_ref[g, d, i, 0:1, :], alog_ref[g, d, i, 1:2, :], zs)
                cb = jnp.broadcast_to(c, (SUBLANES, n_state))
                s = z + cmul(acar_ref[g, d, 0], acar_ref[g, d, 1], cb)
                if d == 0:
                    e = jnp.where(row >= 1, pltpu.roll(s, 1, 0), cb)
                    c_new = s[SUBLANES - 1:SUBLANES, :]
                else:
                    e = jnp.where(row < SUBLANES - 1, pltpu.roll(s, SUBLANES - 1, 0), cb)
                    c_new = s[0:1, :]
                e_scr[g, pl.ds(r0, SUBLANES), pl.ds(d * n_state, n_state)] = e
                new.append(c_new)
        return tuple(new)

    init = tuple(jnp.zeros((1, n_state), F32) for _ in range(2 * S5_GB))
    lax.fori_loop(0, nblk, step, init)

    for g in range(S5_GB):
        y_ref[g] = y_ref[g] + _dot(e_scr[g].astype(BF16), v_ref[g])


def _s5_scan(u_chunks, d_rows, m_cat, v_cat, a_log, a_car, nc_ctx):
    n_g, nc, width = u_chunks.shape
    n_state = 2 * S5_STATE
    blk = lambda *shape: pl.BlockSpec((S5_GB,) + shape, lambda i: (i,) + (0,) * len(shape))
    return pl.pallas_call(
        functools.partial(_s5_kernel, nc=nc, nc_ctx=nc_ctx),
        grid=(n_g // S5_GB,),
        in_specs=[blk(nc, width), blk(1, width), blk(width, width + 2 * n_state),
                  blk(2 * n_state, width), blk(2, 3, 2, n_state), blk(2, 2, SUBLANES, n_state)],
        out_specs=blk(nc, width),
        out_shape=jax.ShapeDtypeStruct((n_g, nc, width), F32),
        scratch_shapes=[pltpu.VMEM((S5_GB, 2, nc, n_state), F32),
                        pltpu.VMEM((S5_GB, nc, 2 * n_state), F32)],
        compiler_params=_cparams(("arbitrary",)),
        name="s5_scan",
    )(u_chunks, d_rows, m_cat, v_cat, a_log, a_car)


def _s5_mixer_pre_glu(u_ctx, u_lat, prep, d_skip):
    m_cat, v_cat, a_log, a_car = prep
    l, w = u_ctx.shape
    s = u_lat.shape[0]
    n_g = w // S5_GROUP
    u = jnp.concatenate([u_ctx, u_lat], axis=0)
    nc = (l + s) // S5_CHUNK
    u_chunks = jnp.transpose(u.reshape(nc, S5_CHUNK, n_g, S5_GROUP), (2, 0, 1, 3)).reshape(n_g, nc, -1)
    d_rows = jnp.tile(d_skip.astype(F32).reshape(n_g, 1, S5_GROUP), (1, 1, S5_CHUNK))
    y = _s5_scan(u_chunks, d_rows, m_cat, v_cat, a_log, a_car, l // S5_CHUNK)
    y = jnp.transpose(y.reshape(n_g, nc, S5_CHUNK, S5_GROUP), (1, 2, 0, 3)).reshape(l + s, w)
    return y[:l], y[l:]


def _merge_kernel(x_ref, ya_ref, y5_ref, ys_ref, ga_ref, gb_ref, gs_ref, wglu_ref, wb_ref, wo_ref,
                  gt_ref, g_ref, sh_ref, sc_ref, xo_ref, h_ref, ht_ref):
    z = _gelu(y5_ref[...])
    yb = z * jax.nn.sigmoid(_dot(z.astype(BF16), wglu_ref[...]))
    m = (ga_ref[...].astype(F32) * _dot(ya_ref[...], wb_ref[0])
         + gb_ref[...].astype(F32) * _dot(yb.astype(BF16), wb_ref[1])
         + gs_ref[...].astype(F32) * _dot(ys_ref[...], wb_ref[2]))
    x = x_ref[...] + gt_ref[...] * _dot(m.astype(BF16), wo_ref[...])
    xo_ref[...] = x
    h = _norm_mod(x, g_ref[...], sh_ref[...], sc_ref[...])
    h_ref[...] = h.astype(BF16)
    ht_ref[...] = h.T.astype(BF16)


def _merge(x, ya, y5, ys, ga, gb, gs, w_glu, w_branch, w_out, gate, g, shift, scale):
    n, d = x.shape
    bw = ya.shape[1]
    tm = min(ROW_BLOCK, n)
    row = lambda w: pl.BlockSpec((tm, w), lambda i: (i, 0))
    full = lambda a: pl.BlockSpec(a.shape, lambda i: (0,) * a.ndim)
    vec = pl.BlockSpec((1, d), lambda i: (0, 0))
    return pl.pallas_call(
        _merge_kernel,
        grid=(n // tm,),
        in_specs=[row(d), row(bw), row(bw), row(bw), row(d), row(d), row(d),
                  full(w_glu), full(w_branch), full(w_out), vec, vec, vec, vec],
        out_specs=[row(d), row(d), pl.BlockSpec((d, tm), lambda i: (0, i))],
        out_shape=[jax.ShapeDtypeStruct((n, d), F32), jax.ShapeDtypeStruct((n, d), BF16),
                   jax.ShapeDtypeStruct((d, n), BF16)],
        compiler_params=_cparams(("arbitrary",)),
        name="merge",
    )(x, ya, y5, ys, ga, gb, gs, w_glu, w_branch, w_out, gate, g, shift, scale)


def _knock_out_16(s, order, exact):
    rank = jnp.full(s.shape, float(PEER_TOPK), F32)
    vals = []
    for r in range(PEER_TOPK):
        m = jnp.max(s, axis=0, keepdims=True)
        hit = s == m
        if exact:
            first = jnp.min(jnp.where(hit, order, float(PEER_TOPK * PEER_NKEYS)), axis=0, keepdims=True)
            hit = order == first
        rank = jnp.where(hit, float(r), rank)
        s = jnp.where(hit, -jnp.inf, s)
        vals.append(m)
    n_out = jnp.sum(jnp.where(rank < float(PEER_TOPK), 1.0, 0.0), axis=0, keepdims=True)
    return rank, jnp.concatenate(vals, axis=0), n_out


def _bf16_pair_word(x):
    hi = pltpu.bitcast(x.astype(BF16).astype(F32), jnp.uint32)
    return hi | (hi >> 16)


def _pair_tiles():
    tiles = [(0, 0, 8), (0, 8, 8)]
    for a in range(1, 8):
        tiles.append((a, 0, PEER_TOPK // (a + 1)))
    return tiles


def _route_kernel(h_ref, wq_ref, k1_ref, k2_ref, cnt_ref, c1_ref, rk_ref, e2_ref, q_scr):
    tb = h_ref.shape[0]
    q_scr[...] = _dot(h_ref[...], wq_ref[...]).astype(BF16)
    half = PEER_QDIM // 2
    iota = lax.broadcasted_iota(jnp.int32, (PEER_NKEYS, tb), 0).astype(F32)
    row8 = lax.broadcasted_iota(jnp.int32, (SUBLANES, tb), 0).astype(F32)
    k1 = k1_ref[...]
    k2 = k2_ref[...]

    def emit(hd, s1, s2, exact):
        rank1, v1, n1 = _knock_out_16(s1, iota, exact)
        rank2, v2, n2 = _knock_out_16(s2, iota, exact)
        e1v = jnp.exp(v1 - v1[0:1])
        e2v = jnp.exp(v2 - v2[0:1])
        tiles, flats, gates = [], [], []
        for a, b0, nv in _pair_tiles():
            c = v1[a:a + 1] + v2[b0:b0 + SUBLANES]
            tiles.append(jnp.where(row8 < nv, c, -jnp.inf))
            flats.append(a * PEER_TOPK + b0 + row8)
            gates.append(e1v[a:a + 1] * e2v[b0:b0 + SUBLANES])
        tiles.append(v1[SUBLANES:] + v2[0:1])
        flats.append((row8 + SUBLANES) * PEER_TOPK)
        gates.append(e1v[SUBLANES:] * e2v[0:1])
        cand = jnp.concatenate(tiles, axis=0)
        flat = jnp.concatenate(flats, axis=0)
        gate = jnp.concatenate(gates, axis=0)
        rank_c, _, n_c = _knock_out_16(cand, flat, exact)
        self_ = jnp.where(rank_c < float(PEER_TOPK), 1.0, 0.0)
        z = jnp.sum(self_ * gate, axis=0, keepdims=True)
        cnt = [self_[0:8].sum(axis=0, keepdims=True) + self_[8:16].sum(axis=0, keepdims=True)]
        for t in range(2, 9):
            cnt.append(self_[t * SUBLANES:(t + 1) * SUBLANES].sum(axis=0, keepdims=True))
        cnt = jnp.concatenate(cnt + [self_[9 * SUBLANES:]], axis=0)
        cnt1 = jnp.zeros((PEER_NKEYS, tb), F32)
        for a in range(PEER_TOPK):
            cnt1 = jnp.where(rank1 == float(a), cnt[a:a + 1], cnt1)
        cnt_ref[hd] = _bf16_pair_word(cnt1)
        c1_ref[hd] = _bf16_pair_word(jnp.exp(s1 - v1[0:1]) * (0.5 / z))
        rk_ref[hd] = rank2.astype(BF16)
        e2_ref[hd] = jnp.exp(s2 - v2[0:1]).astype(BF16)
        want = float(PEER_TOPK)
        return jnp.where((n1 != want) | (n2 != want) | (n_c != want), 1.0, 0.0)

    def head(hd, _):
        c0 = pl.multiple_of(hd * PEER_QDIM, PEER_QDIM)
        s1 = _dot_nt(k1, q_scr[:, pl.ds(c0, half)])
        s2 = _dot_nt(k2, q_scr[:, pl.ds(c0 + half, half)])
        tied = jnp.max(emit(hd, s1, s2, exact=False))

        @pl.when(tied > 0.0)
        def _():
            emit(hd, s1, s2, exact=True)
        return 0

    lax.fori_loop(0, PEER_HEADS, head, 0)


def _peer_route(h, w_q, k1, k2):
    n, d = h.shape
    tb = min(PEER_ROUTE_BLOCK, n)
    qw = w_q.shape[1]
    out_blk = pl.BlockSpec((PEER_HEADS, PEER_NKEYS, tb), lambda i: (0, 0, i))
    shp = lambda dt: jax.ShapeDtypeStruct((PEER_HEADS, PEER_NKEYS, n), dt)
    return pl.pallas_call(
        _route_kernel,
        grid=(n // tb,),
        in_specs=[pl.BlockSpec((tb, d), lambda i: (i, 0)),
                  pl.BlockSpec((d, qw), lambda i: (0, 0)),
                  pl.BlockSpec(k1.shape, lambda i: (0, 0)),
                  pl.BlockSpec(k2.shape, lambda i: (0, 0))],
        out_specs=[out_blk] * 4,
        out_shape=[shp(jnp.uint32), shp(jnp.uint32), shp(BF16), shp(BF16)],
        scratch_shapes=[pltpu.VMEM((tb, qw), BF16)],
        compiler_params=_cparams(("arbitrary",)),
        name="peer_route",
    )(h, w_q, k1, k2)


def _dense_kernel(ht_ref, x_ref, gt_ref, u_ref, vt_ref, cnt_ref, c1_ref, rk_ref, e2_ref, o_ref,
                  acc_ref, w_scr):
    eb = pl.program_id(1)
    tb = ht_ref.shape[1]
    n_exp = u_ref.shape[0]
    n_sub = n_exp // PEER_SUB
    i_per_sub = PEER_SUB // PEER_NKEYS

    @pl.when(eb == 0)
    def _():
        acc_ref[...] = jnp.zeros_like(acc_ref)

    def scores(sb):
        return _dot(u_ref[sb * PEER_SUB:(sb + 1) * PEER_SUB, :], ht_ref[...])

    pk_rows = 2 * SUBLANES
    n_pk = PEER_NKEYS // pk_rows

    def row_tile(ref, hd, i):
        words = jnp.broadcast_to(ref[hd, pl.ds(i, 1), :], (SUBLANES, tb))
        return pltpu.bitcast(words, BF16)[None]

    a_next = scores(0)
    for sb in range(n_sub):
        a_cur = a_next
        if sb + 1 < n_sub:
            a_next = scores(sb + 1)
        for il in range(i_per_sub):
            i = eb * (n_exp // PEER_NKEYS) + sb * i_per_sub + il
            gsum = jnp.zeros((n_pk, pk_rows, tb), BF16)
            for hd in range(PEER_HEADS):
                cnt = row_tile(cnt_ref, hd, i)
                c1 = row_tile(c1_ref, hd, i)
                rk = rk_ref[hd].reshape(n_pk, pk_rows, tb)
                e2 = e2_ref[hd].reshape(n_pk, pk_rows, tb)
                gsum = gsum + jnp.where(rk < cnt, e2 * c1, jnp.zeros((), BF16))
            r0 = sb * PEER_SUB + il * PEER_NKEYS
            a_i = a_cur[il * PEER_NKEYS:(il + 1) * PEER_NKEYS]
            half_gate = gsum.reshape(PEER_NKEYS, tb).astype(F32)
            w_i = a_i * (1.0 + lax.erf(a_i * (1.0 / math.sqrt(2.0)))) * half_gate
            w_scr[r0:r0 + PEER_NKEYS, :] = w_i.astype(BF16)
        done = (sb + 1) * PEER_SUB
        if done % PEER_ACC_CHUNK == 0:
            c0 = done - PEER_ACC_CHUNK
            acc_ref[...] += _dot(vt_ref[:, c0:done], w_scr[c0:done, :])

    @pl.when(eb == pl.num_programs(1) - 1)
    def _():
        o_ref[...] = x_ref[...] + gt_ref[...] * acc_ref[...].T


def _peer_dense(h_t, x, gate, u, v_t, cnt1, c1, rank2, e2):
    d, n = h_t.shape
    n_e = u.shape[0]
    tb = min(PEER_TOK_BLOCK, n)
    eb = PEER_EXP_BLOCK
    tab = pl.BlockSpec((PEER_HEADS, PEER_NKEYS, tb), lambda i, e: (0, 0, i))
    return pl.pallas_call(
        _dense_kernel,
        grid=(n // tb, n_e // eb),
        in_specs=[pl.BlockSpec((d, tb), lambda i, e: (0, i)),
                  pl.BlockSpec((tb, d), lambda i, e: (i, 0)),
                  pl.BlockSpec((1, d), lambda i, e: (0, 0)),
                  pl.BlockSpec((eb, d), lambda i, e: (e, 0)),
                  pl.BlockSpec((d, eb), lambda i, e: (0, e)),
                  tab, tab, tab, tab],
        out_specs=pl.BlockSpec((tb, d), lambda i, e: (i, 0)),
        out_shape=jax.ShapeDtypeStruct((n, d), F32),
        scratch_shapes=[pltpu.VMEM((d, tb), F32), pltpu.VMEM((eb, tb), BF16)],
        compiler_params=_cparams(("arbitrary", "arbitrary")),
        name="peer_dense",
    )(h_t, x, gate, u, v_t, cnt1, c1, rank2, e2)


def _final_norm_kernel(x_ref, g_ref, o_ref):
    x = x_ref[...]
    o_ref[...] = x * lax.rsqrt(jnp.mean(x * x, axis=-1, keepdims=True) + EPS) * g_ref[...]


def _final_norm(x, g):
    n, d = x.shape
    tm = min(2 * ROW_BLOCK, n)
    return pl.pallas_call(
        _final_norm_kernel,
        grid=(n // tm,),
        in_specs=[pl.BlockSpec((tm, d), lambda i: (i, 0)), pl.BlockSpec((1, d), lambda i: (0, 0))],
        out_specs=pl.BlockSpec((tm, d), lambda i: (i, 0)),
        out_shape=jax.ShapeDtypeStruct((n, d), F32),
        compiler_params=_cparams(("arbitrary",)),
        name="final_norm",
    )(x, g)


def kernel(x, c, ctx, c_ctx, w_mod, b_mod, g_mix, g_ffn, w_in, na_rpb, s5_lam_re, s5_lam_im,
           s5_b_re, s5_b_im, s5_c_re, s5_c_im, s5_log_step, s5_d, s5_w_glu, sw_sink, w_branch,
           w_out, peer_w_q, peer_sub_keys, peer_u, peer_v, g_final):
    batch, seq, d = x.shape
    l_ctx = ctx.shape[1]
    depth = w_mod.shape[0]
    assert batch == 1 and seq % (NA_ROWS * GRID_W) == 0 and l_ctx % (S5_CHUNK * SUBLANES) == 0
    assert seq // GRID_W >= NA_KROWS and seq >= 3 * SW_BLOCK

    cc = jnp.zeros((SUBLANES, d), F32).at[0].set(c[0]).at[1].set(c_ctx)
    mod = _mod_vectors(cc, w_mod, b_mod).reshape(depth, SUBLANES, 6, d)
    rope_tabs = _rope_tables(seq)
    row = lambda v: v.reshape(1, d)

    xx, xc = x[0], ctx[0]
    for l in range(depth):
        need_ctx = l < depth - 1
        m_lat, m_ctx = mod[l, 0], mod[l, 1]
        w_in_l = w_in[l].astype(BF16)
        g_mix_l = row(g_mix[l])
        g_ffn_l = row(g_ffn[l])

        qa, ka, va, ub, qs, ks, vs, ga, gb, gs = _inproj(
            xx, g_mix_l, row(m_lat[0]), row(m_lat[1]), w_in_l, rope_tabs)
        qa_c, ka_c, va_c, ub_c, qs_c, ks_c, vs_c, ga_c, gb_c, gs_c = _inproj(
            xc, g_mix_l, row(m_ctx[0]), row(m_ctx[1]), w_in_l, None)

        ya = _na_attention(qa, ka, va, ka_c, va_c, na_rpb[l])
        ys = _sw_attention(qs, ks, vs, ks_c, vs_c, sw_sink[l])
        prep = _s5_prepare(s5_lam_re[l], s5_lam_im[l], s5_b_re[l], s5_b_im[l], s5_c_re[l],
                           s5_c_im[l], s5_log_step[l])
        y5_c, y5 = _s5_mixer_pre_glu(ub_c, ub, prep, s5_d[l])

        w_glu_l = s5_w_glu[l].astype(BF16)
        w_branch_l = w_branch[l].astype(BF16)
        w_out_l = w_out[l].astype(BF16)
        xx, hx2, hx2_t = _merge(xx, ya, y5, ys, ga, gb, gs, w_glu_l, w_branch_l, w_out_l,
                                row(m_lat[2]), g_ffn_l, row(m_lat[3]), row(m_lat[4]))

        w_q_l = peer_w_q[l].astype(BF16)
        k1 = peer_sub_keys[l, 0].astype(BF16)
        k2 = peer_sub_keys[l, 1].astype(BF16)
        u_l = peer_u[l].astype(BF16)
        vt_l = peer_v[l].astype(BF16).T
        if need_ctx:
            ya_c = _ctx_attention(qa_c, ka_c, va_c, None)
            ys_c = _ctx_attention(qs_c, ks_c, vs_c, sw_sink[l])
            xc, hc2, hc2_t = _merge(xc, ya_c, y5_c, ys_c, ga_c, gb_c, gs_c, w_glu_l, w_branch_l, w_out_l,
                                    row(m_ctx[2]), g_ffn_l, row(m_ctx[3]), row(m_ctx[4]))
            xc = _peer_dense(hc2_t, xc, row(m_ctx[5]), u_l, vt_l, *_peer_route(hc2, w_q_l, k1, k2))
        xx = _peer_dense(hx2_t, xx, row(m_lat[5]), u_l, vt_l, *_peer_route(hx2, w_q_l, k1, k2))

    return _final_norm(xx, row(g_final))[None]
```

```python
import functools
import math

import numpy as np
import jax
import jax.numpy as jnp
from jax import lax
from jax.experimental import pallas as pl
from jax.experimental.pallas import tpu as pltpu

F32 = jnp.float32
BF16 = jnp.bfloat16

GRID_W = 64
HEAD_DIM = 64
NA_WIN_H = 8
NA_WIN_W = 16
S5_GROUP = 16
S5_STATE = 64
SW_WINDOW = 128
ROPE_BASE = 10000.0
PEER_HEADS = 8
PEER_NKEYS = 128
PEER_QDIM = 256
PEER_TOPK = 16
EPS = 1e-6
NEG_INF = -1e30

LANES = 128
SUBLANES = 8
VMEM_LIMIT_BYTES = 56 * 1024 * 1024

ROW_BLOCK = 256
NA_ROWS = 4
NA_KROWS = NA_ROWS + NA_WIN_H - 1
NA_LANES = 256
SW_BLOCK = 128
S5_CHUNK = 16
S5_LG = LANES // S5_GROUP
PEER_ROUTE_BLOCK = 256
PEER_TOK_BLOCK = 512
PEER_EXP_BLOCK = 1024
PEER_SUB = 256
PEER_ACC_CHUNK = 512


def _cparams(sem):
    return pltpu.CompilerParams(dimension_semantics=sem, vmem_limit_bytes=VMEM_LIMIT_BYTES)


def _dot(a, b):
    return jnp.dot(a, b, preferred_element_type=F32)


def _dot_nt(a, b):
    return lax.dot_general(a, b, (((1,), (1,)), ((), ())), preferred_element_type=F32)


def _gelu(x):
    return 0.5 * x * (1.0 + lax.erf(x * (1.0 / math.sqrt(2.0))))


def _mod_kernel(cc_ref, w_ref, b_ref, o_ref):
    a = cc_ref[...]
    a = a * jax.nn.sigmoid(a)
    o_ref[0] = _dot(a.astype(BF16), w_ref[0].astype(BF16)) + b_ref[0]


def _mod_vectors(cc, w_mod, b_mod):
    depth, d, n6 = w_mod.shape
    tn = 1024
    return pl.pallas_call(
        _mod_kernel,
        grid=(depth, n6 // tn),
        in_specs=[pl.BlockSpec((SUBLANES, d), lambda l, j: (0, 0)),
                  pl.BlockSpec((1, d, tn), lambda l, j: (l, 0, j)),
                  pl.BlockSpec((1, 1, tn), lambda l, j: (l, 0, j))],
        out_specs=pl.BlockSpec((1, SUBLANES, tn), lambda l, j: (l, 0, j)),
        out_shape=jax.ShapeDtypeStruct((depth, SUBLANES, n6), F32),
        compiler_params=_cparams(("arbitrary", "arbitrary")),
        name="mod_vectors",
    )(cc, w_mod, b_mod.reshape(depth, 1, n6))


def _norm_mod(x, g, shift, scale):
    y = x * lax.rsqrt(jnp.mean(x * x, axis=-1, keepdims=True) + EPS)
    return (y * g) * (1.0 + scale) + shift


def _rope(x, cos, sin, lane_lo):
    up = pltpu.roll(x, LANES - 16, 1)
    dn = pltpu.roll(x, 16, 1)
    return x * cos + jnp.where(lane_lo, up, dn) * sin


def _inproj_kernel(x_ref, g_ref, sh_ref, sc_ref, w_ref, *rest, bw, kvw, d_model, rope):
    if rope:
        cos_ref, sin_ref = rest[:2]
        rest = rest[2:]
    qa_ref, ka_ref, va_ref, ub_ref, qs_ref, ks_ref, vs_ref, ga_ref, gb_ref, gs_ref = rest
    h = _norm_mod(x_ref[...], g_ref[...], sh_ref[...], sc_ref[...]).astype(BF16)

    def proj(c0, width):
        return _dot(h, w_ref[:, c0:c0 + width])

    c = 0
    qa_ref[...] = proj(c, bw).astype(BF16); c += bw
    ka_ref[...] = proj(c, bw).astype(BF16); c += bw
    va_ref[...] = proj(c, bw).astype(BF16); c += bw
    ub = proj(c, bw); c += bw
    for t in range(bw // LANES):
        ub_ref[t] = ub[:, t * LANES:(t + 1) * LANES]
    qs = proj(c, bw); c += bw
    ks = proj(c, kvw); c += kvw
    vs_ref[...] = proj(c, kvw).astype(BF16); c += kvw
    if rope:
        cos = cos_ref[...]
        sin = sin_ref[...]
        lane = lax.broadcasted_iota(jnp.int32, cos.shape, 1)
        lane_lo = (lane % 32) < 16
        for p in range(bw // LANES):
            sl = slice(p * LANES, (p + 1) * LANES)
            qs_ref[:, sl] = _rope(qs[:, sl], cos, sin, lane_lo).astype(BF16)
        for p in range(kvw // LANES):
            sl = slice(p * LANES, (p + 1) * LANES)
            ks_ref[:, sl] = _rope(ks[:, sl], cos, sin, lane_lo).astype(BF16)
    else:
        qs_ref[...] = qs.astype(BF16)
        ks_ref[...] = ks.astype(BF16)
    ga_ref[...] = jax.nn.sigmoid(proj(c, d_model)).astype(BF16); c += d_model
    gb_ref[...] = jax.nn.sigmoid(proj(c, d_model)).astype(BF16); c += d_model
    gs_ref[...] = jax.nn.sigmoid(proj(c, d_model)).astype(BF16)


def _inproj(x, g, shift, scale, w_in, rope_tabs):
    n, d = x.shape
    bw = d // 2
    kvw = bw // 4
    tm = min(ROW_BLOCK, n)
    rope = rope_tabs is not None
    row = lambda i: (i, 0)
    fixed = lambda i: (0, 0)
    in_specs = [pl.BlockSpec((tm, d), row), pl.BlockSpec((1, d), fixed), pl.BlockSpec((1, d), fixed),
                pl.BlockSpec((1, d), fixed), pl.BlockSpec(w_in.shape, fixed)]
    args = [x, g, shift, scale, w_in]
    if rope:
        in_specs += [pl.BlockSpec((tm, LANES), row)] * 2
        args += list(rope_tabs)
    widths = [bw, bw, bw, bw, bw, kvw, kvw, d, d, d]
    dtypes = [BF16, BF16, BF16, F32, BF16, BF16, BF16, BF16, BF16, BF16]
    s5_slot = 3
    return pl.pallas_call(
        functools.partial(_inproj_kernel, bw=bw, kvw=kvw, d_model=d, rope=rope),
        grid=(n // tm,),
        in_specs=in_specs,
        out_specs=[pl.BlockSpec((bw // LANES, tm, LANES), lambda i: (0, i, 0)) if k == s5_slot
                   else pl.BlockSpec((tm, w), row) for k, w in enumerate(widths)],
        out_shape=[jax.ShapeDtypeStruct((bw // LANES, n, LANES) if k == s5_slot else (n, w), dt)
                   for k, (w, dt) in enumerate(zip(widths, dtypes))],
        compiler_params=_cparams(("arbitrary",)),
        name="inproj",
    )(*args)


def _rope_tables(seq):
    t = jnp.arange(seq)
    half = HEAD_DIM // 4
    inv = ROPE_BASE ** (-jnp.arange(half, dtype=F32) / half)
    sign = jnp.concatenate([-jnp.ones((half,), F32), jnp.ones((half,), F32)])

    def axis_tabs(pos):
        ang = pos.astype(F32)[:, None] * inv[None, :]
        c = jnp.cos(ang)
        s = jnp.sin(ang)
        return jnp.concatenate([c, c], -1), jnp.concatenate([s, s], -1) * sign

    cr, sr = axis_tabs(t // GRID_W)
    cc, sc = axis_tabs(t % GRID_W)
    cos = jnp.concatenate([cr, cc], -1)
    sin = jnp.concatenate([sr, sc], -1)
    return jnp.tile(cos, (1, LANES // HEAD_DIM)), jnp.tile(sin, (1, LANES // HEAD_DIM))


def _na_patterns(rows):
    kh = NA_WIN_H
    nb = rows // NA_ROWS
    kr0s, sigs = [], []
    for b in range(nb):
        r_lo = b * NA_ROWS
        kr0 = int(np.clip(r_lo - kh // 2, 0, rows - NA_KROWS))
        sig = tuple((int(np.clip(r - kh // 2, 0, rows - kh)) - kr0, r - kr0)
                    for r in range(r_lo, r_lo + NA_ROWS))
        kr0s.append(kr0)
        sigs.append(sig)
    uniq = sorted(set(sigs))
    ids = np.array([uniq.index(s) for s in sigs], np.int32)
    return np.array(kr0s, np.int32), ids, uniq


def _na_bias(rpb, uniq, n_ctx):
    kw = NA_WIN_W
    cols = np.arange(GRID_W)
    col_start = np.clip(cols - kw // 2, 0, GRID_W - kw)
    n_pat = len(uniq)
    kk = np.arange(NA_KROWS)
    col_sel = (cols[None, None, :] - cols[None, :, None] + kw - 1
               == np.arange(2 * kw - 1)[:, None, None])
    col_ok = ((cols[None, :] >= col_start[:, None]) & (cols[None, :] < col_start[:, None] + kw))
    row_sel = np.zeros((n_pat, NA_ROWS, NA_KROWS, 2 * NA_WIN_H - 1), bool)
    row_ok = np.zeros((n_pat, NA_ROWS, NA_KROWS), bool)
    for p, sig in enumerate(uniq):
        for rq, (r0_rel, r_rel) in enumerate(sig):
            row_ok[p, rq] = (kk >= r0_rel) & (kk < r0_rel + NA_WIN_H)
            row_sel[p, rq] = (kk[:, None] - r_rel + NA_WIN_H - 1) == np.arange(2 * NA_WIN_H - 1)[None, :]
    hp = lax.Precision.HIGHEST
    by_col = jnp.einsum('hrv,vcj->hrcj', rpb.astype(F32), jnp.asarray(col_sel, F32), precision=hp)
    bias = jnp.einsum('pqkr,hrcj->phqckj', jnp.asarray(row_sel, F32), by_col, precision=hp)
    ok = row_ok[:, None, :, None, :, None] & col_ok[None, None, None, :, None, :]
    bias = jnp.where(jnp.asarray(ok), bias, NEG_INF)
    bias = bias.reshape(n_pat, rpb.shape[0], NA_ROWS * GRID_W, NA_KROWS * GRID_W)
    return jnp.concatenate([bias, jnp.zeros(bias.shape[:3] + (n_ctx,), F32)], axis=-1)


def _na_kernel(kr0_ref, pat_ref, q_ref, k_ref, v_ref, kc_ref, vc_ref, bias_ref, o_ref):
    b = pl.program_id(1)
    start = pl.multiple_of(kr0_ref[b] * GRID_W, GRID_W)
    nk = NA_KROWS * GRID_W
    scale = HEAD_DIM ** -0.5
    one = jnp.ones((), BF16)
    hp = LANES // HEAD_DIM
    for lt in range(q_ref.shape[1] // LANES):
        lanes = slice(lt * LANES, (lt + 1) * LANES)
        q = q_ref[:, lanes]
        k_all = jnp.concatenate([k_ref[pl.ds(start, nk), lanes], kc_ref[:, lanes]], axis=0)
        v_all = jnp.concatenate([v_ref[pl.ds(start, nk), lanes], vc_ref[:, lanes]], axis=0)
        lane = lax.broadcasted_iota(jnp.int32, v_all.shape, 1)
        ext = []
        for h in range(hp):
            sl = slice(h * HEAD_DIM, (h + 1) * HEAD_DIM)
            own = (lane >= h * HEAD_DIM) & (lane < (h + 1) * HEAD_DIM)
            s = _dot_nt(q[:, sl], k_all[:, sl]) * scale + bias_ref[0, lt * hp + h]
            p = jnp.exp(s - jnp.max(s, axis=-1, keepdims=True)).astype(BF16)
            ext.append(_dot(p, jnp.where(own, v_all, one)))
        out_lane = lax.broadcasted_iota(jnp.int32, ext[0].shape, 1)
        low = out_lane < HEAD_DIM
        num = jnp.where(low, ext[0], ext[1])
        den = pltpu.roll(jnp.where(low, ext[1], ext[0]), HEAD_DIM, 1)
        o_ref[:, lanes] = (num / den).astype(o_ref.dtype)


def _na_attention(q, k, v, kc, vc, bias):
    s, bw = q.shape
    l = kc.shape[0]
    rows = s // GRID_W
    kr0s, ids, _ = _na_patterns(rows)
    tq = NA_ROWS * GRID_W
    nk = NA_KROWS * GRID_W
    hp = NA_LANES // HEAD_DIM
    grid_spec = pltpu.PrefetchScalarGridSpec(
        num_scalar_prefetch=2,
        grid=(bw // NA_LANES, rows // NA_ROWS),
        in_specs=[pl.BlockSpec((tq, NA_LANES), lambda p, b, kr, pt: (b, p)),
                  pl.BlockSpec((s, NA_LANES), lambda p, b, kr, pt: (0, p)),
                  pl.BlockSpec((s, NA_LANES), lambda p, b, kr, pt: (0, p)),
                  pl.BlockSpec((l, NA_LANES), lambda p, b, kr, pt: (0, p)),
                  pl.BlockSpec((l, NA_LANES), lambda p, b, kr, pt: (0, p)),
                  pl.BlockSpec((1, hp, tq, nk + l), lambda p, b, kr, pt: (pt[b], p, 0, 0))],
        out_specs=pl.BlockSpec((tq, NA_LANES), lambda p, b, kr, pt: (b, p)),
    )
    return pl.pallas_call(
        _na_kernel,
        grid_spec=grid_spec,
        out_shape=jax.ShapeDtypeStruct((s, bw), BF16),
        compiler_params=_cparams(("arbitrary", "arbitrary")),
        name="na_attention",
    )(jnp.asarray(kr0s), jnp.asarray(ids), q, k, v, kc, vc, bias)


def _ctx_attn_kernel(*refs, n_heads, n_rep, has_sink):
    if has_sink:
        sink_ref, q_ref, k_ref, v_ref, o_ref = refs
    else:
        q_ref, k_ref, v_ref, o_ref = refs
    scale = HEAD_DIM ** -0.5
    q = q_ref[...]
    k = k_ref[...]
    v = v_ref[...]
    outs = []
    for h in range(n_heads):
        g = h // n_rep
        sl = slice(h * HEAD_DIM, (h + 1) * HEAD_DIM)
        gl = slice(g * HEAD_DIM, (g + 1) * HEAD_DIM)
        s = _dot_nt(q[:, sl], k[:, gl]) * scale
        m = jnp.max(s, axis=-1, keepdims=True)
        if has_sink:
            m = jnp.maximum(m, sink_ref[h])
        p = jnp.exp(s - m)
        den = jnp.sum(p, axis=-1, keepdims=True)
        if has_sink:
            den = den + jnp.exp(sink_ref[h] - m)
        outs.append(_dot(p.astype(BF16), v[:, gl]) / den)
    o_ref[...] = jnp.concatenate(outs, axis=-1).astype(o_ref.dtype)


def _ctx_attention(q, k, v, sink):
    l, bw = q.shape
    n_heads = bw // HEAD_DIM
    n_rep = n_heads // (k.shape[1] // HEAD_DIM)
    has_sink = sink is not None
    full = lambda a: pl.BlockSpec(a.shape, lambda i: (0, 0))
    in_specs = [full(q), full(k), full(v)]
    args = [q, k, v]
    if has_sink:
        in_specs = [pl.BlockSpec(memory_space=pltpu.SMEM)] + in_specs
        args = [sink] + args
    return pl.pallas_call(
        functools.partial(_ctx_attn_kernel, n_heads=n_heads, n_rep=n_rep, has_sink=has_sink),
        grid=(1,),
        in_specs=in_specs,
        out_specs=pl.BlockSpec((l, bw), lambda i: (0, 0)),
        out_shape=jax.ShapeDtypeStruct((l, bw), BF16),
        compiler_params=_cparams(("arbitrary",)),
        name="ctx_attention",
    )(*args)


def _swa_kernel(sink_ref, q_ref, k_ref, v_ref, kc_ref, vc_ref, o_ref, *, seq, n_kv, n_rep):
    n = pl.program_id(0)
    nk = 3 * SW_BLOCK
    scale = HEAD_DIM ** -0.5
    ws = jnp.clip((n - 1) * SW_BLOCK, 0, seq - nk)
    ws = pl.multiple_of(ws, SW_BLOCK)
    q = q_ref[...]
    n_all = nk + kc_ref.shape[0]
    k_all = jnp.concatenate([k_ref[pl.ds(ws, nk), :], kc_ref[...]], axis=0)
    v_all = jnp.concatenate([v_ref[pl.ds(ws, nk), :], vc_ref[...]], axis=0)
    qpos = n * SW_BLOCK + lax.broadcasted_iota(jnp.int32, (SW_BLOCK, n_all), 0)
    col = lax.broadcasted_iota(jnp.int32, (SW_BLOCK, n_all), 1)
    visible = (col >= nk) | (jnp.abs(ws + col - qpos) <= SW_WINDOW)
    maskb = jnp.where(visible, 0.0, NEG_INF).astype(F32)
    lane = lax.broadcasted_iota(jnp.int32, v_all.shape, 1)
    one = jnp.ones((), BF16)
    outs = [None] * (n_kv * n_rep)
    for g in range(n_kv):
        gl = slice(g * HEAD_DIM, (g + 1) * HEAD_DIM)
        own = (lane >= g * HEAD_DIM) & (lane < (g + 1) * HEAD_DIM)
        qg = jnp.concatenate([q[:, (g * n_rep + r) * HEAD_DIM:(g * n_rep + r + 1) * HEAD_DIM]
                              for r in range(n_rep)], axis=0)
        s = _dot_nt(qg, k_all[:, gl]) * scale
        s = (s.reshape(n_rep, SW_BLOCK, n_all) + maskb[None]).reshape(n_rep * SW_BLOCK, n_all)
        sink = jnp.concatenate([jnp.full((SW_BLOCK, 1), sink_ref[g * n_rep + r], F32)
                                for r in range(n_rep)], axis=0)
        m = jnp.maximum(jnp.max(s, axis=-1, keepdims=True), sink)
        p = jnp.exp(s - m).astype(BF16)
        ext = _dot(p, jnp.where(own, v_all, one))
        den = pltpu.roll(ext, HEAD_DIM, 1) + jnp.exp(sink - m)
        o = ext / den
        for r in range(n_rep):
            outs[g * n_rep + r] = o[r * SW_BLOCK:(r + 1) * SW_BLOCK, gl]
    o_ref[...] = jnp.concatenate(outs, axis=-1).astype(o_ref.dtype)


def _sw_attention(q, k, v, kc, vc, sink):
    s, bw = q.shape
    l = kc.shape[0]
    kvw = k.shape[1]
    n_kv = kvw // HEAD_DIM
    n_rep = (bw // HEAD_DIM) // n_kv
    assert kvw == LANES
    return pl.pallas_call(
        functools.partial(_swa_kernel, seq=s, n_kv=n_kv, n_rep=n_rep),
        grid=(s // SW_BLOCK,),
        in_specs=[pl.BlockSpec(memory_space=pltpu.SMEM),
                  pl.BlockSpec((SW_BLOCK, bw), lambda n: (n, 0)),
                  pl.BlockSpec((s, kvw), lambda n: (0, 0)),
                  pl.BlockSpec((s, kvw), lambda n: (0, 0)),
                  pl.BlockSpec((l, kvw), lambda n: (0, 0)),
                  pl.BlockSpec((l, kvw), lambda n: (0, 0))],
        out_specs=pl.BlockSpec((SW_BLOCK, bw), lambda n: (n, 0)),
        out_shape=jax.ShapeDtypeStruct((s, bw), BF16),
        compiler_params=_cparams(("arbitrary",)),
        name="sw_attention",
    )(sink, q, k, v, kc, vc)


def _s5_prepare(lam_re, lam_im, b_re, b_im, c_re, c_im, log_step):
    hp = lax.Precision.HIGHEST
    tc = S5_CHUNK
    lam = lax.complex(lam_re.astype(F32), lam_im.astype(F32))
    dt = jnp.exp(log_step.astype(F32))[..., None]
    lam_dt = lam * dt
    lam_bar = jnp.exp(lam_dt)
    b_bar = ((lam_bar - 1.0) / lam)[..., None] * lax.complex(b_re.astype(F32), b_im.astype(F32))
    c_mat = lax.complex(c_re.astype(F32), c_im.astype(F32))
    kk = jnp.arange(tc + 1, dtype=F32)
    pw = jnp.exp(lam_dt[None] * kk[:, None, None, None])
    n_g, n_p, n_h = b_bar.shape[1:]

    cw = c_mat[None] * pw[:tc, :, :, None, :]
    kern = (jnp.einsum('kdghp,dgpj->kdghj', cw.real, b_bar.real, precision=hp)
            - jnp.einsum('kdghp,dgpj->kdghj', cw.imag, b_bar.imag, precision=hp))
    t_i = np.arange(tc)
    lag_f = t_i[None, :] - t_i[:, None]
    m_f = jnp.where((lag_f >= 0)[:, :, None, None, None],
                    kern[np.clip(lag_f, 0, tc - 1), 0], 0.0)
    m_r = jnp.where((lag_f <= 0)[:, :, None, None, None],
                    kern[np.clip(-lag_f, 0, tc - 1), 1], 0.0)
    m_intra = jnp.transpose(m_f + m_r, (2, 0, 4, 1, 3)).reshape(n_g, tc * n_h, tc * n_h)

    def cat_ri(z):
        return jnp.concatenate([z.real, z.imag], axis=-1)

    w_f = pw[tc - 1 - t_i, 0][:, :, None, :] * jnp.transpose(b_bar[0], (0, 2, 1))[None]
    w_r = pw[t_i, 1][:, :, None, :] * jnp.transpose(b_bar[1], (0, 2, 1))[None]
    w_f = jnp.transpose(cat_ri(w_f), (1, 0, 2, 3)).reshape(n_g, tc * n_h, 2 * n_p)
    w_r = jnp.transpose(cat_ri(w_r), (1, 0, 2, 3)).reshape(n_g, tc * n_h, 2 * n_p)
    m_cat = jnp.concatenate([m_intra, w_f, w_r], axis=-1).astype(BF16)

    def v_of(c_dir, pw_sel):
        cp = c_dir[None] * pw_sel[:, :, None, :]
        v = jnp.concatenate([cp.real, -cp.imag], axis=-1)
        return jnp.transpose(v, (1, 3, 0, 2)).reshape(n_g, 2 * n_p, tc * n_h)
    v_cat = jnp.concatenate([v_of(c_mat[0], pw[t_i + 1, 0]), v_of(c_mat[1], pw[tc - t_i, 1])],
                            axis=1).astype(BF16)

    def coef(k):
        a = jnp.exp(lam_dt * (tc * k))
        return jnp.stack([jnp.concatenate([a.real, a.real], -1),
                          jnp.concatenate([-a.imag, a.imag], -1)], axis=2)
    a_log = jnp.stack([coef(1.0), coef(2.0), coef(4.0)], axis=2)
    a_car = jnp.stack([coef(float(j)) for j in range(1, SUBLANES + 1)], axis=3)
    a_car = jnp.stack([a_car[0], a_car[1][:, :, ::-1]], axis=0)
    return _s5_expand(m_cat, v_cat, jnp.transpose(a_log, (1, 0, 2, 3, 4)), jnp.transpose(a_car, (1, 0, 2, 3, 4)))


def _s5_expand(m_cat, v_cat, a_log, a_car):
    n_g, width, _ = m_cat.shape
    n_t = n_g // S5_LG
    tc, hg, n_state = S5_CHUNK, S5_GROUP, 2 * S5_STATE
    eye = jnp.eye(S5_LG, dtype=m_cat.dtype)

    m5 = jnp.transpose(m_cat[:, :, :width].reshape(n_t, S5_LG, tc, hg, tc, hg), (0, 2, 1, 3, 4, 5))
    m_x = (m5[:, :, :, :, :, None, :] * eye[None, None, :, None, None, :, None]
           ).reshape(n_t, tc * S5_LG * hg, tc * S5_LG * hg)

    def expand_w(w):
        w5 = jnp.transpose(w.reshape(n_t, S5_LG, tc, hg, n_state), (0, 2, 1, 3, 4))
        return (w5[:, :, :, :, None, :] * eye[None, None, :, None, :, None]
                ).reshape(n_t, tc * S5_LG * hg, S5_LG * n_state)
    mw = jnp.concatenate([m_x, expand_w(m_cat[:, :, width:width + n_state]),
                          expand_w(m_cat[:, :, width + n_state:])], axis=-1)

    v5 = jnp.transpose(v_cat.reshape(n_t, S5_LG, 2, n_state, tc, hg), (0, 2, 1, 3, 4, 5))
    v_x = (v5[:, :, :, :, :, None, :] * eye[None, None, :, None, None, :, None]
           ).reshape(n_t, 2 * S5_LG * n_state, tc * S5_LG * hg)

    def lanes_of(a):
        a = a.reshape((n_t, S5_LG) + a.shape[1:])
        a = jnp.moveaxis(a, 1, -2)
        return a.reshape(a.shape[:-2] + (S5_LG * n_state,))
    return mw, v_x, lanes_of(a_log), lanes_of(a_car)


def _s5_in_kernel(u_ref, d_ref, mw_ref, yp_ref, z_ref):
    u = u_ref[0]
    width = u.shape[1]
    zz = _dot(u.astype(BF16), mw_ref[0])
    yp_ref[0] = zz[:, :width] + u * d_ref[0]
    z_ref[0] = zz[:, width:]


def _s5_scan_kernel(z_ref, alog_ref, acar_ref, e_ref, *, nc, nc_ctx):
    n_state = 2 * S5_STATE
    half = z_ref.shape[2] // 2
    nblk = nc // SUBLANES
    nblk_ctx = nc_ctx // SUBLANES
    row = lax.broadcasted_iota(jnp.int32, (SUBLANES, n_state), 0)

    def cmul(a1, a2, s):
        return a1 * s + a2 * pltpu.roll(s, S5_STATE, 1)

    def step(t, carry):
        blk = (t, jnp.where(t < nblk_ctx, nblk_ctx - 1 - t, nblk + nblk_ctx - 1 - t))
        new = []
        for d in range(2):
            r0 = pl.multiple_of(blk[d] * SUBLANES, SUBLANES)
            for j in range(half // n_state):
                tile = slice(j * n_state, (j + 1) * n_state)
                lanes = slice(d * half + j * n_state, d * half + (j + 1) * n_state)
                z = z_ref[0, pl.ds(r0, SUBLANES), lanes]
                c = carry[len(new)]
                for i, sh in enumerate((1, 2, 4)):
                    if d == 0:
                        zs = jnp.where(row >= sh, pltpu.roll(z, sh, 0), 0.0)
                    else:
                        zs = jnp.where(row < SUBLANES - sh, pltpu.roll(z, SUBLANES - sh, 0), 0.0)
                    z = z + cmul(alog_ref[0, d, i, 0:1, tile], alog_ref[0, d, i, 1:2, tile], zs)
                cb = jnp.broadcast_to(c, (SUBLANES, n_state))
                s = z + cmul(acar_ref[0, d, 0, :, tile], acar_ref[0, d, 1, :, tile], cb)
                if d == 0:
                    e = jnp.where(row >= 1, pltpu.roll(s, 1, 0), cb)
                    c_new = s[SUBLANES - 1:SUBLANES, :]
                else:
                    e = jnp.where(row < SUBLANES - 1, pltpu.roll(s, SUBLANES - 1, 0), cb)
                    c_new = s[0:1, :]
                e_ref[0, pl.ds(r0, SUBLANES), lanes] = e
                new.append(c_new)
        return tuple(new)

    init = tuple(jnp.zeros((1, n_state), F32) for _ in range(2 * (half // n_state)))
    lax.fori_loop(0, nblk, step, init)


def _s5_out_kernel(e_ref, yp_ref, v_ref, y_ref):
    y_ref[0] = yp_ref[0] + _dot(e_ref[0].astype(BF16), v_ref[0])


def _s5_rows(nc):
    return max(r for r in range(2 * SUBLANES, 257, 2 * SUBLANES) if nc % r == 0)


def _s5_mixer_pre_glu(u_ctx, u_lat, prep, d_skip):
    mw, v_x, a_log, a_car = prep
    n_t, l, _ = u_ctx.shape
    s = u_lat.shape[1]
    nc, nc_ctx = (l + s) // S5_CHUNK, l // S5_CHUNK
    width = S5_CHUNK * LANES
    n_z = mw.shape[2] - width
    u = jnp.concatenate([u_ctx, u_lat], axis=1).reshape(n_t, nc, width)
    d_rows = jnp.tile(d_skip.astype(F32).reshape(n_t, 1, LANES), (1, 1, S5_CHUNK))
    rb = _s5_rows(nc)
    rows = lambda w: pl.BlockSpec((1, rb, w), lambda g, r: (g, r, 0))
    per_tile = lambda a: pl.BlockSpec((1,) + a.shape[1:], lambda g, *_: (g,) + (0,) * (a.ndim - 1))
    yp, z = pl.pallas_call(
        _s5_in_kernel,
        grid=(n_t, nc // rb),
        in_specs=[rows(width), per_tile(d_rows), per_tile(mw)],
        out_specs=[rows(width), rows(n_z)],
        out_shape=[jax.ShapeDtypeStruct((n_t, nc, width), F32), jax.ShapeDtypeStruct((n_t, nc, n_z), F32)],
        compiler_params=_cparams(("arbitrary", "arbitrary")),
        name="s5_in",
    )(u, d_rows, mw)
    e = pl.pallas_call(
        functools.partial(_s5_scan_kernel, nc=nc, nc_ctx=nc_ctx),
        grid=(n_t,),
        in_specs=[per_tile(z), per_tile(a_log), per_tile(a_car)],
        out_specs=per_tile(z),
        out_shape=jax.ShapeDtypeStruct(z.shape, F32),
        compiler_params=_cparams(("arbitrary",)),
        name="s5_scan",
    )(z, a_log, a_car)
    y = pl.pallas_call(
        _s5_out_kernel,
        grid=(n_t, nc // rb),
        in_specs=[rows(n_z), rows(width), per_tile(v_x)],
        out_specs=rows(width),
        out_shape=jax.ShapeDtypeStruct((n_t, nc, width), F32),
        compiler_params=_cparams(("arbitrary", "arbitrary")),
        name="s5_out",
    )(e, yp, v_x)
    return y.reshape(n_t, l + s, LANES)


def _merge_kernel(x_ref, ya_ref, y5_ref, ys_ref, ga_ref, gb_ref, gs_ref, wglu_ref, wb_ref, wo_ref,
                  gt_ref, g_ref, sh_ref, sc_ref, xo_ref, h_ref, ht_ref):
    y5 = jnp.concatenate([y5_ref[t] for t in range(y5_ref.shape[0])], axis=-1)
    z = _gelu(y5)
    yb = z * jax.nn.sigmoid(_dot(z.astype(BF16), wglu_ref[...]))
    m = (ga_ref[...].astype(F32) * _dot(ya_ref[...], wb_ref[0])
         + gb_ref[...].astype(F32) * _dot(yb.astype(BF16), wb_ref[1])
         + gs_ref[...].astype(F32) * _dot(ys_ref[...], wb_ref[2]))
    x = x_ref[...] + gt_ref[...] * _dot(m.astype(BF16), wo_ref[...])
    xo_ref[...] = x
    h = _norm_mod(x, g_ref[...], sh_ref[...], sc_ref[...])
    h_ref[...] = h.astype(BF16)
    ht_ref[...] = h.T.astype(BF16)


def _merge(x, ya, y5, y5_row0, ys, ga, gb, gs, w_glu, w_branch, w_out, gate, g, shift, scale):
    n, d = x.shape
    bw = ya.shape[1]
    tm = min(ROW_BLOCK, n)
    assert y5_row0 % tm == 0
    row = lambda w: pl.BlockSpec((tm, w), lambda i: (i, 0))
    full = lambda a: pl.BlockSpec(a.shape, lambda i: (0,) * a.ndim)
    vec = pl.BlockSpec((1, d), lambda i: (0, 0))
    y5_rows = pl.BlockSpec((y5.shape[0], tm, LANES), lambda i: (0, i + y5_row0 // tm, 0))
    return pl.pallas_call(
        _merge_kernel,
        grid=(n // tm,),
        in_specs=[row(d), row(bw), y5_rows, row(bw), row(d), row(d), row(d),
                  full(w_glu), full(w_branch), full(w_out), vec, vec, vec, vec],
        out_specs=[row(d), row(d), pl.BlockSpec((d, tm), lambda i: (0, i))],
        out_shape=[jax.ShapeDtypeStruct((n, d), F32), jax.ShapeDtypeStruct((n, d), BF16),
                   jax.ShapeDtypeStruct((d, n), BF16)],
        compiler_params=_cparams(("arbitrary",)),
        name="merge",
    )(x, ya, y5, ys, ga, gb, gs, w_glu, w_branch, w_out, gate, g, shift, scale)


def _knock_out_16(s, order, exact):
    rank = jnp.full(s.shape, float(PEER_TOPK), F32)
    vals = []
    for r in range(PEER_TOPK):
        m = jnp.max(s, axis=0, keepdims=True)
        hit = s == m
        if exact:
            first = jnp.min(jnp.where(hit, order, float(PEER_TOPK * PEER_NKEYS)), axis=0, keepdims=True)
            hit = order == first
        rank = jnp.where(hit, float(r), rank)
        s = jnp.where(hit, -jnp.inf, s)
        vals.append(m)
    n_out = jnp.sum(jnp.where(rank < float(PEER_TOPK), 1.0, 0.0), axis=0, keepdims=True)
    return rank, jnp.concatenate(vals, axis=0), n_out


def _bf16_pair_word(x):
    hi = pltpu.bitcast(x.astype(BF16).astype(F32), jnp.uint32)
    return hi | (hi >> 16)


def _pair_tiles():
    tiles = [(0, 0, 8), (0, 8, 8)]
    for a in range(1, 8):
        tiles.append((a, 0, PEER_TOPK // (a + 1)))
    return tiles


def _route_kernel(h_ref, wq_ref, k1_ref, k2_ref, cnt_ref, c1_ref, rk_ref, e2_ref, q_scr):
    tb = h_ref.shape[0]
    q_scr[...] = _dot(h_ref[...], wq_ref[...]).astype(BF16)
    half = PEER_QDIM // 2
    iota = lax.broadcasted_iota(jnp.int32, (PEER_NKEYS, tb), 0).astype(F32)
    row8 = lax.broadcasted_iota(jnp.int32, (SUBLANES, tb), 0).astype(F32)
    k1 = k1_ref[...]
    k2 = k2_ref[...]

    def emit(hd, s1, s2, exact):
        rank1, v1, n1 = _knock_out_16(s1, iota, exact)
        rank2, v2, n2 = _knock_out_16(s2, iota, exact)
        e1v = jnp.exp(v1 - v1[0:1])
        e2v = jnp.exp(v2 - v2[0:1])
        tiles, flats, gates = [], [], []
        for a, b0, nv in _pair_tiles():
            c = v1[a:a + 1] + v2[b0:b0 + SUBLANES]
            tiles.append(jnp.where(row8 < nv, c, -jnp.inf))
            flats.append(a * PEER_TOPK + b0 + row8)
            gates.append(e1v[a:a + 1] * e2v[b0:b0 + SUBLANES])
        tiles.append(v1[SUBLANES:] + v2[0:1])
        flats.append((row8 + SUBLANES) * PEER_TOPK)
        gates.append(e1v[SUBLANES:] * e2v[0:1])
        cand = jnp.concatenate(tiles, axis=0)
        flat = jnp.concatenate(flats, axis=0)
        gate = jnp.concatenate(gates, axis=0)
        rank_c, _, n_c = _knock_out_16(cand, flat, exact)
        self_ = jnp.where(rank_c < float(PEER_TOPK), 1.0, 0.0)
        z = jnp.sum(self_ * gate, axis=0, keepdims=True)
        cnt = [self_[0:8].sum(axis=0, keepdims=True) + self_[8:16].sum(axis=0, keepdims=True)]
        for t in range(2, 9):
            cnt.append(self_[t * SUBLANES:(t + 1) * SUBLANES].sum(axis=0, keepdims=True))
        cnt = jnp.concatenate(cnt + [self_[9 * SUBLANES:]], axis=0)
        cnt1 = jnp.zeros((PEER_NKEYS, tb), F32)
        for a in range(PEER_TOPK):
            cnt1 = jnp.where(rank1 == float(a), cnt[a:a + 1], cnt1)
        cnt_ref[hd] = _bf16_pair_word(cnt1)
        c1_ref[hd] = _bf16_pair_word(jnp.exp(s1 - v1[0:1]) * (0.5 / z))
        rk_ref[hd] = rank2.astype(BF16)
        e2_ref[hd] = jnp.exp(s2 - v2[0:1]).astype(BF16)
        want = float(PEER_TOPK)
        return jnp.where((n1 != want) | (n2 != want) | (n_c != want), 1.0, 0.0)

    def head(hd, _):
        c0 = pl.multiple_of(hd * PEER_QDIM, PEER_QDIM)
        s1 = _dot_nt(k1, q_scr[:, pl.ds(c0, half)])
        s2 = _dot_nt(k2, q_scr[:, pl.ds(c0 + half, half)])
        tied = jnp.max(emit(hd, s1, s2, exact=False))

        @pl.when(tied > 0.0)
        def _():
            emit(hd, s1, s2, exact=True)
        return 0

    lax.fori_loop(0, PEER_HEADS, head, 0)


def _peer_route(h, w_q, k1, k2):
    n, d = h.shape
    tb = min(PEER_ROUTE_BLOCK, n)
    qw = w_q.shape[1]
    out_blk = pl.BlockSpec((PEER_HEADS, PEER_NKEYS, tb), lambda i: (0, 0, i))
    shp = lambda dt: jax.ShapeDtypeStruct((PEER_HEADS, PEER_NKEYS, n), dt)
    return pl.pallas_call(
        _route_kernel,
        grid=(n // tb,),
        in_specs=[pl.BlockSpec((tb, d), lambda i: (i, 0)),
                  pl.BlockSpec((d, qw), lambda i: (0, 0)),
                  pl.BlockSpec(k1.shape, lambda i: (0, 0)),
                  pl.BlockSpec(k2.shape, lambda i: (0, 0))],
        out_specs=[out_blk] * 4,
        out_shape=[shp(jnp.uint32), shp(jnp.uint32), shp(BF16), shp(BF16)],
        scratch_shapes=[pltpu.VMEM((tb, qw), BF16)],
        compiler_params=_cparams(("arbitrary",)),
        name="peer_route",
    )(h, w_q, k1, k2)


def _dense_kernel(ht_ref, x_ref, gt_ref, u_ref, vt_ref, cnt_ref, c1_ref, rk_ref, e2_ref, o_ref,
                  acc_ref, w_scr):
    eb = pl.program_id(1)
    tb = ht_ref.shape[1]
    n_exp = u_ref.shape[0]
    n_sub = n_exp // PEER_SUB
    i_per_sub = PEER_SUB // PEER_NKEYS

    @pl.when(eb == 0)
    def _():
        acc_ref[...] = jnp.zeros_like(acc_ref)

    def scores(sb):
        return _dot(u_ref[sb * PEER_SUB:(sb + 1) * PEER_SUB, :], ht_ref[...])

    pk_rows = 2 * SUBLANES
    n_pk = PEER_NKEYS // pk_rows

    def row_tile(ref, hd, i):
        words = jnp.broadcast_to(ref[hd, pl.ds(i, 1), :], (SUBLANES, tb))
        return pltpu.bitcast(words, BF16)[None]

    a_next = scores(0)
    for sb in range(n_sub):
        a_cur = a_next
        if sb + 1 < n_sub:
            a_next = scores(sb + 1)
        for il in range(i_per_sub):
            i = eb * (n_exp // PEER_NKEYS) + sb * i_per_sub + il
            gsum = jnp.zeros((n_pk, pk_rows, tb), BF16)
            for hd in range(PEER_HEADS):
                cnt = row_tile(cnt_ref, hd, i)
                c1 = row_tile(c1_ref, hd, i)
                rk = rk_ref[hd].reshape(n_pk, pk_rows, tb)
                e2 = e2_ref[hd].reshape(n_pk, pk_rows, tb)
                gsum = gsum + jnp.where(rk < cnt, e2 * c1, jnp.zeros((), BF16))
            r0 = sb * PEER_SUB + il * PEER_NKEYS
            a_i = a_cur[il * PEER_NKEYS:(il + 1) * PEER_NKEYS]
            half_gate = gsum.reshape(PEER_NKEYS, tb).astype(F32)
            w_i = a_i * (1.0 + lax.erf(a_i * (1.0 / math.sqrt(2.0)))) * half_gate
            w_scr[r0:r0 + PEER_NKEYS, :] = w_i.astype(BF16)
        done = (sb + 1) * PEER_SUB
        if done % PEER_ACC_CHUNK == 0:
            c0 = done - PEER_ACC_CHUNK
            acc_ref[...] += _dot(vt_ref[:, c0:done], w_scr[c0:done, :])

    @pl.when(eb == pl.num_programs(1) - 1)
    def _():
        o_ref[...] = x_ref[...] + gt_ref[...] * acc_ref[...].T


def _peer_dense(h_t, x, gate, u, v_t, cnt1, c1, rank2, e2):
    d, n = h_t.shape
    n_e = u.shape[0]
    tb = min(PEER_TOK_BLOCK, n)
    eb = PEER_EXP_BLOCK
    tab = pl.BlockSpec((PEER_HEADS, PEER_NKEYS, tb), lambda i, e: (0, 0, i))
    return pl.pallas_call(
        _dense_kernel,
        grid=(n // tb, n_e // eb),
        in_specs=[pl.BlockSpec((d, tb), lambda i, e: (0, i)),
                  pl.BlockSpec((tb, d), lambda i, e: (i, 0)),
                  pl.BlockSpec((1, d), lambda i, e: (0, 0)),
                  pl.BlockSpec((eb, d), lambda i, e: (e, 0)),
                  pl.BlockSpec((d, eb), lambda i, e: (0, e)),
                  tab, tab, tab, tab],
        out_specs=pl.BlockSpec((tb, d), lambda i, e: (i, 0)),
        out_shape=jax.ShapeDtypeStruct((n, d), F32),
        scratch_shapes=[pltpu.VMEM((d, tb), F32), pltpu.VMEM((eb, tb), BF16)],
        compiler_params=_cparams(("arbitrary", "arbitrary")),
        name="peer_dense",
    )(h_t, x, gate, u, v_t, cnt1, c1, rank2, e2)


def _final_norm_kernel(x_ref, g_ref, o_ref):
    x = x_ref[...]
    o_ref[...] = x * lax.rsqrt(jnp.mean(x * x, axis=-1, keepdims=True) + EPS) * g_ref[...]


def _final_norm(x, g):
    n, d = x.shape
    tm = min(2 * ROW_BLOCK, n)
    return pl.pallas_call(
        _final_norm_kernel,
        grid=(n // tm,),
        in_specs=[pl.BlockSpec((tm, d), lambda i: (i, 0)), pl.BlockSpec((1, d), lambda i: (0, 0))],
        out_specs=pl.BlockSpec((tm, d), lambda i: (i, 0)),
        out_shape=jax.ShapeDtypeStruct((n, d), F32),
        compiler_params=_cparams(("arbitrary",)),
        name="final_norm",
    )(x, g)


def kernel(x, c, ctx, c_ctx, w_mod, b_mod, g_mix, g_ffn, w_in, na_rpb, s5_lam_re, s5_lam_im,
           s5_b_re, s5_b_im, s5_c_re, s5_c_im, s5_log_step, s5_d, s5_w_glu, sw_sink, w_branch,
           w_out, peer_w_q, peer_sub_keys, peer_u, peer_v, g_final):
    batch, seq, d = x.shape
    l_ctx = ctx.shape[1]
    depth = w_mod.shape[0]
    assert batch == 1 and seq % (NA_ROWS * GRID_W) == 0 and l_ctx % (S5_CHUNK * SUBLANES) == 0
    assert seq // GRID_W >= NA_KROWS and seq >= 3 * SW_BLOCK

    cc = jnp.zeros((SUBLANES, d), F32).at[0].set(c[0]).at[1].set(c_ctx)
    mod = _mod_vectors(cc, w_mod, b_mod).reshape(depth, SUBLANES, 6, d)
    rope_tabs = _rope_tables(seq)
    row = lambda v: v.reshape(1, d)
    na_uniq = _na_patterns(seq // GRID_W)[2]
    na_bias = jax.vmap(lambda r: _na_bias(r, na_uniq, l_ctx))(na_rpb)
    s5_prep = jax.vmap(_s5_prepare)(s5_lam_re, s5_lam_im, s5_b_re, s5_b_im, s5_c_re, s5_c_im, s5_log_step)

    xx, xc = x[0], ctx[0]
    for l in range(depth):
        need_ctx = l < depth - 1
        m_lat, m_ctx = mod[l, 0], mod[l, 1]
        w_in_l = w_in[l].astype(BF16)
        g_mix_l = row(g_mix[l])
        g_ffn_l = row(g_ffn[l])

        qa, ka, va, ub, qs, ks, vs, ga, gb, gs = _inproj(
            xx, g_mix_l, row(m_lat[0]), row(m_lat[1]), w_in_l, rope_tabs)
        qa_c, ka_c, va_c, ub_c, qs_c, ks_c, vs_c, ga_c, gb_c, gs_c = _inproj(
            xc, g_mix_l, row(m_ctx[0]), row(m_ctx[1]), w_in_l, None)

        ya = _na_attention(qa, ka, va, ka_c, va_c, na_bias[l])
        ys = _sw_attention(qs, ks, vs, ks_c, vs_c, sw_sink[l])
        y5 = _s5_mixer_pre_glu(ub_c, ub, [a[l] for a in s5_prep], s5_d[l])

        w_glu_l = s5_w_glu[l].astype(BF16)
        w_branch_l = w_branch[l].astype(BF16)
        w_out_l = w_out[l].astype(BF16)
        xx, hx2, hx2_t = _merge(xx, ya, y5, l_ctx, ys, ga, gb, gs, w_glu_l, w_branch_l, w_out_l,
                                row(m_lat[2]), g_ffn_l, row(m_lat[3]), row(m_lat[4]))

        w_q_l = peer_w_q[l].astype(BF16)
        k1 = peer_sub_keys[l, 0].astype(BF16)
        k2 = peer_sub_keys[l, 1].astype(BF16)
        u_l = peer_u[l].astype(BF16)
        vt_l = peer_v[l].astype(BF16).T
        if need_ctx:
            ya_c = _ctx_attention(qa_c, ka_c, va_c, None)
            ys_c = _ctx_attention(qs_c, ks_c, vs_c, sw_sink[l])
            xc, hc2, hc2_t = _merge(xc, ya_c, y5, 0, ys_c, ga_c, gb_c, gs_c, w_glu_l, w_branch_l, w_out_l,
                                    row(m_ctx[2]), g_ffn_l, row(m_ctx[3]), row(m_ctx[4]))
            xc = _peer_dense(hc2_t, xc, row(m_ctx[5]), u_l, vt_l, *_peer_route(hc2, w_q_l, k1, k2))
        xx = _peer_dense(hx2_t, xx, row(m_lat[5]), u_l, vt_l, *_peer_route(hx2, w_q_l, k1, k2))

    return _final_norm(xx, row(g_final))[None]
```

```python
import functools
import math

import numpy as np
import jax
import jax.numpy as jnp
from jax import lax
from jax.experimental import pallas as pl
from jax.experimental.pallas import tpu as pltpu

F32 = jnp.float32
BF16 = jnp.bfloat16

GRID_W = 64
HEAD_DIM = 64
NA_WIN_H = 8
NA_WIN_W = 16
S5_GROUP = 16
S5_STATE = 64
SW_WINDOW = 128
ROPE_BASE = 10000.0
PEER_HEADS = 8
PEER_NKEYS = 128
PEER_QDIM = 256
PEER_TOPK = 16
EPS = 1e-6
NEG_INF = -1e30

LANES = 128
SUBLANES = 8
VMEM_LIMIT_BYTES = 56 * 1024 * 1024

ROW_BLOCK = 256
NA_ROWS = 4
NA_KROWS = NA_ROWS + NA_WIN_H - 1
NA_LANES = 256
SW_BLOCK = 128
S5_CHUNK = 16
S5_LG = LANES // S5_GROUP
PEER_ROUTE_BLOCK = 256
PEER_TOK_BLOCK = 512
PEER_EXP_BLOCK = 1024
PEER_SUB = 256
PEER_ACC_CHUNK = 512


def _cparams(sem):
    return pltpu.CompilerParams(dimension_semantics=sem, vmem_limit_bytes=VMEM_LIMIT_BYTES)


def _dot(a, b):
    return jnp.dot(a, b, preferred_element_type=F32)


def _dot_nt(a, b):
    return lax.dot_general(a, b, (((1,), (1,)), ((), ())), preferred_element_type=F32)


def _gelu(x):
    return 0.5 * x * (1.0 + lax.erf(x * (1.0 / math.sqrt(2.0))))


def _mod_kernel(cc_ref, w_ref, b_ref, o_ref):
    a = cc_ref[...]
    a = a * jax.nn.sigmoid(a)
    o_ref[0] = _dot(a.astype(BF16), w_ref[0].astype(BF16)) + b_ref[0]


def _mod_vectors(cc, w_mod, b_mod):
    depth, d, n6 = w_mod.shape
    tn = 1024
    return pl.pallas_call(
        _mod_kernel,
        grid=(depth, n6 // tn),
        in_specs=[pl.BlockSpec((SUBLANES, d), lambda l, j: (0, 0)),
                  pl.BlockSpec((1, d, tn), lambda l, j: (l, 0, j)),
                  pl.BlockSpec((1, 1, tn), lambda l, j: (l, 0, j))],
        out_specs=pl.BlockSpec((1, SUBLANES, tn), lambda l, j: (l, 0, j)),
        out_shape=jax.ShapeDtypeStruct((depth, SUBLANES, n6), F32),
        compiler_params=_cparams(("arbitrary", "arbitrary")),
        name="mod_vectors",
    )(cc, w_mod, b_mod.reshape(depth, 1, n6))


def _norm_mod(x, g, shift, scale):
    y = x * lax.rsqrt(jnp.mean(x * x, axis=-1, keepdims=True) + EPS)
    return (y * g) * (1.0 + scale) + shift


def _rope(x, cos, sin, lane_lo):
    up = pltpu.roll(x, LANES - 16, 1)
    dn = pltpu.roll(x, 16, 1)
    return x * cos + jnp.where(lane_lo, up, dn) * sin


def _inproj_kernel(x_ref, g_ref, sh_ref, sc_ref, w_ref, *rest, bw, kvw, d_model, rope):
    if rope:
        cos_ref, sin_ref = rest[:2]
        rest = rest[2:]
    qa_ref, ka_ref, va_ref, ub_ref, qs_ref, ks_ref, vs_ref, ga_ref, gb_ref, gs_ref = rest
    h = _norm_mod(x_ref[...], g_ref[...], sh_ref[...], sc_ref[...]).astype(BF16)

    def proj(c0, width):
        return _dot(h, w_ref[:, c0:c0 + width])

    c = 0
    qa_ref[...] = proj(c, bw).astype(BF16); c += bw
    ka_ref[...] = proj(c, bw).astype(BF16); c += bw
    va_ref[...] = proj(c, bw).astype(BF16); c += bw
    ub = proj(c, bw); c += bw
    for t in range(bw // LANES):
        ub_ref[t] = ub[:, t * LANES:(t + 1) * LANES]
    qs = proj(c, bw); c += bw
    ks = proj(c, kvw); c += kvw
    vs_ref[...] = proj(c, kvw).astype(BF16); c += kvw
    if rope:
        cos = cos_ref[...]
        sin = sin_ref[...]
        lane = lax.broadcasted_iota(jnp.int32, cos.shape, 1)
        lane_lo = (lane % 32) < 16
        for p in range(bw // LANES):
            sl = slice(p * LANES, (p + 1) * LANES)
            qs_ref[:, sl] = _rope(qs[:, sl], cos, sin, lane_lo).astype(BF16)
        for p in range(kvw // LANES):
            sl = slice(p * LANES, (p + 1) * LANES)
            ks_ref[:, sl] = _rope(ks[:, sl], cos, sin, lane_lo).astype(BF16)
    else:
        qs_ref[...] = qs.astype(BF16)
        ks_ref[...] = ks.astype(BF16)
    ga_ref[...] = jax.nn.sigmoid(proj(c, d_model)).astype(BF16); c += d_model
    gb_ref[...] = jax.nn.sigmoid(proj(c, d_model)).astype(BF16); c += d_model
    gs_ref[...] = jax.nn.sigmoid(proj(c, d_model)).astype(BF16)


def _inproj(x, g, shift, scale, w_in, rope_tabs):
    n, d = x.shape
    bw = d // 2
    kvw = bw // 4
    tm = min(ROW_BLOCK, n)
    rope = rope_tabs is not None
    row = lambda i: (i, 0)
    fixed = lambda i: (0, 0)
    in_specs = [pl.BlockSpec((tm, d), row), pl.BlockSpec((1, d), fixed), pl.BlockSpec((1, d), fixed),
                pl.BlockSpec((1, d), fixed), pl.BlockSpec(w_in.shape, fixed)]
    args = [x, g, shift, scale, w_in]
    if rope:
        in_specs += [pl.BlockSpec((tm, LANES), row)] * 2
        args += list(rope_tabs)
    widths = [bw, bw, bw, bw, bw, kvw, kvw, d, d, d]
    dtypes = [BF16, BF16, BF16, F32, BF16, BF16, BF16, BF16, BF16, BF16]
    s5_slot = 3
    return pl.pallas_call(
        functools.partial(_inproj_kernel, bw=bw, kvw=kvw, d_model=d, rope=rope),
        grid=(n // tm,),
        in_specs=in_specs,
        out_specs=[pl.BlockSpec((bw // LANES, tm, LANES), lambda i: (0, i, 0)) if k == s5_slot
                   else pl.BlockSpec((tm, w), row) for k, w in enumerate(widths)],
        out_shape=[jax.ShapeDtypeStruct((bw // LANES, n, LANES) if k == s5_slot else (n, w), dt)
                   for k, (w, dt) in enumerate(zip(widths, dtypes))],
        compiler_params=_cparams(("arbitrary",)),
        name="inproj",
    )(*args)


def _rope_tables(seq):
    t = jnp.arange(seq)
    half = HEAD_DIM // 4
    inv = ROPE_BASE ** (-jnp.arange(half, dtype=F32) / half)
    sign = jnp.concatenate([-jnp.ones((half,), F32), jnp.ones((half,), F32)])

    def axis_tabs(pos):
        ang = pos.astype(F32)[:, None] * inv[None, :]
        c = jnp.cos(ang)
        s = jnp.sin(ang)
        return jnp.concatenate([c, c], -1), jnp.concatenate([s, s], -1) * sign

    cr, sr = axis_tabs(t // GRID_W)
    cc, sc = axis_tabs(t % GRID_W)
    cos = jnp.concatenate([cr, cc], -1)
    sin = jnp.concatenate([sr, sc], -1)
    return jnp.tile(cos, (1, LANES // HEAD_DIM)), jnp.tile(sin, (1, LANES // HEAD_DIM))


def _na_patterns(rows):
    kh = NA_WIN_H
    nb = rows // NA_ROWS
    kr0s, sigs = [], []
    for b in range(nb):
        r_lo = b * NA_ROWS
        kr0 = int(np.clip(r_lo - kh // 2, 0, rows - NA_KROWS))
        sig = tuple((int(np.clip(r - kh // 2, 0, rows - kh)) - kr0, r - kr0)
                    for r in range(r_lo, r_lo + NA_ROWS))
        kr0s.append(kr0)
        sigs.append(sig)
    uniq = sorted(set(sigs))
    ids = np.array([uniq.index(s) for s in sigs], np.int32)
    return np.array(kr0s, np.int32), ids, uniq


def _na_bias(rpb, uniq, n_ctx):
    kw = NA_WIN_W
    cols = np.arange(GRID_W)
    col_start = np.clip(cols - kw // 2, 0, GRID_W - kw)
    n_pat = len(uniq)
    kk = np.arange(NA_KROWS)
    col_sel = (cols[None, None, :] - cols[None, :, None] + kw - 1
               == np.arange(2 * kw - 1)[:, None, None])
    col_ok = ((cols[None, :] >= col_start[:, None]) & (cols[None, :] < col_start[:, None] + kw))
    row_sel = np.zeros((n_pat, NA_ROWS, NA_KROWS, 2 * NA_WIN_H - 1), bool)
    row_ok = np.zeros((n_pat, NA_ROWS, NA_KROWS), bool)
    for p, sig in enumerate(uniq):
        for rq, (r0_rel, r_rel) in enumerate(sig):
            row_ok[p, rq] = (kk >= r0_rel) & (kk < r0_rel + NA_WIN_H)
            row_sel[p, rq] = (kk[:, None] - r_rel + NA_WIN_H - 1) == np.arange(2 * NA_WIN_H - 1)[None, :]
    hp = lax.Precision.HIGHEST
    by_col = jnp.einsum('hrv,vcj->hrcj', rpb.astype(F32), jnp.asarray(col_sel, F32), precision=hp)
    bias = jnp.einsum('pqkr,hrcj->phqckj', jnp.asarray(row_sel, F32), by_col, precision=hp)
    ok = row_ok[:, None, :, None, :, None] & col_ok[None, None, None, :, None, :]
    bias = jnp.where(jnp.asarray(ok), bias, NEG_INF)
    bias = bias.reshape(n_pat, rpb.shape[0], NA_ROWS * GRID_W, NA_KROWS * GRID_W)
    return jnp.concatenate([bias, jnp.zeros(bias.shape[:3] + (n_ctx,), F32)], axis=-1)


def _na_kernel(kr0_ref, pat_ref, q_ref, k_ref, v_ref, kc_ref, vc_ref, bias_ref, o_ref):
    b = pl.program_id(1)
    start = pl.multiple_of(kr0_ref[b] * GRID_W, GRID_W)
    nk = NA_KROWS * GRID_W
    scale = HEAD_DIM ** -0.5
    one = jnp.ones((), BF16)
    hp = LANES // HEAD_DIM
    for lt in range(q_ref.shape[1] // LANES):
        lanes = slice(lt * LANES, (lt + 1) * LANES)
        q = q_ref[:, lanes]
        k_all = jnp.concatenate([k_ref[pl.ds(start, nk), lanes], kc_ref[:, lanes]], axis=0)
        v_all = jnp.concatenate([v_ref[pl.ds(start, nk), lanes], vc_ref[:, lanes]], axis=0)
        lane = lax.broadcasted_iota(jnp.int32, v_all.shape, 1)
        ext = []
        for h in range(hp):
            sl = slice(h * HEAD_DIM, (h + 1) * HEAD_DIM)
            own = (lane >= h * HEAD_DIM) & (lane < (h + 1) * HEAD_DIM)
            s = _dot_nt(q[:, sl], k_all[:, sl]) * scale + bias_ref[0, lt * hp + h]
            p = jnp.exp(s - jnp.max(s, axis=-1, keepdims=True)).astype(BF16)
            ext.append(_dot(p, jnp.where(own, v_all, one)))
        out_lane = lax.broadcasted_iota(jnp.int32, ext[0].shape, 1)
        low = out_lane < HEAD_DIM
        num = jnp.where(low, ext[0], ext[1])
        den = pltpu.roll(jnp.where(low, ext[1], ext[0]), HEAD_DIM, 1)
        o_ref[:, lanes] = (num / den).astype(o_ref.dtype)


def _na_attention(q, k, v, kc, vc, bias):
    s, bw = q.shape
    l = kc.shape[0]
    rows = s // GRID_W
    kr0s, ids, _ = _na_patterns(rows)
    tq = NA_ROWS * GRID_W
    nk = NA_KROWS * GRID_W
    hp = NA_LANES // HEAD_DIM
    grid_spec = pltpu.PrefetchScalarGridSpec(
        num_scalar_prefetch=2,
        grid=(bw // NA_LANES, rows // NA_ROWS),
        in_specs=[pl.BlockSpec((tq, NA_LANES), lambda p, b, kr, pt: (b, p)),
                  pl.BlockSpec((s, NA_LANES), lambda p, b, kr, pt: (0, p)),
                  pl.BlockSpec((s, NA_LANES), lambda p, b, kr, pt: (0, p)),
                  pl.BlockSpec((l, NA_LANES), lambda p, b, kr, pt: (0, p)),
                  pl.BlockSpec((l, NA_LANES), lambda p, b, kr, pt: (0, p)),
                  pl.BlockSpec((1, hp, tq, nk + l), lambda p, b, kr, pt: (pt[b], p, 0, 0))],
        out_specs=pl.BlockSpec((tq, NA_LANES), lambda p, b, kr, pt: (b, p)),
    )
    return pl.pallas_call(
        _na_kernel,
        grid_spec=grid_spec,
        out_shape=jax.ShapeDtypeStruct((s, bw), BF16),
        compiler_params=_cparams(("arbitrary", "arbitrary")),
        name="na_attention",
    )(jnp.asarray(kr0s), jnp.asarray(ids), q, k, v, kc, vc, bias)


def _ctx_attn_kernel(*refs, n_heads, n_rep, has_sink):
    if has_sink:
        sink_ref, q_ref, k_ref, v_ref, o_ref = refs
    else:
        q_ref, k_ref, v_ref, o_ref = refs
    scale = HEAD_DIM ** -0.5
    q = q_ref[...]
    k = k_ref[...]
    v = v_ref[...]
    outs = []
    for h in range(n_heads):
        g = h // n_rep
        sl = slice(h * HEAD_DIM, (h + 1) * HEAD_DIM)
        gl = slice(g * HEAD_DIM, (g + 1) * HEAD_DIM)
        s = _dot_nt(q[:, sl], k[:, gl]) * scale
        m = jnp.max(s, axis=-1, keepdims=True)
        if has_sink:
            m = jnp.maximum(m, sink_ref[h])
        p = jnp.exp(s - m)
        den = jnp.sum(p, axis=-1, keepdims=True)
        if has_sink:
            den = den + jnp.exp(sink_ref[h] - m)
        outs.append(_dot(p.astype(BF16), v[:, gl]) / den)
    o_ref[...] = jnp.concatenate(outs, axis=-1).astype(o_ref.dtype)


def _ctx_attention(q, k, v, sink):
    l, bw = q.shape
    n_heads = bw // HEAD_DIM
    n_rep = n_heads // (k.shape[1] // HEAD_DIM)
    has_sink = sink is not None
    full = lambda a: pl.BlockSpec(a.shape, lambda i: (0, 0))
    in_specs = [full(q), full(k), full(v)]
    args = [q, k, v]
    if has_sink:
        in_specs = [pl.BlockSpec(memory_space=pltpu.SMEM)] + in_specs
        args = [sink] + args
    return pl.pallas_call(
        functools.partial(_ctx_attn_kernel, n_heads=n_heads, n_rep=n_rep, has_sink=has_sink),
        grid=(1,),
        in_specs=in_specs,
        out_specs=pl.BlockSpec((l, bw), lambda i: (0, 0)),
        out_shape=jax.ShapeDtypeStruct((l, bw), BF16),
        compiler_params=_cparams(("arbitrary",)),
        name="ctx_attention",
    )(*args)


def _swa_kernel(sink_ref, q_ref, k_ref, v_ref, kc_ref, vc_ref, o_ref, *, seq, n_kv, n_rep):
    n = pl.program_id(0)
    nk = 3 * SW_BLOCK
    scale = HEAD_DIM ** -0.5
    ws = jnp.clip((n - 1) * SW_BLOCK, 0, seq - nk)
    ws = pl.multiple_of(ws, SW_BLOCK)
    q = q_ref[...]
    n_all = nk + kc_ref.shape[0]
    k_all = jnp.concatenate([k_ref[pl.ds(ws, nk), :], kc_ref[...]], axis=0)
    v_all = jnp.concatenate([v_ref[pl.ds(ws, nk), :], vc_ref[...]], axis=0)
    qpos = n * SW_BLOCK + lax.broadcasted_iota(jnp.int32, (SW_BLOCK, n_all), 0)
    col = lax.broadcasted_iota(jnp.int32, (SW_BLOCK, n_all), 1)
    visible = (col >= nk) | (jnp.abs(ws + col - qpos) <= SW_WINDOW)
    maskb = jnp.where(visible, 0.0, NEG_INF).astype(F32)
    lane = lax.broadcasted_iota(jnp.int32, v_all.shape, 1)
    one = jnp.ones((), BF16)
    outs = [None] * (n_kv * n_rep)
    for g in range(n_kv):
        gl = slice(g * HEAD_DIM, (g + 1) * HEAD_DIM)
        own = (lane >= g * HEAD_DIM) & (lane < (g + 1) * HEAD_DIM)
        qg = jnp.concatenate([q[:, (g * n_rep + r) * HEAD_DIM:(g * n_rep + r + 1) * HEAD_DIM]
                              for r in range(n_rep)], axis=0)
        s = _dot_nt(qg, k_all[:, gl]) * scale
        s = (s.reshape(n_rep, SW_BLOCK, n_all) + maskb[None]).reshape(n_rep * SW_BLOCK, n_all)
        sink = jnp.concatenate([jnp.full((SW_BLOCK, 1), sink_ref[g * n_rep + r], F32)
                                for r in range(n_rep)], axis=0)
        m = jnp.maximum(jnp.max(s, axis=-1, keepdims=True), sink)
        p = jnp.exp(s - m).astype(BF16)
        ext = _dot(p, jnp.where(own, v_all, one))
        den = pltpu.roll(ext, HEAD_DIM, 1) + jnp.exp(sink - m)
        o = ext / den
        for r in range(n_rep):
            outs[g * n_rep + r] = o[r * SW_BLOCK:(r + 1) * SW_BLOCK, gl]
    o_ref[...] = jnp.concatenate(outs, axis=-1).astype(o_ref.dtype)


def _sw_attention(q, k, v, kc, vc, sink):
    s, bw = q.shape
    l = kc.shape[0]
    kvw = k.shape[1]
    n_kv = kvw // HEAD_DIM
    n_rep = (bw // HEAD_DIM) // n_kv
    assert kvw == LANES
    return pl.pallas_call(
        functools.partial(_swa_kernel, seq=s, n_kv=n_kv, n_rep=n_rep),
        grid=(s // SW_BLOCK,),
        in_specs=[pl.BlockSpec(memory_space=pltpu.SMEM),
                  pl.BlockSpec((SW_BLOCK, bw), lambda n: (n, 0)),
                  pl.BlockSpec((s, kvw), lambda n: (0, 0)),
                  pl.BlockSpec((s, kvw), lambda n: (0, 0)),
                  pl.BlockSpec((l, kvw), lambda n: (0, 0)),
                  pl.BlockSpec((l, kvw), lambda n: (0, 0))],
        out_specs=pl.BlockSpec((SW_BLOCK, bw), lambda n: (n, 0)),
        out_shape=jax.ShapeDtypeStruct((s, bw), BF16),
        compiler_params=_cparams(("arbitrary",)),
        name="sw_attention",
    )(sink, q, k, v, kc, vc)


def _s5_prepare(lam_re, lam_im, b_re, b_im, c_re, c_im, log_step):
    hp = lax.Precision.HIGHEST
    tc = S5_CHUNK
    lam = lax.complex(lam_re.astype(F32), lam_im.astype(F32))
    dt = jnp.exp(log_step.astype(F32))[..., None]
    lam_dt = lam * dt
    lam_bar = jnp.exp(lam_dt)
    b_bar = ((lam_bar - 1.0) / lam)[..., None] * lax.complex(b_re.astype(F32), b_im.astype(F32))
    c_mat = lax.complex(c_re.astype(F32), c_im.astype(F32))
    kk = jnp.arange(tc + 1, dtype=F32)
    pw = jnp.exp(lam_dt[None] * kk[:, None, None, None])
    n_g, n_p, n_h = b_bar.shape[1:]

    cw = c_mat[None] * pw[:tc, :, :, None, :]
    kern = (jnp.einsum('kdghp,dgpj->kdghj', cw.real, b_bar.real, precision=hp)
            - jnp.einsum('kdghp,dgpj->kdghj', cw.imag, b_bar.imag, precision=hp))
    t_i = np.arange(tc)
    lag_f = t_i[None, :] - t_i[:, None]
    m_f = jnp.where((lag_f >= 0)[:, :, None, None, None],
                    kern[np.clip(lag_f, 0, tc - 1), 0], 0.0)
    m_r = jnp.where((lag_f <= 0)[:, :, None, None, None],
                    kern[np.clip(-lag_f, 0, tc - 1), 1], 0.0)
    m_intra = jnp.transpose(m_f + m_r, (2, 0, 4, 1, 3)).reshape(n_g, tc * n_h, tc * n_h)

    def cat_ri(z):
        return jnp.concatenate([z.real, z.imag], axis=-1)

    w_f = pw[tc - 1 - t_i, 0][:, :, None, :] * jnp.transpose(b_bar[0], (0, 2, 1))[None]
    w_r = pw[t_i, 1][:, :, None, :] * jnp.transpose(b_bar[1], (0, 2, 1))[None]
    w_f = jnp.transpose(cat_ri(w_f), (1, 0, 2, 3)).reshape(n_g, tc * n_h, 2 * n_p)
    w_r = jnp.transpose(cat_ri(w_r), (1, 0, 2, 3)).reshape(n_g, tc * n_h, 2 * n_p)
    m_cat = jnp.concatenate([m_intra, w_f, w_r], axis=-1).astype(BF16)

    def v_of(c_dir, pw_sel):
        cp = c_dir[None] * pw_sel[:, :, None, :]
        v = jnp.concatenate([cp.real, -cp.imag], axis=-1)
        return jnp.transpose(v, (1, 3, 0, 2)).reshape(n_g, 2 * n_p, tc * n_h)
    v_cat = jnp.concatenate([v_of(c_mat[0], pw[t_i + 1, 0]), v_of(c_mat[1], pw[tc - t_i, 1])],
                            axis=1).astype(BF16)

    def coef(k):
        a = jnp.exp(lam_dt * (tc * k))
        return jnp.stack([jnp.concatenate([a.real, a.real], -1),
                          jnp.concatenate([-a.imag, a.imag], -1)], axis=2)
    a_log = jnp.stack([coef(1.0), coef(2.0), coef(4.0)], axis=2)
    a_car = jnp.stack([coef(float(j)) for j in range(1, SUBLANES + 1)], axis=3)
    a_car = jnp.stack([a_car[0], a_car[1][:, :, ::-1]], axis=0)
    return _s5_expand(m_cat, v_cat, jnp.transpose(a_log, (1, 0, 2, 3, 4)), jnp.transpose(a_car, (1, 0, 2, 3, 4)))


def _s5_expand(m_cat, v_cat, a_log, a_car):
    n_g = m_cat.shape[0]
    n_t = n_g // S5_LG
    n_state = 2 * S5_STATE

    def lanes_of(a):
        a = a.reshape((n_t, S5_LG) + a.shape[1:])
        a = jnp.moveaxis(a, 1, -2)
        return a.reshape(a.shape[:-2] + (S5_LG * n_state,))
    return (m_cat.reshape((n_t, S5_LG) + m_cat.shape[1:]), v_cat.reshape((n_t, S5_LG) + v_cat.shape[1:]),
            lanes_of(a_log), lanes_of(a_car))


def _s5_lane_perm():
    r = np.arange(S5_CHUNK * LANES)
    t, a, h = r // LANES, (r % LANES) // S5_GROUP, r % S5_GROUP
    dest = a * (S5_CHUNK * S5_GROUP) + t * S5_GROUP + h
    return (jnp.asarray(dest)[:, None] == jnp.arange(S5_CHUNK * LANES)[None, :]).astype(BF16)


def _s5_in_kernel(u_ref, perm_ref, m_ref, yp_ref, z_ref):
    width = S5_CHUNK * S5_GROUP
    n_state = 2 * S5_STATE
    half = S5_LG * n_state
    up = _dot(u_ref[0].astype(BF16), perm_ref[...]).astype(BF16)
    for a in range(S5_LG):
        zz = _dot(up[:, a * width:(a + 1) * width], m_ref[0, a])
        yp_ref[0, :, a * width:(a + 1) * width] = zz[:, :width]
        z_ref[0, :, a * n_state:(a + 1) * n_state] = zz[:, width:width + n_state]
        z_ref[0, :, half + a * n_state:half + (a + 1) * n_state] = zz[:, width + n_state:]


def _s5_scan_kernel(z_ref, alog_ref, acar_ref, e_ref, *, nc, nc_ctx):
    n_state = 2 * S5_STATE
    half = z_ref.shape[2] // 2
    nblk = nc // SUBLANES
    nblk_ctx = nc_ctx // SUBLANES
    row = lax.broadcasted_iota(jnp.int32, (SUBLANES, n_state), 0)

    def cmul(a1, a2, s):
        return a1 * s + a2 * pltpu.roll(s, S5_STATE, 1)

    def step(t, carry):
        blk = (t, jnp.where(t < nblk_ctx, nblk_ctx - 1 - t, nblk + nblk_ctx - 1 - t))
        new = []
        for d in range(2):
            r0 = pl.multiple_of(blk[d] * SUBLANES, SUBLANES)
            for j in range(half // n_state):
                tile = slice(j * n_state, (j + 1) * n_state)
                lanes = slice(d * half + j * n_state, d * half + (j + 1) * n_state)
                z = z_ref[0, pl.ds(r0, SUBLANES), lanes]
                c = carry[len(new)]
                for i, sh in enumerate((1, 2, 4)):
                    if d == 0:
                        zs = jnp.where(row >= sh, pltpu.roll(z, sh, 0), 0.0)
                    else:
                        zs = jnp.where(row < SUBLANES - sh, pltpu.roll(z, SUBLANES - sh, 0), 0.0)
                    z = z + cmul(alog_ref[0, d, i, 0:1, tile], alog_ref[0, d, i, 1:2, tile], zs)
                cb = jnp.broadcast_to(c, (SUBLANES, n_state))
                s = z + cmul(acar_ref[0, d, 0, :, tile], acar_ref[0, d, 1, :, tile], cb)
                if d == 0:
                    e = jnp.where(row >= 1, pltpu.roll(s, 1, 0), cb)
                    c_new = s[SUBLANES - 1:SUBLANES, :]
                else:
                    e = jnp.where(row < SUBLANES - 1, pltpu.roll(s, SUBLANES - 1, 0), cb)
                    c_new = s[0:1, :]
                e_ref[0, pl.ds(r0, SUBLANES), lanes] = e
                new.append(c_new)
        return tuple(new)

    init = tuple(jnp.zeros((1, n_state), F32) for _ in range(2 * (half // n_state)))
    lax.fori_loop(0, nblk, step, init)


def _s5_out_kernel(e_ref, yp_ref, u_ref, d_ref, perm_ref, v_ref, y_ref):
    width = S5_CHUNK * S5_GROUP
    n_state = 2 * S5_STATE
    half = S5_LG * n_state
    parts = []
    for a in range(S5_LG):
        e_a = jnp.concatenate([e_ref[0, :, a * n_state:(a + 1) * n_state],
                               e_ref[0, :, half + a * n_state:half + (a + 1) * n_state]], axis=-1)
        parts.append(yp_ref[0, :, a * width:(a + 1) * width] + _dot(e_a.astype(BF16), v_ref[0, a]))
    y = jnp.concatenate(parts, axis=-1)
    hi = y.astype(BF16)
    lo = (y - hi.astype(F32)).astype(BF16)
    perm = perm_ref[...]
    y_ref[0] = _dot_nt(hi, perm) + _dot_nt(lo, perm) + u_ref[0] * d_ref[0]


def _s5_rows(nc):
    return max(r for r in range(2 * SUBLANES, 257, 2 * SUBLANES) if nc % r == 0)


def _s5_mixer_pre_glu(u_ctx, u_lat, prep, d_skip):
    m_cat, v_cat, a_log, a_car = prep
    n_t, l, _ = u_ctx.shape
    s = u_lat.shape[1]
    nc, nc_ctx = (l + s) // S5_CHUNK, l // S5_CHUNK
    width = S5_CHUNK * LANES
    n_z = 2 * S5_LG * 2 * S5_STATE
    u = jnp.concatenate([u_ctx, u_lat], axis=1).reshape(n_t, nc, width)
    d_rows = jnp.tile(d_skip.astype(F32).reshape(n_t, 1, LANES), (1, 1, S5_CHUNK))
    perm = _s5_lane_perm()
    rb = _s5_rows(nc)
    rows = lambda w: pl.BlockSpec((1, rb, w), lambda g, r: (g, r, 0))
    per_tile = lambda a: pl.BlockSpec((1,) + a.shape[1:], lambda g, *_: (g,) + (0,) * (a.ndim - 1))
    whole = lambda a: pl.BlockSpec(a.shape, lambda g, r: (0,) * a.ndim)
    yp, z = pl.pallas_call(
        _s5_in_kernel,
        grid=(n_t, nc // rb),
        in_specs=[rows(width), whole(perm), per_tile(m_cat)],
        out_specs=[rows(width), rows(n_z)],
        out_shape=[jax.ShapeDtypeStruct((n_t, nc, width), F32), jax.ShapeDtypeStruct((n_t, nc, n_z), F32)],
        compiler_params=_cparams(("arbitrary", "arbitrary")),
        name="s5_in",
    )(u, perm, m_cat)
    e = pl.pallas_call(
        functools.partial(_s5_scan_kernel, nc=nc, nc_ctx=nc_ctx),
        grid=(n_t,),
        in_specs=[per_tile(z), per_tile(a_log), per_tile(a_car)],
        out_specs=per_tile(z),
        out_shape=jax.ShapeDtypeStruct(z.shape, F32),
        compiler_params=_cparams(("arbitrary",)),
        name="s5_scan",
    )(z, a_log, a_car)
    y = pl.pallas_call(
        _s5_out_kernel,
        grid=(n_t, nc // rb),
        in_specs=[rows(n_z), rows(width), rows(width), per_tile(d_rows), whole(perm), per_tile(v_cat)],
        out_specs=rows(width),
        out_shape=jax.ShapeDtypeStruct((n_t, nc, width), F32),
        compiler_params=_cparams(("arbitrary", "arbitrary")),
        name="s5_out",
    )(e, yp, u, d_rows, perm, v_cat)
    return y.reshape(n_t, l + s, LANES)


def _merge_kernel(x_ref, ya_ref, y5_ref, ys_ref, ga_ref, gb_ref, gs_ref, wglu_ref, wb_ref, wo_ref,
                  gt_ref, g_ref, sh_ref, sc_ref, xo_ref, h_ref, ht_ref):
    y5 = jnp.concatenate([y5_ref[t] for t in range(y5_ref.shape[0])], axis=-1)
    z = _gelu(y5)
    yb = z * jax.nn.sigmoid(_dot(z.astype(BF16), wglu_ref[...]))
    m = (ga_ref[...].astype(F32) * _dot(ya_ref[...], wb_ref[0])
         + gb_ref[...].astype(F32) * _dot(yb.astype(BF16), wb_ref[1])
         + gs_ref[...].astype(F32) * _dot(ys_ref[...], wb_ref[2]))
    x = x_ref[...] + gt_ref[...] * _dot(m.astype(BF16), wo_ref[...])
    xo_ref[...] = x
    h = _norm_mod(x, g_ref[...], sh_ref[...], sc_ref[...])
    h_ref[...] = h.astype(BF16)
    ht_ref[...] = h.T.astype(BF16)


def _merge(x, ya, y5, y5_row0, ys, ga, gb, gs, w_glu, w_branch, w_out, gate, g, shift, scale):
    n, d = x.shape
    bw = ya.shape[1]
    tm = min(ROW_BLOCK, n)
    assert y5_row0 % tm == 0
    row = lambda w: pl.BlockSpec((tm, w), lambda i: (i, 0))
    full = lambda a: pl.BlockSpec(a.shape, lambda i: (0,) * a.ndim)
    vec = pl.BlockSpec((1, d), lambda i: (0, 0))
    y5_rows = pl.BlockSpec((y5.shape[0], tm, LANES), lambda i: (0, i + y5_row0 // tm, 0))
    return pl.pallas_call(
        _merge_kernel,
        grid=(n // tm,),
        in_specs=[row(d), row(bw), y5_rows, row(bw), row(d), row(d), row(d),
                  full(w_glu), full(w_branch), full(w_out), vec, vec, vec, vec],
        out_specs=[row(d), row(d), pl.BlockSpec((d, tm), lambda i: (0, i))],
        out_shape=[jax.ShapeDtypeStruct((n, d), F32), jax.ShapeDtypeStruct((n, d), BF16),
                   jax.ShapeDtypeStruct((d, n), BF16)],
        compiler_params=_cparams(("arbitrary",)),
        name="merge",
    )(x, ya, y5, ys, ga, gb, gs, w_glu, w_branch, w_out, gate, g, shift, scale)


def _knock_out_16(s, order, exact):
    rank = jnp.full(s.shape, float(PEER_TOPK), F32)
    vals = []
    for r in range(PEER_TOPK):
        m = jnp.max(s, axis=0, keepdims=True)
        hit = s == m
        if exact:
            first = jnp.min(jnp.where(hit, order, float(PEER_TOPK * PEER_NKEYS)), axis=0, keepdims=True)
            hit = order == first
        rank = jnp.where(hit, float(r), rank)
        s = jnp.where(hit, -jnp.inf, s)
        vals.append(m)
    n_out = jnp.sum(jnp.where(rank < float(PEER_TOPK), 1.0, 0.0), axis=0, keepdims=True)
    return rank, jnp.concatenate(vals, axis=0), n_out


def _bf16_pair_word(x):
    hi = pltpu.bitcast(x.astype(BF16).astype(F32), jnp.uint32)
    return hi | (hi >> 16)


def _pair_tiles():
    tiles = [(0, 0, 8), (0, 8, 8)]
    for a in range(1, 8):
        tiles.append((a, 0, PEER_TOPK // (a + 1)))
    return tiles


def _route_kernel(h_ref, wq_ref, k1_ref, k2_ref, cnt_ref, c1_ref, rk_ref, e2_ref, q_scr):
    tb = h_ref.shape[0]
    q_scr[...] = _dot(h_ref[...], wq_ref[...]).astype(BF16)
    half = PEER_QDIM // 2
    iota = lax.broadcasted_iota(jnp.int32, (PEER_NKEYS, tb), 0).astype(F32)
    row8 = lax.broadcasted_iota(jnp.int32, (SUBLANES, tb), 0).astype(F32)
    k1 = k1_ref[...]
    k2 = k2_ref[...]

    def emit(hd, s1, s2, exact):
        rank1, v1, n1 = _knock_out_16(s1, iota, exact)
        rank2, v2, n2 = _knock_out_16(s2, iota, exact)
        e1v = jnp.exp(v1 - v1[0:1])
        e2v = jnp.exp(v2 - v2[0:1])
        tiles, flats, gates = [], [], []
        for a, b0, nv in _pair_tiles():
            c = v1[a:a + 1] + v2[b0:b0 + SUBLANES]
            tiles.append(jnp.where(row8 < nv, c, -jnp.inf))
            flats.append(a * PEER_TOPK + b0 + row8)
            gates.append(e1v[a:a + 1] * e2v[b0:b0 + SUBLANES])
        tiles.append(v1[SUBLANES:] + v2[0:1])
        flats.append((row8 + SUBLANES) * PEER_TOPK)
        gates.append(e1v[SUBLANES:] * e2v[0:1])
        cand = jnp.concatenate(tiles, axis=0)
        flat = jnp.concatenate(flats, axis=0)
        gate = jnp.concatenate(gates, axis=0)
        rank_c, _, n_c = _knock_out_16(cand, flat, exact)
        self_ = jnp.where(rank_c < float(PEER_TOPK), 1.0, 0.0)
        z = jnp.sum(self_ * gate, axis=0, keepdims=True)
        cnt = [self_[0:8].sum(axis=0, keepdims=True) + self_[8:16].sum(axis=0, keepdims=True)]
        for t in range(2, 9):
            cnt.append(self_[t * SUBLANES:(t + 1) * SUBLANES].sum(axis=0, keepdims=True))
        cnt = jnp.concatenate(cnt + [self_[9 * SUBLANES:]], axis=0)
        cnt1 = jnp.zeros((PEER_NKEYS, tb), F32)
        for a in range(PEER_TOPK):
            cnt1 = jnp.where(rank1 == float(a), cnt[a:a + 1], cnt1)
        cnt_ref[hd] = _bf16_pair_word(cnt1)
        c1_ref[hd] = _bf16_pair_word(jnp.exp(s1 - v1[0:1]) * (0.5 / z))
        rk_ref[hd] = rank2.astype(BF16)
        e2_ref[hd] = jnp.exp(s2 - v2[0:1]).astype(BF16)
        want = float(PEER_TOPK)
        return jnp.where((n1 != want) | (n2 != want) | (n_c != want), 1.0, 0.0)

    def head(hd, _):
        c0 = pl.multiple_of(hd * PEER_QDIM, PEER_QDIM)
        s1 = _dot_nt(k1, q_scr[:, pl.ds(c0, half)])
        s2 = _dot_nt(k2, q_scr[:, pl.ds(c0 + half, half)])
        tied = jnp.max(emit(hd, s1, s2, exact=False))

        @pl.when(tied > 0.0)
        def _():
            emit(hd, s1, s2, exact=True)
        return 0

    lax.fori_loop(0, PEER_HEADS, head, 0)


def _peer_route(h, w_q, k1, k2):
    n, d = h.shape
    tb = min(PEER_ROUTE_BLOCK, n)
    qw = w_q.shape[1]
    out_blk = pl.BlockSpec((PEER_HEADS, PEER_NKEYS, tb), lambda i: (0, 0, i))
    shp = lambda dt: jax.ShapeDtypeStruct((PEER_HEADS, PEER_NKEYS, n), dt)
    return pl.pallas_call(
        _route_kernel,
        grid=(n // tb,),
        in_specs=[pl.BlockSpec((tb, d), lambda i: (i, 0)),
                  pl.BlockSpec((d, qw), lambda i: (0, 0)),
                  pl.BlockSpec(k1.shape, lambda i: (0, 0)),
                  pl.BlockSpec(k2.shape, lambda i: (0, 0))],
        out_specs=[out_blk] * 4,
        out_shape=[shp(jnp.uint32), shp(jnp.uint32), shp(BF16), shp(BF16)],
        scratch_shapes=[pltpu.VMEM((tb, qw), BF16)],
        compiler_params=_cparams(("arbitrary",)),
        name="peer_route",
    )(h, w_q, k1, k2)


def _dense_kernel(ht_ref, x_ref, gt_ref, u_ref, vt_ref, cnt_ref, c1_ref, rk_ref, e2_ref, o_ref,
                  acc_ref, w_scr):
    eb = pl.program_id(1)
    tb = ht_ref.shape[1]
    n_exp = u_ref.shape[0]
    n_sub = n_exp // PEER_SUB
    i_per_sub = PEER_SUB // PEER_NKEYS

    @pl.when(eb == 0)
    def _():
        acc_ref[...] = jnp.zeros_like(acc_ref)

    def scores(sb):
        return _dot(u_ref[sb * PEER_SUB:(sb + 1) * PEER_SUB, :], ht_ref[...])

    pk_rows = 2 * SUBLANES
    n_pk = PEER_NKEYS // pk_rows

    def row_tile(ref, hd, i):
        words = jnp.broadcast_to(ref[hd, pl.ds(i, 1), :], (SUBLANES, tb))
        return pltpu.bitcast(words, BF16)[None]

    a_next = scores(0)
    for sb in range(n_sub):
        a_cur = a_next
        if sb + 1 < n_sub:
            a_next = scores(sb + 1)
        for il in range(i_per_sub):
            i = eb * (n_exp // PEER_NKEYS) + sb * i_per_sub + il
            gsum = jnp.zeros((n_pk, pk_rows, tb), BF16)
            for hd in range(PEER_HEADS):
                cnt = row_tile(cnt_ref, hd, i)
                c1 = row_tile(c1_ref, hd, i)
                rk = rk_ref[hd].reshape(n_pk, pk_rows, tb)
                e2 = e2_ref[hd].reshape(n_pk, pk_rows, tb)
                gsum = gsum + jnp.where(rk < cnt, e2 * c1, jnp.zeros((), BF16))
            r0 = sb * PEER_SUB + il * PEER_NKEYS
            a_i = a_cur[il * PEER_NKEYS:(il + 1) * PEER_NKEYS]
            half_gate = gsum.reshape(PEER_NKEYS, tb).astype(F32)
            w_i = a_i * (1.0 + lax.erf(a_i * (1.0 / math.sqrt(2.0)))) * half_gate
            w_scr[r0:r0 + PEER_NKEYS, :] = w_i.astype(BF16)
        done = (sb + 1) * PEER_SUB
        if done % PEER_ACC_CHUNK == 0:
            c0 = done - PEER_ACC_CHUNK
            acc_ref[...] += _dot(vt_ref[:, c0:done], w_scr[c0:done, :])

    @pl.when(eb == pl.num_programs(1) - 1)
    def _():
        o_ref[...] = x_ref[...] + gt_ref[...] * acc_ref[...].T


def _peer_dense(h_t, x, gate, u, v_t, cnt1, c1, rank2, e2):
    d, n = h_t.shape
    n_e = u.shape[0]
    tb = min(PEER_TOK_BLOCK, n)
    eb = PEER_EXP_BLOCK
    tab = pl.BlockSpec((PEER_HEADS, PEER_NKEYS, tb), lambda i, e: (0, 0, i))
    return pl.pallas_call(
        _dense_kernel,
        grid=(n // tb, n_e // eb),
        in_specs=[pl.BlockSpec((d, tb), lambda i, e: (0, i)),
                  pl.BlockSpec((tb, d), lambda i, e: (i, 0)),
                  pl.BlockSpec((1, d), lambda i, e: (0, 0)),
                  pl.BlockSpec((eb, d), lambda i, e: (e, 0)),
                  pl.BlockSpec((d, eb), lambda i, e: (0, e)),
                  tab, tab, tab, tab],
        out_specs=pl.BlockSpec((tb, d), lambda i, e: (i, 0)),
        out_shape=jax.ShapeDtypeStruct((n, d), F32),
        scratch_shapes=[pltpu.VMEM((d, tb), F32), pltpu.VMEM((eb, tb), BF16)],
        compiler_params=_cparams(("arbitrary", "arbitrary")),
        name="peer_dense",
    )(h_t, x, gate, u, v_t, cnt1, c1, rank2, e2)


def _final_norm_kernel(x_ref, g_ref, o_ref):
    x = x_ref[...]
    o_ref[...] = x * lax.rsqrt(jnp.mean(x * x, axis=-1, keepdims=True) + EPS) * g_ref[...]


def _final_norm(x, g):
    n, d = x.shape
    tm = min(2 * ROW_BLOCK, n)
    return pl.pallas_call(
        _final_norm_kernel,
        grid=(n // tm,),
        in_specs=[pl.BlockSpec((tm, d), lambda i: (i, 0)), pl.BlockSpec((1, d), lambda i: (0, 0))],
        out_specs=pl.BlockSpec((tm, d), lambda i: (i, 0)),
        out_shape=jax.ShapeDtypeStruct((n, d), F32),
        compiler_params=_cparams(("arbitrary",)),
        name="final_norm",
    )(x, g)


def kernel(x, c, ctx, c_ctx, w_mod, b_mod, g_mix, g_ffn, w_in, na_rpb, s5_lam_re, s5_lam_im,
           s5_b_re, s5_b_im, s5_c_re, s5_c_im, s5_log_step, s5_d, s5_w_glu, sw_sink, w_branch,
           w_out, peer_w_q, peer_sub_keys, peer_u, peer_v, g_final):
    batch, seq, d = x.shape
    l_ctx = ctx.shape[1]
    depth = w_mod.shape[0]
    assert batch == 1 and seq % (NA_ROWS * GRID_W) == 0 and l_ctx % (S5_CHUNK * SUBLANES) == 0
    assert seq // GRID_W >= NA_KROWS and seq >= 3 * SW_BLOCK

    cc = jnp.zeros((SUBLANES, d), F32).at[0].set(c[0]).at[1].set(c_ctx)
    mod = _mod_vectors(cc, w_mod, b_mod).reshape(depth, SUBLANES, 6, d)
    rope_tabs = _rope_tables(seq)
    row = lambda v: v.reshape(1, d)
    na_uniq = _na_patterns(seq // GRID_W)[2]
    na_bias = jax.vmap(lambda r: _na_bias(r, na_uniq, l_ctx))(na_rpb)
    s5_prep = jax.vmap(_s5_prepare)(s5_lam_re, s5_lam_im, s5_b_re, s5_b_im, s5_c_re, s5_c_im, s5_log_step)

    xx, xc = x[0], ctx[0]
    for l in range(depth):
        need_ctx = l < depth - 1
        m_lat, m_ctx = mod[l, 0], mod[l, 1]
        w_in_l = w_in[l].astype(BF16)
        g_mix_l = row(g_mix[l])
        g_ffn_l = row(g_ffn[l])

        qa, ka, va, ub, qs, ks, vs, ga, gb, gs = _inproj(
            xx, g_mix_l, row(m_lat[0]), row(m_lat[1]), w_in_l, rope_tabs)
        qa_c, ka_c, va_c, ub_c, qs_c, ks_c, vs_c, ga_c, gb_c, gs_c = _inproj(
            xc, g_mix_l, row(m_ctx[0]), row(m_ctx[1]), w_in_l, None)

        ya = _na_attention(qa, ka, va, ka_c, va_c, na_bias[l])
        ys = _sw_attention(qs, ks, vs, ks_c, vs_c, sw_sink[l])
        y5 = _s5_mixer_pre_glu(ub_c, ub, [a[l] for a in s5_prep], s5_d[l])

        w_glu_l = s5_w_glu[l].astype(BF16)
        w_branch_l = w_branch[l].astype(BF16)
        w_out_l = w_out[l].astype(BF16)
        xx, hx2, hx2_t = _merge(xx, ya, y5, l_ctx, ys, ga, gb, gs, w_glu_l, w_branch_l, w_out_l,
                                row(m_lat[2]), g_ffn_l, row(m_lat[3]), row(m_lat[4]))

        w_q_l = peer_w_q[l].astype(BF16)
        k1 = peer_sub_keys[l, 0].astype(BF16)
        k2 = peer_sub_keys[l, 1].astype(BF16)
        u_l = peer_u[l].astype(BF16)
        vt_l = peer_v[l].astype(BF16).T
        if need_ctx:
            ya_c = _ctx_attention(qa_c, ka_c, va_c, None)
            ys_c = _ctx_attention(qs_c, ks_c, vs_c, sw_sink[l])
            xc, hc2, hc2_t = _merge(xc, ya_c, y5, 0, ys_c, ga_c, gb_c, gs_c, w_glu_l, w_branch_l, w_out_l,
                                    row(m_ctx[2]), g_ffn_l, row(m_ctx[3]), row(m_ctx[4]))
            xc = _peer_dense(hc2_t, xc, row(m_ctx[5]), u_l, vt_l, *_peer_route(hc2, w_q_l, k1, k2))
        xx = _peer_dense(hx2_t, xx, row(m_lat[5]), u_l, vt_l, *_peer_route(hx2, w_q_l, k1, k2))

    return _final_norm(xx, row(g_final))[None]
```

```python
import functools
import math

import numpy as np
import jax
import jax.numpy as jnp
from jax import lax
from jax.experimental import pallas as pl
from jax.experimental.pallas import tpu as pltpu

F32 = jnp.float32
BF16 = jnp.bfloat16

GRID_W = 64
HEAD_DIM = 64
NA_WIN_H = 8
NA_WIN_W = 16
S5_GROUP = 16
S5_STATE = 64
SW_WINDOW = 128
ROPE_BASE = 10000.0
PEER_HEADS = 8
PEER_NKEYS = 128
PEER_QDIM = 256
PEER_TOPK = 16
EPS = 1e-6
NEG_INF = -1e30

LANES = 128
SUBLANES = 8
VMEM_LIMIT_BYTES = 56 * 1024 * 1024

ROW_BLOCK = 256
NA_ROWS = 4
NA_KROWS = NA_ROWS + NA_WIN_H - 1
NA_LANES = 256
SW_BLOCK = 128
S5_CHUNK = 16
S5_LG = LANES // S5_GROUP
PEER_ROUTE_BLOCK = 256
PEER_ROUTE_HEADS = 2
PEER_TOK_BLOCK = 512
PEER_EXP_BLOCK = 2048
PEER_SUB = 256
PEER_ACC_CHUNK = 1024


def _cparams(sem):
    return pltpu.CompilerParams(dimension_semantics=sem, vmem_limit_bytes=VMEM_LIMIT_BYTES)


def _dot(a, b):
    return jnp.dot(a, b, preferred_element_type=F32)


def _dot_nt(a, b):
    return lax.dot_general(a, b, (((1,), (1,)), ((), ())), preferred_element_type=F32)


def _gelu(x):
    return 0.5 * x * (1.0 + lax.erf(x * (1.0 / math.sqrt(2.0))))


def _mod_kernel(cc_ref, w_ref, b_ref, o_ref):
    a = cc_ref[...]
    a = a * jax.nn.sigmoid(a)
    o_ref[0] = _dot(a.astype(BF16), w_ref[0].astype(BF16)) + b_ref[0]


def _mod_vectors(cc, w_mod, b_mod):
    depth, d, n6 = w_mod.shape
    tn = 1024
    return pl.pallas_call(
        _mod_kernel,
        grid=(depth, n6 // tn),
        in_specs=[pl.BlockSpec((SUBLANES, d), lambda l, j: (0, 0)),
                  pl.BlockSpec((1, d, tn), lambda l, j: (l, 0, j)),
                  pl.BlockSpec((1, 1, tn), lambda l, j: (l, 0, j))],
        out_specs=pl.BlockSpec((1, SUBLANES, tn), lambda l, j: (l, 0, j)),
        out_shape=jax.ShapeDtypeStruct((depth, SUBLANES, n6), F32),
        compiler_params=_cparams(("arbitrary", "arbitrary")),
        name="mod_vectors",
    )(cc, w_mod, b_mod.reshape(depth, 1, n6))


def _norm_mod(x, g, shift, scale):
    y = x * lax.rsqrt(jnp.mean(x * x, axis=-1, keepdims=True) + EPS)
    return (y * g) * (1.0 + scale) + shift


def _rope(x, cos, sin, lane_lo):
    up = pltpu.roll(x, LANES - 16, 1)
    dn = pltpu.roll(x, 16, 1)
    return x * cos + jnp.where(lane_lo, up, dn) * sin


def _inproj_kernel(x_ref, g_ref, sh_ref, sc_ref, w_ref, *rest, bw, kvw, d_model, rope):
    if rope:
        cos_ref, sin_ref = rest[:2]
        rest = rest[2:]
    qa_ref, ka_ref, va_ref, ub_ref, qs_ref, ks_ref, vs_ref, ga_ref, gb_ref, gs_ref = rest
    h = _norm_mod(x_ref[...], g_ref[...], sh_ref[...], sc_ref[...]).astype(BF16)

    def proj(c0, width):
        return _dot(h, w_ref[:, c0:c0 + width])

    c = 0
    qa_ref[...] = proj(c, bw).astype(BF16); c += bw
    ka_ref[...] = proj(c, bw).astype(BF16); c += bw
    va_ref[...] = proj(c, bw).astype(BF16); c += bw
    ub = proj(c, bw); c += bw
    for t in range(bw // LANES):
        ub_ref[t] = ub[:, t * LANES:(t + 1) * LANES]
    qs = proj(c, bw); c += bw
    ks = proj(c, kvw); c += kvw
    vs_ref[...] = proj(c, kvw).astype(BF16); c += kvw
    if rope:
        cos = cos_ref[...]
        sin = sin_ref[...]
        lane = lax.broadcasted_iota(jnp.int32, cos.shape, 1)
        lane_lo = (lane % 32) < 16
        for p in range(bw // LANES):
            sl = slice(p * LANES, (p + 1) * LANES)
            qs_ref[:, sl] = _rope(qs[:, sl], cos, sin, lane_lo).astype(BF16)
        for p in range(kvw // LANES):
            sl = slice(p * LANES, (p + 1) * LANES)
            ks_ref[:, sl] = _rope(ks[:, sl], cos, sin, lane_lo).astype(BF16)
    else:
        qs_ref[...] = qs.astype(BF16)
        ks_ref[...] = ks.astype(BF16)
    ga_ref[...] = jax.nn.sigmoid(proj(c, d_model)).astype(BF16); c += d_model
    gb_ref[...] = jax.nn.sigmoid(proj(c, d_model)).astype(BF16); c += d_model
    gs_ref[...] = jax.nn.sigmoid(proj(c, d_model)).astype(BF16)


def _inproj(x, g, shift, scale, w_in, rope_tabs):
    n, d = x.shape
    bw = d // 2
    kvw = bw // 4
    tm = min(ROW_BLOCK, n)
    rope = rope_tabs is not None
    row = lambda i: (i, 0)
    fixed = lambda i: (0, 0)
    in_specs = [pl.BlockSpec((tm, d), row), pl.BlockSpec((1, d), fixed), pl.BlockSpec((1, d), fixed),
                pl.BlockSpec((1, d), fixed), pl.BlockSpec(w_in.shape, fixed)]
    args = [x, g, shift, scale, w_in]
    if rope:
        in_specs += [pl.BlockSpec((tm, LANES), row)] * 2
        args += list(rope_tabs)
    widths = [bw, bw, bw, bw, bw, kvw, kvw, d, d, d]
    dtypes = [BF16, BF16, BF16, F32, BF16, BF16, BF16, BF16, BF16, BF16]
    s5_slot = 3
    return pl.pallas_call(
        functools.partial(_inproj_kernel, bw=bw, kvw=kvw, d_model=d, rope=rope),
        grid=(n // tm,),
        in_specs=in_specs,
        out_specs=[pl.BlockSpec((bw // LANES, tm, LANES), lambda i: (0, i, 0)) if k == s5_slot
                   else pl.BlockSpec((tm, w), row) for k, w in enumerate(widths)],
        out_shape=[jax.ShapeDtypeStruct((bw // LANES, n, LANES) if k == s5_slot else (n, w), dt)
                   for k, (w, dt) in enumerate(zip(widths, dtypes))],
        compiler_params=_cparams(("arbitrary",)),
        name="inproj",
    )(*args)


def _rope_tables(seq):
    t = jnp.arange(seq)
    half = HEAD_DIM // 4
    inv = ROPE_BASE ** (-jnp.arange(half, dtype=F32) / half)
    sign = jnp.concatenate([-jnp.ones((half,), F32), jnp.ones((half,), F32)])

    def axis_tabs(pos):
        ang = pos.astype(F32)[:, None] * inv[None, :]
        c = jnp.cos(ang)
        s = jnp.sin(ang)
        return jnp.concatenate([c, c], -1), jnp.concatenate([s, s], -1) * sign

    cr, sr = axis_tabs(t // GRID_W)
    cc, sc = axis_tabs(t % GRID_W)
    cos = jnp.concatenate([cr, cc], -1)
    sin = jnp.concatenate([sr, sc], -1)
    return jnp.tile(cos, (1, LANES // HEAD_DIM)), jnp.tile(sin, (1, LANES // HEAD_DIM))


def _na_patterns(rows):
    kh = NA_WIN_H
    nb = rows // NA_ROWS
    kr0s, sigs = [], []
    for b in range(nb):
        r_lo = b * NA_ROWS
        kr0 = int(np.clip(r_lo - kh // 2, 0, rows - NA_KROWS))
        sig = tuple((int(np.clip(r - kh // 2, 0, rows - kh)) - kr0, r - kr0)
                    for r in range(r_lo, r_lo + NA_ROWS))
        kr0s.append(kr0)
        sigs.append(sig)
    uniq = sorted(set(sigs))
    ids = np.array([uniq.index(s) for s in sigs], np.int32)
    return np.array(kr0s, np.int32), ids, uniq


def _na_bias(rpb, uniq, n_ctx):
    kw = NA_WIN_W
    cols = np.arange(GRID_W)
    col_start = np.clip(cols - kw // 2, 0, GRID_W - kw)
    n_pat = len(uniq)
    kk = np.arange(NA_KROWS)
    col_sel = (cols[None, None, :] - cols[None, :, None] + kw - 1
               == np.arange(2 * kw - 1)[:, None, None])
    col_ok = ((cols[None, :] >= col_start[:, None]) & (cols[None, :] < col_start[:, None] + kw))
    row_sel = np.zeros((n_pat, NA_ROWS, NA_KROWS, 2 * NA_WIN_H - 1), bool)
    row_ok = np.zeros((n_pat, NA_ROWS, NA_KROWS), bool)
    for p, sig in enumerate(uniq):
        for rq, (r0_rel, r_rel) in enumerate(sig):
            row_ok[p, rq] = (kk >= r0_rel) & (kk < r0_rel + NA_WIN_H)
            row_sel[p, rq] = (kk[:, None] - r_rel + NA_WIN_H - 1) == np.arange(2 * NA_WIN_H - 1)[None, :]
    hp = lax.Precision.HIGHEST
    by_col = jnp.einsum('hrv,vcj->hrcj', rpb.astype(F32), jnp.asarray(col_sel, F32), precision=hp)
    bias = jnp.einsum('pqkr,hrcj->phqckj', jnp.asarray(row_sel, F32), by_col, precision=hp)
    ok = row_ok[:, None, :, None, :, None] & col_ok[None, None, None, :, None, :]
    bias = jnp.where(jnp.asarray(ok), bias, NEG_INF)
    bias = bias.reshape(n_pat, rpb.shape[0], NA_ROWS * GRID_W, NA_KROWS * GRID_W)
    return jnp.concatenate([bias, jnp.zeros(bias.shape[:3] + (n_ctx,), F32)], axis=-1)


def _na_kernel(kr0_ref, pat_ref, q_ref, k_ref, v_ref, kc_ref, vc_ref, bias_ref, o_ref):
    b = pl.program_id(1)
    start = pl.multiple_of(kr0_ref[b] * GRID_W, GRID_W)
    nk = NA_KROWS * GRID_W
    scale = HEAD_DIM ** -0.5
    one = jnp.ones((), BF16)
    hp = LANES // HEAD_DIM
    for lt in range(q_ref.shape[1] // LANES):
        lanes = slice(lt * LANES, (lt + 1) * LANES)
        q = q_ref[:, lanes]
        k_all = jnp.concatenate([k_ref[pl.ds(start, nk), lanes], kc_ref[:, lanes]], axis=0)
        v_all = jnp.concatenate([v_ref[pl.ds(start, nk), lanes], vc_ref[:, lanes]], axis=0)
        lane = lax.broadcasted_iota(jnp.int32, v_all.shape, 1)
        ext = []
        for h in range(hp):
            sl = slice(h * HEAD_DIM, (h + 1) * HEAD_DIM)
            own = (lane >= h * HEAD_DIM) & (lane < (h + 1) * HEAD_DIM)
            s = _dot_nt(q[:, sl], k_all[:, sl]) * scale + bias_ref[0, lt * hp + h]
            p = jnp.exp(s - jnp.max(s, axis=-1, keepdims=True)).astype(BF16)
            ext.append(_dot(p, jnp.where(own, v_all, one)))
        out_lane = lax.broadcasted_iota(jnp.int32, ext[0].shape, 1)
        low = out_lane < HEAD_DIM
        num = jnp.where(low, ext[0], ext[1])
        den = pltpu.roll(jnp.where(low, ext[1], ext[0]), HEAD_DIM, 1)
        o_ref[:, lanes] = (num / den).astype(o_ref.dtype)


def _na_attention(q, k, v, kc, vc, bias):
    s, bw = q.shape
    l = kc.shape[0]
    rows = s // GRID_W
    kr0s, ids, _ = _na_patterns(rows)
    tq = NA_ROWS * GRID_W
    nk = NA_KROWS * GRID_W
    hp = NA_LANES // HEAD_DIM
    grid_spec = pltpu.PrefetchScalarGridSpec(
        num_scalar_prefetch=2,
        grid=(bw // NA_LANES, rows // NA_ROWS),
        in_specs=[pl.BlockSpec((tq, NA_LANES), lambda p, b, kr, pt: (b, p)),
                  pl.BlockSpec((s, NA_LANES), lambda p, b, kr, pt: (0, p)),
                  pl.BlockSpec((s, NA_LANES), lambda p, b, kr, pt: (0, p)),
                  pl.BlockSpec((l, NA_LANES), lambda p, b, kr, pt: (0, p)),
                  pl.BlockSpec((l, NA_LANES), lambda p, b, kr, pt: (0, p)),
                  pl.BlockSpec((1, hp, tq, nk + l), lambda p, b, kr, pt: (pt[b], p, 0, 0))],
        out_specs=pl.BlockSpec((tq, NA_LANES), lambda p, b, kr, pt: (b, p)),
    )
    return pl.pallas_call(
        _na_kernel,
        grid_spec=grid_spec,
        out_shape=jax.ShapeDtypeStruct((s, bw), BF16),
        compiler_params=_cparams(("arbitrary", "arbitrary")),
        name="na_attention",
    )(jnp.asarray(kr0s), jnp.asarray(ids), q, k, v, kc, vc, bias)


def _ctx_attn_kernel(*refs, n_heads, n_rep, has_sink):
    if has_sink:
        sink_ref, q_ref, k_ref, v_ref, o_ref = refs
    else:
        q_ref, k_ref, v_ref, o_ref = refs
    scale = HEAD_DIM ** -0.5
    q = q_ref[...]
    k = k_ref[...]
    v = v_ref[...]
    outs = []
    for h in range(n_heads):
        g = h // n_rep
        sl = slice(h * HEAD_DIM, (h + 1) * HEAD_DIM)
        gl = slice(g * HEAD_DIM, (g + 1) * HEAD_DIM)
        s = _dot_nt(q[:, sl], k[:, gl]) * scale
        m = jnp.max(s, axis=-1, keepdims=True)
        if has_sink:
            m = jnp.maximum(m, sink_ref[h])
        p = jnp.exp(s - m)
        den = jnp.sum(p, axis=-1, keepdims=True)
        if has_sink:
            den = den + jnp.exp(sink_ref[h] - m)
        outs.append(_dot(p.astype(BF16), v[:, gl]) / den)
    o_ref[...] = jnp.concatenate(outs, axis=-1).astype(o_ref.dtype)


def _ctx_attention(q, k, v, sink):
    l, bw = q.shape
    n_heads = bw // HEAD_DIM
    n_rep = n_heads // (k.shape[1] // HEAD_DIM)
    has_sink = sink is not None
    full = lambda a: pl.BlockSpec(a.shape, lambda i: (0, 0))
    in_specs = [full(q), full(k), full(v)]
    args = [q, k, v]
    if has_sink:
        in_specs = [pl.BlockSpec(memory_space=pltpu.SMEM)] + in_specs
        args = [sink] + args
    return pl.pallas_call(
        functools.partial(_ctx_attn_kernel, n_heads=n_heads, n_rep=n_rep, has_sink=has_sink),
        grid=(1,),
        in_specs=in_specs,
        out_specs=pl.BlockSpec((l, bw), lambda i: (0, 0)),
        out_shape=jax.ShapeDtypeStruct((l, bw), BF16),
        compiler_params=_cparams(("arbitrary",)),
        name="ctx_attention",
    )(*args)


def _swa_kernel(sink_ref, q_ref, k_ref, v_ref, kc_ref, vc_ref, o_ref, *, seq, n_kv, n_rep):
    n = pl.program_id(0)
    nk = 3 * SW_BLOCK
    scale = HEAD_DIM ** -0.5
    ws = jnp.clip((n - 1) * SW_BLOCK, 0, seq - nk)
    ws = pl.multiple_of(ws, SW_BLOCK)
    q = q_ref[...]
    n_all = nk + kc_ref.shape[0]
    k_all = jnp.concatenate([k_ref[pl.ds(ws, nk), :], kc_ref[...]], axis=0)
    v_all = jnp.concatenate([v_ref[pl.ds(ws, nk), :], vc_ref[...]], axis=0)
    qpos = n * SW_BLOCK + lax.broadcasted_iota(jnp.int32, (SW_BLOCK, n_all), 0)
    col = lax.broadcasted_iota(jnp.int32, (SW_BLOCK, n_all), 1)
    visible = (col >= nk) | (jnp.abs(ws + col - qpos) <= SW_WINDOW)
    maskb = jnp.where(visible, 0.0, NEG_INF).astype(F32)
    lane = lax.broadcasted_iota(jnp.int32, v_all.shape, 1)
    one = jnp.ones((), BF16)
    outs = [None] * (n_kv * n_rep)
    for g in range(n_kv):
        gl = slice(g * HEAD_DIM, (g + 1) * HEAD_DIM)
        own = (lane >= g * HEAD_DIM) & (lane < (g + 1) * HEAD_DIM)
        qg = jnp.concatenate([q[:, (g * n_rep + r) * HEAD_DIM:(g * n_rep + r + 1) * HEAD_DIM]
                              for r in range(n_rep)], axis=0)
        s = _dot_nt(qg, k_all[:, gl]) * scale
        s = (s.reshape(n_rep, SW_BLOCK, n_all) + maskb[None]).reshape(n_rep * SW_BLOCK, n_all)
        sink = jnp.concatenate([jnp.full((SW_BLOCK, 1), sink_ref[g * n_rep + r], F32)
                                for r in range(n_rep)], axis=0)
        m = jnp.maximum(jnp.max(s, axis=-1, keepdims=True), sink)
        p = jnp.exp(s - m).astype(BF16)
        ext = _dot(p, jnp.where(own, v_all, one))
        den = pltpu.roll(ext, HEAD_DIM, 1) + jnp.exp(sink - m)
        o = ext / den
        for r in range(n_rep):
            outs[g * n_rep + r] = o[r * SW_BLOCK:(r + 1) * SW_BLOCK, gl]
    o_ref[...] = jnp.concatenate(outs, axis=-1).astype(o_ref.dtype)


def _sw_attention(q, k, v, kc, vc, sink):
    s, bw = q.shape
    l = kc.shape[0]
    kvw = k.shape[1]
    n_kv = kvw // HEAD_DIM
    n_rep = (bw // HEAD_DIM) // n_kv
    assert kvw == LANES
    return pl.pallas_call(
        functools.partial(_swa_kernel, seq=s, n_kv=n_kv, n_rep=n_rep),
        grid=(s // SW_BLOCK,),
        in_specs=[pl.BlockSpec(memory_space=pltpu.SMEM),
                  pl.BlockSpec((SW_BLOCK, bw), lambda n: (n, 0)),
                  pl.BlockSpec((s, kvw), lambda n: (0, 0)),
                  pl.BlockSpec((s, kvw), lambda n: (0, 0)),
                  pl.BlockSpec((l, kvw), lambda n: (0, 0)),
                  pl.BlockSpec((l, kvw), lambda n: (0, 0))],
        out_specs=pl.BlockSpec((SW_BLOCK, bw), lambda n: (n, 0)),
        out_shape=jax.ShapeDtypeStruct((s, bw), BF16),
        compiler_params=_cparams(("arbitrary",)),
        name="sw_attention",
    )(sink, q, k, v, kc, vc)


def _s5_prepare(lam_re, lam_im, b_re, b_im, c_re, c_im, log_step):
    hp = lax.Precision.HIGHEST
    tc = S5_CHUNK
    lam = lax.complex(lam_re.astype(F32), lam_im.astype(F32))
    dt = jnp.exp(log_step.astype(F32))[..., None]
    lam_dt = lam * dt
    lam_bar = jnp.exp(lam_dt)
    b_bar = ((lam_bar - 1.0) / lam)[..., None] * lax.complex(b_re.astype(F32), b_im.astype(F32))
    c_mat = lax.complex(c_re.astype(F32), c_im.astype(F32))
    kk = jnp.arange(tc + 1, dtype=F32)
    pw = jnp.exp(lam_dt[None] * kk[:, None, None, None])
    n_g, n_p, n_h = b_bar.shape[1:]

    cw = c_mat[None] * pw[:tc, :, :, None, :]
    kern = (jnp.einsum('kdghp,dgpj->kdghj', cw.real, b_bar.real, precision=hp)
            - jnp.einsum('kdghp,dgpj->kdghj', cw.imag, b_bar.imag, precision=hp))
    t_i = np.arange(tc)
    lag_f = t_i[None, :] - t_i[:, None]
    m_f = jnp.where((lag_f >= 0)[:, :, None, None, None],
                    kern[np.clip(lag_f, 0, tc - 1), 0], 0.0)
    m_r = jnp.where((lag_f <= 0)[:, :, None, None, None],
                    kern[np.clip(-lag_f, 0, tc - 1), 1], 0.0)
    m_intra = jnp.transpose(m_f + m_r, (2, 0, 4, 1, 3)).reshape(n_g, tc * n_h, tc * n_h)

    def cat_ri(z):
        return jnp.concatenate([z.real, z.imag], axis=-1)

    w_f = pw[tc - 1 - t_i, 0][:, :, None, :] * jnp.transpose(b_bar[0], (0, 2, 1))[None]
    w_r = pw[t_i, 1][:, :, None, :] * jnp.transpose(b_bar[1], (0, 2, 1))[None]
    w_f = jnp.transpose(cat_ri(w_f), (1, 0, 2, 3)).reshape(n_g, tc * n_h, 2 * n_p)
    w_r = jnp.transpose(cat_ri(w_r), (1, 0, 2, 3)).reshape(n_g, tc * n_h, 2 * n_p)
    m_cat = jnp.concatenate([m_intra, w_f, w_r], axis=-1).astype(BF16)

    def v_of(c_dir, pw_sel):
        cp = c_dir[None] * pw_sel[:, :, None, :]
        v = jnp.concatenate([cp.real, -cp.imag], axis=-1)
        return jnp.transpose(v, (1, 3, 0, 2)).reshape(n_g, 2 * n_p, tc * n_h)
    v_cat = jnp.concatenate([v_of(c_mat[0], pw[t_i + 1, 0]), v_of(c_mat[1], pw[tc - t_i, 1])],
                            axis=1).astype(BF16)

    def coef(k):
        a = jnp.exp(lam_dt * (tc * k))
        return jnp.stack([jnp.concatenate([a.real, a.real], -1),
                          jnp.concatenate([-a.imag, a.imag], -1)], axis=2)
    a_log = jnp.stack([coef(1.0), coef(2.0), coef(4.0)], axis=2)
    a_car = jnp.stack([coef(float(j)) for j in range(1, SUBLANES + 1)], axis=3)
    a_car = jnp.stack([a_car[0], a_car[1][:, :, ::-1]], axis=0)
    return _s5_expand(m_cat, v_cat, jnp.transpose(a_log, (1, 0, 2, 3, 4)), jnp.transpose(a_car, (1, 0, 2, 3, 4)))


def _s5_expand(m_cat, v_cat, a_log, a_car):
    n_g = m_cat.shape[0]
    n_t = n_g // S5_LG
    n_state = 2 * S5_STATE

    def lanes_of(a):
        a = a.reshape((n_t, S5_LG) + a.shape[1:])
        a = jnp.moveaxis(a, 1, -2)
        return a.reshape(a.shape[:-2] + (S5_LG * n_state,))
    return (m_cat.reshape((n_t, S5_LG) + m_cat.shape[1:]), v_cat.reshape((n_t, S5_LG) + v_cat.shape[1:]),
            lanes_of(a_log), lanes_of(a_car))


def _s5_lane_perm():
    r = np.arange(S5_CHUNK * LANES)
    t, a, h = r // LANES, (r % LANES) // S5_GROUP, r % S5_GROUP
    dest = a * (S5_CHUNK * S5_GROUP) + t * S5_GROUP + h
    return (jnp.asarray(dest)[:, None] == jnp.arange(S5_CHUNK * LANES)[None, :]).astype(BF16)


def _s5_in_kernel(u_ref, perm_ref, m_ref, yp_ref, z_ref):
    width = S5_CHUNK * S5_GROUP
    n_state = 2 * S5_STATE
    half = S5_LG * n_state
    up = _dot(u_ref[0].astype(BF16), perm_ref[...]).astype(BF16)
    for a in range(S5_LG):
        zz = _dot(up[:, a * width:(a + 1) * width], m_ref[0, a])
        yp_ref[0, :, a * width:(a + 1) * width] = zz[:, :width]
        z_ref[0, :, a * n_state:(a + 1) * n_state] = zz[:, width:width + n_state]
        z_ref[0, :, half + a * n_state:half + (a + 1) * n_state] = zz[:, width + n_state:]


def _s5_scan_kernel(z_ref, alog_ref, acar_ref, e_ref, *, nc, nc_ctx):
    n_state = 2 * S5_STATE
    half = z_ref.shape[2] // 2
    nblk = nc // SUBLANES
    nblk_ctx = nc_ctx // SUBLANES
    row = lax.broadcasted_iota(jnp.int32, (SUBLANES, n_state), 0)

    def cmul(a1, a2, s):
        return a1 * s + a2 * pltpu.roll(s, S5_STATE, 1)

    def step(t, carry):
        blk = (t, jnp.where(t < nblk_ctx, nblk_ctx - 1 - t, nblk + nblk_ctx - 1 - t))
        new = []
        for d in range(2):
            r0 = pl.multiple_of(blk[d] * SUBLANES, SUBLANES)
            for j in range(half // n_state):
                tile = slice(j * n_state, (j + 1) * n_state)
                lanes = slice(d * half + j * n_state, d * half + (j + 1) * n_state)
                z = z_ref[0, pl.ds(r0, SUBLANES), lanes]
                c = carry[len(new)]
                for i, sh in enumerate((1, 2, 4)):
                    if d == 0:
                        zs = jnp.where(row >= sh, pltpu.roll(z, sh, 0), 0.0)
                    else:
                        zs = jnp.where(row < SUBLANES - sh, pltpu.roll(z, SUBLANES - sh, 0), 0.0)
                    z = z + cmul(alog_ref[0, d, i, 0:1, tile], alog_ref[0, d, i, 1:2, tile], zs)
                cb = jnp.broadcast_to(c, (SUBLANES, n_state))
                s = z + cmul(acar_ref[0, d, 0, :, tile], acar_ref[0, d, 1, :, tile], cb)
                if d == 0:
                    e = jnp.where(row >= 1, pltpu.roll(s, 1, 0), cb)
                    c_new = s[SUBLANES - 1:SUBLANES, :]
                else:
                    e = jnp.where(row < SUBLANES - 1, pltpu.roll(s, SUBLANES - 1, 0), cb)
                    c_new = s[0:1, :]
                e_ref[0, pl.ds(r0, SUBLANES), lanes] = e
                new.append(c_new)
        return tuple(new)

    init = tuple(jnp.zeros((1, n_state), F32) for _ in range(2 * (half // n_state)))
    lax.fori_loop(0, nblk, step, init)


def _s5_out_kernel(e_ref, yp_ref, u_ref, d_ref, perm_ref, v_ref, y_ref):
    width = S5_CHUNK * S5_GROUP
    n_state = 2 * S5_STATE
    half = S5_LG * n_state
    parts = []
    for a in range(S5_LG):
        e_a = jnp.concatenate([e_ref[0, :, a * n_state:(a + 1) * n_state],
                               e_ref[0, :, half + a * n_state:half + (a + 1) * n_state]], axis=-1)
        parts.append(yp_ref[0, :, a * width:(a + 1) * width] + _dot(e_a.astype(BF16), v_ref[0, a]))
    y = jnp.concatenate(parts, axis=-1)
    hi = y.astype(BF16)
    lo = (y - hi.astype(F32)).astype(BF16)
    perm = perm_ref[...]
    y_ref[0] = _dot_nt(hi, perm) + _dot_nt(lo, perm) + u_ref[0] * d_ref[0]


def _s5_rows(nc):
    return max(r for r in range(2 * SUBLANES, 257, 2 * SUBLANES) if nc % r == 0)


def _s5_mixer_pre_glu(u_ctx, u_lat, prep, d_skip):
    m_cat, v_cat, a_log, a_car = prep
    n_t, l, _ = u_ctx.shape
    s = u_lat.shape[1]
    nc, nc_ctx = (l + s) // S5_CHUNK, l // S5_CHUNK
    width = S5_CHUNK * LANES
    n_z = 2 * S5_LG * 2 * S5_STATE
    u = jnp.concatenate([u_ctx, u_lat], axis=1).reshape(n_t, nc, width)
    d_rows = jnp.tile(d_skip.astype(F32).reshape(n_t, 1, LANES), (1, 1, S5_CHUNK))
    perm = _s5_lane_perm()
    rb = _s5_rows(nc)
    rows = lambda w: pl.BlockSpec((1, rb, w), lambda g, r: (g, r, 0))
    per_tile = lambda a: pl.BlockSpec((1,) + a.shape[1:], lambda g, *_: (g,) + (0,) * (a.ndim - 1))
    whole = lambda a: pl.BlockSpec(a.shape, lambda g, r: (0,) * a.ndim)
    yp, z = pl.pallas_call(
        _s5_in_kernel,
        grid=(n_t, nc // rb),
        in_specs=[rows(width), whole(perm), per_tile(m_cat)],
        out_specs=[rows(width), rows(n_z)],
        out_shape=[jax.ShapeDtypeStruct((n_t, nc, width), F32), jax.ShapeDtypeStruct((n_t, nc, n_z), F32)],
        compiler_params=_cparams(("arbitrary", "arbitrary")),
        name="s5_in",
    )(u, perm, m_cat)
    e = pl.pallas_call(
        functools.partial(_s5_scan_kernel, nc=nc, nc_ctx=nc_ctx),
        grid=(n_t,),
        in_specs=[per_tile(z), per_tile(a_log), per_tile(a_car)],
        out_specs=per_tile(z),
        out_shape=jax.ShapeDtypeStruct(z.shape, F32),
        compiler_params=_cparams(("arbitrary",)),
        name="s5_scan",
    )(z, a_log, a_car)
    y = pl.pallas_call(
        _s5_out_kernel,
        grid=(n_t, nc // rb),
        in_specs=[rows(n_z), rows(width), rows(width), per_tile(d_rows), whole(perm), per_tile(v_cat)],
        out_specs=rows(width),
        out_shape=jax.ShapeDtypeStruct((n_t, nc, width), F32),
        compiler_params=_cparams(("arbitrary", "arbitrary")),
        name="s5_out",
    )(e, yp, u, d_rows, perm, v_cat)
    return y.reshape(n_t, l + s, LANES)


def _merge_kernel(x_ref, ya_ref, y5_ref, ys_ref, ga_ref, gb_ref, gs_ref, wglu_ref, wb_ref, wo_ref,
                  gt_ref, g_ref, sh_ref, sc_ref, xo_ref, h_ref, ht_ref):
    y5 = jnp.concatenate([y5_ref[t] for t in range(y5_ref.shape[0])], axis=-1)
    z = _gelu(y5)
    yb = z * jax.nn.sigmoid(_dot(z.astype(BF16), wglu_ref[...]))
    m = (ga_ref[...].astype(F32) * _dot(ya_ref[...], wb_ref[0])
         + gb_ref[...].astype(F32) * _dot(yb.astype(BF16), wb_ref[1])
         + gs_ref[...].astype(F32) * _dot(ys_ref[...], wb_ref[2]))
    x = x_ref[...] + gt_ref[...] * _dot(m.astype(BF16), wo_ref[...])
    xo_ref[...] = x
    h = _norm_mod(x, g_ref[...], sh_ref[...], sc_ref[...])
    h_ref[...] = h.astype(BF16)
    ht_ref[...] = h.T.astype(BF16)


def _merge(x, ya, y5, y5_row0, ys, ga, gb, gs, w_glu, w_branch, w_out, gate, g, shift, scale):
    n, d = x.shape
    bw = ya.shape[1]
    tm = min(ROW_BLOCK, n)
    assert y5_row0 % tm == 0
    row = lambda w: pl.BlockSpec((tm, w), lambda i: (i, 0))
    full = lambda a: pl.BlockSpec(a.shape, lambda i: (0,) * a.ndim)
    vec = pl.BlockSpec((1, d), lambda i: (0, 0))
    y5_rows = pl.BlockSpec((y5.shape[0], tm, LANES), lambda i: (0, i + y5_row0 // tm, 0))
    return pl.pallas_call(
        _merge_kernel,
        grid=(n // tm,),
        in_specs=[row(d), row(bw), y5_rows, row(bw), row(d), row(d), row(d),
                  full(w_glu), full(w_branch), full(w_out), vec, vec, vec, vec],
        out_specs=[row(d), row(d), pl.BlockSpec((d, tm), lambda i: (0, i))],
        out_shape=[jax.ShapeDtypeStruct((n, d), F32), jax.ShapeDtypeStruct((n, d), BF16),
                   jax.ShapeDtypeStruct((d, n), BF16)],
        compiler_params=_cparams(("arbitrary",)),
        name="merge",
    )(x, ya, y5, ys, ga, gb, gs, w_glu, w_branch, w_out, gate, g, shift, scale)


def _knock_out_16(s, order, exact, want_rank=True):
    rank = jnp.full(s.shape, float(PEER_TOPK), F32) if want_rank else None
    live = s
    vals = []
    for r in range(PEER_TOPK):
        m = jnp.max(live, axis=0, keepdims=True)
        hit = live == m
        if exact:
            first = jnp.min(jnp.where(hit, order, float(PEER_TOPK * PEER_NKEYS)), axis=0, keepdims=True)
            hit = order == first
        if want_rank:
            rank = jnp.where(hit, float(r), rank)
        live = jnp.where(hit, -jnp.inf, live)
        vals.append(m)
    out = (rank < float(PEER_TOPK)) if want_rank else (live != s)
    n_out = jnp.sum(jnp.where(out, 1.0, 0.0), axis=0, keepdims=True)
    return rank, jnp.concatenate(vals, axis=0), n_out


def _bf16_pair_word(x):
    hi = pltpu.bitcast(x.astype(BF16).astype(F32), jnp.uint32)
    return hi | (hi >> 16)


def _pair_tiles():
    tiles = [(0, 0, 8), (0, 8, 8)]
    for a in range(1, 8):
        tiles.append((a, 0, PEER_TOPK // (a + 1)))
    return tiles


def _route_kernel(h_ref, wq_ref, k1_ref, k2_ref, cnt_ref, c1_ref, rk_ref, e2_ref, q_scr):
    tb = h_ref.shape[0]
    q_scr[...] = _dot(h_ref[...], wq_ref[...]).astype(BF16)
    half = PEER_QDIM // 2
    iota = lax.broadcasted_iota(jnp.int32, (PEER_NKEYS, tb), 0).astype(F32)
    row8 = lax.broadcasted_iota(jnp.int32, (SUBLANES, tb), 0).astype(F32)
    k1 = k1_ref[...]
    k2 = k2_ref[...]

    def emit(hd, s1, s2, exact):
        rank1, v1, n1 = _knock_out_16(s1, iota, exact, want_rank=exact)
        rank2, v2, n2 = _knock_out_16(s2, iota, exact)
        e1v = jnp.exp(v1 - v1[0:1])
        e2v = jnp.exp(v2 - v2[0:1])
        tiles, flats, gates = [], [], []
        for a, b0, nv in _pair_tiles():
            c = v1[a:a + 1] + v2[b0:b0 + SUBLANES]
            tiles.append(jnp.where(row8 < nv, c, -jnp.inf))
            flats.append(a * PEER_TOPK + b0 + row8)
            gates.append(e1v[a:a + 1] * e2v[b0:b0 + SUBLANES])
        tiles.append(v1[SUBLANES:] + v2[0:1])
        flats.append((row8 + SUBLANES) * PEER_TOPK)
        gates.append(e1v[SUBLANES:] * e2v[0:1])
        cand = jnp.concatenate(tiles, axis=0)
        flat = jnp.concatenate(flats, axis=0)
        gate = jnp.concatenate(gates, axis=0)
        rank_c, _, n_c = _knock_out_16(cand, flat, exact)
        self_ = jnp.where(rank_c < float(PEER_TOPK), 1.0, 0.0)
        z = jnp.sum(self_ * gate, axis=0, keepdims=True)
        cnt = [self_[0:8].sum(axis=0, keepdims=True) + self_[8:16].sum(axis=0, keepdims=True)]
        for t in range(2, 9):
            cnt.append(self_[t * SUBLANES:(t + 1) * SUBLANES].sum(axis=0, keepdims=True))
        cnt = jnp.concatenate(cnt + [self_[9 * SUBLANES:]], axis=0)
        cnt1 = jnp.zeros((PEER_NKEYS, tb), F32)
        for a in range(PEER_TOPK):
            is_a = (rank1 == float(a)) if exact else (s1 == v1[a:a + 1])
            cnt1 = jnp.where(is_a, cnt[a:a + 1], cnt1)
        cnt_ref[hd] = _bf16_pair_word(cnt1)
        c1_ref[hd] = _bf16_pair_word(jnp.exp(s1 - v1[0:1]) * (0.5 / z))
        rk_ref[hd] = rank2.astype(BF16)
        e2_ref[hd] = jnp.exp(s2 - v2[0:1]).astype(BF16)
        want = float(PEER_TOPK)
        return jnp.where((n1 != want) | (n2 != want) | (n_c != want), 1.0, 0.0)

    def head_group(hg, _):
        scores = []
        for k in range(PEER_ROUTE_HEADS):
            hd = hg * PEER_ROUTE_HEADS + k
            c0 = pl.multiple_of(hd * PEER_QDIM, PEER_QDIM)
            s1 = _dot_nt(k1, q_scr[:, pl.ds(c0, half)])
            s2 = _dot_nt(k2, q_scr[:, pl.ds(c0 + half, half)])
            scores.append((hd, s1, s2))
        tied = [jnp.max(emit(hd, s1, s2, exact=False)) for hd, s1, s2 in scores]
        for (hd, s1, s2), t in zip(scores, tied):
            @pl.when(t > 0.0)
            def _():
                emit(hd, s1, s2, exact=True)
        return 0

    lax.fori_loop(0, PEER_HEADS // PEER_ROUTE_HEADS, head_group, 0)


def _peer_route(h, w_q, k1, k2):
    n, d = h.shape
    tb = min(PEER_ROUTE_BLOCK, n)
    qw = w_q.shape[1]
    out_blk = pl.BlockSpec((PEER_HEADS, PEER_NKEYS, tb), lambda i: (0, 0, i))
    shp = lambda dt: jax.ShapeDtypeStruct((PEER_HEADS, PEER_NKEYS, n), dt)
    return pl.pallas_call(
        _route_kernel,
        grid=(n // tb,),
        in_specs=[pl.BlockSpec((tb, d), lambda i: (i, 0)),
                  pl.BlockSpec((d, qw), lambda i: (0, 0)),
                  pl.BlockSpec(k1.shape, lambda i: (0, 0)),
                  pl.BlockSpec(k2.shape, lambda i: (0, 0))],
        out_specs=[out_blk] * 4,
        out_shape=[shp(jnp.uint32), shp(jnp.uint32), shp(BF16), shp(BF16)],
        scratch_shapes=[pltpu.VMEM((tb, qw), BF16)],
        compiler_params=_cparams(("arbitrary",)),
        name="peer_route",
    )(h, w_q, k1, k2)


def _dense_kernel(ht_ref, x_ref, gt_ref, u_ref, vt_ref, cnt_ref, c1_ref, rk_ref, e2_ref, o_ref,
                  acc_ref, w_scr):
    eb = pl.program_id(1)
    tb = ht_ref.shape[1]
    n_exp = u_ref.shape[0]
    n_sub = n_exp // PEER_SUB
    i_per_sub = PEER_SUB // PEER_NKEYS

    @pl.when(eb == 0)
    def _():
        acc_ref[...] = jnp.zeros_like(acc_ref)

    def scores(sb):
        return _dot(u_ref[sb * PEER_SUB:(sb + 1) * PEER_SUB, :], ht_ref[...])

    pk_rows = 2 * SUBLANES
    n_pk = PEER_NKEYS // pk_rows

    def row_tile(ref, hd, i):
        words = jnp.broadcast_to(ref[hd, pl.ds(i, 1), :], (SUBLANES, tb))
        return pltpu.bitcast(words, BF16)[None]

    a_next = scores(0)
    for sb in range(n_sub):
        a_cur = a_next
        if sb + 1 < n_sub:
            a_next = scores(sb + 1)
        for il in range(i_per_sub):
            i = eb * (n_exp // PEER_NKEYS) + sb * i_per_sub + il
            gsum = jnp.zeros((n_pk, pk_rows, tb), BF16)
            for hd in range(PEER_HEADS):
                cnt = row_tile(cnt_ref, hd, i)
                c1 = row_tile(c1_ref, hd, i)
                rk = rk_ref[hd].reshape(n_pk, pk_rows, tb)
                e2 = e2_ref[hd].reshape(n_pk, pk_rows, tb)
                gsum = gsum + jnp.where(rk < cnt, e2 * c1, jnp.zeros((), BF16))
            r0 = sb * PEER_SUB + il * PEER_NKEYS
            a_i = a_cur[il * PEER_NKEYS:(il + 1) * PEER_NKEYS]
            half_gate = gsum.reshape(PEER_NKEYS, tb).astype(F32)
            w_i = a_i * (1.0 + lax.erf(a_i * (1.0 / math.sqrt(2.0)))) * half_gate
            w_scr[r0:r0 + PEER_NKEYS, :] = w_i.astype(BF16)
        done = (sb + 1) * PEER_SUB
        if done % PEER_ACC_CHUNK == 0:
            c0 = done - PEER_ACC_CHUNK
            acc_ref[...] += _dot(vt_ref[:, c0:done], w_scr[c0:done, :])

    @pl.when(eb == pl.num_programs(1) - 1)
    def _():
        o_ref[...] = x_ref[...] + gt_ref[...] * acc_ref[...].T


def _peer_dense(h_t, x, gate, u, v_t, cnt1, c1, rank2, e2):
    d, n = h_t.shape
    n_e = u.shape[0]
    tb = min(PEER_TOK_BLOCK, n)
    eb = PEER_EXP_BLOCK
    tab = pl.BlockSpec((PEER_HEADS, PEER_NKEYS, tb), lambda i, e: (0, 0, i))
    return pl.pallas_call(
        _dense_kernel,
        grid=(n // tb, n_e // eb),
        in_specs=[pl.BlockSpec((d, tb), lambda i, e: (0, i)),
                  pl.BlockSpec((tb, d), lambda i, e: (i, 0)),
                  pl.BlockSpec((1, d), lambda i, e: (0, 0)),
                  pl.BlockSpec((eb, d), lambda i, e: (e, 0)),
                  pl.BlockSpec((d, eb), lambda i, e: (0, e)),
                  tab, tab, tab, tab],
        out_specs=pl.BlockSpec((tb, d), lambda i, e: (i, 0)),
        out_shape=jax.ShapeDtypeStruct((n, d), F32),
        scratch_shapes=[pltpu.VMEM((d, tb), F32), pltpu.VMEM((eb, tb), BF16)],
        compiler_params=_cparams(("arbitrary", "arbitrary")),
        name="peer_dense",
    )(h_t, x, gate, u, v_t, cnt1, c1, rank2, e2)


def _final_norm_kernel(x_ref, g_ref, o_ref):
    x = x_ref[...]
    o_ref[...] = x * lax.rsqrt(jnp.mean(x * x, axis=-1, keepdims=True) + EPS) * g_ref[...]


def _final_norm(x, g):
    n, d = x.shape
    tm = min(2 * ROW_BLOCK, n)
    return pl.pallas_call(
        _final_norm_kernel,
        grid=(n // tm,),
        in_specs=[pl.BlockSpec((tm, d), lambda i: (i, 0)), pl.BlockSpec((1, d), lambda i: (0, 0))],
        out_specs=pl.BlockSpec((tm, d), lambda i: (i, 0)),
        out_shape=jax.ShapeDtypeStruct((n, d), F32),
        compiler_params=_cparams(("arbitrary",)),
        name="final_norm",
    )(x, g)


def kernel(x, c, ctx, c_ctx, w_mod, b_mod, g_mix, g_ffn, w_in, na_rpb, s5_lam_re, s5_lam_im,
           s5_b_re, s5_b_im, s5_c_re, s5_c_im, s5_log_step, s5_d, s5_w_glu, sw_sink, w_branch,
           w_out, peer_w_q, peer_sub_keys, peer_u, peer_v, g_final):
    batch, seq, d = x.shape
    l_ctx = ctx.shape[1]
    depth = w_mod.shape[0]
    assert batch == 1 and seq % (NA_ROWS * GRID_W) == 0 and l_ctx % (S5_CHUNK * SUBLANES) == 0
    assert seq // GRID_W >= NA_KROWS and seq >= 3 * SW_BLOCK

    cc = jnp.zeros((SUBLANES, d), F32).at[0].set(c[0]).at[1].set(c_ctx)
    mod = _mod_vectors(cc, w_mod, b_mod).reshape(depth, SUBLANES, 6, d)
    rope_tabs = _rope_tables(seq)
    row = lambda v: v.reshape(1, d)
    na_uniq = _na_patterns(seq // GRID_W)[2]
    na_bias = jax.vmap(lambda r: _na_bias(r, na_uniq, l_ctx))(na_rpb)
    s5_prep = jax.vmap(_s5_prepare)(s5_lam_re, s5_lam_im, s5_b_re, s5_b_im, s5_c_re, s5_c_im, s5_log_step)

    xx, xc = x[0], ctx[0]
    for l in range(depth):
        need_ctx = l < depth - 1
        m_lat, m_ctx = mod[l, 0], mod[l, 1]
        w_in_l = w_in[l].astype(BF16)
        g_mix_l = row(g_mix[l])
        g_ffn_l = row(g_ffn[l])

        qa, ka, va, ub, qs, ks, vs, ga, gb, gs = _inproj(
            xx, g_mix_l, row(m_lat[0]), row(m_lat[1]), w_in_l, rope_tabs)
        qa_c, ka_c, va_c, ub_c, qs_c, ks_c, vs_c, ga_c, gb_c, gs_c = _inproj(
            xc, g_mix_l, row(m_ctx[0]), row(m_ctx[1]), w_in_l, None)

        ya = _na_attention(qa, ka, va, ka_c, va_c, na_bias[l])
        ys = _sw_attention(qs, ks, vs, ks_c, vs_c, sw_sink[l])
        y5 = _s5_mixer_pre_glu(ub_c, ub, [a[l] for a in s5_prep], s5_d[l])

        w_glu_l = s5_w_glu[l].astype(BF16)
        w_branch_l = w_branch[l].astype(BF16)
        w_out_l = w_out[l].astype(BF16)
        xx, hx2, hx2_t = _merge(xx, ya, y5, l_ctx, ys, ga, gb, gs, w_glu_l, w_branch_l, w_out_l,
                                row(m_lat[2]), g_ffn_l, row(m_lat[3]), row(m_lat[4]))

        w_q_l = peer_w_q[l].astype(BF16)
        k1 = peer_sub_keys[l, 0].astype(BF16)
        k2 = peer_sub_keys[l, 1].astype(BF16)
        u_l = peer_u[l].astype(BF16)
        vt_l = peer_v[l].astype(BF16).T
        if need_ctx:
            ya_c = _ctx_attention(qa_c, ka_c, va_c, None)
            ys_c = _ctx_attention(qs_c, ks_c, vs_c, sw_sink[l])
            xc, hc2, hc2_t = _merge(xc, ya_c, y5, 0, ys_c, ga_c, gb_c, gs_c, w_glu_l, w_branch_l, w_out_l,
                                    row(m_ctx[2]), g_ffn_l, row(m_ctx[3]), row(m_ctx[4]))
            xc = _peer_dense(hc2_t, xc, row(m_ctx[5]), u_l, vt_l, *_peer_route(hc2, w_q_l, k1, k2))
        xx = _peer_dense(hx2_t, xx, row(m_lat[5]), u_l, vt_l, *_peer_route(hx2, w_q_l, k1, k2))

    return _final_norm(xx, row(g_final))[None]
```

```python
import functools
import math

import numpy as np
import jax
import jax.numpy as jnp
from jax import lax
from jax.experimental import pallas as pl
from jax.experimental.pallas import tpu as pltpu

F32 = jnp.float32
BF16 = jnp.bfloat16

GRID_W = 64
HEAD_DIM = 64
NA_WIN_H = 8
NA_WIN_W = 16
S5_GROUP = 16
S5_STATE = 64
SW_WINDOW = 128
ROPE_BASE = 10000.0
PEER_HEADS = 8
PEER_NKEYS = 128
PEER_QDIM = 256
PEER_TOPK = 16
EPS = 1e-6
NEG_INF = -1e30

LANES = 128
SUBLANES = 8
VMEM_LIMIT_BYTES = 56 * 1024 * 1024

ROW_BLOCK = 256
NA_ROWS = 4
NA_KROWS = NA_ROWS + NA_WIN_H - 1
NA_LANES = 256
SW_BLOCK = 128
S5_CHUNK = 16
S5_LG = LANES // S5_GROUP
PEER_ROUTE_BLOCK = 256
PEER_ROUTE_HEADS = 2
PEER_TOK_BLOCK = 512
PEER_EXP_BLOCK = 2048
PEER_SUB = 256
PEER_ACC_CHUNK = 1024


def _cparams(sem):
    return pltpu.CompilerParams(dimension_semantics=sem, vmem_limit_bytes=VMEM_LIMIT_BYTES)


def _dot(a, b):
    return jnp.dot(a, b, preferred_element_type=F32)


def _dot_nt(a, b):
    return lax.dot_general(a, b, (((1,), (1,)), ((), ())), preferred_element_type=F32)


def _gelu(x):
    return 0.5 * x * (1.0 + lax.erf(x * (1.0 / math.sqrt(2.0))))


def _mod_kernel(cc_ref, w_ref, b_ref, o_ref):
    a = cc_ref[...]
    a = a * jax.nn.sigmoid(a)
    o_ref[0] = _dot(a.astype(BF16), w_ref[0].astype(BF16)) + b_ref[0]


def _mod_vectors(cc, w_mod, b_mod):
    depth, d, n6 = w_mod.shape
    tn = 1024
    return pl.pallas_call(
        _mod_kernel,
        grid=(depth, n6 // tn),
        in_specs=[pl.BlockSpec((SUBLANES, d), lambda l, j: (0, 0)),
                  pl.BlockSpec((1, d, tn), lambda l, j: (l, 0, j)),
                  pl.BlockSpec((1, 1, tn), lambda l, j: (l, 0, j))],
        out_specs=pl.BlockSpec((1, SUBLANES, tn), lambda l, j: (l, 0, j)),
        out_shape=jax.ShapeDtypeStruct((depth, SUBLANES, n6), F32),
        compiler_params=_cparams(("arbitrary", "arbitrary")),
        name="mod_vectors",
    )(cc, w_mod, b_mod.reshape(depth, 1, n6))


def _norm_mod(x, g, shift, scale):
    y = x * lax.rsqrt(jnp.mean(x * x, axis=-1, keepdims=True) + EPS)
    return (y * g) * (1.0 + scale) + shift


def _rope(x, cos, sin, lane_lo):
    up = pltpu.roll(x, LANES - 16, 1)
    dn = pltpu.roll(x, 16, 1)
    return x * cos + jnp.where(lane_lo, up, dn) * sin


def _inproj_kernel(x_ref, g_ref, sh_ref, sc_ref, w_ref, *rest, bw, kvw, d_model, rope):
    if rope:
        cos_ref, sin_ref = rest[:2]
        rest = rest[2:]
    qa_ref, ka_ref, va_ref, ub_ref, qs_ref, ks_ref, vs_ref, ga_ref, gb_ref, gs_ref = rest
    h = _norm_mod(x_ref[...], g_ref[...], sh_ref[...], sc_ref[...]).astype(BF16)

    def proj(c0, width):
        return _dot(h, w_ref[:, c0:c0 + width])

    c = 0
    qa_ref[...] = proj(c, bw).astype(BF16); c += bw
    ka_ref[...] = proj(c, bw).astype(BF16); c += bw
    va_ref[...] = proj(c, bw).astype(BF16); c += bw
    ub = proj(c, bw); c += bw
    for t in range(bw // LANES):
        ub_ref[t] = ub[:, t * LANES:(t + 1) * LANES]
    qs = proj(c, bw); c += bw
    ks = proj(c, kvw); c += kvw
    vs_ref[...] = proj(c, kvw).astype(BF16); c += kvw
    if rope:
        cos = cos_ref[...]
        sin = sin_ref[...]
        lane = lax.broadcasted_iota(jnp.int32, cos.shape, 1)
        lane_lo = (lane % 32) < 16
        for p in range(bw // LANES):
            sl = slice(p * LANES, (p + 1) * LANES)
            qs_ref[:, sl] = _rope(qs[:, sl], cos, sin, lane_lo).astype(BF16)
        for p in range(kvw // LANES):
            sl = slice(p * LANES, (p + 1) * LANES)
            ks_ref[:, sl] = _rope(ks[:, sl], cos, sin, lane_lo).astype(BF16)
    else:
        qs_ref[...] = qs.astype(BF16)
        ks_ref[...] = ks.astype(BF16)
    ga_ref[...] = jax.nn.sigmoid(proj(c, d_model)).astype(BF16); c += d_model
    gb_ref[...] = jax.nn.sigmoid(proj(c, d_model)).astype(BF16); c += d_model
    gs_ref[...] = jax.nn.sigmoid(proj(c, d_model)).astype(BF16)


def _inproj(x, g, shift, scale, w_in, rope_tabs):
    n, d = x.shape
    bw = d // 2
    kvw = bw // 4
    tm = min(ROW_BLOCK, n)
    rope = rope_tabs is not None
    row = lambda i: (i, 0)
    fixed = lambda i: (0, 0)
    in_specs = [pl.BlockSpec((tm, d), row), pl.BlockSpec((1, d), fixed), pl.BlockSpec((1, d), fixed),
                pl.BlockSpec((1, d), fixed), pl.BlockSpec(w_in.shape, fixed)]
    args = [x, g, shift, scale, w_in]
    if rope:
        in_specs += [pl.BlockSpec((tm, LANES), row)] * 2
        args += list(rope_tabs)
    widths = [bw, bw, bw, bw, bw, kvw, kvw, d, d, d]
    dtypes = [BF16, BF16, BF16, F32, BF16, BF16, BF16, BF16, BF16, BF16]
    s5_slot = 3
    return pl.pallas_call(
        functools.partial(_inproj_kernel, bw=bw, kvw=kvw, d_model=d, rope=rope),
        grid=(n // tm,),
        in_specs=in_specs,
        out_specs=[pl.BlockSpec((bw // LANES, tm, LANES), lambda i: (0, i, 0)) if k == s5_slot
                   else pl.BlockSpec((tm, w), row) for k, w in enumerate(widths)],
        out_shape=[jax.ShapeDtypeStruct((bw // LANES, n, LANES) if k == s5_slot else (n, w), dt)
                   for k, (w, dt) in enumerate(zip(widths, dtypes))],
        compiler_params=_cparams(("arbitrary",)),
        name="inproj",
    )(*args)


def _rope_tables(seq):
    t = jnp.arange(seq)
    half = HEAD_DIM // 4
    inv = ROPE_BASE ** (-jnp.arange(half, dtype=F32) / half)
    sign = jnp.concatenate([-jnp.ones((half,), F32), jnp.ones((half,), F32)])

    def axis_tabs(pos):
        ang = pos.astype(F32)[:, None] * inv[None, :]
        c = jnp.cos(ang)
        s = jnp.sin(ang)
        return jnp.concatenate([c, c], -1), jnp.concatenate([s, s], -1) * sign

    cr, sr = axis_tabs(t // GRID_W)
    cc, sc = axis_tabs(t % GRID_W)
    cos = jnp.concatenate([cr, cc], -1)
    sin = jnp.concatenate([sr, sc], -1)
    return jnp.tile(cos, (1, LANES // HEAD_DIM)), jnp.tile(sin, (1, LANES // HEAD_DIM))


def _na_patterns(rows):
    kh = NA_WIN_H
    nb = rows // NA_ROWS
    kr0s, sigs = [], []
    for b in range(nb):
        r_lo = b * NA_ROWS
        kr0 = int(np.clip(r_lo - kh // 2, 0, rows - NA_KROWS))
        sig = tuple((int(np.clip(r - kh // 2, 0, rows - kh)) - kr0, r - kr0)
                    for r in range(r_lo, r_lo + NA_ROWS))
        kr0s.append(kr0)
        sigs.append(sig)
    uniq = sorted(set(sigs))
    ids = np.array([uniq.index(s) for s in sigs], np.int32)
    return np.array(kr0s, np.int32), ids, uniq


def _na_bias(rpb, uniq, n_ctx):
    kw = NA_WIN_W
    cols = np.arange(GRID_W)
    col_start = np.clip(cols - kw // 2, 0, GRID_W - kw)
    n_pat = len(uniq)
    kk = np.arange(NA_KROWS)
    col_sel = (cols[None, None, :] - cols[None, :, None] + kw - 1
               == np.arange(2 * kw - 1)[:, None, None])
    col_ok = ((cols[None, :] >= col_start[:, None]) & (cols[None, :] < col_start[:, None] + kw))
    row_sel = np.zeros((n_pat, NA_ROWS, NA_KROWS, 2 * NA_WIN_H - 1), bool)
    row_ok = np.zeros((n_pat, NA_ROWS, NA_KROWS), bool)
    for p, sig in enumerate(uniq):
        for rq, (r0_rel, r_rel) in enumerate(sig):
            row_ok[p, rq] = (kk >= r0_rel) & (kk < r0_rel + NA_WIN_H)
            row_sel[p, rq] = (kk[:, None] - r_rel + NA_WIN_H - 1) == np.arange(2 * NA_WIN_H - 1)[None, :]
    hp = lax.Precision.HIGHEST
    by_col = jnp.einsum('hrv,vcj->hrcj', rpb.astype(F32), jnp.asarray(col_sel, F32), precision=hp)
    bias = jnp.einsum('pqkr,hrcj->phqckj', jnp.asarray(row_sel, F32), by_col, precision=hp)
    ok = row_ok[:, None, :, None, :, None] & col_ok[None, None, None, :, None, :]
    bias = jnp.where(jnp.asarray(ok), bias, NEG_INF)
    bias = bias.reshape(n_pat, rpb.shape[0], NA_ROWS * GRID_W, NA_KROWS * GRID_W)
    return jnp.concatenate([bias, jnp.zeros(bias.shape[:3] + (n_ctx,), F32)], axis=-1)


def _na_kernel(kr0_ref, pat_ref, q_ref, k_ref, v_ref, kc_ref, vc_ref, bias_ref, o_ref):
    b = pl.program_id(1)
    start = pl.multiple_of(kr0_ref[b] * GRID_W, GRID_W)
    nk = NA_KROWS * GRID_W
    scale = HEAD_DIM ** -0.5
    one = jnp.ones((), BF16)
    hp = LANES // HEAD_DIM
    for lt in range(q_ref.shape[1] // LANES):
        lanes = slice(lt * LANES, (lt + 1) * LANES)
        q = q_ref[:, lanes]
        k_all = jnp.concatenate([k_ref[pl.ds(start, nk), lanes], kc_ref[:, lanes]], axis=0)
        v_all = jnp.concatenate([v_ref[pl.ds(start, nk), lanes], vc_ref[:, lanes]], axis=0)
        lane = lax.broadcasted_iota(jnp.int32, v_all.shape, 1)
        ext = []
        for h in range(hp):
            sl = slice(h * HEAD_DIM, (h + 1) * HEAD_DIM)
            own = (lane >= h * HEAD_DIM) & (lane < (h + 1) * HEAD_DIM)
            s = _dot_nt(q[:, sl], k_all[:, sl]) * scale + bias_ref[0, lt * hp + h]
            p = jnp.exp(s - jnp.max(s, axis=-1, keepdims=True)).astype(BF16)
            ext.append(_dot(p, jnp.where(own, v_all, one)))
        out_lane = lax.broadcasted_iota(jnp.int32, ext[0].shape, 1)
        low = out_lane < HEAD_DIM
        num = jnp.where(low, ext[0], ext[1])
        den = pltpu.roll(jnp.where(low, ext[1], ext[0]), HEAD_DIM, 1)
        o_ref[:, lanes] = (num / den).astype(o_ref.dtype)


def _na_attention(q, k, v, kc, vc, bias):
    s, bw = q.shape
    l = kc.shape[0]
    rows = s // GRID_W
    kr0s, ids, _ = _na_patterns(rows)
    tq = NA_ROWS * GRID_W
    nk = NA_KROWS * GRID_W
    hp = NA_LANES // HEAD_DIM
    grid_spec = pltpu.PrefetchScalarGridSpec(
        num_scalar_prefetch=2,
        grid=(bw // NA_LANES, rows // NA_ROWS),
        in_specs=[pl.BlockSpec((tq, NA_LANES), lambda p, b, kr, pt: (b, p)),
                  pl.BlockSpec((s, NA_LANES), lambda p, b, kr, pt: (0, p)),
                  pl.BlockSpec((s, NA_LANES), lambda p, b, kr, pt: (0, p)),
                  pl.BlockSpec((l, NA_LANES), lambda p, b, kr, pt: (0, p)),
                  pl.BlockSpec((l, NA_LANES), lambda p, b, kr, pt: (0, p)),
                  pl.BlockSpec((1, hp, tq, nk + l), lambda p, b, kr, pt: (pt[b], p, 0, 0))],
        out_specs=pl.BlockSpec((tq, NA_LANES), lambda p, b, kr, pt: (b, p)),
    )
    return pl.pallas_call(
        _na_kernel,
        grid_spec=grid_spec,
        out_shape=jax.ShapeDtypeStruct((s, bw), BF16),
        compiler_params=_cparams(("arbitrary", "arbitrary")),
        name="na_attention",
    )(jnp.asarray(kr0s), jnp.asarray(ids), q, k, v, kc, vc, bias)


def _ctx_attn_kernel(*refs, n_heads, n_rep, has_sink):
    if has_sink:
        sink_ref, q_ref, k_ref, v_ref, o_ref = refs
    else:
        q_ref, k_ref, v_ref, o_ref = refs
    scale = HEAD_DIM ** -0.5
    q = q_ref[...]
    k = k_ref[...]
    v = v_ref[...]
    outs = []
    for h in range(n_heads):
        g = h // n_rep
        sl = slice(h * HEAD_DIM, (h + 1) * HEAD_DIM)
        gl = slice(g * HEAD_DIM, (g + 1) * HEAD_DIM)
        s = _dot_nt(q[:, sl], k[:, gl]) * scale
        m = jnp.max(s, axis=-1, keepdims=True)
        if has_sink:
            m = jnp.maximum(m, sink_ref[h])
        p = jnp.exp(s - m)
        den = jnp.sum(p, axis=-1, keepdims=True)
        if has_sink:
            den = den + jnp.exp(sink_ref[h] - m)
        outs.append(_dot(p.astype(BF16), v[:, gl]) / den)
    o_ref[...] = jnp.concatenate(outs, axis=-1).astype(o_ref.dtype)


def _ctx_attention(q, k, v, sink):
    l, bw = q.shape
    n_heads = bw // HEAD_DIM
    n_rep = n_heads // (k.shape[1] // HEAD_DIM)
    has_sink = sink is not None
    full = lambda a: pl.BlockSpec(a.shape, lambda i: (0, 0))
    in_specs = [full(q), full(k), full(v)]
    args = [q, k, v]
    if has_sink:
        in_specs = [pl.BlockSpec(memory_space=pltpu.SMEM)] + in_specs
        args = [sink] + args
    return pl.pallas_call(
        functools.partial(_ctx_attn_kernel, n_heads=n_heads, n_rep=n_rep, has_sink=has_sink),
        grid=(1,),
        in_specs=in_specs,
        out_specs=pl.BlockSpec((l, bw), lambda i: (0, 0)),
        out_shape=jax.ShapeDtypeStruct((l, bw), BF16),
        compiler_params=_cparams(("arbitrary",)),
        name="ctx_attention",
    )(*args)


def _swa_kernel(sink_ref, q_ref, k_ref, v_ref, kc_ref, vc_ref, o_ref, *, seq, n_kv, n_rep):
    n = pl.program_id(0)
    nk = 3 * SW_BLOCK
    scale = HEAD_DIM ** -0.5
    ws = jnp.clip((n - 1) * SW_BLOCK, 0, seq - nk)
    ws = pl.multiple_of(ws, SW_BLOCK)
    q = q_ref[...]
    n_all = nk + kc_ref.shape[0]
    k_all = jnp.concatenate([k_ref[pl.ds(ws, nk), :], kc_ref[...]], axis=0)
    v_all = jnp.concatenate([v_ref[pl.ds(ws, nk), :], vc_ref[...]], axis=0)
    qpos = n * SW_BLOCK + lax.broadcasted_iota(jnp.int32, (SW_BLOCK, n_all), 0)
    col = lax.broadcasted_iota(jnp.int32, (SW_BLOCK, n_all), 1)
    visible = (col >= nk) | (jnp.abs(ws + col - qpos) <= SW_WINDOW)
    maskb = jnp.where(visible, 0.0, NEG_INF).astype(F32)
    lane = lax.broadcasted_iota(jnp.int32, v_all.shape, 1)
    one = jnp.ones((), BF16)
    outs = [None] * (n_kv * n_rep)
    for g in range(n_kv):
        gl = slice(g * HEAD_DIM, (g + 1) * HEAD_DIM)
        own = (lane >= g * HEAD_DIM) & (lane < (g + 1) * HEAD_DIM)
        qg = jnp.concatenate([q[:, (g * n_rep + r) * HEAD_DIM:(g * n_rep + r + 1) * HEAD_DIM]
                              for r in range(n_rep)], axis=0)
        s = _dot_nt(qg, k_all[:, gl]) * scale
        s = (s.reshape(n_rep, SW_BLOCK, n_all) + maskb[None]).reshape(n_rep * SW_BLOCK, n_all)
        sink = jnp.concatenate([jnp.full((SW_BLOCK, 1), sink_ref[g * n_rep + r], F32)
                                for r in range(n_rep)], axis=0)
        m = jnp.maximum(jnp.max(s, axis=-1, keepdims=True), sink)
        p = jnp.exp(s - m).astype(BF16)
        ext = _dot(p, jnp.where(own, v_all, one))
        den = pltpu.roll(ext, HEAD_DIM, 1) + jnp.exp(sink - m)
        o = ext / den
        for r in range(n_rep):
            outs[g * n_rep + r] = o[r * SW_BLOCK:(r + 1) * SW_BLOCK, gl]
    o_ref[...] = jnp.concatenate(outs, axis=-1).astype(o_ref.dtype)


def _sw_attention(q, k, v, kc, vc, sink):
    s, bw = q.shape
    l = kc.shape[0]
    kvw = k.shape[1]
    n_kv = kvw // HEAD_DIM
    n_rep = (bw // HEAD_DIM) // n_kv
    assert kvw == LANES
    return pl.pallas_call(
        functools.partial(_swa_kernel, seq=s, n_kv=n_kv, n_rep=n_rep),
        grid=(s // SW_BLOCK,),
        in_specs=[pl.BlockSpec(memory_space=pltpu.SMEM),
                  pl.BlockSpec((SW_BLOCK, bw), lambda n: (n, 0)),
                  pl.BlockSpec((s, kvw), lambda n: (0, 0)),
                  pl.BlockSpec((s, kvw), lambda n: (0, 0)),
                  pl.BlockSpec((l, kvw), lambda n: (0, 0)),
                  pl.BlockSpec((l, kvw), lambda n: (0, 0))],
        out_specs=pl.BlockSpec((SW_BLOCK, bw), lambda n: (n, 0)),
        out_shape=jax.ShapeDtypeStruct((s, bw), BF16),
        compiler_params=_cparams(("arbitrary",)),
        name="sw_attention",
    )(sink, q, k, v, kc, vc)


def _s5_prepare(lam_re, lam_im, b_re, b_im, c_re, c_im, log_step):
    hp = lax.Precision.HIGHEST
    tc = S5_CHUNK
    lam = lax.complex(lam_re.astype(F32), lam_im.astype(F32))
    dt = jnp.exp(log_step.astype(F32))[..., None]
    lam_dt = lam * dt
    lam_bar = jnp.exp(lam_dt)
    b_bar = ((lam_bar - 1.0) / lam)[..., None] * lax.complex(b_re.astype(F32), b_im.astype(F32))
    c_mat = lax.complex(c_re.astype(F32), c_im.astype(F32))
    kk = jnp.arange(tc + 1, dtype=F32)
    pw = jnp.exp(lam_dt[None] * kk[:, None, None, None])
    n_g, n_p, n_h = b_bar.shape[1:]

    cw = jnp.transpose(c_mat[None] * pw[:tc, :, :, None, :], (1, 2, 0, 3, 4))
    kt = (jnp.einsum('dgpj,dgkhp->dgjkh', b_bar.real, cw.real, precision=hp)
          - jnp.einsum('dgpj,dgkhp->dgjkh', b_bar.imag, cw.imag, precision=hp))
    kt = jnp.stack([kt[0], kt[1][:, :, ::-1]], axis=1).reshape(n_g, 2, n_h, tc * n_h)
    t_i = np.arange(tc)

    def cat_ri(z):
        return jnp.concatenate([z.real, z.imag], axis=-1)

    w_f = pw[tc - 1 - t_i, 0][:, :, None, :] * jnp.transpose(b_bar[0], (0, 2, 1))[None]
    w_r = pw[t_i, 1][:, :, None, :] * jnp.transpose(b_bar[1], (0, 2, 1))[None]
    w_f = jnp.transpose(cat_ri(w_f), (1, 0, 2, 3)).reshape(n_g, tc * n_h, 2 * n_p)
    w_r = jnp.transpose(cat_ri(w_r), (1, 0, 2, 3)).reshape(n_g, tc * n_h, 2 * n_p)
    w_cat = jnp.concatenate([w_f, w_r], axis=-1).astype(BF16)

    def v_of(c_dir, pw_sel):
        cp = c_dir[None] * pw_sel[:, :, None, :]
        v = jnp.concatenate([cp.real, -cp.imag], axis=-1)
        return jnp.transpose(v, (1, 3, 0, 2)).reshape(n_g, 2 * n_p, tc * n_h)
    v_cat = jnp.concatenate([v_of(c_mat[0], pw[t_i + 1, 0]), v_of(c_mat[1], pw[tc - t_i, 1])],
                            axis=1).astype(BF16)

    def coef(k):
        a = jnp.exp(lam_dt * (tc * k))
        return jnp.stack([jnp.concatenate([a.real, a.real], -1),
                          jnp.concatenate([-a.imag, a.imag], -1)], axis=2)
    a_log = jnp.stack([coef(1.0), coef(2.0), coef(4.0)], axis=2)
    a_car = jnp.stack([coef(float(j)) for j in range(1, SUBLANES + 1)], axis=3)
    a_car = jnp.stack([a_car[0], a_car[1][:, :, ::-1]], axis=0)
    return _s5_expand(kt, w_cat, v_cat, jnp.transpose(a_log, (1, 0, 2, 3, 4)),
                      jnp.transpose(a_car, (1, 0, 2, 3, 4)))


def _s5_expand(kt, w_cat, v_cat, a_log, a_car):
    n_g = kt.shape[0]
    n_t = n_g // S5_LG
    n_state = 2 * S5_STATE
    by_tile = lambda a: a.reshape((n_t, S5_LG) + a.shape[1:])

    def lanes_of(a):
        a = jnp.moveaxis(by_tile(a), 1, -2)
        return a.reshape(a.shape[:-2] + (S5_LG * n_state,))
    return by_tile(kt), by_tile(w_cat), by_tile(v_cat), lanes_of(a_log), lanes_of(a_car)


def _s5_lane_perm():
    r = np.arange(S5_CHUNK * LANES)
    t, a, h = r // LANES, (r % LANES) // S5_GROUP, r % S5_GROUP
    dest = a * (S5_CHUNK * S5_GROUP) + t * S5_GROUP + h
    return (jnp.asarray(dest)[:, None] == jnp.arange(S5_CHUNK * LANES)[None, :]).astype(BF16)


def _chunk_rows(tok_ref):
    n_rows = tok_ref.shape[1] // S5_CHUNK
    return jnp.concatenate([tok_ref[0, pl.ds(t, n_rows, stride=S5_CHUNK), :] for t in range(S5_CHUNK)], axis=-1)


def _s5_in_kernel(u_ref, perm_ref, kt_ref, w_ref, yp_ref, z_ref, m_scr):
    width = S5_CHUNK * S5_GROUP
    n_state = 2 * S5_STATE
    half = S5_LG * n_state

    @pl.when(pl.program_id(1) == 0)
    def _():
        lane = lax.broadcasted_iota(jnp.int32, (S5_GROUP, width), 1)
        for a in range(S5_LG):
            k_fwd = kt_ref[0, a, 0]
            k_rev = kt_ref[0, a, 1]
            for tau in range(S5_CHUNK):
                right = tau * S5_GROUP
                left = (S5_CHUNK - 1 - tau) * S5_GROUP
                blk = (jnp.where(lane >= right, pltpu.roll(k_fwd, right, 1), 0.0)
                       + jnp.where(lane < width - left, pltpu.roll(k_rev, (width - left) % width, 1), 0.0))
                m_scr[a, tau * S5_GROUP:(tau + 1) * S5_GROUP, 0:width] = blk.astype(BF16)
            m_scr[a, :, width:] = w_ref[0, a]

    up = _dot(_chunk_rows(u_ref).astype(BF16), perm_ref[...]).astype(BF16)
    for a in range(S5_LG):
        zz = _dot(up[:, a * width:(a + 1) * width], m_scr[a])
        yp_ref[0, :, a * width:(a + 1) * width] = zz[:, :width]
        z_ref[0, :, a * n_state:(a + 1) * n_state] = zz[:, width:width + n_state]
        z_ref[0, :, half + a * n_state:half + (a + 1) * n_state] = zz[:, width + n_state:]


def _s5_scan_kernel(z_ref, alog_ref, acar_ref, e_ref, *, nc, nc_ctx):
    n_state = 2 * S5_STATE
    half = z_ref.shape[2] // 2
    nblk = nc // SUBLANES
    nblk_ctx = nc_ctx // SUBLANES
    row = lax.broadcasted_iota(jnp.int32, (SUBLANES, n_state), 0)

    def cmul(a1, a2, s):
        return a1 * s + a2 * pltpu.roll(s, S5_STATE, 1)

    def step(t, carry):
        blk = (t, jnp.where(t < nblk_ctx, nblk_ctx - 1 - t, nblk + nblk_ctx - 1 - t))
        new = []
        for d in range(2):
            r0 = pl.multiple_of(blk[d] * SUBLANES, SUBLANES)
            for j in range(half // n_state):
                tile = slice(j * n_state, (j + 1) * n_state)
                lanes = slice(d * half + j * n_state, d * half + (j + 1) * n_state)
                z = z_ref[0, pl.ds(r0, SUBLANES), lanes]
                c = carry[len(new)]
                for i, sh in enumerate((1, 2, 4)):
                    if d == 0:
                        zs = jnp.where(row >= sh, pltpu.roll(z, sh, 0), 0.0)
                    else:
                        zs = jnp.where(row < SUBLANES - sh, pltpu.roll(z, SUBLANES - sh, 0), 0.0)
                    z = z + cmul(alog_ref[0, d, i, 0:1, tile], alog_ref[0, d, i, 1:2, tile], zs)
                cb = jnp.broadcast_to(c, (SUBLANES, n_state))
                s = z + cmul(acar_ref[0, d, 0, :, tile], acar_ref[0, d, 1, :, tile], cb)
                if d == 0:
                    e = jnp.where(row >= 1, pltpu.roll(s, 1, 0), cb)
                    c_new = s[SUBLANES - 1:SUBLANES, :]
                else:
                    e = jnp.where(row < SUBLANES - 1, pltpu.roll(s, SUBLANES - 1, 0), cb)
                    c_new = s[0:1, :]
                e_ref[0, pl.ds(r0, SUBLANES), lanes] = e
                new.append(c_new)
        return tuple(new)

    init = tuple(jnp.zeros((1, n_state), F32) for _ in range(2 * (half // n_state)))
    lax.fori_loop(0, nblk, step, init)


def _s5_out_kernel(e_ref, yp_ref, u_ref, d_ref, perm_ref, v_ref, y_ref):
    width = S5_CHUNK * S5_GROUP
    n_state = 2 * S5_STATE
    half = S5_LG * n_state
    parts = []
    for a in range(S5_LG):
        e_a = jnp.concatenate([e_ref[0, :, a * n_state:(a + 1) * n_state],
                               e_ref[0, :, half + a * n_state:half + (a + 1) * n_state]], axis=-1)
        parts.append(yp_ref[0, :, a * width:(a + 1) * width] + _dot(e_a.astype(BF16), v_ref[0, a]))
    y = jnp.concatenate(parts, axis=-1)
    hi = y.astype(BF16)
    lo = (y - hi.astype(F32)).astype(BF16)
    perm = perm_ref[...]
    y = _dot_nt(hi, perm) + _dot_nt(lo, perm) + _chunk_rows(u_ref) * d_ref[0]
    n_rows = y.shape[0]
    for t in range(S5_CHUNK):
        y_ref[0, pl.ds(t, n_rows, stride=S5_CHUNK), :] = y[:, t * LANES:(t + 1) * LANES]


def _s5_rows(nc):
    return max(r for r in range(2 * SUBLANES, 257, 2 * SUBLANES) if nc % r == 0)


def _s5_mixer_pre_glu(u_ctx, u_lat, prep, d_skip):
    kt, w_cat, v_cat, a_log, a_car = prep
    n_t, l, _ = u_ctx.shape
    s = u_lat.shape[1]
    nc, nc_ctx = (l + s) // S5_CHUNK, l // S5_CHUNK
    width = S5_CHUNK * LANES
    n_z = 2 * S5_LG * 2 * S5_STATE
    u = jnp.concatenate([u_ctx, u_lat], axis=1)
    d_rows = jnp.tile(d_skip.astype(F32).reshape(n_t, 1, LANES), (1, 1, S5_CHUNK))
    perm = _s5_lane_perm()
    rb = _s5_rows(nc)
    rows = lambda w: pl.BlockSpec((1, rb, w), lambda g, r: (g, r, 0))
    per_tile = lambda a: pl.BlockSpec((1,) + a.shape[1:], lambda g, *_: (g,) + (0,) * (a.ndim - 1))
    whole = lambda a: pl.BlockSpec(a.shape, lambda g, r: (0,) * a.ndim)
    tokens = pl.BlockSpec((1, rb * S5_CHUNK, LANES), lambda g, r: (g, r, 0))
    width_g = S5_CHUNK * S5_GROUP
    yp, z = pl.pallas_call(
        _s5_in_kernel,
        grid=(n_t, nc // rb),
        in_specs=[tokens, whole(perm), per_tile(kt), per_tile(w_cat)],
        out_specs=[rows(width), rows(n_z)],
        out_shape=[jax.ShapeDtypeStruct((n_t, nc, width), F32), jax.ShapeDtypeStruct((n_t, nc, n_z), F32)],
        scratch_shapes=[pltpu.VMEM((S5_LG, width_g, width_g + 4 * S5_STATE), BF16)],
        compiler_params=_cparams(("arbitrary", "arbitrary")),
        name="s5_in",
    )(u, perm, kt, w_cat)
    e = pl.pallas_call(
        functools.partial(_s5_scan_kernel, nc=nc, nc_ctx=nc_ctx),
        grid=(n_t,),
        in_specs=[per_tile(z), per_tile(a_log), per_tile(a_car)],
        out_specs=per_tile(z),
        out_shape=jax.ShapeDtypeStruct(z.shape, F32),
        compiler_params=_cparams(("arbitrary",)),
        name="s5_scan",
    )(z, a_log, a_car)
    y = pl.pallas_call(
        _s5_out_kernel,
        grid=(n_t, nc // rb),
        in_specs=[rows(n_z), rows(width), tokens, per_tile(d_rows), whole(perm), per_tile(v_cat)],
        out_specs=tokens,
        out_shape=jax.ShapeDtypeStruct((n_t, l + s, LANES), F32),
        compiler_params=_cparams(("arbitrary", "arbitrary")),
        name="s5_out",
    )(e, yp, u, d_rows, perm, v_cat)
    return y


def _merge_kernel(x_ref, ya_ref, y5_ref, ys_ref, ga_ref, gb_ref, gs_ref, wglu_ref, wb_ref, wo_ref,
                  gt_ref, g_ref, sh_ref, sc_ref, xo_ref, h_ref, ht_ref):
    y5 = jnp.concatenate([y5_ref[t] for t in range(y5_ref.shape[0])], axis=-1)
    z = _gelu(y5)
    yb = z * jax.nn.sigmoid(_dot(z.astype(BF16), wglu_ref[...]))
    m = (ga_ref[...].astype(F32) * _dot(ya_ref[...], wb_ref[0])
         + gb_ref[...].astype(F32) * _dot(yb.astype(BF16), wb_ref[1])
         + gs_ref[...].astype(F32) * _dot(ys_ref[...], wb_ref[2]))
    x = x_ref[...] + gt_ref[...] * _dot(m.astype(BF16), wo_ref[...])
    xo_ref[...] = x
    h = _norm_mod(x, g_ref[...], sh_ref[...], sc_ref[...])
    h_ref[...] = h.astype(BF16)
    ht_ref[...] = h.T.astype(BF16)


def _merge(x, ya, y5, y5_row0, ys, ga, gb, gs, w_glu, w_branch, w_out, gate, g, shift, scale):
    n, d = x.shape
    bw = ya.shape[1]
    tm = min(ROW_BLOCK, n)
    assert y5_row0 % tm == 0
    row = lambda w: pl.BlockSpec((tm, w), lambda i: (i, 0))
    full = lambda a: pl.BlockSpec(a.shape, lambda i: (0,) * a.ndim)
    vec = pl.BlockSpec((1, d), lambda i: (0, 0))
    y5_rows = pl.BlockSpec((y5.shape[0], tm, LANES), lambda i: (0, i + y5_row0 // tm, 0))
    return pl.pallas_call(
        _merge_kernel,
        grid=(n // tm,),
        in_specs=[row(d), row(bw), y5_rows, row(bw), row(d), row(d), row(d),
                  full(w_glu), full(w_branch), full(w_out), vec, vec, vec, vec],
        out_specs=[row(d), row(d), pl.BlockSpec((d, tm), lambda i: (0, i))],
        out_shape=[jax.ShapeDtypeStruct((n, d), F32), jax.ShapeDtypeStruct((n, d), BF16),
                   jax.ShapeDtypeStruct((d, n), BF16)],
        compiler_params=_cparams(("arbitrary",)),
        name="merge",
    )(x, ya, y5, ys, ga, gb, gs, w_glu, w_branch, w_out, gate, g, shift, scale)


def _knock_out_16(s, order, exact, want_rank=True):
    rank = jnp.full(s.shape, float(PEER_TOPK), F32) if want_rank else None
    live = s
    vals = []
    for r in range(PEER_TOPK):
        m = jnp.max(live, axis=0, keepdims=True)
        hit = live == m
        if exact:
            first = jnp.min(jnp.where(hit, order, float(PEER_TOPK * PEER_NKEYS)), axis=0, keepdims=True)
            hit = order == first
        if want_rank:
            rank = jnp.where(hit, float(r), rank)
        live = jnp.where(hit, -jnp.inf, live)
        vals.append(m)
    out = (rank < float(PEER_TOPK)) if want_rank else (live != s)
    n_out = jnp.sum(jnp.where(out, 1.0, 0.0), axis=0, keepdims=True)
    return rank, jnp.concatenate(vals, axis=0), n_out


def _bf16_pair_word(x):
    hi = pltpu.bitcast(x.astype(BF16).astype(F32), jnp.uint32)
    return hi | (hi >> 16)


def _pair_tiles():
    tiles = [(0, 0, 8), (0, 8, 8)]
    for a in range(1, 8):
        tiles.append((a, 0, PEER_TOPK // (a + 1)))
    return tiles


def _route_kernel(h_ref, wq_ref, k1_ref, k2_ref, cnt_ref, c1_ref, rk_ref, e2_ref, q_scr):
    tb = h_ref.shape[0]
    q_scr[...] = _dot(h_ref[...], wq_ref[...]).astype(BF16)
    half = PEER_QDIM // 2
    iota = lax.broadcasted_iota(jnp.int32, (PEER_NKEYS, tb), 0).astype(F32)
    row8 = lax.broadcasted_iota(jnp.int32, (SUBLANES, tb), 0).astype(F32)
    k1 = k1_ref[...]
    k2 = k2_ref[...]

    def emit(hd, s1, s2, exact):
        rank1, v1, n1 = _knock_out_16(s1, iota, exact, want_rank=exact)
        rank2, v2, n2 = _knock_out_16(s2, iota, exact)
        e1v = jnp.exp(v1 - v1[0:1])
        e2v = jnp.exp(v2 - v2[0:1])
        tiles, flats, gates = [], [], []
        for a, b0, nv in _pair_tiles():
            c = v1[a:a + 1] + v2[b0:b0 + SUBLANES]
            tiles.append(jnp.where(row8 < nv, c, -jnp.inf))
            flats.append(a * PEER_TOPK + b0 + row8)
            gates.append(e1v[a:a + 1] * e2v[b0:b0 + SUBLANES])
        tiles.append(v1[SUBLANES:] + v2[0:1])
        flats.append((row8 + SUBLANES) * PEER_TOPK)
        gates.append(e1v[SUBLANES:] * e2v[0:1])
        cand = jnp.concatenate(tiles, axis=0)
        flat = jnp.concatenate(flats, axis=0)
        gate = jnp.concatenate(gates, axis=0)
        rank_c, _, n_c = _knock_out_16(cand, flat, exact)
        self_ = jnp.where(rank_c < float(PEER_TOPK), 1.0, 0.0)
        z = jnp.sum(self_ * gate, axis=0, keepdims=True)
        cnt = [self_[0:8].sum(axis=0, keepdims=True) + self_[8:16].sum(axis=0, keepdims=True)]
        for t in range(2, 9):
            cnt.append(self_[t * SUBLANES:(t + 1) * SUBLANES].sum(axis=0, keepdims=True))
        cnt = jnp.concatenate(cnt + [self_[9 * SUBLANES:]], axis=0)
        cnt1 = jnp.zeros((PEER_NKEYS, tb), F32)
        for a in range(PEER_TOPK):
            is_a = (rank1 == float(a)) if exact else (s1 == v1[a:a + 1])
            cnt1 = jnp.where(is_a, cnt[a:a + 1], cnt1)
        cnt_ref[hd] = _bf16_pair_word(cnt1)
        c1_ref[hd] = _bf16_pair_word(jnp.exp(s1 - v1[0:1]) * (0.5 / z))
        rk_ref[hd] = rank2.astype(BF16)
        e2_ref[hd] = jnp.exp(s2 - v2[0:1]).astype(BF16)
        want = float(PEER_TOPK)
        return jnp.where((n1 != want) | (n2 != want) | (n_c != want), 1.0, 0.0)

    def head_group(hg, _):
        scores = []
        for k in range(PEER_ROUTE_HEADS):
            hd = hg * PEER_ROUTE_HEADS + k
            c0 = pl.multiple_of(hd * PEER_QDIM, PEER_QDIM)
            s1 = _dot_nt(k1, q_scr[:, pl.ds(c0, half)])
            s2 = _dot_nt(k2, q_scr[:, pl.ds(c0 + half, half)])
            scores.append((hd, s1, s2))
        tied = [jnp.max(emit(hd, s1, s2, exact=False)) for hd, s1, s2 in scores]
        for (hd, s1, s2), t in zip(scores, tied):
            @pl.when(t > 0.0)
            def _():
                emit(hd, s1, s2, exact=True)
        return 0

    lax.fori_loop(0, PEER_HEADS // PEER_ROUTE_HEADS, head_group, 0)


def _peer_route(h, w_q, k1, k2):
    n, d = h.shape
    tb = min(PEER_ROUTE_BLOCK, n)
    qw = w_q.shape[1]
    out_blk = pl.BlockSpec((PEER_HEADS, PEER_NKEYS, tb), lambda i: (0, 0, i))
    shp = lambda dt: jax.ShapeDtypeStruct((PEER_HEADS, PEER_NKEYS, n), dt)
    return pl.pallas_call(
        _route_kernel,
        grid=(n // tb,),
        in_specs=[pl.BlockSpec((tb, d), lambda i: (i, 0)),
                  pl.BlockSpec((d, qw), lambda i: (0, 0)),
                  pl.BlockSpec(k1.shape, lambda i: (0, 0)),
                  pl.BlockSpec(k2.shape, lambda i: (0, 0))],
        out_specs=[out_blk] * 4,
        out_shape=[shp(jnp.uint32), shp(jnp.uint32), shp(BF16), shp(BF16)],
        scratch_shapes=[pltpu.VMEM((tb, qw), BF16)],
        compiler_params=_cparams(("arbitrary",)),
        name="peer_route",
    )(h, w_q, k1, k2)


def _dense_kernel(ht_ref, x_ref, gt_ref, u_ref, vt_ref, cnt_ref, c1_ref, rk_ref, e2_ref, o_ref,
                  acc_ref, w_scr):
    eb = pl.program_id(1)
    tb = ht_ref.shape[1]
    n_exp = u_ref.shape[0]
    n_sub = n_exp // PEER_SUB
    i_per_sub = PEER_SUB // PEER_NKEYS

    @pl.when(eb == 0)
    def _():
        acc_ref[...] = jnp.zeros_like(acc_ref)

    def scores(sb):
        return _dot(u_ref[sb * PEER_SUB:(sb + 1) * PEER_SUB, :], ht_ref[...])

    pk_rows = 2 * SUBLANES
    n_pk = PEER_NKEYS // pk_rows

    def row_tile(ref, hd, i):
        words = jnp.broadcast_to(ref[hd, pl.ds(i, 1), :], (SUBLANES, tb))
        return pltpu.bitcast(words, BF16)[None]

    a_next = scores(0)
    for sb in range(n_sub):
        a_cur = a_next
        if sb + 1 < n_sub:
            a_next = scores(sb + 1)
        for il in range(i_per_sub):
            i = eb * (n_exp // PEER_NKEYS) + sb * i_per_sub + il
            gsum = jnp.zeros((n_pk, pk_rows, tb), BF16)
            for hd in range(PEER_HEADS):
                cnt = row_tile(cnt_ref, hd, i)
                c1 = row_tile(c1_ref, hd, i)
                rk = rk_ref[hd].reshape(n_pk, pk_rows, tb)
                e2 = e2_ref[hd].reshape(n_pk, pk_rows, tb)
                gsum = gsum + jnp.where(rk < cnt, e2 * c1, jnp.zeros((), BF16))
            r0 = sb * PEER_SUB + il * PEER_NKEYS
            a_i = a_cur[il * PEER_NKEYS:(il + 1) * PEER_NKEYS]
            half_gate = gsum.reshape(PEER_NKEYS, tb).astype(F32)
            w_i = a_i * (1.0 + lax.erf(a_i * (1.0 / math.sqrt(2.0)))) * half_gate
            w_scr[r0:r0 + PEER_NKEYS, :] = w_i.astype(BF16)
        done = (sb + 1) * PEER_SUB
        if done % PEER_ACC_CHUNK == 0:
            c0 = done - PEER_ACC_CHUNK
            acc_ref[...] += _dot(vt_ref[:, c0:done], w_scr[c0:done, :])

    @pl.when(eb == pl.num_programs(1) - 1)
    def _():
        o_ref[...] = x_ref[...] + gt_ref[...] * acc_ref[...].T


def _peer_dense(h_t, x, gate, u, v_t, layer, cnt1, c1, rank2, e2):
    d, n = h_t.shape
    n_e = u.shape[1]
    tb = min(PEER_TOK_BLOCK, n)
    eb = PEER_EXP_BLOCK
    tab = pl.BlockSpec((PEER_HEADS, PEER_NKEYS, tb), lambda i, e: (0, 0, i))
    return pl.pallas_call(
        _dense_kernel,
        grid=(n // tb, n_e // eb),
        in_specs=[pl.BlockSpec((d, tb), lambda i, e: (0, i)),
                  pl.BlockSpec((tb, d), lambda i, e: (i, 0)),
                  pl.BlockSpec((1, d), lambda i, e: (0, 0)),
                  pl.BlockSpec((None, eb, d), lambda i, e: (layer, e, 0)),
                  pl.BlockSpec((None, d, eb), lambda i, e: (layer, 0, e)),
                  tab, tab, tab, tab],
        out_specs=pl.BlockSpec((tb, d), lambda i, e: (i, 0)),
        out_shape=jax.ShapeDtypeStruct((n, d), F32),
        scratch_shapes=[pltpu.VMEM((d, tb), F32), pltpu.VMEM((eb, tb), BF16)],
        compiler_params=_cparams(("arbitrary", "arbitrary")),
        name="peer_dense",
    )(h_t, x, gate, u, v_t, cnt1, c1, rank2, e2)


def _final_norm_kernel(x_ref, g_ref, o_ref):
    x = x_ref[...]
    o_ref[...] = x * lax.rsqrt(jnp.mean(x * x, axis=-1, keepdims=True) + EPS) * g_ref[...]


def _final_norm(x, g):
    n, d = x.shape
    tm = min(2 * ROW_BLOCK, n)
    return pl.pallas_call(
        _final_norm_kernel,
        grid=(n // tm,),
        in_specs=[pl.BlockSpec((tm, d), lambda i: (i, 0)), pl.BlockSpec((1, d), lambda i: (0, 0))],
        out_specs=pl.BlockSpec((tm, d), lambda i: (i, 0)),
        out_shape=jax.ShapeDtypeStruct((n, d), F32),
        compiler_params=_cparams(("arbitrary",)),
        name="final_norm",
    )(x, g)


def kernel(x, c, ctx, c_ctx, w_mod, b_mod, g_mix, g_ffn, w_in, na_rpb, s5_lam_re, s5_lam_im,
           s5_b_re, s5_b_im, s5_c_re, s5_c_im, s5_log_step, s5_d, s5_w_glu, sw_sink, w_branch,
           w_out, peer_w_q, peer_sub_keys, peer_u, peer_v, g_final):
    batch, seq, d = x.shape
    l_ctx = ctx.shape[1]
    depth = w_mod.shape[0]
    assert batch == 1 and seq % (NA_ROWS * GRID_W) == 0 and l_ctx % (S5_CHUNK * SUBLANES) == 0
    assert seq // GRID_W >= NA_KROWS and seq >= 3 * SW_BLOCK

    cc = jnp.zeros((SUBLANES, d), F32).at[0].set(c[0]).at[1].set(c_ctx)
    mod = _mod_vectors(cc, w_mod, b_mod).reshape(depth, SUBLANES, 6, d)
    rope_tabs = _rope_tables(seq)
    row = lambda v: v.reshape(1, d)
    na_uniq = _na_patterns(seq // GRID_W)[2]
    na_bias = jax.vmap(lambda r: _na_bias(r, na_uniq, l_ctx))(na_rpb)
    s5_prep = jax.vmap(_s5_prepare)(s5_lam_re, s5_lam_im, s5_b_re, s5_b_im, s5_c_re, s5_c_im, s5_log_step)
    u_all = peer_u.astype(BF16)
    vt_all = jnp.transpose(peer_v.astype(BF16), (0, 2, 1))

    xx, xc = x[0], ctx[0]
    for l in range(depth):
        need_ctx = l < depth - 1
        m_lat, m_ctx = mod[l, 0], mod[l, 1]
        w_in_l = w_in[l].astype(BF16)
        g_mix_l = row(g_mix[l])
        g_ffn_l = row(g_ffn[l])

        qa, ka, va, ub, qs, ks, vs, ga, gb, gs = _inproj(
            xx, g_mix_l, row(m_lat[0]), row(m_lat[1]), w_in_l, rope_tabs)
        qa_c, ka_c, va_c, ub_c, qs_c, ks_c, vs_c, ga_c, gb_c, gs_c = _inproj(
            xc, g_mix_l, row(m_ctx[0]), row(m_ctx[1]), w_in_l, None)

        ya = _na_attention(qa, ka, va, ka_c, va_c, na_bias[l])
        ys = _sw_attention(qs, ks, vs, ks_c, vs_c, sw_sink[l])
        y5 = _s5_mixer_pre_glu(ub_c, ub, [a[l] for a in s5_prep], s5_d[l])

        w_glu_l = s5_w_glu[l].astype(BF16)
        w_branch_l = w_branch[l].astype(BF16)
        w_out_l = w_out[l].astype(BF16)
        xx, hx2, hx2_t = _merge(xx, ya, y5, l_ctx, ys, ga, gb, gs, w_glu_l, w_branch_l, w_out_l,
                                row(m_lat[2]), g_ffn_l, row(m_lat[3]), row(m_lat[4]))

        w_q_l = peer_w_q[l].astype(BF16)
        k1 = peer_sub_keys[l, 0].astype(BF16)
        k2 = peer_sub_keys[l, 1].astype(BF16)
        if need_ctx:
            ya_c = _ctx_attention(qa_c, ka_c, va_c, None)
            ys_c = _ctx_attention(qs_c, ks_c, vs_c, sw_sink[l])
            xc, hc2, hc2_t = _merge(xc, ya_c, y5, 0, ys_c, ga_c, gb_c, gs_c, w_glu_l, w_branch_l, w_out_l,
                                    row(m_ctx[2]), g_ffn_l, row(m_ctx[3]), row(m_ctx[4]))
            xc = _peer_dense(hc2_t, xc, row(m_ctx[5]), u_all, vt_all, l, *_peer_route(hc2, w_q_l, k1, k2))
        xx = _peer_dense(hx2_t, xx, row(m_lat[5]), u_all, vt_all, l, *_peer_route(hx2, w_q_l, k1, k2))

    return _final_norm(xx, row(g_final))[None]
```

```python
import functools
import math

import numpy as np
import jax
import jax.numpy as jnp
from jax import lax
from jax.experimental import pallas as pl
from jax.experimental.pallas import tpu as pltpu

F32 = jnp.float32
BF16 = jnp.bfloat16

GRID_W = 64
HEAD_DIM = 64
NA_WIN_H = 8
NA_WIN_W = 16
S5_GROUP = 16
S5_STATE = 64
SW_WINDOW = 128
ROPE_BASE = 10000.0
PEER_HEADS = 8
PEER_NKEYS = 128
PEER_QDIM = 256
PEER_TOPK = 16
EPS = 1e-6
NEG_INF = -1e30

LANES = 128
SUBLANES = 8
VMEM_LIMIT_BYTES = 56 * 1024 * 1024

ROW_BLOCK = 256
NA_ROWS = 4
NA_KROWS = NA_ROWS + NA_WIN_H - 1
NA_LANES = 256
SW_BLOCK = 128
S5_CHUNK = 16
S5_LG = LANES // S5_GROUP
PEER_ROUTE_BLOCK = 256
PEER_ROUTE_HEADS = 4
PEER_TOK_BLOCK = 512
PEER_EXP_BLOCK = 2048
PEER_SUB = 512
PEER_ACC_CHUNK = 1024


def _cparams(sem):
    return pltpu.CompilerParams(dimension_semantics=sem, vmem_limit_bytes=VMEM_LIMIT_BYTES)


def _dot(a, b):
    return jnp.dot(a, b, preferred_element_type=F32)


def _dot_nt(a, b):
    return lax.dot_general(a, b, (((1,), (1,)), ((), ())), preferred_element_type=F32)


def _gelu(x):
    return 0.5 * x * (1.0 + lax.erf(x * (1.0 / math.sqrt(2.0))))


def _mod_kernel(cc_ref, w_ref, b_ref, o_ref):
    a = cc_ref[...]
    a = a * jax.nn.sigmoid(a)
    o_ref[0] = _dot(a.astype(BF16), w_ref[0].astype(BF16)) + b_ref[0]


def _mod_vectors(cc, w_mod, b_mod):
    depth, d, n6 = w_mod.shape
    tn = 1024
    return pl.pallas_call(
        _mod_kernel,
        grid=(depth, n6 // tn),
        in_specs=[pl.BlockSpec((SUBLANES, d), lambda l, j: (0, 0)),
                  pl.BlockSpec((1, d, tn), lambda l, j: (l, 0, j)),
                  pl.BlockSpec((1, 1, tn), lambda l, j: (l, 0, j))],
        out_specs=pl.BlockSpec((1, SUBLANES, tn), lambda l, j: (l, 0, j)),
        out_shape=jax.ShapeDtypeStruct((depth, SUBLANES, n6), F32),
        compiler_params=_cparams(("arbitrary", "arbitrary")),
        name="mod_vectors",
    )(cc, w_mod, b_mod.reshape(depth, 1, n6))


def _norm_mod(x, g, shift, scale):
    y = x * lax.rsqrt(jnp.mean(x * x, axis=-1, keepdims=True) + EPS)
    return (y * g) * (1.0 + scale) + shift


def _rope(x, cos, sin, lane_lo):
    up = pltpu.roll(x, LANES - 16, 1)
    dn = pltpu.roll(x, 16, 1)
    return x * cos + jnp.where(lane_lo, up, dn) * sin


def _inproj_kernel(x_ref, g_ref, sh_ref, sc_ref, w_ref, *rest, bw, kvw, d_model, rope):
    if rope:
        cos_ref, sin_ref = rest[:2]
        rest = rest[2:]
    qa_ref, ka_ref, va_ref, ub_ref, qs_ref, ks_ref, vs_ref, ga_ref, gb_ref, gs_ref = rest
    h = _norm_mod(x_ref[...], g_ref[...], sh_ref[...], sc_ref[...]).astype(BF16)

    def proj(c0, width):
        return _dot(h, w_ref[:, c0:c0 + width])

    c = 0
    qa_ref[...] = proj(c, bw).astype(BF16); c += bw
    ka_ref[...] = proj(c, bw).astype(BF16); c += bw
    va_ref[...] = proj(c, bw).astype(BF16); c += bw
    ub = proj(c, bw); c += bw
    for t in range(bw // LANES):
        ub_ref[t] = ub[:, t * LANES:(t + 1) * LANES]
    qs = proj(c, bw); c += bw
    ks = proj(c, kvw); c += kvw
    vs_ref[...] = proj(c, kvw).astype(BF16); c += kvw
    if rope:
        cos = cos_ref[...]
        sin = sin_ref[...]
        lane = lax.broadcasted_iota(jnp.int32, cos.shape, 1)
        lane_lo = (lane % 32) < 16
        for p in range(bw // LANES):
            sl = slice(p * LANES, (p + 1) * LANES)
            qs_ref[:, sl] = _rope(qs[:, sl], cos, sin, lane_lo).astype(BF16)
        for p in range(kvw // LANES):
            sl = slice(p * LANES, (p + 1) * LANES)
            ks_ref[:, sl] = _rope(ks[:, sl], cos, sin, lane_lo).astype(BF16)
    else:
        qs_ref[...] = qs.astype(BF16)
        ks_ref[...] = ks.astype(BF16)
    ga_ref[...] = jax.nn.sigmoid(proj(c, d_model)).astype(BF16); c += d_model
    gb_ref[...] = jax.nn.sigmoid(proj(c, d_model)).astype(BF16); c += d_model
    gs_ref[...] = jax.nn.sigmoid(proj(c, d_model)).astype(BF16)


def _inproj(x, g, shift, scale, w_in, rope_tabs):
    n, d = x.shape
    bw = d // 2
    kvw = bw // 4
    tm = min(ROW_BLOCK, n)
    rope = rope_tabs is not None
    row = lambda i: (i, 0)
    fixed = lambda i: (0, 0)
    in_specs = [pl.BlockSpec((tm, d), row), pl.BlockSpec((1, d), fixed), pl.BlockSpec((1, d), fixed),
                pl.BlockSpec((1, d), fixed), pl.BlockSpec(w_in.shape, fixed)]
    args = [x, g, shift, scale, w_in]
    if rope:
        in_specs += [pl.BlockSpec((tm, LANES), row)] * 2
        args += list(rope_tabs)
    widths = [bw, bw, bw, bw, bw, kvw, kvw, d, d, d]
    dtypes = [BF16, BF16, BF16, F32, BF16, BF16, BF16, BF16, BF16, BF16]
    s5_slot = 3
    return pl.pallas_call(
        functools.partial(_inproj_kernel, bw=bw, kvw=kvw, d_model=d, rope=rope),
        grid=(n // tm,),
        in_specs=in_specs,
        out_specs=[pl.BlockSpec((bw // LANES, tm, LANES), lambda i: (0, i, 0)) if k == s5_slot
                   else pl.BlockSpec((tm, w), row) for k, w in enumerate(widths)],
        out_shape=[jax.ShapeDtypeStruct((bw // LANES, n, LANES) if k == s5_slot else (n, w), dt)
                   for k, (w, dt) in enumerate(zip(widths, dtypes))],
        compiler_params=_cparams(("arbitrary",)),
        name="inproj",
    )(*args)


def _rope_tables(seq):
    t = jnp.arange(seq)
    half = HEAD_DIM // 4
    inv = ROPE_BASE ** (-jnp.arange(half, dtype=F32) / half)
    sign = jnp.concatenate([-jnp.ones((half,), F32), jnp.ones((half,), F32)])

    def axis_tabs(pos):
        ang = pos.astype(F32)[:, None] * inv[None, :]
        c = jnp.cos(ang)
        s = jnp.sin(ang)
        return jnp.concatenate([c, c], -1), jnp.concatenate([s, s], -1) * sign

    cr, sr = axis_tabs(t // GRID_W)
    cc, sc = axis_tabs(t % GRID_W)
    cos = jnp.concatenate([cr, cc], -1)
    sin = jnp.concatenate([sr, sc], -1)
    return jnp.tile(cos, (1, LANES // HEAD_DIM)), jnp.tile(sin, (1, LANES // HEAD_DIM))


def _na_patterns(rows):
    kh = NA_WIN_H
    nb = rows // NA_ROWS
    kr0s, sigs = [], []
    for b in range(nb):
        r_lo = b * NA_ROWS
        kr0 = int(np.clip(r_lo - kh // 2, 0, rows - NA_KROWS))
        sig = tuple((int(np.clip(r - kh // 2, 0, rows - kh)) - kr0, r - kr0)
                    for r in range(r_lo, r_lo + NA_ROWS))
        kr0s.append(kr0)
        sigs.append(sig)
    uniq = sorted(set(sigs))
    ids = np.array([uniq.index(s) for s in sigs], np.int32)
    return np.array(kr0s, np.int32), ids, uniq


def _na_bias(rpb, uniq, n_ctx):
    kw = NA_WIN_W
    cols = np.arange(GRID_W)
    col_start = np.clip(cols - kw // 2, 0, GRID_W - kw)
    n_pat = len(uniq)
    kk = np.arange(NA_KROWS)
    col_sel = (cols[None, None, :] - cols[None, :, None] + kw - 1
               == np.arange(2 * kw - 1)[:, None, None])
    col_ok = ((cols[None, :] >= col_start[:, None]) & (cols[None, :] < col_start[:, None] + kw))
    row_sel = np.zeros((n_pat, NA_ROWS, NA_KROWS, 2 * NA_WIN_H - 1), bool)
    row_ok = np.zeros((n_pat, NA_ROWS, NA_KROWS), bool)
    for p, sig in enumerate(uniq):
        for rq, (r0_rel, r_rel) in enumerate(sig):
            row_ok[p, rq] = (kk >= r0_rel) & (kk < r0_rel + NA_WIN_H)
            row_sel[p, rq] = (kk[:, None] - r_rel + NA_WIN_H - 1) == np.arange(2 * NA_WIN_H - 1)[None, :]
    hp = lax.Precision.HIGHEST
    by_col = jnp.einsum('hrv,vcj->hrcj', rpb.astype(F32), jnp.asarray(col_sel, F32), precision=hp)
    bias = jnp.einsum('pqkr,hrcj->phqckj', jnp.asarray(row_sel, F32), by_col, precision=hp)
    ok = row_ok[:, None, :, None, :, None] & col_ok[None, None, None, :, None, :]
    bias = jnp.where(jnp.asarray(ok), bias, NEG_INF)
    bias = bias.reshape(n_pat, rpb.shape[0], NA_ROWS * GRID_W, NA_KROWS * GRID_W)
    return jnp.concatenate([bias, jnp.zeros(bias.shape[:3] + (n_ctx,), F32)], axis=-1)


def _na_kernel(kr0_ref, pat_ref, q_ref, k_ref, v_ref, kc_ref, vc_ref, bias_ref, o_ref):
    b = pl.program_id(1)
    start = pl.multiple_of(kr0_ref[b] * GRID_W, GRID_W)
    nk = NA_KROWS * GRID_W
    scale = HEAD_DIM ** -0.5
    one = jnp.ones((), BF16)
    hp = LANES // HEAD_DIM
    for lt in range(q_ref.shape[1] // LANES):
        lanes = slice(lt * LANES, (lt + 1) * LANES)
        q = q_ref[:, lanes]
        k_all = jnp.concatenate([k_ref[pl.ds(start, nk), lanes], kc_ref[:, lanes]], axis=0)
        v_all = jnp.concatenate([v_ref[pl.ds(start, nk), lanes], vc_ref[:, lanes]], axis=0)
        lane = lax.broadcasted_iota(jnp.int32, v_all.shape, 1)
        ext = []
        for h in range(hp):
            sl = slice(h * HEAD_DIM, (h + 1) * HEAD_DIM)
            own = (lane >= h * HEAD_DIM) & (lane < (h + 1) * HEAD_DIM)
            s = _dot_nt(q[:, sl], k_all[:, sl]) * scale + bias_ref[0, lt * hp + h]
            p = jnp.exp(s - jnp.max(s, axis=-1, keepdims=True)).astype(BF16)
            ext.append(_dot(p, jnp.where(own, v_all, one)))
        out_lane = lax.broadcasted_iota(jnp.int32, ext[0].shape, 1)
        low = out_lane < HEAD_DIM
        num = jnp.where(low, ext[0], ext[1])
        den = pltpu.roll(jnp.where(low, ext[1], ext[0]), HEAD_DIM, 1)
        o_ref[:, lanes] = (num / den).astype(o_ref.dtype)


def _na_attention(q, k, v, kc, vc, bias):
    s, bw = q.shape
    l = kc.shape[0]
    rows = s // GRID_W
    kr0s, ids, _ = _na_patterns(rows)
    tq = NA_ROWS * GRID_W
    nk = NA_KROWS * GRID_W
    hp = NA_LANES // HEAD_DIM
    grid_spec = pltpu.PrefetchScalarGridSpec(
        num_scalar_prefetch=2,
        grid=(bw // NA_LANES, rows // NA_ROWS),
        in_specs=[pl.BlockSpec((tq, NA_LANES), lambda p, b, kr, pt: (b, p)),
                  pl.BlockSpec((s, NA_LANES), lambda p, b, kr, pt: (0, p)),
                  pl.BlockSpec((s, NA_LANES), lambda p, b, kr, pt: (0, p)),
                  pl.BlockSpec((l, NA_LANES), lambda p, b, kr, pt: (0, p)),
                  pl.BlockSpec((l, NA_LANES), lambda p, b, kr, pt: (0, p)),
                  pl.BlockSpec((1, hp, tq, nk + l), lambda p, b, kr, pt: (pt[b], p, 0, 0))],
        out_specs=pl.BlockSpec((tq, NA_LANES), lambda p, b, kr, pt: (b, p)),
    )
    return pl.pallas_call(
        _na_kernel,
        grid_spec=grid_spec,
        out_shape=jax.ShapeDtypeStruct((s, bw), BF16),
        compiler_params=_cparams(("arbitrary", "arbitrary")),
        name="na_attention",
    )(jnp.asarray(kr0s), jnp.asarray(ids), q, k, v, kc, vc, bias)


def _ctx_attn_kernel(*refs, n_heads, n_rep, has_sink):
    if has_sink:
        sink_ref, q_ref, k_ref, v_ref, o_ref = refs
    else:
        q_ref, k_ref, v_ref, o_ref = refs
    scale = HEAD_DIM ** -0.5
    q = q_ref[...]
    k = k_ref[...]
    v = v_ref[...]
    outs = []
    for h in range(n_heads):
        g = h // n_rep
        sl = slice(h * HEAD_DIM, (h + 1) * HEAD_DIM)
        gl = slice(g * HEAD_DIM, (g + 1) * HEAD_DIM)
        s = _dot_nt(q[:, sl], k[:, gl]) * scale
        m = jnp.max(s, axis=-1, keepdims=True)
        if has_sink:
            m = jnp.maximum(m, sink_ref[h])
        p = jnp.exp(s - m)
        den = jnp.sum(p, axis=-1, keepdims=True)
        if has_sink:
            den = den + jnp.exp(sink_ref[h] - m)
        outs.append(_dot(p.astype(BF16), v[:, gl]) / den)
    o_ref[...] = jnp.concatenate(outs, axis=-1).astype(o_ref.dtype)


def _ctx_attention(q, k, v, sink):
    l, bw = q.shape
    n_heads = bw // HEAD_DIM
    n_rep = n_heads // (k.shape[1] // HEAD_DIM)
    has_sink = sink is not None
    full = lambda a: pl.BlockSpec(a.shape, lambda i: (0, 0))
    in_specs = [full(q), full(k), full(v)]
    args = [q, k, v]
    if has_sink:
        in_specs = [pl.BlockSpec(memory_space=pltpu.SMEM)] + in_specs
        args = [sink] + args
    return pl.pallas_call(
        functools.partial(_ctx_attn_kernel, n_heads=n_heads, n_rep=n_rep, has_sink=has_sink),
        grid=(1,),
        in_specs=in_specs,
        out_specs=pl.BlockSpec((l, bw), lambda i: (0, 0)),
        out_shape=jax.ShapeDtypeStruct((l, bw), BF16),
        compiler_params=_cparams(("arbitrary",)),
        name="ctx_attention",
    )(*args)


def _swa_kernel(sink_ref, q_ref, k_ref, v_ref, kc_ref, vc_ref, o_ref, *, seq, n_kv, n_rep):
    n = pl.program_id(0)
    nk = 3 * SW_BLOCK
    scale = HEAD_DIM ** -0.5
    ws = jnp.clip((n - 1) * SW_BLOCK, 0, seq - nk)
    ws = pl.multiple_of(ws, SW_BLOCK)
    q = q_ref[...]
    n_all = nk + kc_ref.shape[0]
    k_all = jnp.concatenate([k_ref[pl.ds(ws, nk), :], kc_ref[...]], axis=0)
    v_all = jnp.concatenate([v_ref[pl.ds(ws, nk), :], vc_ref[...]], axis=0)
    qpos = n * SW_BLOCK + lax.broadcasted_iota(jnp.int32, (SW_BLOCK, n_all), 0)
    col = lax.broadcasted_iota(jnp.int32, (SW_BLOCK, n_all), 1)
    visible = (col >= nk) | (jnp.abs(ws + col - qpos) <= SW_WINDOW)
    maskb = jnp.where(visible, 0.0, NEG_INF).astype(F32)
    lane = lax.broadcasted_iota(jnp.int32, v_all.shape, 1)
    one = jnp.ones((), BF16)
    outs = [None] * (n_kv * n_rep)
    for g in range(n_kv):
        gl = slice(g * HEAD_DIM, (g + 1) * HEAD_DIM)
        own = (lane >= g * HEAD_DIM) & (lane < (g + 1) * HEAD_DIM)
        qg = jnp.concatenate([q[:, (g * n_rep + r) * HEAD_DIM:(g * n_rep + r + 1) * HEAD_DIM]
                              for r in range(n_rep)], axis=0)
        s = _dot_nt(qg, k_all[:, gl]) * scale
        s = (s.reshape(n_rep, SW_BLOCK, n_all) + maskb[None]).reshape(n_rep * SW_BLOCK, n_all)
        sink = jnp.concatenate([jnp.full((SW_BLOCK, 1), sink_ref[g * n_rep + r], F32)
                                for r in range(n_rep)], axis=0)
        m = jnp.maximum(jnp.max(s, axis=-1, keepdims=True), sink)
        p = jnp.exp(s - m).astype(BF16)
        ext = _dot(p, jnp.where(own, v_all, one))
        den = pltpu.roll(ext, HEAD_DIM, 1) + jnp.exp(sink - m)
        o = ext / den
        for r in range(n_rep):
            outs[g * n_rep + r] = o[r * SW_BLOCK:(r + 1) * SW_BLOCK, gl]
    o_ref[...] = jnp.concatenate(outs, axis=-1).astype(o_ref.dtype)


def _sw_attention(q, k, v, kc, vc, sink):
    s, bw = q.shape
    l = kc.shape[0]
    kvw = k.shape[1]
    n_kv = kvw // HEAD_DIM
    n_rep = (bw // HEAD_DIM) // n_kv
    assert kvw == LANES
    return pl.pallas_call(
        functools.partial(_swa_kernel, seq=s, n_kv=n_kv, n_rep=n_rep),
        grid=(s // SW_BLOCK,),
        in_specs=[pl.BlockSpec(memory_space=pltpu.SMEM),
                  pl.BlockSpec((SW_BLOCK, bw), lambda n: (n, 0)),
                  pl.BlockSpec((s, kvw), lambda n: (0, 0)),
                  pl.BlockSpec((s, kvw), lambda n: (0, 0)),
                  pl.BlockSpec((l, kvw), lambda n: (0, 0)),
                  pl.BlockSpec((l, kvw), lambda n: (0, 0))],
        out_specs=pl.BlockSpec((SW_BLOCK, bw), lambda n: (n, 0)),
        out_shape=jax.ShapeDtypeStruct((s, bw), BF16),
        compiler_params=_cparams(("arbitrary",)),
        name="sw_attention",
    )(sink, q, k, v, kc, vc)


def _s5_prepare(lam_re, lam_im, b_re, b_im, c_re, c_im, log_step):
    hp = lax.Precision.HIGHEST
    tc = S5_CHUNK
    lam = lax.complex(lam_re.astype(F32), lam_im.astype(F32))
    dt = jnp.exp(log_step.astype(F32))[..., None]
    lam_dt = lam * dt
    lam_bar = jnp.exp(lam_dt)
    b_bar = ((lam_bar - 1.0) / lam)[..., None] * lax.complex(b_re.astype(F32), b_im.astype(F32))
    c_mat = lax.complex(c_re.astype(F32), c_im.astype(F32))
    kk = jnp.arange(tc + 1, dtype=F32)
    pw = jnp.exp(lam_dt[None] * kk[:, None, None, None])
    n_g, n_p, n_h = b_bar.shape[1:]

    cw = jnp.transpose(c_mat[None] * pw[:tc, :, :, None, :], (1, 2, 0, 3, 4))
    kt = (jnp.einsum('dgpj,dgkhp->dgjkh', b_bar.real, cw.real, precision=hp)
          - jnp.einsum('dgpj,dgkhp->dgjkh', b_bar.imag, cw.imag, precision=hp))
    kt = jnp.stack([kt[0], kt[1][:, :, ::-1]], axis=1).reshape(n_g, 2, n_h, tc * n_h)
    t_i = np.arange(tc)

    def cat_ri(z):
        return jnp.concatenate([z.real, z.imag], axis=-1)

    w_f = pw[tc - 1 - t_i, 0][:, :, None, :] * jnp.transpose(b_bar[0], (0, 2, 1))[None]
    w_r = pw[t_i, 1][:, :, None, :] * jnp.transpose(b_bar[1], (0, 2, 1))[None]
    w_f = jnp.transpose(cat_ri(w_f), (1, 0, 2, 3)).reshape(n_g, tc * n_h, 2 * n_p)
    w_r = jnp.transpose(cat_ri(w_r), (1, 0, 2, 3)).reshape(n_g, tc * n_h, 2 * n_p)
    w_cat = jnp.concatenate([w_f, w_r], axis=-1).astype(BF16)

    def v_of(c_dir, pw_sel):
        cp = c_dir[None] * pw_sel[:, :, None, :]
        v = jnp.concatenate([cp.real, -cp.imag], axis=-1)
        return jnp.transpose(v, (1, 3, 0, 2)).reshape(n_g, 2 * n_p, tc * n_h)
    v_cat = jnp.concatenate([v_of(c_mat[0], pw[t_i + 1, 0]), v_of(c_mat[1], pw[tc - t_i, 1])],
                            axis=1).astype(BF16)

    def coef(k):
        a = jnp.exp(lam_dt * (tc * k))
        return jnp.stack([jnp.concatenate([a.real, a.real], -1),
                          jnp.concatenate([-a.imag, a.imag], -1)], axis=2)
    a_log = jnp.stack([coef(1.0), coef(2.0), coef(4.0)], axis=2)
    a_car = jnp.stack([coef(float(j)) for j in range(1, SUBLANES + 1)], axis=3)
    a_car = jnp.stack([a_car[0], a_car[1][:, :, ::-1]], axis=0)
    return _s5_expand(kt, w_cat, v_cat, jnp.transpose(a_log, (1, 0, 2, 3, 4)),
                      jnp.transpose(a_car, (1, 0, 2, 3, 4)))


def _s5_expand(kt, w_cat, v_cat, a_log, a_car):
    n_g = kt.shape[0]
    n_t = n_g // S5_LG
    n_state = 2 * S5_STATE
    by_tile = lambda a: a.reshape((n_t, S5_LG) + a.shape[1:])

    def lanes_of(a):
        a = jnp.moveaxis(by_tile(a), 1, -2)
        return a.reshape(a.shape[:-2] + (S5_LG * n_state,))
    return by_tile(kt), by_tile(w_cat), by_tile(v_cat), lanes_of(a_log), lanes_of(a_car)


def _s5_lane_perm():
    r = np.arange(S5_CHUNK * LANES)
    t, a, h = r // LANES, (r % LANES) // S5_GROUP, r % S5_GROUP
    dest = a * (S5_CHUNK * S5_GROUP) + t * S5_GROUP + h
    return (jnp.asarray(dest)[:, None] == jnp.arange(S5_CHUNK * LANES)[None, :]).astype(BF16)


def _chunk_rows(tok_ref):
    n_rows = tok_ref.shape[1] // S5_CHUNK
    return jnp.concatenate([tok_ref[0, pl.ds(t, n_rows, stride=S5_CHUNK), :] for t in range(S5_CHUNK)], axis=-1)


def _s5_in_kernel(u_ref, perm_ref, kt_ref, w_ref, yp_ref, z_ref, m_scr):
    width = S5_CHUNK * S5_GROUP
    n_state = 2 * S5_STATE
    half = S5_LG * n_state

    @pl.when(pl.program_id(1) == 0)
    def _():
        lane = lax.broadcasted_iota(jnp.int32, (S5_GROUP, width), 1)
        for a in range(S5_LG):
            k_fwd = kt_ref[0, a, 0]
            k_rev = kt_ref[0, a, 1]
            for tau in range(S5_CHUNK):
                right = tau * S5_GROUP
                left = (S5_CHUNK - 1 - tau) * S5_GROUP
                blk = (jnp.where(lane >= right, pltpu.roll(k_fwd, right, 1), 0.0)
                       + jnp.where(lane < width - left, pltpu.roll(k_rev, (width - left) % width, 1), 0.0))
                m_scr[a, tau * S5_GROUP:(tau + 1) * S5_GROUP, 0:width] = blk.astype(BF16)
            m_scr[a, :, width:] = w_ref[0, a]

    up = _dot(_chunk_rows(u_ref).astype(BF16), perm_ref[...]).astype(BF16)
    for a in range(S5_LG):
        zz = _dot(up[:, a * width:(a + 1) * width], m_scr[a])
        yp_ref[0, :, a * width:(a + 1) * width] = zz[:, :width]
        z_ref[0, :, a * n_state:(a + 1) * n_state] = zz[:, width:width + n_state]
        z_ref[0, :, half + a * n_state:half + (a + 1) * n_state] = zz[:, width + n_state:]


def _s5_scan_kernel(z_ref, alog_ref, acar_ref, e_ref, *, nc, nc_ctx):
    n_state = 2 * S5_STATE
    half = z_ref.shape[2] // 2
    nblk = nc // SUBLANES
    nblk_ctx = nc_ctx // SUBLANES
    row = lax.broadcasted_iota(jnp.int32, (SUBLANES, n_state), 0)

    def cmul(a1, a2, s):
        return a1 * s + a2 * pltpu.roll(s, S5_STATE, 1)

    def step(t, carry):
        blk = (t, jnp.where(t < nblk_ctx, nblk_ctx - 1 - t, nblk + nblk_ctx - 1 - t))
        new = []
        for d in range(2):
            r0 = pl.multiple_of(blk[d] * SUBLANES, SUBLANES)
            for j in range(half // n_state):
                tile = slice(j * n_state, (j + 1) * n_state)
                lanes = slice(d * half + j * n_state, d * half + (j + 1) * n_state)
                z = z_ref[0, pl.ds(r0, SUBLANES), lanes]
                c = carry[len(new)]
                for i, sh in enumerate((1, 2, 4)):
                    if d == 0:
                        zs = jnp.where(row >= sh, pltpu.roll(z, sh, 0), 0.0)
                    else:
                        zs = jnp.where(row < SUBLANES - sh, pltpu.roll(z, SUBLANES - sh, 0), 0.0)
                    z = z + cmul(alog_ref[0, d, i, 0:1, tile], alog_ref[0, d, i, 1:2, tile], zs)
                cb = jnp.broadcast_to(c, (SUBLANES, n_state))
                s = z + cmul(acar_ref[0, d, 0, :, tile], acar_ref[0, d, 1, :, tile], cb)
                if d == 0:
                    e = jnp.where(row >= 1, pltpu.roll(s, 1, 0), cb)
                    c_new = s[SUBLANES - 1:SUBLANES, :]
                else:
                    e = jnp.where(row < SUBLANES - 1, pltpu.roll(s, SUBLANES - 1, 0), cb)
                    c_new = s[0:1, :]
                e_ref[0, pl.ds(r0, SUBLANES), lanes] = e
                new.append(c_new)
        return tuple(new)

    init = tuple(jnp.zeros((1, n_state), F32) for _ in range(2 * (half // n_state)))
    lax.fori_loop(0, nblk, step, init, unroll=2)


def _s5_out_kernel(e_ref, yp_ref, u_ref, d_ref, perm_ref, v_ref, y_ref):
    width = S5_CHUNK * S5_GROUP
    n_state = 2 * S5_STATE
    half = S5_LG * n_state
    parts = []
    for a in range(S5_LG):
        e_a = jnp.concatenate([e_ref[0, :, a * n_state:(a + 1) * n_state],
                               e_ref[0, :, half + a * n_state:half + (a + 1) * n_state]], axis=-1)
        parts.append(yp_ref[0, :, a * width:(a + 1) * width] + _dot(e_a.astype(BF16), v_ref[0, a]))
    y = jnp.concatenate(parts, axis=-1)
    hi = y.astype(BF16)
    lo = (y - hi.astype(F32)).astype(BF16)
    perm = perm_ref[...]
    y = _dot_nt(hi, perm) + _dot_nt(lo, perm) + _chunk_rows(u_ref) * d_ref[0]
    n_rows = y.shape[0]
    for t in range(S5_CHUNK):
        y_ref[0, pl.ds(t, n_rows, stride=S5_CHUNK), :] = y[:, t * LANES:(t + 1) * LANES]


def _s5_rows(nc):
    return max(r for r in range(2 * SUBLANES, 257, 2 * SUBLANES) if nc % r == 0)


def _s5_mixer_pre_glu(u_ctx, u_lat, prep, d_skip):
    kt, w_cat, v_cat, a_log, a_car = prep
    n_t, l, _ = u_ctx.shape
    s = u_lat.shape[1]
    nc, nc_ctx = (l + s) // S5_CHUNK, l // S5_CHUNK
    width = S5_CHUNK * LANES
    n_z = 2 * S5_LG * 2 * S5_STATE
    u = jnp.concatenate([u_ctx, u_lat], axis=1)
    d_rows = jnp.tile(d_skip.astype(F32).reshape(n_t, 1, LANES), (1, 1, S5_CHUNK))
    perm = _s5_lane_perm()
    rb = _s5_rows(nc)
    rows = lambda w: pl.BlockSpec((1, rb, w), lambda g, r: (g, r, 0))
    per_tile = lambda a: pl.BlockSpec((1,) + a.shape[1:], lambda g, *_: (g,) + (0,) * (a.ndim - 1))
    whole = lambda a: pl.BlockSpec(a.shape, lambda g, r: (0,) * a.ndim)
    tokens = pl.BlockSpec((1, rb * S5_CHUNK, LANES), lambda g, r: (g, r, 0))
    width_g = S5_CHUNK * S5_GROUP
    yp, z = pl.pallas_call(
        _s5_in_kernel,
        grid=(n_t, nc // rb),
        in_specs=[tokens, whole(perm), per_tile(kt), per_tile(w_cat)],
        out_specs=[rows(width), rows(n_z)],
        out_shape=[jax.ShapeDtypeStruct((n_t, nc, width), F32), jax.ShapeDtypeStruct((n_t, nc, n_z), F32)],
        scratch_shapes=[pltpu.VMEM((S5_LG, width_g, width_g + 4 * S5_STATE), BF16)],
        compiler_params=_cparams(("arbitrary", "arbitrary")),
        name="s5_in",
    )(u, perm, kt, w_cat)
    e = pl.pallas_call(
        functools.partial(_s5_scan_kernel, nc=nc, nc_ctx=nc_ctx),
        grid=(n_t,),
        in_specs=[per_tile(z), per_tile(a_log), per_tile(a_car)],
        out_specs=per_tile(z),
        out_shape=jax.ShapeDtypeStruct(z.shape, F32),
        compiler_params=_cparams(("arbitrary",)),
        name="s5_scan",
    )(z, a_log, a_car)
    y = pl.pallas_call(
        _s5_out_kernel,
        grid=(n_t, nc // rb),
        in_specs=[rows(n_z), rows(width), tokens, per_tile(d_rows), whole(perm), per_tile(v_cat)],
        out_specs=tokens,
        out_shape=jax.ShapeDtypeStruct((n_t, l + s, LANES), F32),
        compiler_params=_cparams(("arbitrary", "arbitrary")),
        name="s5_out",
    )(e, yp, u, d_rows, perm, v_cat)
    return y


def _merge_kernel(x_ref, ya_ref, y5_ref, ys_ref, ga_ref, gb_ref, gs_ref, wglu_ref, wb_ref, wo_ref,
                  gt_ref, g_ref, sh_ref, sc_ref, xo_ref, h_ref, ht_ref):
    y5 = jnp.concatenate([y5_ref[t] for t in range(y5_ref.shape[0])], axis=-1)
    z = _gelu(y5)
    yb = z * jax.nn.sigmoid(_dot(z.astype(BF16), wglu_ref[...]))
    m = (ga_ref[...].astype(F32) * _dot(ya_ref[...], wb_ref[0])
         + gb_ref[...].astype(F32) * _dot(yb.astype(BF16), wb_ref[1])
         + gs_ref[...].astype(F32) * _dot(ys_ref[...], wb_ref[2]))
    x = x_ref[...] + gt_ref[...] * _dot(m.astype(BF16), wo_ref[...])
    xo_ref[...] = x
    h = _norm_mod(x, g_ref[...], sh_ref[...], sc_ref[...])
    h_ref[...] = h.astype(BF16)
    ht_ref[...] = h.T.astype(BF16)


def _merge(x, ya, y5, y5_row0, ys, ga, gb, gs, w_glu, w_branch, w_out, gate, g, shift, scale):
    n, d = x.shape
    bw = ya.shape[1]
    tm = min(ROW_BLOCK, n)
    assert y5_row0 % tm == 0
    row = lambda w: pl.BlockSpec((tm, w), lambda i: (i, 0))
    full = lambda a: pl.BlockSpec(a.shape, lambda i: (0,) * a.ndim)
    vec = pl.BlockSpec((1, d), lambda i: (0, 0))
    y5_rows = pl.BlockSpec((y5.shape[0], tm, LANES), lambda i: (0, i + y5_row0 // tm, 0))
    return pl.pallas_call(
        _merge_kernel,
        grid=(n // tm,),
        in_specs=[row(d), row(bw), y5_rows, row(bw), row(d), row(d), row(d),
                  full(w_glu), full(w_branch), full(w_out), vec, vec, vec, vec],
        out_specs=[row(d), row(d), pl.BlockSpec((d, tm), lambda i: (0, i))],
        out_shape=[jax.ShapeDtypeStruct((n, d), F32), jax.ShapeDtypeStruct((n, d), BF16),
                   jax.ShapeDtypeStruct((d, n), BF16)],
        compiler_params=_cparams(("arbitrary",)),
        name="merge",
    )(x, ya, y5, ys, ga, gb, gs, w_glu, w_branch, w_out, gate, g, shift, scale)


def _knock_out_16(s, order, exact, want_rank=True):
    rank = jnp.full(s.shape, float(PEER_TOPK), F32) if want_rank else None
    live = s
    vals = []
    for r in range(PEER_TOPK):
        m = jnp.max(live, axis=0, keepdims=True)
        hit = live == m
        if exact:
            first = jnp.min(jnp.where(hit, order, float(PEER_TOPK * PEER_NKEYS)), axis=0, keepdims=True)
            hit = order == first
        if want_rank:
            rank = jnp.where(hit, float(r), rank)
        live = jnp.where(hit, -jnp.inf, live)
        vals.append(m)
    out = (rank < float(PEER_TOPK)) if want_rank else (live != s)
    n_out = jnp.sum(jnp.where(out, 1.0, 0.0), axis=0, keepdims=True)
    return rank, jnp.concatenate(vals, axis=0), n_out


def _bf16_pair_word(x):
    hi = pltpu.bitcast(x.astype(BF16).astype(F32), jnp.uint32)
    return hi | (hi >> 16)


def _pair_tiles():
    tiles = [(0, 0, 8), (0, 8, 8)]
    for a in range(1, 8):
        tiles.append((a, 0, PEER_TOPK // (a + 1)))
    return tiles


def _route_kernel(h_ref, wq_ref, k1_ref, k2_ref, cnt_ref, c1_ref, rk_ref, e2_ref, q_scr):
    tb = h_ref.shape[0]
    q_scr[...] = _dot(h_ref[...], wq_ref[...]).astype(BF16)
    half = PEER_QDIM // 2
    iota = lax.broadcasted_iota(jnp.int32, (PEER_NKEYS, tb), 0).astype(F32)
    row8 = lax.broadcasted_iota(jnp.int32, (SUBLANES, tb), 0).astype(F32)
    k1 = k1_ref[...]
    k2 = k2_ref[...]

    def emit(hd, s1, s2, exact):
        rank1, v1, n1 = _knock_out_16(s1, iota, exact, want_rank=exact)
        rank2, v2, n2 = _knock_out_16(s2, iota, exact)
        e1v = jnp.exp(v1 - v1[0:1])
        e2v = jnp.exp(v2 - v2[0:1])
        tiles, flats, gates = [], [], []
        for a, b0, nv in _pair_tiles():
            c = v1[a:a + 1] + v2[b0:b0 + SUBLANES]
            tiles.append(jnp.where(row8 < nv, c, -jnp.inf))
            flats.append(a * PEER_TOPK + b0 + row8)
            gates.append(e1v[a:a + 1] * e2v[b0:b0 + SUBLANES])
        tiles.append(v1[SUBLANES:] + v2[0:1])
        flats.append((row8 + SUBLANES) * PEER_TOPK)
        gates.append(e1v[SUBLANES:] * e2v[0:1])
        cand = jnp.concatenate(tiles, axis=0)
        flat = jnp.concatenate(flats, axis=0)
        gate = jnp.concatenate(gates, axis=0)
        rank_c, _, n_c = _knock_out_16(cand, flat, exact)
        self_ = jnp.where(rank_c < float(PEER_TOPK), 1.0, 0.0)
        z = jnp.sum(self_ * gate, axis=0, keepdims=True)
        cnt = [self_[0:8].sum(axis=0, keepdims=True) + self_[8:16].sum(axis=0, keepdims=True)]
        for t in range(2, 9):
            cnt.append(self_[t * SUBLANES:(t + 1) * SUBLANES].sum(axis=0, keepdims=True))
        cnt = jnp.concatenate(cnt + [self_[9 * SUBLANES:]], axis=0)
        cnt1 = jnp.zeros((PEER_NKEYS, tb), F32)
        for a in range(PEER_TOPK):
            is_a = (rank1 == float(a)) if exact else (s1 == v1[a:a + 1])
            cnt1 = jnp.where(is_a, cnt[a:a + 1], cnt1)
        cnt_ref[hd] = _bf16_pair_word(cnt1)
        c1_ref[hd] = _bf16_pair_word(jnp.exp(s1 - v1[0:1]) * (0.5 / z))
        rk_ref[hd] = rank2.astype(BF16)
        e2_ref[hd] = jnp.exp(s2 - v2[0:1]).astype(BF16)
        want = float(PEER_TOPK)
        return jnp.where((n1 != want) | (n2 != want) | (n_c != want), 1.0, 0.0)

    def head_group(hg, _):
        scores = []
        for k in range(PEER_ROUTE_HEADS):
            hd = hg * PEER_ROUTE_HEADS + k
            c0 = pl.multiple_of(hd * PEER_QDIM, PEER_QDIM)
            s1 = _dot_nt(k1, q_scr[:, pl.ds(c0, half)])
            s2 = _dot_nt(k2, q_scr[:, pl.ds(c0 + half, half)])
            scores.append((hd, s1, s2))
        tied = [jnp.max(emit(hd, s1, s2, exact=False)) for hd, s1, s2 in scores]
        for (hd, s1, s2), t in zip(scores, tied):
            @pl.when(t > 0.0)
            def _():
                emit(hd, s1, s2, exact=True)
        return 0

    lax.fori_loop(0, PEER_HEADS // PEER_ROUTE_HEADS, head_group, 0)


def _peer_route(h, w_q, k1, k2):
    n, d = h.shape
    tb = min(PEER_ROUTE_BLOCK, n)
    qw = w_q.shape[1]
    out_blk = pl.BlockSpec((PEER_HEADS, PEER_NKEYS, tb), lambda i: (0, 0, i))
    shp = lambda dt: jax.ShapeDtypeStruct((PEER_HEADS, PEER_NKEYS, n), dt)
    return pl.pallas_call(
        _route_kernel,
        grid=(n // tb,),
        in_specs=[pl.BlockSpec((tb, d), lambda i: (i, 0)),
                  pl.BlockSpec((d, qw), lambda i: (0, 0)),
                  pl.BlockSpec(k1.shape, lambda i: (0, 0)),
                  pl.BlockSpec(k2.shape, lambda i: (0, 0))],
        out_specs=[out_blk] * 4,
        out_shape=[shp(jnp.uint32), shp(jnp.uint32), shp(BF16), shp(BF16)],
        scratch_shapes=[pltpu.VMEM((tb, qw), BF16)],
        compiler_params=_cparams(("arbitrary",)),
        name="peer_route",
    )(h, w_q, k1, k2)


def _dense_kernel(ht_ref, x_ref, gt_ref, u_ref, vt_ref, cnt_ref, c1_ref, rk_ref, e2_ref, o_ref,
                  acc_ref, w_scr):
    eb = pl.program_id(1)
    tb = ht_ref.shape[1]
    n_exp = u_ref.shape[0]
    n_sub = n_exp // PEER_SUB
    i_per_sub = PEER_SUB // PEER_NKEYS

    @pl.when(eb == 0)
    def _():
        acc_ref[...] = jnp.zeros_like(acc_ref)

    def scores(sb):
        return _dot(u_ref[sb * PEER_SUB:(sb + 1) * PEER_SUB, :], ht_ref[...])

    pk_rows = 2 * SUBLANES
    n_pk = PEER_NKEYS // pk_rows

    def row_tile(ref, hd, i):
        words = jnp.broadcast_to(ref[hd, pl.ds(i, 1), :], (SUBLANES, tb))
        return pltpu.bitcast(words, BF16)[None]

    a_next = scores(0)
    for sb in range(n_sub):
        a_cur = a_next
        if sb + 1 < n_sub:
            a_next = scores(sb + 1)
        for il in range(i_per_sub):
            i = eb * (n_exp // PEER_NKEYS) + sb * i_per_sub + il
            gsum = jnp.zeros((n_pk, pk_rows, tb), BF16)
            for hd in range(PEER_HEADS):
                cnt = row_tile(cnt_ref, hd, i)
                c1 = row_tile(c1_ref, hd, i)
                rk = rk_ref[hd].reshape(n_pk, pk_rows, tb)
                e2 = e2_ref[hd].reshape(n_pk, pk_rows, tb)
                gsum = gsum + jnp.where(rk < cnt, e2 * c1, jnp.zeros((), BF16))
            r0 = sb * PEER_SUB + il * PEER_NKEYS
            a_i = a_cur[il * PEER_NKEYS:(il + 1) * PEER_NKEYS]
            half_gate = gsum.reshape(PEER_NKEYS, tb).astype(F32)
            w_i = a_i * (1.0 + lax.erf(a_i * (1.0 / math.sqrt(2.0)))) * half_gate
            w_scr[r0:r0 + PEER_NKEYS, :] = w_i.astype(BF16)
        done = (sb + 1) * PEER_SUB
        if done % PEER_ACC_CHUNK == 0:
            c0 = done - PEER_ACC_CHUNK
            acc_ref[...] += _dot(vt_ref[:, c0:done], w_scr[c0:done, :])

    @pl.when(eb == pl.num_programs(1) - 1)
    def _():
        o_ref[...] = x_ref[...] + gt_ref[...] * acc_ref[...].T


def _peer_dense(h_t, x, gate, u, v_t, layer, cnt1, c1, rank2, e2):
    d, n = h_t.shape
    n_e = u.shape[1]
    tb = min(PEER_TOK_BLOCK, n)
    eb = PEER_EXP_BLOCK
    tab = pl.BlockSpec((PEER_HEADS, PEER_NKEYS, tb), lambda i, e: (0, 0, i))
    return pl.pallas_call(
        _dense_kernel,
        grid=(n // tb, n_e // eb),
        in_specs=[pl.BlockSpec((d, tb), lambda i, e: (0, i)),
                  pl.BlockSpec((tb, d), lambda i, e: (i, 0)),
                  pl.BlockSpec((1, d), lambda i, e: (0, 0)),
                  pl.BlockSpec((None, eb, d), lambda i, e: (layer, e, 0)),
                  pl.BlockSpec((None, d, eb), lambda i, e: (layer, 0, e)),
                  tab, tab, tab, tab],
        out_specs=pl.BlockSpec((tb, d), lambda i, e: (i, 0)),
        out_shape=jax.ShapeDtypeStruct((n, d), F32),
        scratch_shapes=[pltpu.VMEM((d, tb), F32), pltpu.VMEM((eb, tb), BF16)],
        compiler_params=_cparams(("arbitrary", "arbitrary")),
        name="peer_dense",
    )(h_t, x, gate, u, v_t, cnt1, c1, rank2, e2)


def _final_norm_kernel(x_ref, g_ref, o_ref):
    x = x_ref[...]
    o_ref[...] = x * lax.rsqrt(jnp.mean(x * x, axis=-1, keepdims=True) + EPS) * g_ref[...]


def _final_norm(x, g):
    n, d = x.shape
    tm = min(2 * ROW_BLOCK, n)
    return pl.pallas_call(
        _final_norm_kernel,
        grid=(n // tm,),
        in_specs=[pl.BlockSpec((tm, d), lambda i: (i, 0)), pl.BlockSpec((1, d), lambda i: (0, 0))],
        out_specs=pl.BlockSpec((tm, d), lambda i: (i, 0)),
        out_shape=jax.ShapeDtypeStruct((n, d), F32),
        compiler_params=_cparams(("arbitrary",)),
        name="final_norm",
    )(x, g)


def kernel(x, c, ctx, c_ctx, w_mod, b_mod, g_mix, g_ffn, w_in, na_rpb, s5_lam_re, s5_lam_im,
           s5_b_re, s5_b_im, s5_c_re, s5_c_im, s5_log_step, s5_d, s5_w_glu, sw_sink, w_branch,
           w_out, peer_w_q, peer_sub_keys, peer_u, peer_v, g_final):
    batch, seq, d = x.shape
    l_ctx = ctx.shape[1]
    depth = w_mod.shape[0]
    assert batch == 1 and seq % (NA_ROWS * GRID_W) == 0 and l_ctx % (S5_CHUNK * SUBLANES) == 0
    assert seq // GRID_W >= NA_KROWS and seq >= 3 * SW_BLOCK

    cc = jnp.zeros((SUBLANES, d), F32).at[0].set(c[0]).at[1].set(c_ctx)
    mod = _mod_vectors(cc, w_mod, b_mod).reshape(depth, SUBLANES, 6, d)
    rope_tabs = _rope_tables(seq)
    row = lambda v: v.reshape(1, d)
    na_uniq = _na_patterns(seq // GRID_W)[2]
    na_bias = jax.vmap(lambda r: _na_bias(r, na_uniq, l_ctx))(na_rpb)
    s5_prep = jax.vmap(_s5_prepare)(s5_lam_re, s5_lam_im, s5_b_re, s5_b_im, s5_c_re, s5_c_im, s5_log_step)
    u_all = peer_u.astype(BF16)
    vt_all = jnp.transpose(peer_v.astype(BF16), (0, 2, 1))

    xx, xc = x[0], ctx[0]
    for l in range(depth):
        need_ctx = l < depth - 1
        m_lat, m_ctx = mod[l, 0], mod[l, 1]
        w_in_l = w_in[l].astype(BF16)
        g_mix_l = row(g_mix[l])
        g_ffn_l = row(g_ffn[l])

        qa, ka, va, ub, qs, ks, vs, ga, gb, gs = _inproj(
            xx, g_mix_l, row(m_lat[0]), row(m_lat[1]), w_in_l, rope_tabs)
        qa_c, ka_c, va_c, ub_c, qs_c, ks_c, vs_c, ga_c, gb_c, gs_c = _inproj(
            xc, g_mix_l, row(m_ctx[0]), row(m_ctx[1]), w_in_l, None)

        ya = _na_attention(qa, ka, va, ka_c, va_c, na_bias[l])
        ys = _sw_attention(qs, ks, vs, ks_c, vs_c, sw_sink[l])
        y5 = _s5_mixer_pre_glu(ub_c, ub, [a[l] for a in s5_prep], s5_d[l])

        w_glu_l = s5_w_glu[l].astype(BF16)
        w_branch_l = w_branch[l].astype(BF16)
        w_out_l = w_out[l].astype(BF16)
        xx, hx2, hx2_t = _merge(xx, ya, y5, l_ctx, ys, ga, gb, gs, w_glu_l, w_branch_l, w_out_l,
                                row(m_lat[2]), g_ffn_l, row(m_lat[3]), row(m_lat[4]))

        w_q_l = peer_w_q[l].astype(BF16)
        k1 = peer_sub_keys[l, 0].astype(BF16)
        k2 = peer_sub_keys[l, 1].astype(BF16)
        if need_ctx:
            ya_c = _ctx_attention(qa_c, ka_c, va_c, None)
            ys_c = _ctx_attention(qs_c, ks_c, vs_c, sw_sink[l])
            xc, hc2, hc2_t = _merge(xc, ya_c, y5, 0, ys_c, ga_c, gb_c, gs_c, w_glu_l, w_branch_l, w_out_l,
                                    row(m_ctx[2]), g_ffn_l, row(m_ctx[3]), row(m_ctx[4]))
            xc = _peer_dense(hc2_t, xc, row(m_ctx[5]), u_all, vt_all, l, *_peer_route(hc2, w_q_l, k1, k2))
        xx = _peer_dense(hx2_t, xx, row(m_lat[5]), u_all, vt_all, l, *_peer_route(hx2, w_q_l, k1, k2))

    return _final_norm(xx, row(g_final))[None]
```

```python
import functools
import math

import numpy as np
import jax
import jax.numpy as jnp
from jax import lax
from jax.experimental import pallas as pl
from jax.experimental.pallas import tpu as pltpu

F32 = jnp.float32
BF16 = jnp.bfloat16

GRID_W = 64
HEAD_DIM = 64
NA_WIN_H = 8
NA_WIN_W = 16
S5_GROUP = 16
S5_STATE = 64
SW_WINDOW = 128
ROPE_BASE = 10000.0
PEER_HEADS = 8
PEER_NKEYS = 128
PEER_QDIM = 256
PEER_TOPK = 16
EPS = 1e-6
NEG_INF = -1e30

LANES = 128
SUBLANES = 8
VMEM_LIMIT_BYTES = 56 * 1024 * 1024

ROW_BLOCK = 256
NA_ROWS = 4
NA_KROWS = NA_ROWS + NA_WIN_H - 1
NA_LANES = 256
SW_BLOCK = 128
S5_CHUNK = 16
S5_LG = LANES // S5_GROUP
PEER_ROUTE_BLOCK = 256
PEER_ROUTE_HEADS = 4
PEER_TOK_BLOCK = 512
PEER_EXP_BLOCK = 2048
PEER_SUB = 512
PEER_ACC_CHUNK = 1024


def _cparams(sem):
    return pltpu.CompilerParams(dimension_semantics=sem, vmem_limit_bytes=VMEM_LIMIT_BYTES)


def _dot(a, b):
    return jnp.dot(a, b, preferred_element_type=F32)


def _dot_nt(a, b):
    return lax.dot_general(a, b, (((1,), (1,)), ((), ())), preferred_element_type=F32)


def _gelu(x):
    return 0.5 * x * (1.0 + lax.erf(x * (1.0 / math.sqrt(2.0))))


def _mod_kernel(cc_ref, w_ref, b_ref, o_ref):
    a = cc_ref[...]
    a = a * jax.nn.sigmoid(a)
    o_ref[0] = _dot(a.astype(BF16), w_ref[0].astype(BF16)) + b_ref[0]


def _mod_vectors(cc, w_mod, b_mod):
    depth, d, n6 = w_mod.shape
    tn = 1024
    return pl.pallas_call(
        _mod_kernel,
        grid=(depth, n6 // tn),
        in_specs=[pl.BlockSpec((SUBLANES, d), lambda l, j: (0, 0)),
                  pl.BlockSpec((1, d, tn), lambda l, j: (l, 0, j)),
                  pl.BlockSpec((1, 1, tn), lambda l, j: (l, 0, j))],
        out_specs=pl.BlockSpec((1, SUBLANES, tn), lambda l, j: (l, 0, j)),
        out_shape=jax.ShapeDtypeStruct((depth, SUBLANES, n6), F32),
        compiler_params=_cparams(("arbitrary", "arbitrary")),
        name="mod_vectors",
    )(cc, w_mod, b_mod.reshape(depth, 1, n6))


def _norm_mod(x, g, shift, scale):
    y = x * lax.rsqrt(jnp.mean(x * x, axis=-1, keepdims=True) + EPS)
    return (y * g) * (1.0 + scale) + shift


def _rope(x, cos, sin, lane_lo):
    up = pltpu.roll(x, LANES - 16, 1)
    dn = pltpu.roll(x, 16, 1)
    return x * cos + jnp.where(lane_lo, up, dn) * sin


def _inproj_kernel(x_ref, g_ref, sh_ref, sc_ref, w_ref, *rest, bw, kvw, d_model, rope):
    if rope:
        cos_ref, sin_ref = rest[:2]
        rest = rest[2:]
    qa_ref, ka_ref, va_ref, ub_ref, qs_ref, ks_ref, vs_ref, ga_ref, gb_ref, gs_ref = rest
    h = _norm_mod(x_ref[...], g_ref[...], sh_ref[...], sc_ref[...]).astype(BF16)

    def proj(c0, width):
        return _dot(h, w_ref[:, c0:c0 + width])

    c = 0
    qa_ref[...] = proj(c, bw).astype(BF16); c += bw
    ka_ref[...] = proj(c, bw).astype(BF16); c += bw
    va_ref[...] = proj(c, bw).astype(BF16); c += bw
    ub = proj(c, bw); c += bw
    for t in range(bw // LANES):
        ub_ref[t] = ub[:, t * LANES:(t + 1) * LANES]
    qs = proj(c, bw); c += bw
    ks = proj(c, kvw); c += kvw
    vs_ref[...] = proj(c, kvw).astype(BF16); c += kvw
    if rope:
        cos = cos_ref[...]
        sin = sin_ref[...]
        lane = lax.broadcasted_iota(jnp.int32, cos.shape, 1)
        lane_lo = (lane % 32) < 16
        for p in range(bw // LANES):
            sl = slice(p * LANES, (p + 1) * LANES)
            qs_ref[:, sl] = _rope(qs[:, sl], cos, sin, lane_lo).astype(BF16)
        for p in range(kvw // LANES):
            sl = slice(p * LANES, (p + 1) * LANES)
            ks_ref[:, sl] = _rope(ks[:, sl], cos, sin, lane_lo).astype(BF16)
    else:
        qs_ref[...] = qs.astype(BF16)
        ks_ref[...] = ks.astype(BF16)
    ga_ref[...] = jax.nn.sigmoid(proj(c, d_model)).astype(BF16); c += d_model
    gb_ref[...] = jax.nn.sigmoid(proj(c, d_model)).astype(BF16); c += d_model
    gs_ref[...] = jax.nn.sigmoid(proj(c, d_model)).astype(BF16)


def _inproj(x, g, shift, scale, w_in, rope_tabs):
    n, d = x.shape
    bw = d // 2
    kvw = bw // 4
    tm = min(ROW_BLOCK, n)
    rope = rope_tabs is not None
    row = lambda i: (i, 0)
    fixed = lambda i: (0, 0)
    in_specs = [pl.BlockSpec((tm, d), row), pl.BlockSpec((1, d), fixed), pl.BlockSpec((1, d), fixed),
                pl.BlockSpec((1, d), fixed), pl.BlockSpec(w_in.shape, fixed)]
    args = [x, g, shift, scale, w_in]
    if rope:
        in_specs += [pl.BlockSpec((tm, LANES), row)] * 2
        args += list(rope_tabs)
    widths = [bw, bw, bw, bw, bw, kvw, kvw, d, d, d]
    dtypes = [BF16, BF16, BF16, F32, BF16, BF16, BF16, BF16, BF16, BF16]
    s5_slot = 3
    return pl.pallas_call(
        functools.partial(_inproj_kernel, bw=bw, kvw=kvw, d_model=d, rope=rope),
        grid=(n // tm,),
        in_specs=in_specs,
        out_specs=[pl.BlockSpec((bw // LANES, tm, LANES), lambda i: (0, i, 0)) if k == s5_slot
                   else pl.BlockSpec((tm, w), row) for k, w in enumerate(widths)],
        out_shape=[jax.ShapeDtypeStruct((bw // LANES, n, LANES) if k == s5_slot else (n, w), dt)
                   for k, (w, dt) in enumerate(zip(widths, dtypes))],
        compiler_params=_cparams(("arbitrary",)),
        name="inproj",
    )(*args)


def _rope_tables(seq):
    t = jnp.arange(seq)
    half = HEAD_DIM // 4
    inv = ROPE_BASE ** (-jnp.arange(half, dtype=F32) / half)
    sign = jnp.concatenate([-jnp.ones((half,), F32), jnp.ones((half,), F32)])

    def axis_tabs(pos):
        ang = pos.astype(F32)[:, None] * inv[None, :]
        c = jnp.cos(ang)
        s = jnp.sin(ang)
        return jnp.concatenate([c, c], -1), jnp.concatenate([s, s], -1) * sign

    cr, sr = axis_tabs(t // GRID_W)
    cc, sc = axis_tabs(t % GRID_W)
    cos = jnp.concatenate([cr, cc], -1)
    sin = jnp.concatenate([sr, sc], -1)
    return jnp.tile(cos, (1, LANES // HEAD_DIM)), jnp.tile(sin, (1, LANES // HEAD_DIM))


def _na_patterns(rows):
    kh = NA_WIN_H
    nb = rows // NA_ROWS
    kr0s, sigs = [], []
    for b in range(nb):
        r_lo = b * NA_ROWS
        kr0 = int(np.clip(r_lo - kh // 2, 0, rows - NA_KROWS))
        sig = tuple((int(np.clip(r - kh // 2, 0, rows - kh)) - kr0, r - kr0)
                    for r in range(r_lo, r_lo + NA_ROWS))
        kr0s.append(kr0)
        sigs.append(sig)
    uniq = sorted(set(sigs))
    ids = np.array([uniq.index(s) for s in sigs], np.int32)
    return np.array(kr0s, np.int32), ids, uniq


def _na_bias(rpb, uniq, n_ctx):
    kw = NA_WIN_W
    cols = np.arange(GRID_W)
    col_start = np.clip(cols - kw // 2, 0, GRID_W - kw)
    n_pat = len(uniq)
    kk = np.arange(NA_KROWS)
    col_sel = (cols[None, None, :] - cols[None, :, None] + kw - 1
               == np.arange(2 * kw - 1)[:, None, None])
    col_ok = ((cols[None, :] >= col_start[:, None]) & (cols[None, :] < col_start[:, None] + kw))
    row_sel = np.zeros((n_pat, NA_ROWS, NA_KROWS, 2 * NA_WIN_H - 1), bool)
    row_ok = np.zeros((n_pat, NA_ROWS, NA_KROWS), bool)
    for p, sig in enumerate(uniq):
        for rq, (r0_rel, r_rel) in enumerate(sig):
            row_ok[p, rq] = (kk >= r0_rel) & (kk < r0_rel + NA_WIN_H)
            row_sel[p, rq] = (kk[:, None] - r_rel + NA_WIN_H - 1) == np.arange(2 * NA_WIN_H - 1)[None, :]
    hp = lax.Precision.HIGHEST
    by_col = jnp.einsum('hrv,vcj->hrcj', rpb.astype(F32), jnp.asarray(col_sel, F32), precision=hp)
    bias = jnp.einsum('pqkr,hrcj->phqckj', jnp.asarray(row_sel, F32), by_col, precision=hp)
    ok = row_ok[:, None, :, None, :, None] & col_ok[None, None, None, :, None, :]
    bias = jnp.where(jnp.asarray(ok), bias, NEG_INF)
    bias = bias.reshape(n_pat, rpb.shape[0], NA_ROWS * GRID_W, NA_KROWS * GRID_W)
    return jnp.concatenate([bias, jnp.zeros(bias.shape[:3] + (n_ctx,), F32)], axis=-1)


def _na_kernel(kr0_ref, pat_ref, q_ref, k_ref, v_ref, kc_ref, vc_ref, bias_ref, o_ref):
    b = pl.program_id(1)
    start = pl.multiple_of(kr0_ref[b] * GRID_W, GRID_W)
    nk = NA_KROWS * GRID_W
    scale = HEAD_DIM ** -0.5
    one = jnp.ones((), BF16)
    hp = LANES // HEAD_DIM
    for lt in range(q_ref.shape[1] // LANES):
        lanes = slice(lt * LANES, (lt + 1) * LANES)
        q = q_ref[:, lanes]
        k_all = jnp.concatenate([k_ref[pl.ds(start, nk), lanes], kc_ref[:, lanes]], axis=0)
        v_all = jnp.concatenate([v_ref[pl.ds(start, nk), lanes], vc_ref[:, lanes]], axis=0)
        lane = lax.broadcasted_iota(jnp.int32, v_all.shape, 1)
        ext = []
        for h in range(hp):
            sl = slice(h * HEAD_DIM, (h + 1) * HEAD_DIM)
            own = (lane >= h * HEAD_DIM) & (lane < (h + 1) * HEAD_DIM)
            s = _dot_nt(q[:, sl], k_all[:, sl]) * scale + bias_ref[0, lt * hp + h]
            p = jnp.exp(s - jnp.max(s, axis=-1, keepdims=True)).astype(BF16)
            ext.append(_dot(p, jnp.where(own, v_all, one)))
        out_lane = lax.broadcasted_iota(jnp.int32, ext[0].shape, 1)
        low = out_lane < HEAD_DIM
        num = jnp.where(low, ext[0], ext[1])
        den = pltpu.roll(jnp.where(low, ext[1], ext[0]), HEAD_DIM, 1)
        o_ref[:, lanes] = (num / den).astype(o_ref.dtype)


def _na_attention(q, k, v, kc, vc, bias):
    s, bw = q.shape
    l = kc.shape[0]
    rows = s // GRID_W
    kr0s, ids, _ = _na_patterns(rows)
    tq = NA_ROWS * GRID_W
    nk = NA_KROWS * GRID_W
    hp = NA_LANES // HEAD_DIM
    grid_spec = pltpu.PrefetchScalarGridSpec(
        num_scalar_prefetch=2,
        grid=(bw // NA_LANES, rows // NA_ROWS),
        in_specs=[pl.BlockSpec((tq, NA_LANES), lambda p, b, kr, pt: (b, p)),
                  pl.BlockSpec((s, NA_LANES), lambda p, b, kr, pt: (0, p)),
                  pl.BlockSpec((s, NA_LANES), lambda p, b, kr, pt: (0, p)),
                  pl.BlockSpec((l, NA_LANES), lambda p, b, kr, pt: (0, p)),
                  pl.BlockSpec((l, NA_LANES), lambda p, b, kr, pt: (0, p)),
                  pl.BlockSpec((1, hp, tq, nk + l), lambda p, b, kr, pt: (pt[b], p, 0, 0))],
        out_specs=pl.BlockSpec((tq, NA_LANES), lambda p, b, kr, pt: (b, p)),
    )
    return pl.pallas_call(
        _na_kernel,
        grid_spec=grid_spec,
        out_shape=jax.ShapeDtypeStruct((s, bw), BF16),
        compiler_params=_cparams(("arbitrary", "arbitrary")),
        name="na_attention",
    )(jnp.asarray(kr0s), jnp.asarray(ids), q, k, v, kc, vc, bias)


def _ctx_attn_kernel(*refs, n_heads, n_rep, has_sink):
    if has_sink:
        sink_ref, q_ref, k_ref, v_ref, o_ref = refs
    else:
        q_ref, k_ref, v_ref, o_ref = refs
    scale = HEAD_DIM ** -0.5
    q = q_ref[...]
    k = k_ref[...]
    v = v_ref[...]
    outs = []
    for h in range(n_heads):
        g = h // n_rep
        sl = slice(h * HEAD_DIM, (h + 1) * HEAD_DIM)
        gl = slice(g * HEAD_DIM, (g + 1) * HEAD_DIM)
        s = _dot_nt(q[:, sl], k[:, gl]) * scale
        m = jnp.max(s, axis=-1, keepdims=True)
        if has_sink:
            m = jnp.maximum(m, sink_ref[h])
        p = jnp.exp(s - m)
        den = jnp.sum(p, axis=-1, keepdims=True)
        if has_sink:
            den = den + jnp.exp(sink_ref[h] - m)
        outs.append(_dot(p.astype(BF16), v[:, gl]) / den)
    o_ref[...] = jnp.concatenate(outs, axis=-1).astype(o_ref.dtype)


def _ctx_attention(q, k, v, sink):
    l, bw = q.shape
    n_heads = bw // HEAD_DIM
    n_rep = n_heads // (k.shape[1] // HEAD_DIM)
    has_sink = sink is not None
    full = lambda a: pl.BlockSpec(a.shape, lambda i: (0, 0))
    in_specs = [full(q), full(k), full(v)]
    args = [q, k, v]
    if has_sink:
        in_specs = [pl.BlockSpec(memory_space=pltpu.SMEM)] + in_specs
        args = [sink] + args
    return pl.pallas_call(
        functools.partial(_ctx_attn_kernel, n_heads=n_heads, n_rep=n_rep, has_sink=has_sink),
        grid=(1,),
        in_specs=in_specs,
        out_specs=pl.BlockSpec((l, bw), lambda i: (0, 0)),
        out_shape=jax.ShapeDtypeStruct((l, bw), BF16),
        compiler_params=_cparams(("arbitrary",)),
        name="ctx_attention",
    )(*args)


def _swa_kernel(sink_ref, q_ref, k_ref, v_ref, kc_ref, vc_ref, o_ref, *, seq, n_kv, n_rep):
    n = pl.program_id(0)
    nk = 3 * SW_BLOCK
    scale = HEAD_DIM ** -0.5
    ws = jnp.clip((n - 1) * SW_BLOCK, 0, seq - nk)
    ws = pl.multiple_of(ws, SW_BLOCK)
    q = q_ref[...]
    n_all = nk + kc_ref.shape[0]
    k_all = jnp.concatenate([k_ref[pl.ds(ws, nk), :], kc_ref[...]], axis=0)
    v_all = jnp.concatenate([v_ref[pl.ds(ws, nk), :], vc_ref[...]], axis=0)
    qpos = n * SW_BLOCK + lax.broadcasted_iota(jnp.int32, (SW_BLOCK, n_all), 0)
    col = lax.broadcasted_iota(jnp.int32, (SW_BLOCK, n_all), 1)
    visible = (col >= nk) | (jnp.abs(ws + col - qpos) <= SW_WINDOW)
    maskb = jnp.where(visible, 0.0, NEG_INF).astype(F32)
    lane = lax.broadcasted_iota(jnp.int32, v_all.shape, 1)
    one = jnp.ones((), BF16)
    outs = [None] * (n_kv * n_rep)
    for g in range(n_kv):
        gl = slice(g * HEAD_DIM, (g + 1) * HEAD_DIM)
        own = (lane >= g * HEAD_DIM) & (lane < (g + 1) * HEAD_DIM)
        qg = jnp.concatenate([q[:, (g * n_rep + r) * HEAD_DIM:(g * n_rep + r + 1) * HEAD_DIM]
                              for r in range(n_rep)], axis=0)
        s = _dot_nt(qg, k_all[:, gl]) * scale
        s = (s.reshape(n_rep, SW_BLOCK, n_all) + maskb[None]).reshape(n_rep * SW_BLOCK, n_all)
        sink = jnp.concatenate([jnp.full((SW_BLOCK, 1), sink_ref[g * n_rep + r], F32)
                                for r in range(n_rep)], axis=0)
        m = jnp.maximum(jnp.max(s, axis=-1, keepdims=True), sink)
        p = jnp.exp(s - m).astype(BF16)
        ext = _dot(p, jnp.where(own, v_all, one))
        den = pltpu.roll(ext, HEAD_DIM, 1) + jnp.exp(sink - m)
        o = ext / den
        for r in range(n_rep):
            outs[g * n_rep + r] = o[r * SW_BLOCK:(r + 1) * SW_BLOCK, gl]
    o_ref[...] = jnp.concatenate(outs, axis=-1).astype(o_ref.dtype)


def _sw_attention(q, k, v, kc, vc, sink):
    s, bw = q.shape
    l = kc.shape[0]
    kvw = k.shape[1]
    n_kv = kvw // HEAD_DIM
    n_rep = (bw // HEAD_DIM) // n_kv
    assert kvw == LANES
    return pl.pallas_call(
        functools.partial(_swa_kernel, seq=s, n_kv=n_kv, n_rep=n_rep),
        grid=(s // SW_BLOCK,),
        in_specs=[pl.BlockSpec(memory_space=pltpu.SMEM),
                  pl.BlockSpec((SW_BLOCK, bw), lambda n: (n, 0)),
                  pl.BlockSpec((s, kvw), lambda n: (0, 0)),
                  pl.BlockSpec((s, kvw), lambda n: (0, 0)),
                  pl.BlockSpec((l, kvw), lambda n: (0, 0)),
                  pl.BlockSpec((l, kvw), lambda n: (0, 0))],
        out_specs=pl.BlockSpec((SW_BLOCK, bw), lambda n: (n, 0)),
        out_shape=jax.ShapeDtypeStruct((s, bw), BF16),
        compiler_params=_cparams(("arbitrary",)),
        name="sw_attention",
    )(sink, q, k, v, kc, vc)


def _s5_prepare(lam_re, lam_im, b_re, b_im, c_re, c_im, log_step):
    tc = S5_CHUNK
    lam = lax.complex(lam_re.astype(F32), lam_im.astype(F32))
    dt = jnp.exp(log_step.astype(F32))[..., None]
    lam_dt = lam * dt
    lam_bar = jnp.exp(lam_dt)
    b_bar = ((lam_bar - 1.0) / lam)[..., None] * lax.complex(b_re.astype(F32), b_im.astype(F32))
    c_mat = lax.complex(c_re.astype(F32), c_im.astype(F32))
    kk = jnp.arange(tc + 1, dtype=F32)
    pw = jnp.exp(lam_dt[None] * kk[:, None, None, None])
    n_g, n_p, n_h = b_bar.shape[1:]

    bt = jnp.transpose(b_bar, (1, 0, 3, 2))
    bt = jnp.stack([bt.real, bt.imag], axis=2)
    cw = c_mat[None] * pw[:tc, :, :, None, :]
    cw = jnp.stack([cw[:, 0], cw[::-1, 1]], axis=1)
    cwt = jnp.transpose(cw, (2, 1, 4, 0, 3)).reshape(n_g, 2, n_p, tc * n_h)
    cwt = jnp.stack([cwt.real, cwt.imag], axis=2)
    t_i = np.arange(tc)

    def cat_ri(z):
        return jnp.concatenate([z.real, z.imag], axis=-1)

    w_f = pw[tc - 1 - t_i, 0][:, :, None, :] * jnp.transpose(b_bar[0], (0, 2, 1))[None]
    w_r = pw[t_i, 1][:, :, None, :] * jnp.transpose(b_bar[1], (0, 2, 1))[None]
    w_f = jnp.transpose(cat_ri(w_f), (1, 0, 2, 3)).reshape(n_g, tc * n_h, 2 * n_p)
    w_r = jnp.transpose(cat_ri(w_r), (1, 0, 2, 3)).reshape(n_g, tc * n_h, 2 * n_p)
    w_cat = jnp.concatenate([w_f, w_r], axis=-1).astype(BF16)

    def v_of(c_dir, pw_sel):
        cp = c_dir[None] * pw_sel[:, :, None, :]
        v = jnp.concatenate([cp.real, -cp.imag], axis=-1)
        return jnp.transpose(v, (1, 3, 0, 2)).reshape(n_g, 2 * n_p, tc * n_h)
    v_cat = jnp.concatenate([v_of(c_mat[0], pw[t_i + 1, 0]), v_of(c_mat[1], pw[tc - t_i, 1])],
                            axis=1).astype(BF16)

    def coef(k):
        a = jnp.exp(lam_dt * (tc * k))
        return jnp.stack([jnp.concatenate([a.real, a.real], -1),
                          jnp.concatenate([-a.imag, a.imag], -1)], axis=2)
    a_log = jnp.stack([coef(1.0), coef(2.0), coef(4.0)], axis=2)
    a_car = jnp.stack([coef(float(j)) for j in range(1, SUBLANES + 1)], axis=3)
    a_car = jnp.stack([a_car[0], a_car[1][:, :, ::-1]], axis=0)
    return _s5_expand(bt, cwt, w_cat, v_cat, jnp.transpose(a_log, (1, 0, 2, 3, 4)),
                      jnp.transpose(a_car, (1, 0, 2, 3, 4)))


def _s5_expand(bt, cwt, w_cat, v_cat, a_log, a_car):
    n_g = bt.shape[0]
    n_t = n_g // S5_LG
    n_state = 2 * S5_STATE
    by_tile = lambda a: a.reshape((n_t, S5_LG) + a.shape[1:])

    def lanes_of(a):
        a = jnp.moveaxis(by_tile(a), 1, -2)
        return a.reshape(a.shape[:-2] + (S5_LG * n_state,))
    return by_tile(bt), by_tile(cwt), by_tile(w_cat), by_tile(v_cat), lanes_of(a_log), lanes_of(a_car)


def _s5_lane_perm():
    r = np.arange(S5_CHUNK * LANES)
    t, a, h = r // LANES, (r % LANES) // S5_GROUP, r % S5_GROUP
    dest = a * (S5_CHUNK * S5_GROUP) + t * S5_GROUP + h
    return (jnp.asarray(dest)[:, None] == jnp.arange(S5_CHUNK * LANES)[None, :]).astype(BF16)


def _chunk_rows(tok_ref):
    n_rows = tok_ref.shape[1] // S5_CHUNK
    return jnp.concatenate([tok_ref[0, pl.ds(t, n_rows, stride=S5_CHUNK), :] for t in range(S5_CHUNK)], axis=-1)


def _s5_in_kernel(u_ref, perm_ref, bt_ref, cwt_ref, w_ref, yp_ref, z_ref, m_scr):
    width = S5_CHUNK * S5_GROUP
    n_state = 2 * S5_STATE
    half = S5_LG * n_state

    @pl.when(pl.program_id(1) == 0)
    def _():
        lane = lax.broadcasted_iota(jnp.int32, (S5_GROUP, width), 1)
        def lag_kernels(a, direction):
            hp = lax.Precision.HIGHEST
            return (jnp.dot(bt_ref[0, a, direction, 0], cwt_ref[0, a, direction, 0], precision=hp,
                            preferred_element_type=F32)
                    - jnp.dot(bt_ref[0, a, direction, 1], cwt_ref[0, a, direction, 1], precision=hp,
                              preferred_element_type=F32))

        for a in range(S5_LG):
            k_fwd = lag_kernels(a, 0)
            k_rev = lag_kernels(a, 1)
            for tau in range(S5_CHUNK):
                right = tau * S5_GROUP
                left = (S5_CHUNK - 1 - tau) * S5_GROUP
                blk = (jnp.where(lane >= right, pltpu.roll(k_fwd, right, 1), 0.0)
                       + jnp.where(lane < width - left, pltpu.roll(k_rev, (width - left) % width, 1), 0.0))
                m_scr[a, tau * S5_GROUP:(tau + 1) * S5_GROUP, 0:width] = blk.astype(BF16)
            m_scr[a, :, width:] = w_ref[0, a]

    up = _dot(_chunk_rows(u_ref).astype(BF16), perm_ref[...]).astype(BF16)
    for a in range(S5_LG):
        zz = _dot(up[:, a * width:(a + 1) * width], m_scr[a])
        yp_ref[0, :, a * width:(a + 1) * width] = zz[:, :width]
        z_ref[0, :, a * n_state:(a + 1) * n_state] = zz[:, width:width + n_state]
        z_ref[0, :, half + a * n_state:half + (a + 1) * n_state] = zz[:, width + n_state:]


def _s5_scan_kernel(z_ref, alog_ref, acar_ref, e_ref, *, nc, nc_ctx):
    n_state = 2 * S5_STATE
    half = z_ref.shape[2] // 2
    nblk = nc // SUBLANES
    nblk_ctx = nc_ctx // SUBLANES
    row = lax.broadcasted_iota(jnp.int32, (SUBLANES, n_state), 0)

    def cmul(a1, a2, s):
        return a1 * s + a2 * pltpu.roll(s, S5_STATE, 1)

    def step(t, carry):
        blk = (t, jnp.where(t < nblk_ctx, nblk_ctx - 1 - t, nblk + nblk_ctx - 1 - t))
        new = []
        for d in range(2):
            r0 = pl.multiple_of(blk[d] * SUBLANES, SUBLANES)
            for j in range(half // n_state):
                tile = slice(j * n_state, (j + 1) * n_state)
                lanes = slice(d * half + j * n_state, d * half + (j + 1) * n_state)
                z = z_ref[0, pl.ds(r0, SUBLANES), lanes]
                c = carry[len(new)]
                for i, sh in enumerate((1, 2, 4)):
                    if d == 0:
                        zs = jnp.where(row >= sh, pltpu.roll(z, sh, 0), 0.0)
                    else:
                        zs = jnp.where(row < SUBLANES - sh, pltpu.roll(z, SUBLANES - sh, 0), 0.0)
                    z = z + cmul(alog_ref[0, d, i, 0:1, tile], alog_ref[0, d, i, 1:2, tile], zs)
                cb = jnp.broadcast_to(c, (SUBLANES, n_state))
                s = z + cmul(acar_ref[0, d, 0, :, tile], acar_ref[0, d, 1, :, tile], cb)
                if d == 0:
                    e = jnp.where(row >= 1, pltpu.roll(s, 1, 0), cb)
                    c_new = s[SUBLANES - 1:SUBLANES, :]
                else:
                    e = jnp.where(row < SUBLANES - 1, pltpu.roll(s, SUBLANES - 1, 0), cb)
                    c_new = s[0:1, :]
                e_ref[0, pl.ds(r0, SUBLANES), lanes] = e
                new.append(c_new)
        return tuple(new)

    init = tuple(jnp.zeros((1, n_state), F32) for _ in range(2 * (half // n_state)))
    lax.fori_loop(0, nblk, step, init, unroll=2)


def _s5_out_kernel(e_ref, yp_ref, u_ref, d_ref, perm_ref, v_ref, y_ref):
    width = S5_CHUNK * S5_GROUP
    n_state = 2 * S5_STATE
    half = S5_LG * n_state
    parts = []
    for a in range(S5_LG):
        e_a = jnp.concatenate([e_ref[0, :, a * n_state:(a + 1) * n_state],
                               e_ref[0, :, half + a * n_state:half + (a + 1) * n_state]], axis=-1)
        parts.append(yp_ref[0, :, a * width:(a + 1) * width] + _dot(e_a.astype(BF16), v_ref[0, a]))
    y = jnp.concatenate(parts, axis=-1)
    hi = y.astype(BF16)
    lo = (y - hi.astype(F32)).astype(BF16)
    perm = perm_ref[...]
    y = _dot_nt(hi, perm) + _dot_nt(lo, perm) + _chunk_rows(u_ref) * d_ref[0]
    n_rows = y.shape[0]
    for t in range(S5_CHUNK):
        y_ref[0, pl.ds(t, n_rows, stride=S5_CHUNK), :] = y[:, t * LANES:(t + 1) * LANES]


def _s5_rows(nc):
    return max(r for r in range(2 * SUBLANES, 257, 2 * SUBLANES) if nc % r == 0)


def _s5_mixer_pre_glu(u_ctx, u_lat, prep, d_skip):
    bt, cwt, w_cat, v_cat, a_log, a_car = prep
    n_t, l, _ = u_ctx.shape
    s = u_lat.shape[1]
    nc, nc_ctx = (l + s) // S5_CHUNK, l // S5_CHUNK
    width = S5_CHUNK * LANES
    n_z = 2 * S5_LG * 2 * S5_STATE
    u = jnp.concatenate([u_ctx, u_lat], axis=1)
    d_rows = jnp.tile(d_skip.astype(F32).reshape(n_t, 1, LANES), (1, 1, S5_CHUNK))
    perm = _s5_lane_perm()
    rb = _s5_rows(nc)
    rows = lambda w: pl.BlockSpec((1, rb, w), lambda g, r: (g, r, 0))
    per_tile = lambda a: pl.BlockSpec((1,) + a.shape[1:], lambda g, *_: (g,) + (0,) * (a.ndim - 1))
    whole = lambda a: pl.BlockSpec(a.shape, lambda g, r: (0,) * a.ndim)
    tokens = pl.BlockSpec((1, rb * S5_CHUNK, LANES), lambda g, r: (g, r, 0))
    width_g = S5_CHUNK * S5_GROUP
    yp, z = pl.pallas_call(
        _s5_in_kernel,
        grid=(n_t, nc // rb),
        in_specs=[tokens, whole(perm), per_tile(bt), per_tile(cwt), per_tile(w_cat)],
        out_specs=[rows(width), rows(n_z)],
        out_shape=[jax.ShapeDtypeStruct((n_t, nc, width), F32), jax.ShapeDtypeStruct((n_t, nc, n_z), F32)],
        scratch_shapes=[pltpu.VMEM((S5_LG, width_g, width_g + 4 * S5_STATE), BF16)],
        compiler_params=_cparams(("arbitrary", "arbitrary")),
        name="s5_in",
    )(u, perm, bt, cwt, w_cat)
    e = pl.pallas_call(
        functools.partial(_s5_scan_kernel, nc=nc, nc_ctx=nc_ctx),
        grid=(n_t,),
        in_specs=[per_tile(z), per_tile(a_log), per_tile(a_car)],
        out_specs=per_tile(z),
        out_shape=jax.ShapeDtypeStruct(z.shape, F32),
        compiler_params=_cparams(("arbitrary",)),
        name="s5_scan",
    )(z, a_log, a_car)
    y = pl.pallas_call(
        _s5_out_kernel,
        grid=(n_t, nc // rb),
        in_specs=[rows(n_z), rows(width), tokens, per_tile(d_rows), whole(perm), per_tile(v_cat)],
        out_specs=tokens,
        out_shape=jax.ShapeDtypeStruct((n_t, l + s, LANES), F32),
        compiler_params=_cparams(("arbitrary", "arbitrary")),
        name="s5_out",
    )(e, yp, u, d_rows, perm, v_cat)
    return y


def _merge_kernel(x_ref, ya_ref, y5_ref, ys_ref, ga_ref, gb_ref, gs_ref, wglu_ref, wb_ref, wo_ref,
                  gt_ref, g_ref, sh_ref, sc_ref, xo_ref, h_ref, ht_ref):
    y5 = jnp.concatenate([y5_ref[t] for t in range(y5_ref.shape[0])], axis=-1)
    z = _gelu(y5)
    yb = z * jax.nn.sigmoid(_dot(z.astype(BF16), wglu_ref[...]))
    m = (ga_ref[...].astype(F32) * _dot(ya_ref[...], wb_ref[0])
         + gb_ref[...].astype(F32) * _dot(yb.astype(BF16), wb_ref[1])
         + gs_ref[...].astype(F32) * _dot(ys_ref[...], wb_ref[2]))
    x = x_ref[...] + gt_ref[...] * _dot(m.astype(BF16), wo_ref[...])
    xo_ref[...] = x
    h = _norm_mod(x, g_ref[...], sh_ref[...], sc_ref[...])
    h_ref[...] = h.astype(BF16)
    ht_ref[...] = h.T.astype(BF16)


def _merge(x, ya, y5, y5_row0, ys, ga, gb, gs, w_glu, w_branch, w_out, gate, g, shift, scale):
    n, d = x.shape
    bw = ya.shape[1]
    tm = min(ROW_BLOCK, n)
    assert y5_row0 % tm == 0
    row = lambda w: pl.BlockSpec((tm, w), lambda i: (i, 0))
    full = lambda a: pl.BlockSpec(a.shape, lambda i: (0,) * a.ndim)
    vec = pl.BlockSpec((1, d), lambda i: (0, 0))
    y5_rows = pl.BlockSpec((y5.shape[0], tm, LANES), lambda i: (0, i + y5_row0 // tm, 0))
    return pl.pallas_call(
        _merge_kernel,
        grid=(n // tm,),
        in_specs=[row(d), row(bw), y5_rows, row(bw), row(d), row(d), row(d),
                  full(w_glu), full(w_branch), full(w_out), vec, vec, vec, vec],
        out_specs=[row(d), row(d), pl.BlockSpec((d, tm), lambda i: (0, i))],
        out_shape=[jax.ShapeDtypeStruct((n, d), F32), jax.ShapeDtypeStruct((n, d), BF16),
                   jax.ShapeDtypeStruct((d, n), BF16)],
        compiler_params=_cparams(("arbitrary",)),
        name="merge",
    )(x, ya, y5, ys, ga, gb, gs, w_glu, w_branch, w_out, gate, g, shift, scale)


def _knock_out_16(s, order, exact, want_rank=True):
    rank = jnp.full(s.shape, float(PEER_TOPK), F32) if want_rank else None
    live = s
    vals = []
    for r in range(PEER_TOPK):
        m = jnp.max(live, axis=0, keepdims=True)
        hit = live == m
        if exact:
            first = jnp.min(jnp.where(hit, order, float(PEER_TOPK * PEER_NKEYS)), axis=0, keepdims=True)
            hit = order == first
        if want_rank:
            rank = jnp.where(hit, float(r), rank)
        live = jnp.where(hit, -jnp.inf, live)
        vals.append(m)
    out = (rank < float(PEER_TOPK)) if want_rank else (live != s)
    n_out = jnp.sum(jnp.where(out, 1.0, 0.0), axis=0, keepdims=True)
    return rank, jnp.concatenate(vals, axis=0), n_out


def _bf16_pair_word(x):
    hi = pltpu.bitcast(x.astype(BF16).astype(F32), jnp.uint32)
    return hi | (hi >> 16)


def _pair_tiles():
    tiles = [(0, 0, 8), (0, 8, 8)]
    for a in range(1, 8):
        tiles.append((a, 0, PEER_TOPK // (a + 1)))
    return tiles


def _route_kernel(h_ref, wq_ref, k1_ref, k2_ref, cnt_ref, c1_ref, rk_ref, e2_ref, q_scr):
    tb = h_ref.shape[0]
    q_scr[...] = _dot(h_ref[...], wq_ref[...]).astype(BF16)
    half = PEER_QDIM // 2
    iota = lax.broadcasted_iota(jnp.int32, (PEER_NKEYS, tb), 0).astype(F32)
    row8 = lax.broadcasted_iota(jnp.int32, (SUBLANES, tb), 0).astype(F32)
    k1 = k1_ref[...]
    k2 = k2_ref[...]

    def emit(hd, s1, s2, exact):
        rank1, v1, n1 = _knock_out_16(s1, iota, exact, want_rank=exact)
        rank2, v2, n2 = _knock_out_16(s2, iota, exact)
        e1v = jnp.exp(v1 - v1[0:1])
        e2v = jnp.exp(v2 - v2[0:1])
        tiles, flats, gates = [], [], []
        for a, b0, nv in _pair_tiles():
            c = v1[a:a + 1] + v2[b0:b0 + SUBLANES]
            tiles.append(jnp.where(row8 < nv, c, -jnp.inf))
            flats.append(a * PEER_TOPK + b0 + row8)
            gates.append(e1v[a:a + 1] * e2v[b0:b0 + SUBLANES])
        tiles.append(v1[SUBLANES:] + v2[0:1])
        flats.append((row8 + SUBLANES) * PEER_TOPK)
        gates.append(e1v[SUBLANES:] * e2v[0:1])
        cand = jnp.concatenate(tiles, axis=0)
        flat = jnp.concatenate(flats, axis=0)
        gate = jnp.concatenate(gates, axis=0)
        rank_c, _, n_c = _knock_out_16(cand, flat, exact)
        self_ = jnp.where(rank_c < float(PEER_TOPK), 1.0, 0.0)
        z = jnp.sum(self_ * gate, axis=0, keepdims=True)
        cnt = [self_[0:8].sum(axis=0, keepdims=True) + self_[8:16].sum(axis=0, keepdims=True)]
        for t in range(2, 9):
            cnt.append(self_[t * SUBLANES:(t + 1) * SUBLANES].sum(axis=0, keepdims=True))
        cnt = jnp.concatenate(cnt + [self_[9 * SUBLANES:]], axis=0)
        cnt1 = jnp.zeros((PEER_NKEYS, tb), F32)
        for a in range(PEER_TOPK):
            is_a = (rank1 == float(a)) if exact else (s1 == v1[a:a + 1])
            cnt1 = jnp.where(is_a, cnt[a:a + 1], cnt1)
        cnt_ref[hd] = _bf16_pair_word(cnt1)
        c1_ref[hd] = _bf16_pair_word(jnp.exp(s1 - v1[0:1]) * (0.5 / z))
        rk_ref[hd] = rank2.astype(BF16)
        e2_ref[hd] = jnp.exp(s2 - v2[0:1]).astype(BF16)
        want = float(PEER_TOPK)
        return jnp.where((n1 != want) | (n2 != want) | (n_c != want), 1.0, 0.0)

    def head_group(hg, _):
        scores = []
        for k in range(PEER_ROUTE_HEADS):
            hd = hg * PEER_ROUTE_HEADS + k
            c0 = pl.multiple_of(hd * PEER_QDIM, PEER_QDIM)
            s1 = _dot_nt(k1, q_scr[:, pl.ds(c0, half)])
            s2 = _dot_nt(k2, q_scr[:, pl.ds(c0 + half, half)])
            scores.append((hd, s1, s2))
        tied = [jnp.max(emit(hd, s1, s2, exact=False)) for hd, s1, s2 in scores]
        for (hd, s1, s2), t in zip(scores, tied):
            @pl.when(t > 0.0)
            def _():
                emit(hd, s1, s2, exact=True)
        return 0

    lax.fori_loop(0, PEER_HEADS // PEER_ROUTE_HEADS, head_group, 0)


def _peer_route(h, w_q, k1, k2):
    n, d = h.shape
    tb = min(PEER_ROUTE_BLOCK, n)
    qw = w_q.shape[1]
    out_blk = pl.BlockSpec((PEER_HEADS, PEER_NKEYS, tb), lambda i: (0, 0, i))
    shp = lambda dt: jax.ShapeDtypeStruct((PEER_HEADS, PEER_NKEYS, n), dt)
    return pl.pallas_call(
        _route_kernel,
        grid=(n // tb,),
        in_specs=[pl.BlockSpec((tb, d), lambda i: (i, 0)),
                  pl.BlockSpec((d, qw), lambda i: (0, 0)),
                  pl.BlockSpec(k1.shape, lambda i: (0, 0)),
                  pl.BlockSpec(k2.shape, lambda i: (0, 0))],
        out_specs=[out_blk] * 4,
        out_shape=[shp(jnp.uint32), shp(jnp.uint32), shp(BF16), shp(BF16)],
        scratch_shapes=[pltpu.VMEM((tb, qw), BF16)],
        compiler_params=_cparams(("arbitrary",)),
        name="peer_route",
    )(h, w_q, k1, k2)


def _dense_kernel(ht_ref, x_ref, gt_ref, u_ref, vt_ref, cnt_ref, c1_ref, rk_ref, e2_ref, o_ref,
                  acc_ref, w_scr):
    eb = pl.program_id(1)
    tb = ht_ref.shape[1]
    n_exp = u_ref.shape[0]
    n_sub = n_exp // PEER_SUB
    i_per_sub = PEER_SUB // PEER_NKEYS

    @pl.when(eb == 0)
    def _():
        acc_ref[...] = jnp.zeros_like(acc_ref)

    def scores(sb):
        return _dot(u_ref[sb * PEER_SUB:(sb + 1) * PEER_SUB, :], ht_ref[...])

    pk_rows = 2 * SUBLANES
    n_pk = PEER_NKEYS // pk_rows

    def row_tile(ref, hd, i):
        words = jnp.broadcast_to(ref[hd, pl.ds(i, 1), :], (SUBLANES, tb))
        return pltpu.bitcast(words, BF16)[None]

    a_next = scores(0)
    for sb in range(n_sub):
        a_cur = a_next
        if sb + 1 < n_sub:
            a_next = scores(sb + 1)
        for il in range(i_per_sub):
            i = eb * (n_exp // PEER_NKEYS) + sb * i_per_sub + il
            gsum = jnp.zeros((n_pk, pk_rows, tb), BF16)
            for hd in range(PEER_HEADS):
                cnt = row_tile(cnt_ref, hd, i)
                c1 = row_tile(c1_ref, hd, i)
                rk = rk_ref[hd].reshape(n_pk, pk_rows, tb)
                e2 = e2_ref[hd].reshape(n_pk, pk_rows, tb)
                gsum = gsum + jnp.where(rk < cnt, e2 * c1, jnp.zeros((), BF16))
            r0 = sb * PEER_SUB + il * PEER_NKEYS
            a_i = a_cur[il * PEER_NKEYS:(il + 1) * PEER_NKEYS]
            half_gate = gsum.reshape(PEER_NKEYS, tb).astype(F32)
            w_i = a_i * (1.0 + lax.erf(a_i * (1.0 / math.sqrt(2.0)))) * half_gate
            w_scr[r0:r0 + PEER_NKEYS, :] = w_i.astype(BF16)
        done = (sb + 1) * PEER_SUB
        if done % PEER_ACC_CHUNK == 0:
            c0 = done - PEER_ACC_CHUNK
            acc_ref[...] += _dot(vt_ref[:, c0:done], w_scr[c0:done, :])

    @pl.when(eb == pl.num_programs(1) - 1)
    def _():
        o_ref[...] = x_ref[...] + gt_ref[...] * acc_ref[...].T


def _peer_dense(h_t, x, gate, u, v_t, layer, cnt1, c1, rank2, e2):
    d, n = h_t.shape
    n_e = u.shape[1]
    tb = min(PEER_TOK_BLOCK, n)
    eb = PEER_EXP_BLOCK
    tab = pl.BlockSpec((PEER_HEADS, PEER_NKEYS, tb), lambda i, e: (0, 0, i))
    return pl.pallas_call(
        _dense_kernel,
        grid=(n // tb, n_e // eb),
        in_specs=[pl.BlockSpec((d, tb), lambda i, e: (0, i)),
                  pl.BlockSpec((tb, d), lambda i, e: (i, 0)),
                  pl.BlockSpec((1, d), lambda i, e: (0, 0)),
                  pl.BlockSpec((None, eb, d), lambda i, e: (layer, e, 0)),
                  pl.BlockSpec((None, d, eb), lambda i, e: (layer, 0, e)),
                  tab, tab, tab, tab],
        out_specs=pl.BlockSpec((tb, d), lambda i, e: (i, 0)),
        out_shape=jax.ShapeDtypeStruct((n, d), F32),
        scratch_shapes=[pltpu.VMEM((d, tb), F32), pltpu.VMEM((eb, tb), BF16)],
        compiler_params=_cparams(("arbitrary", "arbitrary")),
        name="peer_dense",
    )(h_t, x, gate, u, v_t, cnt1, c1, rank2, e2)


def _final_norm_kernel(x_ref, g_ref, o_ref):
    x = x_ref[...]
    o_ref[...] = x * lax.rsqrt(jnp.mean(x * x, axis=-1, keepdims=True) + EPS) * g_ref[...]


def _final_norm(x, g):
    n, d = x.shape
    tm = min(2 * ROW_BLOCK, n)
    return pl.pallas_call(
        _final_norm_kernel,
        grid=(n // tm,),
        in_specs=[pl.BlockSpec((tm, d), lambda i: (i, 0)), pl.BlockSpec((1, d), lambda i: (0, 0))],
        out_specs=pl.BlockSpec((tm, d), lambda i: (i, 0)),
        out_shape=jax.ShapeDtypeStruct((n, d), F32),
        compiler_params=_cparams(("arbitrary",)),
        name="final_norm",
    )(x, g)


def kernel(x, c, ctx, c_ctx, w_mod, b_mod, g_mix, g_ffn, w_in, na_rpb, s5_lam_re, s5_lam_im,
           s5_b_re, s5_b_im, s5_c_re, s5_c_im, s5_log_step, s5_d, s5_w_glu, sw_sink, w_branch,
           w_out, peer_w_q, peer_sub_keys, peer_u, peer_v, g_final):
    batch, seq, d = x.shape
    l_ctx = ctx.shape[1]
    depth = w_mod.shape[0]
    assert batch == 1 and seq % (NA_ROWS * GRID_W) == 0 and l_ctx % (S5_CHUNK * SUBLANES) == 0
    assert seq // GRID_W >= NA_KROWS and seq >= 3 * SW_BLOCK

    cc = jnp.zeros((SUBLANES, d), F32).at[0].set(c[0]).at[1].set(c_ctx)
    mod = _mod_vectors(cc, w_mod, b_mod).reshape(depth, SUBLANES, 6, d)
    rope_tabs = _rope_tables(seq)
    row = lambda v: v.reshape(1, d)
    na_uniq = _na_patterns(seq // GRID_W)[2]
    na_bias = jax.vmap(lambda r: _na_bias(r, na_uniq, l_ctx))(na_rpb)
    s5_prep = jax.vmap(_s5_prepare)(s5_lam_re, s5_lam_im, s5_b_re, s5_b_im, s5_c_re, s5_c_im, s5_log_step)
    u_all = peer_u.astype(BF16)
    vt_all = jnp.transpose(peer_v.astype(BF16), (0, 2, 1))

    xx, xc = x[0], ctx[0]
    for l in range(depth):
        need_ctx = l < depth - 1
        m_lat, m_ctx = mod[l, 0], mod[l, 1]
        w_in_l = w_in[l].astype(BF16)
        g_mix_l = row(g_mix[l])
        g_ffn_l = row(g_ffn[l])

        qa, ka, va, ub, qs, ks, vs, ga, gb, gs = _inproj(
            xx, g_mix_l, row(m_lat[0]), row(m_lat[1]), w_in_l, rope_tabs)
        qa_c, ka_c, va_c, ub_c, qs_c, ks_c, vs_c, ga_c, gb_c, gs_c = _inproj(
            xc, g_mix_l, row(m_ctx[0]), row(m_ctx[1]), w_in_l, None)

        ya = _na_attention(qa, ka, va, ka_c, va_c, na_bias[l])
        ys = _sw_attention(qs, ks, vs, ks_c, vs_c, sw_sink[l])
        y5 = _s5_mixer_pre_glu(ub_c, ub, [a[l] for a in s5_prep], s5_d[l])

        w_glu_l = s5_w_glu[l].astype(BF16)
        w_branch_l = w_branch[l].astype(BF16)
        w_out_l = w_out[l].astype(BF16)
        xx, hx2, hx2_t = _merge(xx, ya, y5, l_ctx, ys, ga, gb, gs, w_glu_l, w_branch_l, w_out_l,
                                row(m_lat[2]), g_ffn_l, row(m_lat[3]), row(m_lat[4]))

        w_q_l = peer_w_q[l].astype(BF16)
        k1 = peer_sub_keys[l, 0].astype(BF16)
        k2 = peer_sub_keys[l, 1].astype(BF16)
        if need_ctx:
            ya_c = _ctx_attention(qa_c, ka_c, va_c, None)
            ys_c = _ctx_attention(qs_c, ks_c, vs_c, sw_sink[l])
            xc, hc2, hc2_t = _merge(xc, ya_c, y5, 0, ys_c, ga_c, gb_c, gs_c, w_glu_l, w_branch_l, w_out_l,
                                    row(m_ctx[2]), g_ffn_l, row(m_ctx[3]), row(m_ctx[4]))
            xc = _peer_dense(hc2_t, xc, row(m_ctx[5]), u_all, vt_all, l, *_peer_route(hc2, w_q_l, k1, k2))
        xx = _peer_dense(hx2_t, xx, row(m_lat[5]), u_all, vt_all, l, *_peer_route(hx2, w_q_l, k1, k2))

    return _final_norm(xx, row(g_final))[None]
```

```python
import functools
import math

import numpy as np
import jax
import jax.numpy as jnp
from jax import lax
from jax.experimental import pallas as pl
from jax.experimental.pallas import tpu as pltpu

F32 = jnp.float32
BF16 = jnp.bfloat16

GRID_W = 64
HEAD_DIM = 64
NA_WIN_H = 8
NA_WIN_W = 16
S5_GROUP = 16
S5_STATE = 64
SW_WINDOW = 128
ROPE_BASE = 10000.0
PEER_HEADS = 8
PEER_NKEYS = 128
PEER_QDIM = 256
PEER_TOPK = 16
EPS = 1e-6
NEG_INF = -1e30

LANES = 128
SUBLANES = 8
VMEM_LIMIT_BYTES = 56 * 1024 * 1024

ROW_BLOCK = 256
NA_ROWS = 4
NA_KROWS = NA_ROWS + NA_WIN_H - 1
NA_LANES = 256
SW_BLOCK = 128
SW_STEP_BLOCKS = 4
S5_CHUNK = 16
S5_LG = LANES // S5_GROUP
PEER_ROUTE_BLOCK = 256
PEER_ROUTE_HEADS = 4
PEER_TOK_BLOCK = 512
PEER_EXP_BLOCK = 2048
PEER_SUB = 512
PEER_ACC_CHUNK = 1024


def _cparams(sem):
    return pltpu.CompilerParams(dimension_semantics=sem, vmem_limit_bytes=VMEM_LIMIT_BYTES)


def _dot(a, b):
    return jnp.dot(a, b, preferred_element_type=F32)


def _dot_nt(a, b):
    return lax.dot_general(a, b, (((1,), (1,)), ((), ())), preferred_element_type=F32)


def _gelu(x):
    return 0.5 * x * (1.0 + lax.erf(x * (1.0 / math.sqrt(2.0))))


def _mod_kernel(cc_ref, w_ref, b_ref, o_ref):
    a = cc_ref[...]
    a = a * jax.nn.sigmoid(a)
    o_ref[0] = _dot(a.astype(BF16), w_ref[0].astype(BF16)) + b_ref[0]


def _mod_vectors(cc, w_mod, b_mod):
    depth, d, n6 = w_mod.shape
    tn = 1024
    return pl.pallas_call(
        _mod_kernel,
        grid=(depth, n6 // tn),
        in_specs=[pl.BlockSpec((SUBLANES, d), lambda l, j: (0, 0)),
                  pl.BlockSpec((1, d, tn), lambda l, j: (l, 0, j)),
                  pl.BlockSpec((1, 1, tn), lambda l, j: (l, 0, j))],
        out_specs=pl.BlockSpec((1, SUBLANES, tn), lambda l, j: (l, 0, j)),
        out_shape=jax.ShapeDtypeStruct((depth, SUBLANES, n6), F32),
        compiler_params=_cparams(("arbitrary", "arbitrary")),
        name="mod_vectors",
    )(cc, w_mod, b_mod.reshape(depth, 1, n6))


def _norm_mod(x, g, shift, scale):
    y = x * lax.rsqrt(jnp.mean(x * x, axis=-1, keepdims=True) + EPS)
    return (y * g) * (1.0 + scale) + shift


def _rope(x, cos, sin, lane_lo):
    up = pltpu.roll(x, LANES - 16, 1)
    dn = pltpu.roll(x, 16, 1)
    return x * cos + jnp.where(lane_lo, up, dn) * sin


def _inproj_kernel(x_ref, g_ref, sh_ref, sc_ref, w_ref, *rest, bw, kvw, d_model, rope):
    if rope:
        cos_ref, sin_ref = rest[:2]
        rest = rest[2:]
    qa_ref, ka_ref, va_ref, ub_ref, qs_ref, ks_ref, vs_ref, ga_ref, gb_ref, gs_ref = rest
    h = _norm_mod(x_ref[...], g_ref[...], sh_ref[...], sc_ref[...]).astype(BF16)

    def proj(c0, width):
        return _dot(h, w_ref[:, c0:c0 + width])

    c = 0
    qa_ref[...] = proj(c, bw).astype(BF16); c += bw
    ka_ref[...] = proj(c, bw).astype(BF16); c += bw
    va_ref[...] = proj(c, bw).astype(BF16); c += bw
    ub = proj(c, bw); c += bw
    for t in range(bw // LANES):
        ub_ref[t] = ub[:, t * LANES:(t + 1) * LANES]
    qs = proj(c, bw); c += bw
    ks = proj(c, kvw); c += kvw
    vs_ref[...] = proj(c, kvw).astype(BF16); c += kvw
    if rope:
        cos = cos_ref[...]
        sin = sin_ref[...]
        lane = lax.broadcasted_iota(jnp.int32, cos.shape, 1)
        lane_lo = (lane % 32) < 16
        for p in range(bw // LANES):
            sl = slice(p * LANES, (p + 1) * LANES)
            qs_ref[:, sl] = _rope(qs[:, sl], cos, sin, lane_lo).astype(BF16)
        for p in range(kvw // LANES):
            sl = slice(p * LANES, (p + 1) * LANES)
            ks_ref[:, sl] = _rope(ks[:, sl], cos, sin, lane_lo).astype(BF16)
    else:
        qs_ref[...] = qs.astype(BF16)
        ks_ref[...] = ks.astype(BF16)
    ga_ref[...] = jax.nn.sigmoid(proj(c, d_model)).astype(BF16); c += d_model
    gb_ref[...] = jax.nn.sigmoid(proj(c, d_model)).astype(BF16); c += d_model
    gs_ref[...] = jax.nn.sigmoid(proj(c, d_model)).astype(BF16)


def _inproj(x, g, shift, scale, w_in, rope_tabs):
    n, d = x.shape
    bw = d // 2
    kvw = bw // 4
    tm = min(ROW_BLOCK, n)
    rope = rope_tabs is not None
    row = lambda i: (i, 0)
    fixed = lambda i: (0, 0)
    in_specs = [pl.BlockSpec((tm, d), row), pl.BlockSpec((1, d), fixed), pl.BlockSpec((1, d), fixed),
                pl.BlockSpec((1, d), fixed), pl.BlockSpec(w_in.shape, fixed)]
    args = [x, g, shift, scale, w_in]
    if rope:
        in_specs += [pl.BlockSpec((tm, LANES), row)] * 2
        args += list(rope_tabs)
    widths = [bw, bw, bw, bw, bw, kvw, kvw, d, d, d]
    dtypes = [BF16, BF16, BF16, F32, BF16, BF16, BF16, BF16, BF16, BF16]
    s5_slot = 3
    return pl.pallas_call(
        functools.partial(_inproj_kernel, bw=bw, kvw=kvw, d_model=d, rope=rope),
        grid=(n // tm,),
        in_specs=in_specs,
        out_specs=[pl.BlockSpec((bw // LANES, tm, LANES), lambda i: (0, i, 0)) if k == s5_slot
                   else pl.BlockSpec((tm, w), row) for k, w in enumerate(widths)],
        out_shape=[jax.ShapeDtypeStruct((bw // LANES, n, LANES) if k == s5_slot else (n, w), dt)
                   for k, (w, dt) in enumerate(zip(widths, dtypes))],
        compiler_params=_cparams(("arbitrary",)),
        name="inproj",
    )(*args)


def _rope_tables(seq):
    t = jnp.arange(seq)
    half = HEAD_DIM // 4
    inv = ROPE_BASE ** (-jnp.arange(half, dtype=F32) / half)
    sign = jnp.concatenate([-jnp.ones((half,), F32), jnp.ones((half,), F32)])

    def axis_tabs(pos):
        ang = pos.astype(F32)[:, None] * inv[None, :]
        c = jnp.cos(ang)
        s = jnp.sin(ang)
        return jnp.concatenate([c, c], -1), jnp.concatenate([s, s], -1) * sign

    cr, sr = axis_tabs(t // GRID_W)
    cc, sc = axis_tabs(t % GRID_W)
    cos = jnp.concatenate([cr, cc], -1)
    sin = jnp.concatenate([sr, sc], -1)
    return jnp.tile(cos, (1, LANES // HEAD_DIM)), jnp.tile(sin, (1, LANES // HEAD_DIM))


def _na_patterns(rows):
    kh = NA_WIN_H
    nb = rows // NA_ROWS
    kr0s, sigs = [], []
    for b in range(nb):
        r_lo = b * NA_ROWS
        kr0 = int(np.clip(r_lo - kh // 2, 0, rows - NA_KROWS))
        sig = tuple((int(np.clip(r - kh // 2, 0, rows - kh)) - kr0, r - kr0)
                    for r in range(r_lo, r_lo + NA_ROWS))
        kr0s.append(kr0)
        sigs.append(sig)
    uniq = sorted(set(sigs))
    ids = np.array([uniq.index(s) for s in sigs], np.int32)
    return np.array(kr0s, np.int32), ids, uniq


def _na_bias(rpb, uniq, n_ctx):
    kw = NA_WIN_W
    cols = np.arange(GRID_W)
    col_start = np.clip(cols - kw // 2, 0, GRID_W - kw)
    n_pat = len(uniq)
    kk = np.arange(NA_KROWS)
    col_sel = (cols[None, None, :] - cols[None, :, None] + kw - 1
               == np.arange(2 * kw - 1)[:, None, None])
    col_ok = ((cols[None, :] >= col_start[:, None]) & (cols[None, :] < col_start[:, None] + kw))
    row_sel = np.zeros((n_pat, NA_ROWS, NA_KROWS, 2 * NA_WIN_H - 1), bool)
    row_ok = np.zeros((n_pat, NA_ROWS, NA_KROWS), bool)
    for p, sig in enumerate(uniq):
        for rq, (r0_rel, r_rel) in enumerate(sig):
            row_ok[p, rq] = (kk >= r0_rel) & (kk < r0_rel + NA_WIN_H)
            row_sel[p, rq] = (kk[:, None] - r_rel + NA_WIN_H - 1) == np.arange(2 * NA_WIN_H - 1)[None, :]
    hp = lax.Precision.HIGHEST
    by_col = jnp.einsum('hrv,vcj->hrcj', rpb.astype(F32), jnp.asarray(col_sel, F32), precision=hp)
    bias = jnp.einsum('pqkr,hrcj->phqckj', jnp.asarray(row_sel, F32), by_col, precision=hp)
    ok = row_ok[:, None, :, None, :, None] & col_ok[None, None, None, :, None, :]
    bias = jnp.where(jnp.asarray(ok), bias, NEG_INF)
    bias = bias.reshape(n_pat, rpb.shape[0], NA_ROWS * GRID_W, NA_KROWS * GRID_W)
    return jnp.concatenate([bias, jnp.zeros(bias.shape[:3] + (n_ctx,), F32)], axis=-1)


def _na_kernel(kr0_ref, pat_ref, q_ref, k_ref, v_ref, kc_ref, vc_ref, bias_ref, o_ref):
    b = pl.program_id(1)
    start = pl.multiple_of(kr0_ref[b] * GRID_W, GRID_W)
    nk = NA_KROWS * GRID_W
    scale = HEAD_DIM ** -0.5
    one = jnp.ones((), BF16)
    hp = LANES // HEAD_DIM
    for lt in range(q_ref.shape[1] // LANES):
        lanes = slice(lt * LANES, (lt + 1) * LANES)
        q = q_ref[:, lanes]
        k_all = jnp.concatenate([k_ref[pl.ds(start, nk), lanes], kc_ref[:, lanes]], axis=0)
        v_all = jnp.concatenate([v_ref[pl.ds(start, nk), lanes], vc_ref[:, lanes]], axis=0)
        lane = lax.broadcasted_iota(jnp.int32, v_all.shape, 1)
        ext = []
        for h in range(hp):
            sl = slice(h * HEAD_DIM, (h + 1) * HEAD_DIM)
            own = (lane >= h * HEAD_DIM) & (lane < (h + 1) * HEAD_DIM)
            s = _dot_nt(q[:, sl], k_all[:, sl]) * scale + bias_ref[0, lt * hp + h]
            p = jnp.exp(s - jnp.max(s, axis=-1, keepdims=True)).astype(BF16)
            ext.append(_dot(p, jnp.where(own, v_all, one)))
        out_lane = lax.broadcasted_iota(jnp.int32, ext[0].shape, 1)
        low = out_lane < HEAD_DIM
        num = jnp.where(low, ext[0], ext[1])
        den = pltpu.roll(jnp.where(low, ext[1], ext[0]), HEAD_DIM, 1)
        o_ref[:, lanes] = (num / den).astype(o_ref.dtype)


def _na_attention(q, k, v, kc, vc, bias):
    s, bw = q.shape
    l = kc.shape[0]
    rows = s // GRID_W
    kr0s, ids, _ = _na_patterns(rows)
    tq = NA_ROWS * GRID_W
    nk = NA_KROWS * GRID_W
    hp = NA_LANES // HEAD_DIM
    grid_spec = pltpu.PrefetchScalarGridSpec(
        num_scalar_prefetch=2,
        grid=(bw // NA_LANES, rows // NA_ROWS),
        in_specs=[pl.BlockSpec((tq, NA_LANES), lambda p, b, kr, pt: (b, p)),
                  pl.BlockSpec((s, NA_LANES), lambda p, b, kr, pt: (0, p)),
                  pl.BlockSpec((s, NA_LANES), lambda p, b, kr, pt: (0, p)),
                  pl.BlockSpec((l, NA_LANES), lambda p, b, kr, pt: (0, p)),
                  pl.BlockSpec((l, NA_LANES), lambda p, b, kr, pt: (0, p)),
                  pl.BlockSpec((1, hp, tq, nk + l), lambda p, b, kr, pt: (pt[b], p, 0, 0))],
        out_specs=pl.BlockSpec((tq, NA_LANES), lambda p, b, kr, pt: (b, p)),
    )
    return pl.pallas_call(
        _na_kernel,
        grid_spec=grid_spec,
        out_shape=jax.ShapeDtypeStruct((s, bw), BF16),
        compiler_params=_cparams(("arbitrary", "arbitrary")),
        name="na_attention",
    )(jnp.asarray(kr0s), jnp.asarray(ids), q, k, v, kc, vc, bias)


def _ctx_attn_kernel(*refs, n_heads, n_rep, has_sink):
    if has_sink:
        sink_ref, q_ref, k_ref, v_ref, o_ref = refs
    else:
        q_ref, k_ref, v_ref, o_ref = refs
    scale = HEAD_DIM ** -0.5
    q = q_ref[...]
    k = k_ref[...]
    v = v_ref[...]
    outs = []
    for h in range(n_heads):
        g = h // n_rep
        sl = slice(h * HEAD_DIM, (h + 1) * HEAD_DIM)
        gl = slice(g * HEAD_DIM, (g + 1) * HEAD_DIM)
        s = _dot_nt(q[:, sl], k[:, gl]) * scale
        m = jnp.max(s, axis=-1, keepdims=True)
        if has_sink:
            m = jnp.maximum(m, sink_ref[h])
        p = jnp.exp(s - m)
        den = jnp.sum(p, axis=-1, keepdims=True)
        if has_sink:
            den = den + jnp.exp(sink_ref[h] - m)
        outs.append(_dot(p.astype(BF16), v[:, gl]) / den)
    o_ref[...] = jnp.concatenate(outs, axis=-1).astype(o_ref.dtype)


def _ctx_attention(q, k, v, sink):
    l, bw = q.shape
    n_heads = bw // HEAD_DIM
    n_rep = n_heads // (k.shape[1] // HEAD_DIM)
    has_sink = sink is not None
    full = lambda a: pl.BlockSpec(a.shape, lambda i: (0, 0))
    in_specs = [full(q), full(k), full(v)]
    args = [q, k, v]
    if has_sink:
        in_specs = [pl.BlockSpec(memory_space=pltpu.SMEM)] + in_specs
        args = [sink] + args
    return pl.pallas_call(
        functools.partial(_ctx_attn_kernel, n_heads=n_heads, n_rep=n_rep, has_sink=has_sink),
        grid=(1,),
        in_specs=in_specs,
        out_specs=pl.BlockSpec((l, bw), lambda i: (0, 0)),
        out_shape=jax.ShapeDtypeStruct((l, bw), BF16),
        compiler_params=_cparams(("arbitrary",)),
        name="ctx_attention",
    )(*args)


def _swa_kernel(sink_ref, q_ref, k_ref, v_ref, kc_ref, vc_ref, o_ref, *, seq, n_kv, n_rep):
    for qb in range(SW_STEP_BLOCKS):
        rows = slice(qb * SW_BLOCK, (qb + 1) * SW_BLOCK)
        _swa_block(pl.program_id(0) * SW_STEP_BLOCKS + qb, sink_ref, q_ref[rows, :], k_ref, v_ref,
                   kc_ref, vc_ref, o_ref.at[rows, :], seq=seq, n_kv=n_kv, n_rep=n_rep)


def _swa_block(n, sink_ref, q, k_ref, v_ref, kc_ref, vc_ref, o_ref, *, seq, n_kv, n_rep):
    nk = 3 * SW_BLOCK
    scale = HEAD_DIM ** -0.5
    ws = jnp.clip((n - 1) * SW_BLOCK, 0, seq - nk)
    ws = pl.multiple_of(ws, SW_BLOCK)
    n_all = nk + kc_ref.shape[0]
    k_all = jnp.concatenate([k_ref[pl.ds(ws, nk), :], kc_ref[...]], axis=0)
    v_all = jnp.concatenate([v_ref[pl.ds(ws, nk), :], vc_ref[...]], axis=0)
    qpos = n * SW_BLOCK + lax.broadcasted_iota(jnp.int32, (SW_BLOCK, n_all), 0)
    col = lax.broadcasted_iota(jnp.int32, (SW_BLOCK, n_all), 1)
    visible = (col >= nk) | (jnp.abs(ws + col - qpos) <= SW_WINDOW)
    maskb = jnp.where(visible, 0.0, NEG_INF).astype(F32)
    lane = lax.broadcasted_iota(jnp.int32, v_all.shape, 1)
    one = jnp.ones((), BF16)
    outs = [None] * (n_kv * n_rep)
    for g in range(n_kv):
        gl = slice(g * HEAD_DIM, (g + 1) * HEAD_DIM)
        own = (lane >= g * HEAD_DIM) & (lane < (g + 1) * HEAD_DIM)
        qg = jnp.concatenate([q[:, (g * n_rep + r) * HEAD_DIM:(g * n_rep + r + 1) * HEAD_DIM]
                              for r in range(n_rep)], axis=0)
        s = _dot_nt(qg, k_all[:, gl]) * scale
        s = (s.reshape(n_rep, SW_BLOCK, n_all) + maskb[None]).reshape(n_rep * SW_BLOCK, n_all)
        sink = jnp.concatenate([jnp.full((SW_BLOCK, 1), sink_ref[g * n_rep + r], F32)
                                for r in range(n_rep)], axis=0)
        m = jnp.maximum(jnp.max(s, axis=-1, keepdims=True), sink)
        p = jnp.exp(s - m).astype(BF16)
        ext = _dot(p, jnp.where(own, v_all, one))
        den = pltpu.roll(ext, HEAD_DIM, 1) + jnp.exp(sink - m)
        o = ext / den
        for r in range(n_rep):
            outs[g * n_rep + r] = o[r * SW_BLOCK:(r + 1) * SW_BLOCK, gl]
    o_ref[...] = jnp.concatenate(outs, axis=-1).astype(o_ref.dtype)


def _sw_attention(q, k, v, kc, vc, sink):
    s, bw = q.shape
    l = kc.shape[0]
    kvw = k.shape[1]
    n_kv = kvw // HEAD_DIM
    n_rep = (bw // HEAD_DIM) // n_kv
    assert kvw == LANES
    return pl.pallas_call(
        functools.partial(_swa_kernel, seq=s, n_kv=n_kv, n_rep=n_rep),
        grid=(s // (SW_BLOCK * SW_STEP_BLOCKS),),
        in_specs=[pl.BlockSpec(memory_space=pltpu.SMEM),
                  pl.BlockSpec((SW_BLOCK * SW_STEP_BLOCKS, bw), lambda n: (n, 0)),
                  pl.BlockSpec((s, kvw), lambda n: (0, 0)),
                  pl.BlockSpec((s, kvw), lambda n: (0, 0)),
                  pl.BlockSpec((l, kvw), lambda n: (0, 0)),
                  pl.BlockSpec((l, kvw), lambda n: (0, 0))],
        out_specs=pl.BlockSpec((SW_BLOCK * SW_STEP_BLOCKS, bw), lambda n: (n, 0)),
        out_shape=jax.ShapeDtypeStruct((s, bw), BF16),
        compiler_params=_cparams(("arbitrary",)),
        name="sw_attention",
    )(sink, q, k, v, kc, vc)


def _s5_prepare(lam_re, lam_im, b_re, b_im, c_re, c_im, log_step):
    tc = S5_CHUNK
    lam = lax.complex(lam_re.astype(F32), lam_im.astype(F32))
    dt = jnp.exp(log_step.astype(F32))[..., None]
    lam_dt = lam * dt
    lam_bar = jnp.exp(lam_dt)
    b_bar = ((lam_bar - 1.0) / lam)[..., None] * lax.complex(b_re.astype(F32), b_im.astype(F32))
    c_mat = lax.complex(c_re.astype(F32), c_im.astype(F32))
    kk = jnp.arange(tc + 1, dtype=F32)
    pw = jnp.exp(lam_dt[None] * kk[:, None, None, None])
    n_g, n_p, n_h = b_bar.shape[1:]

    bt = jnp.transpose(b_bar, (1, 0, 3, 2))
    bt = jnp.stack([bt.real, bt.imag], axis=2)
    cw = c_mat[None] * pw[:tc, :, :, None, :]
    cw = jnp.stack([cw[:, 0], cw[::-1, 1]], axis=1)
    cwt = jnp.transpose(cw, (2, 1, 4, 0, 3)).reshape(n_g, 2, n_p, tc * n_h)
    cwt = jnp.stack([cwt.real, cwt.imag], axis=2)
    t_i = np.arange(tc)

    def cat_ri(z):
        return jnp.concatenate([z.real, z.imag], axis=-1)

    w_f = pw[tc - 1 - t_i, 0][:, :, None, :] * jnp.transpose(b_bar[0], (0, 2, 1))[None]
    w_r = pw[t_i, 1][:, :, None, :] * jnp.transpose(b_bar[1], (0, 2, 1))[None]
    w_f = jnp.transpose(cat_ri(w_f), (1, 0, 2, 3)).reshape(n_g, tc * n_h, 2 * n_p)
    w_r = jnp.transpose(cat_ri(w_r), (1, 0, 2, 3)).reshape(n_g, tc * n_h, 2 * n_p)
    w_cat = jnp.concatenate([w_f, w_r], axis=-1).astype(BF16)

    def v_of(c_dir, pw_sel):
        cp = c_dir[None] * pw_sel[:, :, None, :]
        v = jnp.concatenate([cp.real, -cp.imag], axis=-1)
        return jnp.transpose(v, (1, 3, 0, 2)).reshape(n_g, 2 * n_p, tc * n_h)
    v_cat = jnp.concatenate([v_of(c_mat[0], pw[t_i + 1, 0]), v_of(c_mat[1], pw[tc - t_i, 1])],
                            axis=1).astype(BF16)

    def coef(k):
        a = jnp.exp(lam_dt * (tc * k))
        return jnp.stack([jnp.concatenate([a.real, a.real], -1),
                          jnp.concatenate([-a.imag, a.imag], -1)], axis=2)
    a_log = jnp.stack([coef(1.0), coef(2.0), coef(4.0)], axis=2)
    a_car = jnp.stack([coef(float(j)) for j in range(1, SUBLANES + 1)], axis=3)
    a_car = jnp.stack([a_car[0], a_car[1][:, :, ::-1]], axis=0)
    return _s5_expand(bt, cwt, w_cat, v_cat, jnp.transpose(a_log, (1, 0, 2, 3, 4)),
                      jnp.transpose(a_car, (1, 0, 2, 3, 4)))


def _s5_expand(bt, cwt, w_cat, v_cat, a_log, a_car):
    n_g = bt.shape[0]
    n_t = n_g // S5_LG
    n_state = 2 * S5_STATE
    by_tile = lambda a: a.reshape((n_t, S5_LG) + a.shape[1:])

    def lanes_of(a):
        a = jnp.moveaxis(by_tile(a), 1, -2)
        return a.reshape(a.shape[:-2] + (S5_LG * n_state,))
    return by_tile(bt), by_tile(cwt), by_tile(w_cat), by_tile(v_cat), lanes_of(a_log), lanes_of(a_car)


def _s5_lane_perm():
    r = np.arange(S5_CHUNK * LANES)
    t, a, h = r // LANES, (r % LANES) // S5_GROUP, r % S5_GROUP
    dest = a * (S5_CHUNK * S5_GROUP) + t * S5_GROUP + h
    return (jnp.asarray(dest)[:, None] == jnp.arange(S5_CHUNK * LANES)[None, :]).astype(BF16)


def _chunk_rows(tok_ref):
    n_rows = tok_ref.shape[1] // S5_CHUNK
    return jnp.concatenate([tok_ref[0, pl.ds(t, n_rows, stride=S5_CHUNK), :] for t in range(S5_CHUNK)], axis=-1)


def _s5_in_kernel(u_ref, perm_ref, bt_ref, cwt_ref, w_ref, yp_ref, z_ref, m_scr):
    width = S5_CHUNK * S5_GROUP
    n_state = 2 * S5_STATE
    half = S5_LG * n_state

    @pl.when(pl.program_id(1) == 0)
    def _():
        lane = lax.broadcasted_iota(jnp.int32, (S5_GROUP, width), 1)
        def lag_kernels(a, direction):
            hp = lax.Precision.HIGHEST
            return (jnp.dot(bt_ref[0, a, direction, 0], cwt_ref[0, a, direction, 0], precision=hp,
                            preferred_element_type=F32)
                    - jnp.dot(bt_ref[0, a, direction, 1], cwt_ref[0, a, direction, 1], precision=hp,
                              preferred_element_type=F32))

        for a in range(S5_LG):
            k_fwd = lag_kernels(a, 0)
            k_rev = lag_kernels(a, 1)
            for tau in range(S5_CHUNK):
                right = tau * S5_GROUP
                left = (S5_CHUNK - 1 - tau) * S5_GROUP
                blk = (jnp.where(lane >= right, pltpu.roll(k_fwd, right, 1), 0.0)
                       + jnp.where(lane < width - left, pltpu.roll(k_rev, (width - left) % width, 1), 0.0))
                m_scr[a, tau * S5_GROUP:(tau + 1) * S5_GROUP, 0:width] = blk.astype(BF16)
            m_scr[a, :, width:] = w_ref[0, a]

    up = _dot(_chunk_rows(u_ref).astype(BF16), perm_ref[...]).astype(BF16)
    for a in range(S5_LG):
        zz = _dot(up[:, a * width:(a + 1) * width], m_scr[a])
        yp_ref[0, :, a * width:(a + 1) * width] = zz[:, :width]
        z_ref[0, :, a * n_state:(a + 1) * n_state] = zz[:, width:width + n_state]
        z_ref[0, :, half + a * n_state:half + (a + 1) * n_state] = zz[:, width + n_state:]


def _s5_scan_kernel(z_ref, alog_ref, acar_ref, e_ref, *, nc, nc_ctx):
    n_state = 2 * S5_STATE
    half = z_ref.shape[2] // 2
    nblk = nc // SUBLANES
    nblk_ctx = nc_ctx // SUBLANES
    row = lax.broadcasted_iota(jnp.int32, (SUBLANES, n_state), 0)

    def cmul(a1, a2, s):
        return a1 * s + a2 * pltpu.roll(s, S5_STATE, 1)

    def step(t, carry):
        blk = (t, jnp.where(t < nblk_ctx, nblk_ctx - 1 - t, nblk + nblk_ctx - 1 - t))
        new = []
        for d in range(2):
            r0 = pl.multiple_of(blk[d] * SUBLANES, SUBLANES)
            for j in range(half // n_state):
                tile = slice(j * n_state, (j + 1) * n_state)
                lanes = slice(d * half + j * n_state, d * half + (j + 1) * n_state)
                z = z_ref[0, pl.ds(r0, SUBLANES), lanes]
                c = carry[len(new)]
                for i, sh in enumerate((1, 2, 4)):
                    if d == 0:
                        zs = jnp.where(row >= sh, pltpu.roll(z, sh, 0), 0.0)
                    else:
                        zs = jnp.where(row < SUBLANES - sh, pltpu.roll(z, SUBLANES - sh, 0), 0.0)
                    z = z + cmul(alog_ref[0, d, i, 0:1, tile], alog_ref[0, d, i, 1:2, tile], zs)
                cb = jnp.broadcast_to(c, (SUBLANES, n_state))
                s = z + cmul(acar_ref[0, d, 0, :, tile], acar_ref[0, d, 1, :, tile], cb)
                if d == 0:
                    e = jnp.where(row >= 1, pltpu.roll(s, 1, 0), cb)
                    c_new = s[SUBLANES - 1:SUBLANES, :]
                else:
                    e = jnp.where(row < SUBLANES - 1, pltpu.roll(s, SUBLANES - 1, 0), cb)
                    c_new = s[0:1, :]
                e_ref[0, pl.ds(r0, SUBLANES), lanes] = e
                new.append(c_new)
        return tuple(new)

    init = tuple(jnp.zeros((1, n_state), F32) for _ in range(2 * (half // n_state)))
    lax.fori_loop(0, nblk, step, init, unroll=2)


def _s5_out_kernel(e_ref, yp_ref, u_ref, d_ref, perm_ref, v_ref, y_ref):
    width = S5_CHUNK * S5_GROUP
    n_state = 2 * S5_STATE
    half = S5_LG * n_state
    parts = []
    for a in range(S5_LG):
        e_a = jnp.concatenate([e_ref[0, :, a * n_state:(a + 1) * n_state],
                               e_ref[0, :, half + a * n_state:half + (a + 1) * n_state]], axis=-1)
        parts.append(yp_ref[0, :, a * width:(a + 1) * width] + _dot(e_a.astype(BF16), v_ref[0, a]))
    y = jnp.concatenate(parts, axis=-1)
    hi = y.astype(BF16)
    lo = (y - hi.astype(F32)).astype(BF16)
    perm = perm_ref[...]
    y = _dot_nt(hi, perm) + _dot_nt(lo, perm) + _chunk_rows(u_ref) * d_ref[0]
    n_rows = y.shape[0]
    for t in range(S5_CHUNK):
        y_ref[0, pl.ds(t, n_rows, stride=S5_CHUNK), :] = y[:, t * LANES:(t + 1) * LANES]


def _s5_rows(nc):
    return max(r for r in range(2 * SUBLANES, 257, 2 * SUBLANES) if nc % r == 0)


def _s5_mixer_pre_glu(u_ctx, u_lat, prep, d_skip):
    bt, cwt, w_cat, v_cat, a_log, a_car = prep
    n_t, l, _ = u_ctx.shape
    s = u_lat.shape[1]
    nc, nc_ctx = (l + s) // S5_CHUNK, l // S5_CHUNK
    width = S5_CHUNK * LANES
    n_z = 2 * S5_LG * 2 * S5_STATE
    u = jnp.concatenate([u_ctx, u_lat], axis=1)
    d_rows = jnp.tile(d_skip.astype(F32).reshape(n_t, 1, LANES), (1, 1, S5_CHUNK))
    perm = _s5_lane_perm()
    rb = _s5_rows(nc)
    rows = lambda w: pl.BlockSpec((1, rb, w), lambda g, r: (g, r, 0))
    per_tile = lambda a: pl.BlockSpec((1,) + a.shape[1:], lambda g, *_: (g,) + (0,) * (a.ndim - 1))
    whole = lambda a: pl.BlockSpec(a.shape, lambda g, r: (0,) * a.ndim)
    tokens = pl.BlockSpec((1, rb * S5_CHUNK, LANES), lambda g, r: (g, r, 0))
    width_g = S5_CHUNK * S5_GROUP
    yp, z = pl.pallas_call(
        _s5_in_kernel,
        grid=(n_t, nc // rb),
        in_specs=[tokens, whole(perm), per_tile(bt), per_tile(cwt), per_tile(w_cat)],
        out_specs=[rows(width), rows(n_z)],
        out_shape=[jax.ShapeDtypeStruct((n_t, nc, width), F32), jax.ShapeDtypeStruct((n_t, nc, n_z), F32)],
        scratch_shapes=[pltpu.VMEM((S5_LG, width_g, width_g + 4 * S5_STATE), BF16)],
        compiler_params=_cparams(("arbitrary", "arbitrary")),
        name="s5_in",
    )(u, perm, bt, cwt, w_cat)
    e = pl.pallas_call(
        functools.partial(_s5_scan_kernel, nc=nc, nc_ctx=nc_ctx),
        grid=(n_t,),
        in_specs=[per_tile(z), per_tile(a_log), per_tile(a_car)],
        out_specs=per_tile(z),
        out_shape=jax.ShapeDtypeStruct(z.shape, F32),
        compiler_params=_cparams(("arbitrary",)),
        name="s5_scan",
    )(z, a_log, a_car)
    y = pl.pallas_call(
        _s5_out_kernel,
        grid=(n_t, nc // rb),
        in_specs=[rows(n_z), rows(width), tokens, per_tile(d_rows), whole(perm), per_tile(v_cat)],
        out_specs=tokens,
        out_shape=jax.ShapeDtypeStruct((n_t, l + s, LANES), F32),
        compiler_params=_cparams(("arbitrary", "arbitrary")),
        name="s5_out",
    )(e, yp, u, d_rows, perm, v_cat)
    return y


def _merge_kernel(x_ref, ya_ref, y5_ref, ys_ref, ga_ref, gb_ref, gs_ref, wglu_ref, wb_ref, wo_ref,
                  gt_ref, g_ref, sh_ref, sc_ref, xo_ref, h_ref, ht_ref):
    y5 = jnp.concatenate([y5_ref[t] for t in range(y5_ref.shape[0])], axis=-1)
    z = _gelu(y5)
    yb = z * jax.nn.sigmoid(_dot(z.astype(BF16), wglu_ref[...]))
    m = (ga_ref[...].astype(F32) * _dot(ya_ref[...], wb_ref[0])
         + gb_ref[...].astype(F32) * _dot(yb.astype(BF16), wb_ref[1])
         + gs_ref[...].astype(F32) * _dot(ys_ref[...], wb_ref[2]))
    x = x_ref[...] + gt_ref[...] * _dot(m.astype(BF16), wo_ref[...])
    xo_ref[...] = x
    h = _norm_mod(x, g_ref[...], sh_ref[...], sc_ref[...])
    h_ref[...] = h.astype(BF16)
    ht_ref[...] = h.T.astype(BF16)


def _merge(x, ya, y5, y5_row0, ys, ga, gb, gs, w_glu, w_branch, w_out, gate, g, shift, scale):
    n, d = x.shape
    bw = ya.shape[1]
    tm = min(ROW_BLOCK, n)
    assert y5_row0 % tm == 0
    row = lambda w: pl.BlockSpec((tm, w), lambda i: (i, 0))
    full = lambda a: pl.BlockSpec(a.shape, lambda i: (0,) * a.ndim)
    vec = pl.BlockSpec((1, d), lambda i: (0, 0))
    y5_rows = pl.BlockSpec((y5.shape[0], tm, LANES), lambda i: (0, i + y5_row0 // tm, 0))
    return pl.pallas_call(
        _merge_kernel,
        grid=(n // tm,),
        in_specs=[row(d), row(bw), y5_rows, row(bw), row(d), row(d), row(d),
                  full(w_glu), full(w_branch), full(w_out), vec, vec, vec, vec],
        out_specs=[row(d), row(d), pl.BlockSpec((d, tm), lambda i: (0, i))],
        out_shape=[jax.ShapeDtypeStruct((n, d), F32), jax.ShapeDtypeStruct((n, d), BF16),
                   jax.ShapeDtypeStruct((d, n), BF16)],
        compiler_params=_cparams(("arbitrary",)),
        name="merge",
    )(x, ya, y5, ys, ga, gb, gs, w_glu, w_branch, w_out, gate, g, shift, scale)


def _knock_out_16(s, order, exact, want_rank=True):
    rank = jnp.full(s.shape, float(PEER_TOPK), F32) if want_rank else None
    live = s
    vals = []
    for r in range(PEER_TOPK):
        m = jnp.max(live, axis=0, keepdims=True)
        hit = live == m
        if exact:
            first = jnp.min(jnp.where(hit, order, float(PEER_TOPK * PEER_NKEYS)), axis=0, keepdims=True)
            hit = order == first
        if want_rank:
            rank = jnp.where(hit, float(r), rank)
        live = jnp.where(hit, -jnp.inf, live)
        vals.append(m)
    out = (rank < float(PEER_TOPK)) if want_rank else (live != s)
    n_out = jnp.sum(jnp.where(out, 1.0, 0.0), axis=0, keepdims=True)
    return rank, jnp.concatenate(vals, axis=0), n_out


def _bf16_pair_word(x):
    hi = pltpu.bitcast(x.astype(BF16).astype(F32), jnp.uint32)
    return hi | (hi >> 16)


def _pair_tiles():
    tiles = [(0, 0, 8), (0, 8, 8)]
    for a in range(1, 8):
        tiles.append((a, 0, PEER_TOPK // (a + 1)))
    return tiles


def _route_kernel(h_ref, wq_ref, k1_ref, k2_ref, cnt_ref, c1_ref, rk_ref, e2_ref, q_scr):
    tb = h_ref.shape[0]
    q_scr[...] = _dot(h_ref[...], wq_ref[...]).astype(BF16)
    half = PEER_QDIM // 2
    iota = lax.broadcasted_iota(jnp.int32, (PEER_NKEYS, tb), 0).astype(F32)
    row8 = lax.broadcasted_iota(jnp.int32, (SUBLANES, tb), 0).astype(F32)
    k1 = k1_ref[...]
    k2 = k2_ref[...]

    def emit(hd, s1, s2, exact):
        rank1, v1, n1 = _knock_out_16(s1, iota, exact, want_rank=exact)
        rank2, v2, n2 = _knock_out_16(s2, iota, exact)
        e1v = jnp.exp(v1 - v1[0:1])
        e2v = jnp.exp(v2 - v2[0:1])
        tiles, flats, gates = [], [], []
        for a, b0, nv in _pair_tiles():
            c = v1[a:a + 1] + v2[b0:b0 + SUBLANES]
            tiles.append(jnp.where(row8 < nv, c, -jnp.inf))
            flats.append(a * PEER_TOPK + b0 + row8)
            gates.append(e1v[a:a + 1] * e2v[b0:b0 + SUBLANES])
        tiles.append(v1[SUBLANES:] + v2[0:1])
        flats.append((row8 + SUBLANES) * PEER_TOPK)
        gates.append(e1v[SUBLANES:] * e2v[0:1])
        cand = jnp.concatenate(tiles, axis=0)
        flat = jnp.concatenate(flats, axis=0)
        gate = jnp.concatenate(gates, axis=0)
        rank_c, _, n_c = _knock_out_16(cand, flat, exact)
        self_ = jnp.where(rank_c < float(PEER_TOPK), 1.0, 0.0)
        z = jnp.sum(self_ * gate, axis=0, keepdims=True)
        cnt = [self_[0:8].sum(axis=0, keepdims=True) + self_[8:16].sum(axis=0, keepdims=True)]
        for t in range(2, 9):
            cnt.append(self_[t * SUBLANES:(t + 1) * SUBLANES].sum(axis=0, keepdims=True))
        cnt = jnp.concatenate(cnt + [self_[9 * SUBLANES:]], axis=0)
        cnt1 = jnp.zeros((PEER_NKEYS, tb), F32)
        for a in range(PEER_TOPK):
            is_a = (rank1 == float(a)) if exact else (s1 == v1[a:a + 1])
            cnt1 = jnp.where(is_a, cnt[a:a + 1], cnt1)
        cnt_ref[hd] = _bf16_pair_word(cnt1)
        c1_ref[hd] = _bf16_pair_word(jnp.exp(s1 - v1[0:1]) * (0.5 / z))
        rk_ref[hd] = rank2.astype(BF16)
        e2_ref[hd] = jnp.exp(s2 - v2[0:1]).astype(BF16)
        want = float(PEER_TOPK)
        return jnp.where((n1 != want) | (n2 != want) | (n_c != want), 1.0, 0.0)

    def head_group(hg, _):
        scores = []
        for k in range(PEER_ROUTE_HEADS):
            hd = hg * PEER_ROUTE_HEADS + k
            c0 = pl.multiple_of(hd * PEER_QDIM, PEER_QDIM)
            s1 = _dot_nt(k1, q_scr[:, pl.ds(c0, half)])
            s2 = _dot_nt(k2, q_scr[:, pl.ds(c0 + half, half)])
            scores.append((hd, s1, s2))
        tied = [jnp.max(emit(hd, s1, s2, exact=False)) for hd, s1, s2 in scores]
        for (hd, s1, s2), t in zip(scores, tied):
            @pl.when(t > 0.0)
            def _():
                emit(hd, s1, s2, exact=True)
        return 0

    lax.fori_loop(0, PEER_HEADS // PEER_ROUTE_HEADS, head_group, 0)


def _peer_route(h, w_q, k1, k2):
    n, d = h.shape
    tb = min(PEER_ROUTE_BLOCK, n)
    qw = w_q.shape[1]
    out_blk = pl.BlockSpec((PEER_HEADS, PEER_NKEYS, tb), lambda i: (0, 0, i))
    shp = lambda dt: jax.ShapeDtypeStruct((PEER_HEADS, PEER_NKEYS, n), dt)
    return pl.pallas_call(
        _route_kernel,
        grid=(n // tb,),
        in_specs=[pl.BlockSpec((tb, d), lambda i: (i, 0)),
                  pl.BlockSpec((d, qw), lambda i: (0, 0)),
                  pl.BlockSpec(k1.shape, lambda i: (0, 0)),
                  pl.BlockSpec(k2.shape, lambda i: (0, 0))],
        out_specs=[out_blk] * 4,
        out_shape=[shp(jnp.uint32), shp(jnp.uint32), shp(BF16), shp(BF16)],
        scratch_shapes=[pltpu.VMEM((tb, qw), BF16)],
        compiler_params=_cparams(("arbitrary",)),
        name="peer_route",
    )(h, w_q, k1, k2)


def _dense_kernel(ht_ref, x_ref, gt_ref, u_ref, vt_ref, cnt_ref, c1_ref, rk_ref, e2_ref, o_ref,
                  acc_ref, w_scr):
    eb = pl.program_id(1)
    tb = ht_ref.shape[1]
    n_exp = u_ref.shape[0]
    n_sub = n_exp // PEER_SUB
    i_per_sub = PEER_SUB // PEER_NKEYS

    @pl.when(eb == 0)
    def _():
        acc_ref[...] = jnp.zeros_like(acc_ref)

    def scores(sb):
        return _dot(u_ref[sb * PEER_SUB:(sb + 1) * PEER_SUB, :], ht_ref[...])

    pk_rows = 2 * SUBLANES
    n_pk = PEER_NKEYS // pk_rows

    def row_tile(ref, hd, i):
        words = jnp.broadcast_to(ref[hd, pl.ds(i, 1), :], (SUBLANES, tb))
        return pltpu.bitcast(words, BF16)[None]

    a_next = scores(0)
    for sb in range(n_sub):
        a_cur = a_next
        if sb + 1 < n_sub:
            a_next = scores(sb + 1)
        for il in range(i_per_sub):
            i = eb * (n_exp // PEER_NKEYS) + sb * i_per_sub + il
            gsum = jnp.zeros((n_pk, pk_rows, tb), BF16)
            for hd in range(PEER_HEADS):
                cnt = row_tile(cnt_ref, hd, i)
                c1 = row_tile(c1_ref, hd, i)
                rk = rk_ref[hd].reshape(n_pk, pk_rows, tb)
                e2 = e2_ref[hd].reshape(n_pk, pk_rows, tb)
                gsum = gsum + jnp.where(rk < cnt, e2 * c1, jnp.zeros((), BF16))
            r0 = sb * PEER_SUB + il * PEER_NKEYS
            a_i = a_cur[il * PEER_NKEYS:(il + 1) * PEER_NKEYS]
            half_gate = gsum.reshape(PEER_NKEYS, tb).astype(F32)
            w_i = a_i * (1.0 + lax.erf(a_i * (1.0 / math.sqrt(2.0)))) * half_gate
            w_scr[r0:r0 + PEER_NKEYS, :] = w_i.astype(BF16)
        done = (sb + 1) * PEER_SUB
        if done % PEER_ACC_CHUNK == 0:
            c0 = done - PEER_ACC_CHUNK
            acc_ref[...] += _dot(vt_ref[:, c0:done], w_scr[c0:done, :])

    @pl.when(eb == pl.num_programs(1) - 1)
    def _():
        o_ref[...] = x_ref[...] + gt_ref[...] * acc_ref[...].T


def _peer_dense(h_t, x, gate, u, v_t, layer, cnt1, c1, rank2, e2):
    d, n = h_t.shape
    n_e = u.shape[1]
    tb = min(PEER_TOK_BLOCK, n)
    eb = PEER_EXP_BLOCK
    tab = pl.BlockSpec((PEER_HEADS, PEER_NKEYS, tb), lambda i, e: (0, 0, i))
    return pl.pallas_call(
        _dense_kernel,
        grid=(n // tb, n_e // eb),
        in_specs=[pl.BlockSpec((d, tb), lambda i, e: (0, i)),
                  pl.BlockSpec((tb, d), lambda i, e: (i, 0)),
                  pl.BlockSpec((1, d), lambda i, e: (0, 0)),
                  pl.BlockSpec((None, eb, d), lambda i, e: (layer, e, 0)),
                  pl.BlockSpec((None, d, eb), lambda i, e: (layer, 0, e)),
                  tab, tab, tab, tab],
        out_specs=pl.BlockSpec((tb, d), lambda i, e: (i, 0)),
        out_shape=jax.ShapeDtypeStruct((n, d), F32),
        scratch_shapes=[pltpu.VMEM((d, tb), F32), pltpu.VMEM((eb, tb), BF16)],
        compiler_params=_cparams(("arbitrary", "arbitrary")),
        name="peer_dense",
    )(h_t, x, gate, u, v_t, cnt1, c1, rank2, e2)


def _final_norm_kernel(x_ref, g_ref, o_ref):
    x = x_ref[...]
    o_ref[...] = x * lax.rsqrt(jnp.mean(x * x, axis=-1, keepdims=True) + EPS) * g_ref[...]


def _final_norm(x, g):
    n, d = x.shape
    tm = min(2 * ROW_BLOCK, n)
    return pl.pallas_call(
        _final_norm_kernel,
        grid=(n // tm,),
        in_specs=[pl.BlockSpec((tm, d), lambda i: (i, 0)), pl.BlockSpec((1, d), lambda i: (0, 0))],
        out_specs=pl.BlockSpec((tm, d), lambda i: (i, 0)),
        out_shape=jax.ShapeDtypeStruct((n, d), F32),
        compiler_params=_cparams(("arbitrary",)),
        name="final_norm",
    )(x, g)


def kernel(x, c, ctx, c_ctx, w_mod, b_mod, g_mix, g_ffn, w_in, na_rpb, s5_lam_re, s5_lam_im,
           s5_b_re, s5_b_im, s5_c_re, s5_c_im, s5_log_step, s5_d, s5_w_glu, sw_sink, w_branch,
           w_out, peer_w_q, peer_sub_keys, peer_u, peer_v, g_final):
    batch, seq, d = x.shape
    l_ctx = ctx.shape[1]
    depth = w_mod.shape[0]
    assert batch == 1 and seq % (NA_ROWS * GRID_W) == 0 and l_ctx % (S5_CHUNK * SUBLANES) == 0
    assert seq // GRID_W >= NA_KROWS and seq >= 3 * SW_BLOCK and seq % (SW_BLOCK * SW_STEP_BLOCKS) == 0

    cc = jnp.zeros((SUBLANES, d), F32).at[0].set(c[0]).at[1].set(c_ctx)
    mod = _mod_vectors(cc, w_mod, b_mod).reshape(depth, SUBLANES, 6, d)
    rope_tabs = _rope_tables(seq)
    row = lambda v: v.reshape(1, d)
    na_uniq = _na_patterns(seq // GRID_W)[2]
    na_bias = jax.vmap(lambda r: _na_bias(r, na_uniq, l_ctx))(na_rpb)
    s5_prep = jax.vmap(_s5_prepare)(s5_lam_re, s5_lam_im, s5_b_re, s5_b_im, s5_c_re, s5_c_im, s5_log_step)
    u_all = peer_u.astype(BF16)
    vt_all = jnp.transpose(peer_v.astype(BF16), (0, 2, 1))

    xx, xc = x[0], ctx[0]
    for l in range(depth):
        need_ctx = l < depth - 1
        m_lat, m_ctx = mod[l, 0], mod[l, 1]
        w_in_l = w_in[l].astype(BF16)
        g_mix_l = row(g_mix[l])
        g_ffn_l = row(g_ffn[l])

        qa, ka, va, ub, qs, ks, vs, ga, gb, gs = _inproj(
            xx, g_mix_l, row(m_lat[0]), row(m_lat[1]), w_in_l, rope_tabs)
        qa_c, ka_c, va_c, ub_c, qs_c, ks_c, vs_c, ga_c, gb_c, gs_c = _inproj(
            xc, g_mix_l, row(m_ctx[0]), row(m_ctx[1]), w_in_l, None)

        ya = _na_attention(qa, ka, va, ka_c, va_c, na_bias[l])
        ys = _sw_attention(qs, ks, vs, ks_c, vs_c, sw_sink[l])
        y5 = _s5_mixer_pre_glu(ub_c, ub, [a[l] for a in s5_prep], s5_d[l])

        w_glu_l = s5_w_glu[l].astype(BF16)
        w_branch_l = w_branch[l].astype(BF16)
        w_out_l = w_out[l].astype(BF16)
        xx, hx2, hx2_t = _merge(xx, ya, y5, l_ctx, ys, ga, gb, gs, w_glu_l, w_branch_l, w_out_l,
                                row(m_lat[2]), g_ffn_l, row(m_lat[3]), row(m_lat[4]))

        w_q_l = peer_w_q[l].astype(BF16)
        k1 = peer_sub_keys[l, 0].astype(BF16)
        k2 = peer_sub_keys[l, 1].astype(BF16)
        if need_ctx:
            ya_c = _ctx_attention(qa_c, ka_c, va_c, None)
            ys_c = _ctx_attention(qs_c, ks_c, vs_c, sw_sink[l])
            xc, hc2, hc2_t = _merge(xc, ya_c, y5, 0, ys_c, ga_c, gb_c, gs_c, w_glu_l, w_branch_l, w_out_l,
                                    row(m_ctx[2]), g_ffn_l, row(m_ctx[3]), row(m_ctx[4]))
            xc = _peer_dense(hc2_t, xc, row(m_ctx[5]), u_all, vt_all, l, *_peer_route(hc2, w_q_l, k1, k2))
        xx = _peer_dense(hx2_t, xx, row(m_lat[5]), u_all, vt_all, l, *_peer_route(hx2, w_q_l, k1, k2))

    return _final_norm(xx, row(g_final))[None]
```

```python
import functools
import math

import numpy as np
import jax
import jax.numpy as jnp
from jax import lax
from jax.experimental import pallas as pl
from jax.experimental.pallas import tpu as pltpu

F32 = jnp.float32
BF16 = jnp.bfloat16

GRID_W = 64
HEAD_DIM = 64
NA_WIN_H = 8
NA_WIN_W = 16
S5_GROUP = 16
S5_STATE = 64
SW_WINDOW = 128
ROPE_BASE = 10000.0
PEER_HEADS = 8
PEER_NKEYS = 128
PEER_QDIM = 256
PEER_TOPK = 16
EPS = 1e-6
NEG_INF = -1e30

LANES = 128
SUBLANES = 8
VMEM_LIMIT_BYTES = 56 * 1024 * 1024

ROW_BLOCK = 256
NA_ROWS = 4
NA_KROWS = NA_ROWS + NA_WIN_H - 1
NA_LANES = 256
NA_STEP_BLOCKS = 2
SW_BLOCK = 128
SW_STEP_BLOCKS = 4
S5_CHUNK = 16
S5_LG = LANES // S5_GROUP
PEER_ROUTE_BLOCK = 256
PEER_ROUTE_HEADS = 4
PEER_TOK_BLOCK = 512
PEER_EXP_BLOCK = 2048
PEER_SUB = 512
PEER_ACC_CHUNK = 1024


def _cparams(sem):
    return pltpu.CompilerParams(dimension_semantics=sem, vmem_limit_bytes=VMEM_LIMIT_BYTES)


def _dot(a, b):
    return jnp.dot(a, b, preferred_element_type=F32)


def _dot_nt(a, b):
    return lax.dot_general(a, b, (((1,), (1,)), ((), ())), preferred_element_type=F32)


def _gelu(x):
    return 0.5 * x * (1.0 + lax.erf(x * (1.0 / math.sqrt(2.0))))


def _mod_kernel(cc_ref, w_ref, b_ref, o_ref):
    a = cc_ref[...]
    a = a * jax.nn.sigmoid(a)
    o_ref[0] = _dot(a.astype(BF16), w_ref[0].astype(BF16)) + b_ref[0]


def _mod_vectors(cc, w_mod, b_mod):
    depth, d, n6 = w_mod.shape
    tn = 1024
    return pl.pallas_call(
        _mod_kernel,
        grid=(depth, n6 // tn),
        in_specs=[pl.BlockSpec((SUBLANES, d), lambda l, j: (0, 0)),
                  pl.BlockSpec((1, d, tn), lambda l, j: (l, 0, j)),
                  pl.BlockSpec((1, 1, tn), lambda l, j: (l, 0, j))],
        out_specs=pl.BlockSpec((1, SUBLANES, tn), lambda l, j: (l, 0, j)),
        out_shape=jax.ShapeDtypeStruct((depth, SUBLANES, n6), F32),
        compiler_params=_cparams(("arbitrary", "arbitrary")),
        name="mod_vectors",
    )(cc, w_mod, b_mod.reshape(depth, 1, n6))


def _norm_mod(x, g, shift, scale):
    y = x * lax.rsqrt(jnp.mean(x * x, axis=-1, keepdims=True) + EPS)
    return (y * g) * (1.0 + scale) + shift


def _rope(x, cos, sin, lane_lo):
    up = pltpu.roll(x, LANES - 16, 1)
    dn = pltpu.roll(x, 16, 1)
    return x * cos + jnp.where(lane_lo, up, dn) * sin


def _inproj_kernel(x_ref, g_ref, sh_ref, sc_ref, w_ref, *rest, bw, kvw, d_model, rope):
    if rope:
        cos_ref, sin_ref = rest[:2]
        rest = rest[2:]
    qa_ref, ka_ref, va_ref, ub_ref, qs_ref, ks_ref, vs_ref, ga_ref, gb_ref, gs_ref = rest
    h = _norm_mod(x_ref[...], g_ref[...], sh_ref[...], sc_ref[...]).astype(BF16)

    def proj(c0, width):
        return _dot(h, w_ref[:, c0:c0 + width])

    c = 0
    qa_ref[...] = proj(c, bw).astype(BF16); c += bw
    ka_ref[...] = proj(c, bw).astype(BF16); c += bw
    va_ref[...] = proj(c, bw).astype(BF16); c += bw
    ub = proj(c, bw); c += bw
    for t in range(bw // LANES):
        ub_ref[t] = ub[:, t * LANES:(t + 1) * LANES]
    qs = proj(c, bw); c += bw
    ks = proj(c, kvw); c += kvw
    vs_ref[...] = proj(c, kvw).astype(BF16); c += kvw
    if rope:
        cos = cos_ref[...]
        sin = sin_ref[...]
        lane = lax.broadcasted_iota(jnp.int32, cos.shape, 1)
        lane_lo = (lane % 32) < 16
        for p in range(bw // LANES):
            sl = slice(p * LANES, (p + 1) * LANES)
            qs_ref[:, sl] = _rope(qs[:, sl], cos, sin, lane_lo).astype(BF16)
        for p in range(kvw // LANES):
            sl = slice(p * LANES, (p + 1) * LANES)
            ks_ref[:, sl] = _rope(ks[:, sl], cos, sin, lane_lo).astype(BF16)
    else:
        qs_ref[...] = qs.astype(BF16)
        ks_ref[...] = ks.astype(BF16)
    ga_ref[...] = jax.nn.sigmoid(proj(c, d_model)).astype(BF16); c += d_model
    gb_ref[...] = jax.nn.sigmoid(proj(c, d_model)).astype(BF16); c += d_model
    gs_ref[...] = jax.nn.sigmoid(proj(c, d_model)).astype(BF16)


def _inproj(x, g, shift, scale, w_in, rope_tabs):
    n, d = x.shape
    bw = d // 2
    kvw = bw // 4
    tm = min(ROW_BLOCK, n)
    rope = rope_tabs is not None
    row = lambda i: (i, 0)
    fixed = lambda i: (0, 0)
    in_specs = [pl.BlockSpec((tm, d), row), pl.BlockSpec((1, d), fixed), pl.BlockSpec((1, d), fixed),
                pl.BlockSpec((1, d), fixed), pl.BlockSpec(w_in.shape, fixed)]
    args = [x, g, shift, scale, w_in]
    if rope:
        in_specs += [pl.BlockSpec((tm, LANES), row)] * 2
        args += list(rope_tabs)
    widths = [bw, bw, bw, bw, bw, kvw, kvw, d, d, d]
    dtypes = [BF16, BF16, BF16, F32, BF16, BF16, BF16, BF16, BF16, BF16]
    s5_slot = 3
    return pl.pallas_call(
        functools.partial(_inproj_kernel, bw=bw, kvw=kvw, d_model=d, rope=rope),
        grid=(n // tm,),
        in_specs=in_specs,
        out_specs=[pl.BlockSpec((bw // LANES, tm, LANES), lambda i: (0, i, 0)) if k == s5_slot
                   else pl.BlockSpec((tm, w), row) for k, w in enumerate(widths)],
        out_shape=[jax.ShapeDtypeStruct((bw // LANES, n, LANES) if k == s5_slot else (n, w), dt)
                   for k, (w, dt) in enumerate(zip(widths, dtypes))],
        compiler_params=_cparams(("arbitrary",)),
        name="inproj",
    )(*args)


def _rope_tables(seq):
    t = jnp.arange(seq)
    half = HEAD_DIM // 4
    inv = ROPE_BASE ** (-jnp.arange(half, dtype=F32) / half)
    sign = jnp.concatenate([-jnp.ones((half,), F32), jnp.ones((half,), F32)])

    def axis_tabs(pos):
        ang = pos.astype(F32)[:, None] * inv[None, :]
        c = jnp.cos(ang)
        s = jnp.sin(ang)
        return jnp.concatenate([c, c], -1), jnp.concatenate([s, s], -1) * sign

    cr, sr = axis_tabs(t // GRID_W)
    cc, sc = axis_tabs(t % GRID_W)
    cos = jnp.concatenate([cr, cc], -1)
    sin = jnp.concatenate([sr, sc], -1)
    return jnp.tile(cos, (1, LANES // HEAD_DIM)), jnp.tile(sin, (1, LANES // HEAD_DIM))


def _na_patterns(rows):
    kh = NA_WIN_H
    nb = rows // NA_ROWS
    kr0s, sigs = [], []
    for b in range(nb):
        r_lo = b * NA_ROWS
        kr0 = int(np.clip(r_lo - kh // 2, 0, rows - NA_KROWS))
        sig = tuple((int(np.clip(r - kh // 2, 0, rows - kh)) - kr0, r - kr0)
                    for r in range(r_lo, r_lo + NA_ROWS))
        kr0s.append(kr0)
        sigs.append(sig)
    uniq = sorted(set(sigs))
    ids = np.array([uniq.index(s) for s in sigs], np.int32)
    return np.array(kr0s, np.int32), ids, uniq


NA_TILE_MASKED = 2 * NA_WIN_H - 1
NA_TILE_ZERO = 2 * NA_WIN_H


def _na_tile_index(uniq):
    n_slots = 2 * ((NA_KROWS + 2) // 2)
    idx = np.full((len(uniq), NA_ROWS, n_slots), NA_TILE_ZERO, np.int32)
    for p, sig in enumerate(uniq):
        for rq, (r0_rel, r_rel) in enumerate(sig):
            for kk in range(NA_KROWS):
                inside = r0_rel <= kk < r0_rel + NA_WIN_H
                idx[p, rq, kk] = kk - r_rel + NA_WIN_H - 1 if inside else NA_TILE_MASKED
    return idx.reshape(-1), n_slots


def _na_bias_tiles(rpb):
    kw = NA_WIN_W
    cols = np.arange(GRID_W)
    col_start = np.clip(cols - kw // 2, 0, GRID_W - kw)
    col_sel = (cols[None, None, :] - cols[None, :, None] + kw - 1
               == np.arange(2 * kw - 1)[:, None, None])
    col_ok = ((cols[None, :] >= col_start[:, None]) & (cols[None, :] < col_start[:, None] + kw))
    by_col = jnp.einsum('hrv,vcj->hrcj', rpb.astype(F32), jnp.asarray(col_sel, F32),
                        precision=lax.Precision.HIGHEST)
    by_col = jnp.where(jnp.asarray(col_ok), by_col, NEG_INF)
    n_h = rpb.shape[0]
    tiles = jnp.concatenate([by_col, jnp.full((n_h, 1, GRID_W, GRID_W), NEG_INF, F32),
                             jnp.zeros((n_h, 1, GRID_W, GRID_W), F32)], axis=1)
    zero = jnp.zeros_like(tiles)
    return jnp.concatenate([tiles, zero], axis=-1), jnp.concatenate([zero, tiles], axis=-1)


def _na_kernel(kr0_ref, pat_ref, tile_ref, q_ref, k_ref, v_ref, kc_ref, vc_ref, lo_ref, hi_ref, o_ref, *,
               n_slots):
    tq = NA_ROWS * GRID_W
    for qb in range(NA_STEP_BLOCKS):
        rows = slice(qb * tq, (qb + 1) * tq)
        _na_block(pl.program_id(1) * NA_STEP_BLOCKS + qb, kr0_ref, pat_ref, tile_ref, q_ref.at[rows, :],
                  k_ref, v_ref, kc_ref, vc_ref, lo_ref, hi_ref, o_ref.at[rows, :], n_slots=n_slots)


def _na_block(b, kr0_ref, pat_ref, tile_ref, q_ref, k_ref, v_ref, kc_ref, vc_ref, lo_ref, hi_ref, o_ref, *,
              n_slots):
    start = pl.multiple_of(kr0_ref[b] * GRID_W, GRID_W)
    nk = NA_KROWS * GRID_W
    scale = HEAD_DIM ** -0.5
    one = jnp.ones((), BF16)
    hp = LANES // HEAD_DIM
    tile_base = pat_ref[b] * (NA_ROWS * n_slots)

    def bias_of(head):
        row_blocks = []
        for rq in range(NA_ROWS):
            pairs = []
            for j in range(n_slots // 2):
                even = tile_ref[tile_base + rq * n_slots + 2 * j]
                odd = tile_ref[tile_base + rq * n_slots + 2 * j + 1]
                pairs.append(lo_ref[head, even] + hi_ref[head, odd])
            row_blocks.append(jnp.concatenate(pairs, axis=-1))
        return jnp.concatenate(row_blocks, axis=0)

    for lt in range(q_ref.shape[1] // LANES):
        lanes = slice(lt * LANES, (lt + 1) * LANES)
        q = q_ref[:, lanes]
        k_all = jnp.concatenate([k_ref[pl.ds(start, nk), lanes], kc_ref[:, lanes]], axis=0)
        v_all = jnp.concatenate([v_ref[pl.ds(start, nk), lanes], vc_ref[:, lanes]], axis=0)
        lane = lax.broadcasted_iota(jnp.int32, v_all.shape, 1)
        ext = []
        for h in range(hp):
            sl = slice(h * HEAD_DIM, (h + 1) * HEAD_DIM)
            own = (lane >= h * HEAD_DIM) & (lane < (h + 1) * HEAD_DIM)
            s = _dot_nt(q[:, sl], k_all[:, sl]) * scale
            n_biased = n_slots * GRID_W
            s = jnp.concatenate([s[:, :n_biased] + bias_of(lt * hp + h), s[:, n_biased:]], axis=-1)
            p = jnp.exp(s - jnp.max(s, axis=-1, keepdims=True)).astype(BF16)
            ext.append(_dot(p, jnp.where(own, v_all, one)))
        out_lane = lax.broadcasted_iota(jnp.int32, ext[0].shape, 1)
        low = out_lane < HEAD_DIM
        num = jnp.where(low, ext[0], ext[1])
        den = pltpu.roll(jnp.where(low, ext[1], ext[0]), HEAD_DIM, 1)
        o_ref[:, lanes] = (num / den).astype(o_ref.dtype)


def _na_attention(q, k, v, kc, vc, bias_lo, bias_hi):
    s, bw = q.shape
    l = kc.shape[0]
    rows = s // GRID_W
    kr0s, ids, uniq = _na_patterns(rows)
    tile_idx, n_slots = _na_tile_index(uniq)
    assert n_slots * GRID_W <= NA_KROWS * GRID_W + l
    tq = NA_ROWS * GRID_W * NA_STEP_BLOCKS
    hp = NA_LANES // HEAD_DIM
    tiles = pl.BlockSpec((hp,) + bias_lo.shape[1:], lambda p, b, *_: (p, 0, 0, 0))
    grid_spec = pltpu.PrefetchScalarGridSpec(
        num_scalar_prefetch=3,
        grid=(bw // NA_LANES, rows // (NA_ROWS * NA_STEP_BLOCKS)),
        in_specs=[pl.BlockSpec((tq, NA_LANES), lambda p, b, *_: (b, p)),
                  pl.BlockSpec((s, NA_LANES), lambda p, b, *_: (0, p)),
                  pl.BlockSpec((s, NA_LANES), lambda p, b, *_: (0, p)),
                  pl.BlockSpec((l, NA_LANES), lambda p, b, *_: (0, p)),
                  pl.BlockSpec((l, NA_LANES), lambda p, b, *_: (0, p)),
                  tiles, tiles],
        out_specs=pl.BlockSpec((tq, NA_LANES), lambda p, b, *_: (b, p)),
    )
    return pl.pallas_call(
        functools.partial(_na_kernel, n_slots=n_slots),
        grid_spec=grid_spec,
        out_shape=jax.ShapeDtypeStruct((s, bw), BF16),
        compiler_params=_cparams(("arbitrary", "arbitrary")),
        name="na_attention",
    )(jnp.asarray(kr0s), jnp.asarray(ids), jnp.asarray(tile_idx), q, k, v, kc, vc, bias_lo, bias_hi)


def _ctx_attn_kernel(*refs, n_heads, n_rep, has_sink):
    if has_sink:
        sink_ref, q_ref, k_ref, v_ref, o_ref = refs
    else:
        q_ref, k_ref, v_ref, o_ref = refs
    scale = HEAD_DIM ** -0.5
    q = q_ref[...]
    k = k_ref[...]
    v = v_ref[...]
    outs = []
    for h in range(n_heads):
        g = h // n_rep
        sl = slice(h * HEAD_DIM, (h + 1) * HEAD_DIM)
        gl = slice(g * HEAD_DIM, (g + 1) * HEAD_DIM)
        s = _dot_nt(q[:, sl], k[:, gl]) * scale
        m = jnp.max(s, axis=-1, keepdims=True)
        if has_sink:
            m = jnp.maximum(m, sink_ref[h])
        p = jnp.exp(s - m)
        den = jnp.sum(p, axis=-1, keepdims=True)
        if has_sink:
            den = den + jnp.exp(sink_ref[h] - m)
        outs.append(_dot(p.astype(BF16), v[:, gl]) / den)
    o_ref[...] = jnp.concatenate(outs, axis=-1).astype(o_ref.dtype)


def _ctx_attention(q, k, v, sink):
    l, bw = q.shape
    n_heads = bw // HEAD_DIM
    n_rep = n_heads // (k.shape[1] // HEAD_DIM)
    has_sink = sink is not None
    full = lambda a: pl.BlockSpec(a.shape, lambda i: (0, 0))
    in_specs = [full(q), full(k), full(v)]
    args = [q, k, v]
    if has_sink:
        in_specs = [pl.BlockSpec(memory_space=pltpu.SMEM)] + in_specs
        args = [sink] + args
    return pl.pallas_call(
        functools.partial(_ctx_attn_kernel, n_heads=n_heads, n_rep=n_rep, has_sink=has_sink),
        grid=(1,),
        in_specs=in_specs,
        out_specs=pl.BlockSpec((l, bw), lambda i: (0, 0)),
        out_shape=jax.ShapeDtypeStruct((l, bw), BF16),
        compiler_params=_cparams(("arbitrary",)),
        name="ctx_attention",
    )(*args)


def _swa_kernel(sink_ref, q_ref, k_ref, v_ref, kc_ref, vc_ref, o_ref, *, seq, n_kv, n_rep):
    for qb in range(SW_STEP_BLOCKS):
        rows = slice(qb * SW_BLOCK, (qb + 1) * SW_BLOCK)
        _swa_block(pl.program_id(0) * SW_STEP_BLOCKS + qb, sink_ref, q_ref[rows, :], k_ref, v_ref,
                   kc_ref, vc_ref, o_ref.at[rows, :], seq=seq, n_kv=n_kv, n_rep=n_rep)


def _swa_block(n, sink_ref, q, k_ref, v_ref, kc_ref, vc_ref, o_ref, *, seq, n_kv, n_rep):
    nk = 3 * SW_BLOCK
    scale = HEAD_DIM ** -0.5
    ws = jnp.clip((n - 1) * SW_BLOCK, 0, seq - nk)
    ws = pl.multiple_of(ws, SW_BLOCK)
    n_all = nk + kc_ref.shape[0]
    k_all = jnp.concatenate([k_ref[pl.ds(ws, nk), :], kc_ref[...]], axis=0)
    v_all = jnp.concatenate([v_ref[pl.ds(ws, nk), :], vc_ref[...]], axis=0)
    qpos = n * SW_BLOCK + lax.broadcasted_iota(jnp.int32, (SW_BLOCK, n_all), 0)
    col = lax.broadcasted_iota(jnp.int32, (SW_BLOCK, n_all), 1)
    visible = (col >= nk) | (jnp.abs(ws + col - qpos) <= SW_WINDOW)
    maskb = jnp.where(visible, 0.0, NEG_INF).astype(F32)
    lane = lax.broadcasted_iota(jnp.int32, v_all.shape, 1)
    one = jnp.ones((), BF16)
    outs = [None] * (n_kv * n_rep)
    for g in range(n_kv):
        gl = slice(g * HEAD_DIM, (g + 1) * HEAD_DIM)
        own = (lane >= g * HEAD_DIM) & (lane < (g + 1) * HEAD_DIM)
        qg = jnp.concatenate([q[:, (g * n_rep + r) * HEAD_DIM:(g * n_rep + r + 1) * HEAD_DIM]
                              for r in range(n_rep)], axis=0)
        s = _dot_nt(qg, k_all[:, gl]) * scale
        s = (s.reshape(n_rep, SW_BLOCK, n_all) + maskb[None]).reshape(n_rep * SW_BLOCK, n_all)
        sink = jnp.concatenate([jnp.full((SW_BLOCK, 1), sink_ref[g * n_rep + r], F32)
                                for r in range(n_rep)], axis=0)
        m = jnp.maximum(jnp.max(s, axis=-1, keepdims=True), sink)
        p = jnp.exp(s - m).astype(BF16)
        ext = _dot(p, jnp.where(own, v_all, one))
        den = pltpu.roll(ext, HEAD_DIM, 1) + jnp.exp(sink - m)
        o = ext / den
        for r in range(n_rep):
            outs[g * n_rep + r] = o[r * SW_BLOCK:(r + 1) * SW_BLOCK, gl]
    o_ref[...] = jnp.concatenate(outs, axis=-1).astype(o_ref.dtype)


def _sw_attention(q, k, v, kc, vc, sink):
    s, bw = q.shape
    l = kc.shape[0]
    kvw = k.shape[1]
    n_kv = kvw // HEAD_DIM
    n_rep = (bw // HEAD_DIM) // n_kv
    assert kvw == LANES
    return pl.pallas_call(
        functools.partial(_swa_kernel, seq=s, n_kv=n_kv, n_rep=n_rep),
        grid=(s // (SW_BLOCK * SW_STEP_BLOCKS),),
        in_specs=[pl.BlockSpec(memory_space=pltpu.SMEM),
                  pl.BlockSpec((SW_BLOCK * SW_STEP_BLOCKS, bw), lambda n: (n, 0)),
                  pl.BlockSpec((s, kvw), lambda n: (0, 0)),
                  pl.BlockSpec((s, kvw), lambda n: (0, 0)),
                  pl.BlockSpec((l, kvw), lambda n: (0, 0)),
                  pl.BlockSpec((l, kvw), lambda n: (0, 0))],
        out_specs=pl.BlockSpec((SW_BLOCK * SW_STEP_BLOCKS, bw), lambda n: (n, 0)),
        out_shape=jax.ShapeDtypeStruct((s, bw), BF16),
        compiler_params=_cparams(("arbitrary",)),
        name="sw_attention",
    )(sink, q, k, v, kc, vc)


def _s5_prepare(lam_re, lam_im, b_re, b_im, c_re, c_im, log_step):
    tc = S5_CHUNK
    lam = lax.complex(lam_re.astype(F32), lam_im.astype(F32))
    dt = jnp.exp(log_step.astype(F32))[..., None]
    lam_dt = lam * dt
    lam_bar = jnp.exp(lam_dt)
    b_bar = ((lam_bar - 1.0) / lam)[..., None] * lax.complex(b_re.astype(F32), b_im.astype(F32))
    c_mat = lax.complex(c_re.astype(F32), c_im.astype(F32))
    kk = jnp.arange(tc + 1, dtype=F32)
    pw = jnp.exp(lam_dt[None] * kk[:, None, None, None])
    n_g, n_p, n_h = b_bar.shape[1:]

    bt = jnp.transpose(b_bar, (1, 0, 3, 2))
    bt = jnp.stack([bt.real, bt.imag], axis=2)
    cw = c_mat[None] * pw[:tc, :, :, None, :]
    cw = jnp.stack([cw[:, 0], cw[::-1, 1]], axis=1)
    cwt = jnp.transpose(cw, (2, 1, 4, 0, 3)).reshape(n_g, 2, n_p, tc * n_h)
    cwt = jnp.stack([cwt.real, cwt.imag], axis=2)
    t_i = np.arange(tc)

    def cat_ri(z):
        return jnp.concatenate([z.real, z.imag], axis=-1)

    w_f = pw[tc - 1 - t_i, 0][:, :, None, :] * jnp.transpose(b_bar[0], (0, 2, 1))[None]
    w_r = pw[t_i, 1][:, :, None, :] * jnp.transpose(b_bar[1], (0, 2, 1))[None]
    w_f = jnp.transpose(cat_ri(w_f), (1, 0, 2, 3)).reshape(n_g, tc * n_h, 2 * n_p)
    w_r = jnp.transpose(cat_ri(w_r), (1, 0, 2, 3)).reshape(n_g, tc * n_h, 2 * n_p)
    w_cat = jnp.concatenate([w_f, w_r], axis=-1).astype(BF16)

    def v_of(c_dir, pw_sel):
        cp = c_dir[None] * pw_sel[:, :, None, :]
        v = jnp.concatenate([cp.real, -cp.imag], axis=-1)
        return jnp.transpose(v, (1, 3, 0, 2)).reshape(n_g, 2 * n_p, tc * n_h)
    v_cat = jnp.concatenate([v_of(c_mat[0], pw[t_i + 1, 0]), v_of(c_mat[1], pw[tc - t_i, 1])],
                            axis=1).astype(BF16)

    def coef(k):
        a = jnp.exp(lam_dt * (tc * k))
        return jnp.stack([jnp.concatenate([a.real, a.real], -1),
                          jnp.concatenate([-a.imag, a.imag], -1)], axis=2)
    a_log = jnp.stack([coef(1.0), coef(2.0), coef(4.0)], axis=2)
    a_car = jnp.stack([coef(float(j)) for j in range(1, SUBLANES + 1)], axis=3)
    a_car = jnp.stack([a_car[0], a_car[1][:, :, ::-1]], axis=0)
    return _s5_expand(bt, cwt, w_cat, v_cat, jnp.transpose(a_log, (1, 0, 2, 3, 4)),
                      jnp.transpose(a_car, (1, 0, 2, 3, 4)))


def _s5_expand(bt, cwt, w_cat, v_cat, a_log, a_car):
    n_g = bt.shape[0]
    n_t = n_g // S5_LG
    n_state = 2 * S5_STATE
    by_tile = lambda a: a.reshape((n_t, S5_LG) + a.shape[1:])

    def lanes_of(a):
        a = jnp.moveaxis(by_tile(a), 1, -2)
        return a.reshape(a.shape[:-2] + (S5_LG * n_state,))
    return by_tile(bt), by_tile(cwt), by_tile(w_cat), by_tile(v_cat), lanes_of(a_log), lanes_of(a_car)


def _s5_lane_perm():
    r = np.arange(S5_CHUNK * LANES)
    t, a, h = r // LANES, (r % LANES) // S5_GROUP, r % S5_GROUP
    dest = a * (S5_CHUNK * S5_GROUP) + t * S5_GROUP + h
    return (jnp.asarray(dest)[:, None] == jnp.arange(S5_CHUNK * LANES)[None, :]).astype(BF16)


def _chunk_rows(tok_ref):
    n_rows = tok_ref.shape[1] // S5_CHUNK
    return jnp.concatenate([tok_ref[0, pl.ds(t, n_rows, stride=S5_CHUNK), :] for t in range(S5_CHUNK)], axis=-1)


def _s5_in_kernel(u_ref, perm_ref, bt_ref, cwt_ref, w_ref, yp_ref, z_ref, m_scr):
    width = S5_CHUNK * S5_GROUP
    n_state = 2 * S5_STATE
    half = S5_LG * n_state

    @pl.when(pl.program_id(1) == 0)
    def _():
        lane = lax.broadcasted_iota(jnp.int32, (S5_GROUP, width), 1)
        def lag_kernels(a, direction):
            hp = lax.Precision.HIGHEST
            return (jnp.dot(bt_ref[0, a, direction, 0], cwt_ref[0, a, direction, 0], precision=hp,
                            preferred_element_type=F32)
                    - jnp.dot(bt_ref[0, a, direction, 1], cwt_ref[0, a, direction, 1], precision=hp,
                              preferred_element_type=F32))

        for a in range(S5_LG):
            k_fwd = lag_kernels(a, 0)
            k_rev = lag_kernels(a, 1)
            for tau in range(S5_CHUNK):
                right = tau * S5_GROUP
                left = (S5_CHUNK - 1 - tau) * S5_GROUP
                blk = (jnp.where(lane >= right, pltpu.roll(k_fwd, right, 1), 0.0)
                       + jnp.where(lane < width - left, pltpu.roll(k_rev, (width - left) % width, 1), 0.0))
                m_scr[a, tau * S5_GROUP:(tau + 1) * S5_GROUP, 0:width] = blk.astype(BF16)
            m_scr[a, :, width:] = w_ref[0, a]

    up = _dot(_chunk_rows(u_ref).astype(BF16), perm_ref[...]).astype(BF16)
    for a in range(S5_LG):
        zz = _dot(up[:, a * width:(a + 1) * width], m_scr[a])
        yp_ref[0, :, a * width:(a + 1) * width] = zz[:, :width]
        z_ref[0, :, a * n_state:(a + 1) * n_state] = zz[:, width:width + n_state]
        z_ref[0, :, half + a * n_state:half + (a + 1) * n_state] = zz[:, width + n_state:]


def _s5_scan_kernel(z_ref, alog_ref, acar_ref, e_ref, *, nc, nc_ctx):
    n_state = 2 * S5_STATE
    half = z_ref.shape[2] // 2
    nblk = nc // SUBLANES
    nblk_ctx = nc_ctx // SUBLANES
    row = lax.broadcasted_iota(jnp.int32, (SUBLANES, n_state), 0)

    def cmul(a1, a2, s):
        return a1 * s + a2 * pltpu.roll(s, S5_STATE, 1)

    def step(t, carry):
        blk = (t, jnp.where(t < nblk_ctx, nblk_ctx - 1 - t, nblk + nblk_ctx - 1 - t))
        new = []
        for d in range(2):
            r0 = pl.multiple_of(blk[d] * SUBLANES, SUBLANES)
            for j in range(half // n_state):
                tile = slice(j * n_state, (j + 1) * n_state)
                lanes = slice(d * half + j * n_state, d * half + (j + 1) * n_state)
                z = z_ref[0, pl.ds(r0, SUBLANES), lanes]
                c = carry[len(new)]
                for i, sh in enumerate((1, 2, 4)):
                    if d == 0:
                        zs = jnp.where(row >= sh, pltpu.roll(z, sh, 0), 0.0)
                    else:
                        zs = jnp.where(row < SUBLANES - sh, pltpu.roll(z, SUBLANES - sh, 0), 0.0)
                    z = z + cmul(alog_ref[0, d, i, 0:1, tile], alog_ref[0, d, i, 1:2, tile], zs)
                cb = jnp.broadcast_to(c, (SUBLANES, n_state))
                s = z + cmul(acar_ref[0, d, 0, :, tile], acar_ref[0, d, 1, :, tile], cb)
                if d == 0:
                    e = jnp.where(row >= 1, pltpu.roll(s, 1, 0), cb)
                    c_new = s[SUBLANES - 1:SUBLANES, :]
                else:
                    e = jnp.where(row < SUBLANES - 1, pltpu.roll(s, SUBLANES - 1, 0), cb)
                    c_new = s[0:1, :]
                e_ref[0, pl.ds(r0, SUBLANES), lanes] = e
                new.append(c_new)
        return tuple(new)

    init = tuple(jnp.zeros((1, n_state), F32) for _ in range(2 * (half // n_state)))
    lax.fori_loop(0, nblk, step, init, unroll=2)


def _s5_out_kernel(e_ref, yp_ref, u_ref, d_ref, perm_ref, v_ref, y_ref):
    width = S5_CHUNK * S5_GROUP
    n_state = 2 * S5_STATE
    half = S5_LG * n_state
    parts = []
    for a in range(S5_LG):
        e_a = jnp.concatenate([e_ref[0, :, a * n_state:(a + 1) * n_state],
                               e_ref[0, :, half + a * n_state:half + (a + 1) * n_state]], axis=-1)
        parts.append(yp_ref[0, :, a * width:(a + 1) * width] + _dot(e_a.astype(BF16), v_ref[0, a]))
    y = jnp.concatenate(parts, axis=-1)
    hi = y.astype(BF16)
    lo = (y - hi.astype(F32)).astype(BF16)
    perm = perm_ref[...]
    y = _dot_nt(hi, perm) + _dot_nt(lo, perm) + _chunk_rows(u_ref) * d_ref[0]
    n_rows = y.shape[0]
    for t in range(S5_CHUNK):
        y_ref[0, pl.ds(t, n_rows, stride=S5_CHUNK), :] = y[:, t * LANES:(t + 1) * LANES]


def _s5_rows(nc):
    return max(r for r in range(2 * SUBLANES, 257, 2 * SUBLANES) if nc % r == 0)


def _s5_mixer_pre_glu(u_ctx, u_lat, prep, d_skip):
    bt, cwt, w_cat, v_cat, a_log, a_car = prep
    n_t, l, _ = u_ctx.shape
    s = u_lat.shape[1]
    nc, nc_ctx = (l + s) // S5_CHUNK, l // S5_CHUNK
    width = S5_CHUNK * LANES
    n_z = 2 * S5_LG * 2 * S5_STATE
    u = jnp.concatenate([u_ctx, u_lat], axis=1)
    d_rows = jnp.tile(d_skip.astype(F32).reshape(n_t, 1, LANES), (1, 1, S5_CHUNK))
    perm = _s5_lane_perm()
    rb = _s5_rows(nc)
    rows = lambda w: pl.BlockSpec((1, rb, w), lambda g, r: (g, r, 0))
    per_tile = lambda a: pl.BlockSpec((1,) + a.shape[1:], lambda g, *_: (g,) + (0,) * (a.ndim - 1))
    whole = lambda a: pl.BlockSpec(a.shape, lambda g, r: (0,) * a.ndim)
    tokens = pl.BlockSpec((1, rb * S5_CHUNK, LANES), lambda g, r: (g, r, 0))
    width_g = S5_CHUNK * S5_GROUP
    yp, z = pl.pallas_call(
        _s5_in_kernel,
        grid=(n_t, nc // rb),
        in_specs=[tokens, whole(perm), per_tile(bt), per_tile(cwt), per_tile(w_cat)],
        out_specs=[rows(width), rows(n_z)],
        out_shape=[jax.ShapeDtypeStruct((n_t, nc, width), F32), jax.ShapeDtypeStruct((n_t, nc, n_z), F32)],
        scratch_shapes=[pltpu.VMEM((S5_LG, width_g, width_g + 4 * S5_STATE), BF16)],
        compiler_params=_cparams(("arbitrary", "arbitrary")),
        name="s5_in",
    )(u, perm, bt, cwt, w_cat)
    e = pl.pallas_call(
        functools.partial(_s5_scan_kernel, nc=nc, nc_ctx=nc_ctx),
        grid=(n_t,),
        in_specs=[per_tile(z), per_tile(a_log), per_tile(a_car)],
        out_specs=per_tile(z),
        out_shape=jax.ShapeDtypeStruct(z.shape, F32),
        compiler_params=_cparams(("arbitrary",)),
        name="s5_scan",
    )(z, a_log, a_car)
    y = pl.pallas_call(
        _s5_out_kernel,
        grid=(n_t, nc // rb),
        in_specs=[rows(n_z), rows(width), tokens, per_tile(d_rows), whole(perm), per_tile(v_cat)],
        out_specs=tokens,
        out_shape=jax.ShapeDtypeStruct((n_t, l + s, LANES), F32),
        compiler_params=_cparams(("arbitrary", "arbitrary")),
        name="s5_out",
    )(e, yp, u, d_rows, perm, v_cat)
    return y


def _merge_kernel(x_ref, ya_ref, y5_ref, ys_ref, ga_ref, gb_ref, gs_ref, wglu_ref, wb_ref, wo_ref,
                  gt_ref, g_ref, sh_ref, sc_ref, xo_ref, h_ref, ht_ref):
    y5 = jnp.concatenate([y5_ref[t] for t in range(y5_ref.shape[0])], axis=-1)
    z = _gelu(y5)
    yb = z * jax.nn.sigmoid(_dot(z.astype(BF16), wglu_ref[...]))
    m = (ga_ref[...].astype(F32) * _dot(ya_ref[...], wb_ref[0])
         + gb_ref[...].astype(F32) * _dot(yb.astype(BF16), wb_ref[1])
         + gs_ref[...].astype(F32) * _dot(ys_ref[...], wb_ref[2]))
    x = x_ref[...] + gt_ref[...] * _dot(m.astype(BF16), wo_ref[...])
    xo_ref[...] = x
    h = _norm_mod(x, g_ref[...], sh_ref[...], sc_ref[...])
    h_ref[...] = h.astype(BF16)
    ht_ref[...] = h.T.astype(BF16)


def _merge(x, ya, y5, y5_row0, ys, ga, gb, gs, w_glu, w_branch, w_out, gate, g, shift, scale):
    n, d = x.shape
    bw = ya.shape[1]
    tm = min(ROW_BLOCK, n)
    assert y5_row0 % tm == 0
    row = lambda w: pl.BlockSpec((tm, w), lambda i: (i, 0))
    full = lambda a: pl.BlockSpec(a.shape, lambda i: (0,) * a.ndim)
    vec = pl.BlockSpec((1, d), lambda i: (0, 0))
    y5_rows = pl.BlockSpec((y5.shape[0], tm, LANES), lambda i: (0, i + y5_row0 // tm, 0))
    return pl.pallas_call(
        _merge_kernel,
        grid=(n // tm,),
        in_specs=[row(d), row(bw), y5_rows, row(bw), row(d), row(d), row(d),
                  full(w_glu), full(w_branch), full(w_out), vec, vec, vec, vec],
        out_specs=[row(d), row(d), pl.BlockSpec((d, tm), lambda i: (0, i))],
        out_shape=[jax.ShapeDtypeStruct((n, d), F32), jax.ShapeDtypeStruct((n, d), BF16),
                   jax.ShapeDtypeStruct((d, n), BF16)],
        compiler_params=_cparams(("arbitrary",)),
        name="merge",
    )(x, ya, y5, ys, ga, gb, gs, w_glu, w_branch, w_out, gate, g, shift, scale)


def _knock_out_16(s, order, exact, want_rank=True):
    rank = jnp.full(s.shape, float(PEER_TOPK), F32) if want_rank else None
    live = s
    vals = []
    for r in range(PEER_TOPK):
        m = jnp.max(live, axis=0, keepdims=True)
        hit = live == m
        if exact:
            first = jnp.min(jnp.where(hit, order, float(PEER_TOPK * PEER_NKEYS)), axis=0, keepdims=True)
            hit = order == first
        if want_rank:
            rank = jnp.where(hit, float(r), rank)
        live = jnp.where(hit, -jnp.inf, live)
        vals.append(m)
    out = (rank < float(PEER_TOPK)) if want_rank else (live != s)
    n_out = jnp.sum(jnp.where(out, 1.0, 0.0), axis=0, keepdims=True)
    return rank, jnp.concatenate(vals, axis=0), n_out


def _bf16_pair_word(x):
    hi = pltpu.bitcast(x.astype(BF16).astype(F32), jnp.uint32)
    return hi | (hi >> 16)


def _pair_tiles():
    tiles = [(0, 0, 8), (0, 8, 8)]
    for a in range(1, 8):
        tiles.append((a, 0, PEER_TOPK // (a + 1)))
    return tiles


def _route_kernel(h_ref, wq_ref, k1_ref, k2_ref, cnt_ref, c1_ref, rk_ref, e2_ref, q_scr):
    tb = h_ref.shape[0]
    q_scr[...] = _dot(h_ref[...], wq_ref[...]).astype(BF16)
    half = PEER_QDIM // 2
    iota = lax.broadcasted_iota(jnp.int32, (PEER_NKEYS, tb), 0).astype(F32)
    row8 = lax.broadcasted_iota(jnp.int32, (SUBLANES, tb), 0).astype(F32)
    k1 = k1_ref[...]
    k2 = k2_ref[...]

    def emit(hd, s1, s2, exact):
        rank1, v1, n1 = _knock_out_16(s1, iota, exact, want_rank=exact)
        rank2, v2, n2 = _knock_out_16(s2, iota, exact)
        e1v = jnp.exp(v1 - v1[0:1])
        e2v = jnp.exp(v2 - v2[0:1])
        tiles, flats, gates = [], [], []
        for a, b0, nv in _pair_tiles():
            c = v1[a:a + 1] + v2[b0:b0 + SUBLANES]
            tiles.append(jnp.where(row8 < nv, c, -jnp.inf))
            flats.append(a * PEER_TOPK + b0 + row8)
            gates.append(e1v[a:a + 1] * e2v[b0:b0 + SUBLANES])
        tiles.append(v1[SUBLANES:] + v2[0:1])
        flats.append((row8 + SUBLANES) * PEER_TOPK)
        gates.append(e1v[SUBLANES:] * e2v[0:1])
        cand = jnp.concatenate(tiles, axis=0)
        flat = jnp.concatenate(flats, axis=0)
        gate = jnp.concatenate(gates, axis=0)
        rank_c, _, n_c = _knock_out_16(cand, flat, exact)
        self_ = jnp.where(rank_c < float(PEER_TOPK), 1.0, 0.0)
        z = jnp.sum(self_ * gate, axis=0, keepdims=True)
        cnt = [self_[0:8].sum(axis=0, keepdims=True) + self_[8:16].sum(axis=0, keepdims=True)]
        for t in range(2, 9):
            cnt.append(self_[t * SUBLANES:(t + 1) * SUBLANES].sum(axis=0, keepdims=True))
        cnt = jnp.concatenate(cnt + [self_[9 * SUBLANES:]], axis=0)
        cnt1 = jnp.zeros((PEER_NKEYS, tb), F32)
        for a in range(PEER_TOPK):
            is_a = (rank1 == float(a)) if exact else (s1 == v1[a:a + 1])
            cnt1 = jnp.where(is_a, cnt[a:a + 1], cnt1)
        cnt_ref[hd] = _bf16_pair_word(cnt1)
        c1_ref[hd] = _bf16_pair_word(jnp.exp(s1 - v1[0:1]) * (0.5 / z))
        rk_ref[hd] = rank2.astype(BF16)
        e2_ref[hd] = jnp.exp(s2 - v2[0:1]).astype(BF16)
        want = float(PEER_TOPK)
        return jnp.where((n1 != want) | (n2 != want) | (n_c != want), 1.0, 0.0)

    def head_group(hg, _):
        scores = []
        for k in range(PEER_ROUTE_HEADS):
            hd = hg * PEER_ROUTE_HEADS + k
            c0 = pl.multiple_of(hd * PEER_QDIM, PEER_QDIM)
            s1 = _dot_nt(k1, q_scr[:, pl.ds(c0, half)])
            s2 = _dot_nt(k2, q_scr[:, pl.ds(c0 + half, half)])
            scores.append((hd, s1, s2))
        tied = [jnp.max(emit(hd, s1, s2, exact=False)) for hd, s1, s2 in scores]
        for (hd, s1, s2), t in zip(scores, tied):
            @pl.when(t > 0.0)
            def _():
                emit(hd, s1, s2, exact=True)
        return 0

    lax.fori_loop(0, PEER_HEADS // PEER_ROUTE_HEADS, head_group, 0)


def _peer_route(h, w_q, k1, k2):
    n, d = h.shape
    tb = min(PEER_ROUTE_BLOCK, n)
    qw = w_q.shape[1]
    out_blk = pl.BlockSpec((PEER_HEADS, PEER_NKEYS, tb), lambda i: (0, 0, i))
    shp = lambda dt: jax.ShapeDtypeStruct((PEER_HEADS, PEER_NKEYS, n), dt)
    return pl.pallas_call(
        _route_kernel,
        grid=(n // tb,),
        in_specs=[pl.BlockSpec((tb, d), lambda i: (i, 0)),
                  pl.BlockSpec((d, qw), lambda i: (0, 0)),
                  pl.BlockSpec(k1.shape, lambda i: (0, 0)),
                  pl.BlockSpec(k2.shape, lambda i: (0, 0))],
        out_specs=[out_blk] * 4,
        out_shape=[shp(jnp.uint32), shp(jnp.uint32), shp(BF16), shp(BF16)],
        scratch_shapes=[pltpu.VMEM((tb, qw), BF16)],
        compiler_params=_cparams(("arbitrary",)),
        name="peer_route",
    )(h, w_q, k1, k2)


def _dense_kernel(ht_ref, x_ref, gt_ref, u_ref, vt_ref, cnt_ref, c1_ref, rk_ref, e2_ref, o_ref,
                  acc_ref, w_scr):
    eb = pl.program_id(1)
    tb = ht_ref.shape[1]
    n_exp = u_ref.shape[0]
    n_sub = n_exp // PEER_SUB
    i_per_sub = PEER_SUB // PEER_NKEYS

    @pl.when(eb == 0)
    def _():
        acc_ref[...] = jnp.zeros_like(acc_ref)

    def scores(sb):
        return _dot(u_ref[sb * PEER_SUB:(sb + 1) * PEER_SUB, :], ht_ref[...])

    pk_rows = 2 * SUBLANES
    n_pk = PEER_NKEYS // pk_rows

    def row_tile(ref, hd, i):
        words = jnp.broadcast_to(ref[hd, pl.ds(i, 1), :], (SUBLANES, tb))
        return pltpu.bitcast(words, BF16)[None]

    a_next = scores(0)
    for sb in range(n_sub):
        a_cur = a_next
        if sb + 1 < n_sub:
            a_next = scores(sb + 1)
        for il in range(i_per_sub):
            i = eb * (n_exp // PEER_NKEYS) + sb * i_per_sub + il
            gsum = jnp.zeros((n_pk, pk_rows, tb), BF16)
            for hd in range(PEER_HEADS):
                cnt = row_tile(cnt_ref, hd, i)
                c1 = row_tile(c1_ref, hd, i)
                rk = rk_ref[hd].reshape(n_pk, pk_rows, tb)
                e2 = e2_ref[hd].reshape(n_pk, pk_rows, tb)
                gsum = gsum + jnp.where(rk < cnt, e2 * c1, jnp.zeros((), BF16))
            r0 = sb * PEER_SUB + il * PEER_NKEYS
            a_i = a_cur[il * PEER_NKEYS:(il + 1) * PEER_NKEYS]
            half_gate = gsum.reshape(PEER_NKEYS, tb).astype(F32)
            w_i = a_i * (1.0 + lax.erf(a_i * (1.0 / math.sqrt(2.0)))) * half_gate
            w_scr[r0:r0 + PEER_NKEYS, :] = w_i.astype(BF16)
        done = (sb + 1) * PEER_SUB
        if done % PEER_ACC_CHUNK == 0:
            c0 = done - PEER_ACC_CHUNK
            acc_ref[...] += _dot(vt_ref[:, c0:done], w_scr[c0:done, :])

    @pl.when(eb == pl.num_programs(1) - 1)
    def _():
        o_ref[...] = x_ref[...] + gt_ref[...] * acc_ref[...].T


def _peer_dense(h_t, x, gate, u, v_t, layer, cnt1, c1, rank2, e2):
    d, n = h_t.shape
    n_e = u.shape[1]
    tb = min(PEER_TOK_BLOCK, n)
    eb = PEER_EXP_BLOCK
    tab = pl.BlockSpec((PEER_HEADS, PEER_NKEYS, tb), lambda i, e: (0, 0, i))
    return pl.pallas_call(
        _dense_kernel,
        grid=(n // tb, n_e // eb),
        in_specs=[pl.BlockSpec((d, tb), lambda i, e: (0, i)),
                  pl.BlockSpec((tb, d), lambda i, e: (i, 0)),
                  pl.BlockSpec((1, d), lambda i, e: (0, 0)),
                  pl.BlockSpec((None, eb, d), lambda i, e: (layer, e, 0)),
                  pl.BlockSpec((None, d, eb), lambda i, e: (layer, 0, e)),
                  tab, tab, tab, tab],
        out_specs=pl.BlockSpec((tb, d), lambda i, e: (i, 0)),
        out_shape=jax.ShapeDtypeStruct((n, d), F32),
        scratch_shapes=[pltpu.VMEM((d, tb), F32), pltpu.VMEM((eb, tb), BF16)],
        compiler_params=_cparams(("arbitrary", "arbitrary")),
        name="peer_dense",
    )(h_t, x, gate, u, v_t, cnt1, c1, rank2, e2)


def _final_norm_kernel(x_ref, g_ref, o_ref):
    x = x_ref[...]
    o_ref[...] = x * lax.rsqrt(jnp.mean(x * x, axis=-1, keepdims=True) + EPS) * g_ref[...]


def _final_norm(x, g):
    n, d = x.shape
    tm = min(2 * ROW_BLOCK, n)
    return pl.pallas_call(
        _final_norm_kernel,
        grid=(n // tm,),
        in_specs=[pl.BlockSpec((tm, d), lambda i: (i, 0)), pl.BlockSpec((1, d), lambda i: (0, 0))],
        out_specs=pl.BlockSpec((tm, d), lambda i: (i, 0)),
        out_shape=jax.ShapeDtypeStruct((n, d), F32),
        compiler_params=_cparams(("arbitrary",)),
        name="final_norm",
    )(x, g)


def kernel(x, c, ctx, c_ctx, w_mod, b_mod, g_mix, g_ffn, w_in, na_rpb, s5_lam_re, s5_lam_im,
           s5_b_re, s5_b_im, s5_c_re, s5_c_im, s5_log_step, s5_d, s5_w_glu, sw_sink, w_branch,
           w_out, peer_w_q, peer_sub_keys, peer_u, peer_v, g_final):
    batch, seq, d = x.shape
    l_ctx = ctx.shape[1]
    depth = w_mod.shape[0]
    assert batch == 1 and seq % (NA_ROWS * GRID_W * NA_STEP_BLOCKS) == 0 and l_ctx % (S5_CHUNK * SUBLANES) == 0
    assert seq // GRID_W >= NA_KROWS and seq >= 3 * SW_BLOCK and seq % (SW_BLOCK * SW_STEP_BLOCKS) == 0

    cc = jnp.zeros((SUBLANES, d), F32).at[0].set(c[0]).at[1].set(c_ctx)
    mod = _mod_vectors(cc, w_mod, b_mod).reshape(depth, SUBLANES, 6, d)
    rope_tabs = _rope_tables(seq)
    row = lambda v: v.reshape(1, d)
    na_bias_lo, na_bias_hi = jax.vmap(_na_bias_tiles)(na_rpb)
    s5_prep = jax.vmap(_s5_prepare)(s5_lam_re, s5_lam_im, s5_b_re, s5_b_im, s5_c_re, s5_c_im, s5_log_step)
    u_all = peer_u.astype(BF16)
    vt_all = jnp.transpose(peer_v.astype(BF16), (0, 2, 1))

    xx, xc = x[0], ctx[0]
    for l in range(depth):
        need_ctx = l < depth - 1
        m_lat, m_ctx = mod[l, 0], mod[l, 1]
        w_in_l = w_in[l].astype(BF16)
        g_mix_l = row(g_mix[l])
        g_ffn_l = row(g_ffn[l])

        qa, ka, va, ub, qs, ks, vs, ga, gb, gs = _inproj(
            xx, g_mix_l, row(m_lat[0]), row(m_lat[1]), w_in_l, rope_tabs)
        qa_c, ka_c, va_c, ub_c, qs_c, ks_c, vs_c, ga_c, gb_c, gs_c = _inproj(
            xc, g_mix_l, row(m_ctx[0]), row(m_ctx[1]), w_in_l, None)

        ya = _na_attention(qa, ka, va, ka_c, va_c, na_bias_lo[l], na_bias_hi[l])
        ys = _sw_attention(qs, ks, vs, ks_c, vs_c, sw_sink[l])
        y5 = _s5_mixer_pre_glu(ub_c, ub, [a[l] for a in s5_prep], s5_d[l])

        w_glu_l = s5_w_glu[l].astype(BF16)
        w_branch_l = w_branch[l].astype(BF16)
        w_out_l = w_out[l].astype(BF16)
        xx, hx2, hx2_t = _merge(xx, ya, y5, l_ctx, ys, ga, gb, gs, w_glu_l, w_branch_l, w_out_l,
                                row(m_lat[2]), g_ffn_l, row(m_lat[3]), row(m_lat[4]))

        w_q_l = peer_w_q[l].astype(BF16)
        k1 = peer_sub_keys[l, 0].astype(BF16)
        k2 = peer_sub_keys[l, 1].astype(BF16)
        if need_ctx:
            ya_c = _ctx_attention(qa_c, ka_c, va_c, None)
            ys_c = _ctx_attention(qs_c, ks_c, vs_c, sw_sink[l])
            xc, hc2, hc2_t = _merge(xc, ya_c, y5, 0, ys_c, ga_c, gb_c, gs_c, w_glu_l, w_branch_l, w_out_l,
                                    row(m_ctx[2]), g_ffn_l, row(m_ctx[3]), row(m_ctx[4]))
            xc = _peer_dense(hc2_t, xc, row(m_ctx[5]), u_all, vt_all, l, *_peer_route(hc2, w_q_l, k1, k2))
        xx = _peer_dense(hx2_t, xx, row(m_lat[5]), u_all, vt_all, l, *_peer_route(hx2, w_q_l, k1, k2))

    return _final_norm(xx, row(g_final))[None]
```

```python
import functools
import math

import numpy as np
import jax
import jax.numpy as jnp
from jax import lax
from jax.experimental import pallas as pl
from jax.experimental.pallas import tpu as pltpu

F32 = jnp.float32
BF16 = jnp.bfloat16

GRID_W = 64
HEAD_DIM = 64
NA_WIN_H = 8
NA_WIN_W = 16
S5_GROUP = 16
S5_STATE = 64
SW_WINDOW = 128
ROPE_BASE = 10000.0
PEER_HEADS = 8
PEER_NKEYS = 128
PEER_QDIM = 256
PEER_TOPK = 16
EPS = 1e-6
NEG_INF = -1e30

LANES = 128
SUBLANES = 8
VMEM_LIMIT_BYTES = 56 * 1024 * 1024

ROW_BLOCK = 256
NA_ROWS = 4
NA_KROWS = NA_ROWS + NA_WIN_H - 1
NA_LANES = 256
NA_STEP_BLOCKS = 2
SW_BLOCK = 128
SW_STEP_BLOCKS = 4
S5_CHUNK = 16
S5_LG = LANES // S5_GROUP
PEER_ROUTE_BLOCK = 256
PEER_ROUTE_HEADS = 4
PEER_TOK_BLOCK = 512
PEER_EXP_BLOCK = 2048
PEER_SUB = 512
PEER_ACC_CHUNK = 1024


def _cparams(sem):
    return pltpu.CompilerParams(dimension_semantics=sem, vmem_limit_bytes=VMEM_LIMIT_BYTES)


def _dot(a, b):
    return jnp.dot(a, b, preferred_element_type=F32)


def _dot_nt(a, b):
    return lax.dot_general(a, b, (((1,), (1,)), ((), ())), preferred_element_type=F32)


def _gelu(x):
    return 0.5 * x * (1.0 + lax.erf(x * (1.0 / math.sqrt(2.0))))


def _mod_kernel(cc_ref, w_ref, b_ref, o_ref):
    a = cc_ref[...]
    a = a * jax.nn.sigmoid(a)
    o_ref[0] = _dot(a.astype(BF16), w_ref[0].astype(BF16)) + b_ref[0]


def _mod_vectors(cc, w_mod, b_mod):
    depth, d, n6 = w_mod.shape
    tn = 1024
    return pl.pallas_call(
        _mod_kernel,
        grid=(depth, n6 // tn),
        in_specs=[pl.BlockSpec((SUBLANES, d), lambda l, j: (0, 0)),
                  pl.BlockSpec((1, d, tn), lambda l, j: (l, 0, j)),
                  pl.BlockSpec((1, 1, tn), lambda l, j: (l, 0, j))],
        out_specs=pl.BlockSpec((1, SUBLANES, tn), lambda l, j: (l, 0, j)),
        out_shape=jax.ShapeDtypeStruct((depth, SUBLANES, n6), F32),
        compiler_params=_cparams(("arbitrary", "arbitrary")),
        name="mod_vectors",
    )(cc, w_mod, b_mod.reshape(depth, 1, n6))


def _norm_mod(x, g, shift, scale):
    y = x * lax.rsqrt(jnp.mean(x * x, axis=-1, keepdims=True) + EPS)
    return (y * g) * (1.0 + scale) + shift


def _rope(x, cos, sin, lane_lo):
    up = pltpu.roll(x, LANES - 16, 1)
    dn = pltpu.roll(x, 16, 1)
    return x * cos + jnp.where(lane_lo, up, dn) * sin


def _inproj_kernel(x_ref, g_ref, sh_ref, sc_ref, w_ref, *rest, bw, kvw, d_model, rope):
    if rope:
        cos_ref, sin_ref = rest[:2]
        rest = rest[2:]
    qa_ref, ka_ref, va_ref, ub_ref, qs_ref, ks_ref, vs_ref, ga_ref, gb_ref, gs_ref = rest
    h = _norm_mod(x_ref[...], g_ref[...], sh_ref[...], sc_ref[...]).astype(BF16)

    def proj(c0, width):
        return _dot(h, w_ref[:, c0:c0 + width])

    c = 0
    qa_ref[...] = proj(c, bw).astype(BF16); c += bw
    ka_ref[...] = proj(c, bw).astype(BF16); c += bw
    va_ref[...] = proj(c, bw).astype(BF16); c += bw
    ub = proj(c, bw); c += bw
    for t in range(bw // LANES):
        ub_ref[t] = ub[:, t * LANES:(t + 1) * LANES]
    qs = proj(c, bw); c += bw
    ks = proj(c, kvw); c += kvw
    vs_ref[...] = proj(c, kvw).astype(BF16); c += kvw
    if rope:
        cos = cos_ref[...]
        sin = sin_ref[...]
        lane = lax.broadcasted_iota(jnp.int32, cos.shape, 1)
        lane_lo = (lane % 32) < 16
        for p in range(bw // LANES):
            sl = slice(p * LANES, (p + 1) * LANES)
            qs_ref[:, sl] = _rope(qs[:, sl], cos, sin, lane_lo).astype(BF16)
        for p in range(kvw // LANES):
            sl = slice(p * LANES, (p + 1) * LANES)
            ks_ref[:, sl] = _rope(ks[:, sl], cos, sin, lane_lo).astype(BF16)
    else:
        qs_ref[...] = qs.astype(BF16)
        ks_ref[...] = ks.astype(BF16)
    ga_ref[...] = jax.nn.sigmoid(proj(c, d_model)).astype(BF16); c += d_model
    gb_ref[...] = jax.nn.sigmoid(proj(c, d_model)).astype(BF16); c += d_model
    gs_ref[...] = jax.nn.sigmoid(proj(c, d_model)).astype(BF16)


def _inproj(x, g, shift, scale, w_in, rope_tabs):
    n, d = x.shape
    bw = d // 2
    kvw = bw // 4
    tm = min(ROW_BLOCK, n)
    rope = rope_tabs is not None
    row = lambda i: (i, 0)
    fixed = lambda i: (0, 0)
    in_specs = [pl.BlockSpec((tm, d), row), pl.BlockSpec((1, d), fixed), pl.BlockSpec((1, d), fixed),
                pl.BlockSpec((1, d), fixed), pl.BlockSpec(w_in.shape, fixed)]
    args = [x, g, shift, scale, w_in]
    if rope:
        in_specs += [pl.BlockSpec((tm, LANES), row)] * 2
        args += list(rope_tabs)
    widths = [bw, bw, bw, bw, bw, kvw, kvw, d, d, d]
    dtypes = [BF16, BF16, BF16, F32, BF16, BF16, BF16, BF16, BF16, BF16]
    s5_slot = 3
    return pl.pallas_call(
        functools.partial(_inproj_kernel, bw=bw, kvw=kvw, d_model=d, rope=rope),
        grid=(n // tm,),
        in_specs=in_specs,
        out_specs=[pl.BlockSpec((bw // LANES, tm, LANES), lambda i: (0, i, 0)) if k == s5_slot
                   else pl.BlockSpec((tm, w), row) for k, w in enumerate(widths)],
        out_shape=[jax.ShapeDtypeStruct((bw // LANES, n, LANES) if k == s5_slot else (n, w), dt)
                   for k, (w, dt) in enumerate(zip(widths, dtypes))],
        compiler_params=_cparams(("arbitrary",)),
        name="inproj",
    )(*args)


def _rope_tables(seq):
    t = jnp.arange(seq)
    half = HEAD_DIM // 4
    inv = ROPE_BASE ** (-jnp.arange(half, dtype=F32) / half)
    sign = jnp.concatenate([-jnp.ones((half,), F32), jnp.ones((half,), F32)])

    def axis_tabs(pos):
        ang = pos.astype(F32)[:, None] * inv[None, :]
        c = jnp.cos(ang)
        s = jnp.sin(ang)
        return jnp.concatenate([c, c], -1), jnp.concatenate([s, s], -1) * sign

    cr, sr = axis_tabs(t // GRID_W)
    cc, sc = axis_tabs(t % GRID_W)
    cos = jnp.concatenate([cr, cc], -1)
    sin = jnp.concatenate([sr, sc], -1)
    return jnp.tile(cos, (1, LANES // HEAD_DIM)), jnp.tile(sin, (1, LANES // HEAD_DIM))


def _na_patterns(rows):
    kh = NA_WIN_H
    nb = rows // NA_ROWS
    kr0s, sigs = [], []
    for b in range(nb):
        r_lo = b * NA_ROWS
        kr0 = int(np.clip(r_lo - kh // 2, 0, rows - NA_KROWS))
        sig = tuple((int(np.clip(r - kh // 2, 0, rows - kh)) - kr0, r - kr0)
                    for r in range(r_lo, r_lo + NA_ROWS))
        kr0s.append(kr0)
        sigs.append(sig)
    uniq = sorted(set(sigs))
    ids = np.array([uniq.index(s) for s in sigs], np.int32)
    return np.array(kr0s, np.int32), ids, uniq


NA_TILE_MASKED = 2 * NA_WIN_H - 1
NA_TILE_ZERO = 2 * NA_WIN_H


def _na_tile_index(uniq):
    n_slots = 2 * ((NA_KROWS + 2) // 2)
    idx = np.full((len(uniq), NA_ROWS, n_slots), NA_TILE_ZERO, np.int32)
    for p, sig in enumerate(uniq):
        for rq, (r0_rel, r_rel) in enumerate(sig):
            for kk in range(NA_KROWS):
                inside = r0_rel <= kk < r0_rel + NA_WIN_H
                idx[p, rq, kk] = kk - r_rel + NA_WIN_H - 1 if inside else NA_TILE_MASKED
    return idx.reshape(-1), n_slots


def _na_bias_tiles(rpb):
    kw = NA_WIN_W
    cols = np.arange(GRID_W)
    col_start = np.clip(cols - kw // 2, 0, GRID_W - kw)
    col_sel = (cols[None, None, :] - cols[None, :, None] + kw - 1
               == np.arange(2 * kw - 1)[:, None, None])
    col_ok = ((cols[None, :] >= col_start[:, None]) & (cols[None, :] < col_start[:, None] + kw))
    by_col = jnp.einsum('hrv,vcj->hrcj', rpb.astype(F32), jnp.asarray(col_sel, F32),
                        precision=lax.Precision.HIGHEST)
    by_col = jnp.where(jnp.asarray(col_ok), by_col, NEG_INF)
    n_h = rpb.shape[0]
    tiles = jnp.concatenate([by_col, jnp.full((n_h, 1, GRID_W, GRID_W), NEG_INF, F32),
                             jnp.zeros((n_h, 1, GRID_W, GRID_W), F32)], axis=1)
    zero = jnp.zeros_like(tiles)
    return jnp.concatenate([tiles, zero], axis=-1), jnp.concatenate([zero, tiles], axis=-1)


def _na_kernel(kr0_ref, pat_ref, tile_ref, q_ref, k_ref, v_ref, kc_ref, vc_ref, lo_ref, hi_ref, o_ref, *,
               n_slots):
    tq = NA_ROWS * GRID_W
    for qb in range(NA_STEP_BLOCKS):
        rows = slice(qb * tq, (qb + 1) * tq)
        _na_block(pl.program_id(1) * NA_STEP_BLOCKS + qb, kr0_ref, pat_ref, tile_ref, q_ref.at[rows, :],
                  k_ref, v_ref, kc_ref, vc_ref, lo_ref, hi_ref, o_ref.at[rows, :], n_slots=n_slots)


def _na_block(b, kr0_ref, pat_ref, tile_ref, q_ref, k_ref, v_ref, kc_ref, vc_ref, lo_ref, hi_ref, o_ref, *,
              n_slots):
    start = pl.multiple_of(kr0_ref[b] * GRID_W, GRID_W)
    nk = NA_KROWS * GRID_W
    scale = HEAD_DIM ** -0.5
    one = jnp.ones((), BF16)
    hp = LANES // HEAD_DIM
    tile_base = pat_ref[b] * (NA_ROWS * n_slots)

    def bias_of(head):
        row_blocks = []
        for rq in range(NA_ROWS):
            pairs = []
            for j in range(n_slots // 2):
                even = tile_ref[tile_base + rq * n_slots + 2 * j]
                odd = tile_ref[tile_base + rq * n_slots + 2 * j + 1]
                pairs.append(lo_ref[head, even] + hi_ref[head, odd])
            row_blocks.append(jnp.concatenate(pairs, axis=-1))
        return jnp.concatenate(row_blocks, axis=0)

    for lt in range(q_ref.shape[1] // LANES):
        lanes = slice(lt * LANES, (lt + 1) * LANES)
        q = q_ref[:, lanes]
        k_all = jnp.concatenate([k_ref[pl.ds(start, nk), lanes], kc_ref[:, lanes]], axis=0)
        v_all = jnp.concatenate([v_ref[pl.ds(start, nk), lanes], vc_ref[:, lanes]], axis=0)
        lane = lax.broadcasted_iota(jnp.int32, v_all.shape, 1)
        ext = []
        for h in range(hp):
            sl = slice(h * HEAD_DIM, (h + 1) * HEAD_DIM)
            own = (lane >= h * HEAD_DIM) & (lane < (h + 1) * HEAD_DIM)
            s = _dot_nt(q[:, sl], k_all[:, sl]) * scale
            n_biased = n_slots * GRID_W
            s = jnp.concatenate([s[:, :n_biased] + bias_of(lt * hp + h), s[:, n_biased:]], axis=-1)
            p = jnp.exp(s - jnp.max(s, axis=-1, keepdims=True)).astype(BF16)
            ext.append(_dot(p, jnp.where(own, v_all, one)))
        out_lane = lax.broadcasted_iota(jnp.int32, ext[0].shape, 1)
        low = out_lane < HEAD_DIM
        num = jnp.where(low, ext[0], ext[1])
        den = pltpu.roll(jnp.where(low, ext[1], ext[0]), HEAD_DIM, 1)
        o_ref[:, lanes] = (num / den).astype(o_ref.dtype)


def _na_attention(q, k, v, kc, vc, bias_lo, bias_hi):
    s, bw = q.shape
    l = kc.shape[0]
    rows = s // GRID_W
    kr0s, ids, uniq = _na_patterns(rows)
    tile_idx, n_slots = _na_tile_index(uniq)
    assert n_slots * GRID_W <= NA_KROWS * GRID_W + l
    tq = NA_ROWS * GRID_W * NA_STEP_BLOCKS
    hp = NA_LANES // HEAD_DIM
    tiles = pl.BlockSpec((hp,) + bias_lo.shape[1:], lambda p, b, *_: (p, 0, 0, 0))
    grid_spec = pltpu.PrefetchScalarGridSpec(
        num_scalar_prefetch=3,
        grid=(bw // NA_LANES, rows // (NA_ROWS * NA_STEP_BLOCKS)),
        in_specs=[pl.BlockSpec((tq, NA_LANES), lambda p, b, *_: (b, p)),
                  pl.BlockSpec((s, NA_LANES), lambda p, b, *_: (0, p)),
                  pl.BlockSpec((s, NA_LANES), lambda p, b, *_: (0, p)),
                  pl.BlockSpec((l, NA_LANES), lambda p, b, *_: (0, p)),
                  pl.BlockSpec((l, NA_LANES), lambda p, b, *_: (0, p)),
                  tiles, tiles],
        out_specs=pl.BlockSpec((tq, NA_LANES), lambda p, b, *_: (b, p)),
    )
    return pl.pallas_call(
        functools.partial(_na_kernel, n_slots=n_slots),
        grid_spec=grid_spec,
        out_shape=jax.ShapeDtypeStruct((s, bw), BF16),
        compiler_params=_cparams(("arbitrary", "arbitrary")),
        name="na_attention",
    )(jnp.asarray(kr0s), jnp.asarray(ids), jnp.asarray(tile_idx), q, k, v, kc, vc, bias_lo, bias_hi)


def _ctx_attn_kernel(*refs, n_heads, n_rep, has_sink):
    if has_sink:
        sink_ref, q_ref, k_ref, v_ref, o_ref = refs
    else:
        q_ref, k_ref, v_ref, o_ref = refs
    scale = HEAD_DIM ** -0.5
    q = q_ref[...]
    k = k_ref[...]
    v = v_ref[...]
    outs = []
    for h in range(n_heads):
        g = h // n_rep
        sl = slice(h * HEAD_DIM, (h + 1) * HEAD_DIM)
        gl = slice(g * HEAD_DIM, (g + 1) * HEAD_DIM)
        s = _dot_nt(q[:, sl], k[:, gl]) * scale
        m = jnp.max(s, axis=-1, keepdims=True)
        if has_sink:
            m = jnp.maximum(m, sink_ref[h])
        p = jnp.exp(s - m)
        den = jnp.sum(p, axis=-1, keepdims=True)
        if has_sink:
            den = den + jnp.exp(sink_ref[h] - m)
        outs.append(_dot(p.astype(BF16), v[:, gl]) / den)
    o_ref[...] = jnp.concatenate(outs, axis=-1).astype(o_ref.dtype)


def _ctx_attention(q, k, v, sink):
    l, bw = q.shape
    n_heads = bw // HEAD_DIM
    n_rep = n_heads // (k.shape[1] // HEAD_DIM)
    has_sink = sink is not None
    full = lambda a: pl.BlockSpec(a.shape, lambda i: (0, 0))
    in_specs = [full(q), full(k), full(v)]
    args = [q, k, v]
    if has_sink:
        in_specs = [pl.BlockSpec(memory_space=pltpu.SMEM)] + in_specs
        args = [sink] + args
    return pl.pallas_call(
        functools.partial(_ctx_attn_kernel, n_heads=n_heads, n_rep=n_rep, has_sink=has_sink),
        grid=(1,),
        in_specs=in_specs,
        out_specs=pl.BlockSpec((l, bw), lambda i: (0, 0)),
        out_shape=jax.ShapeDtypeStruct((l, bw), BF16),
        compiler_params=_cparams(("arbitrary",)),
        name="ctx_attention",
    )(*args)


def _swa_kernel(sink_ref, q_ref, k_ref, v_ref, kc_ref, vc_ref, o_ref, *, seq, n_kv, n_rep):
    for qb in range(SW_STEP_BLOCKS):
        rows = slice(qb * SW_BLOCK, (qb + 1) * SW_BLOCK)
        _swa_block(pl.program_id(0) * SW_STEP_BLOCKS + qb, sink_ref, q_ref[rows, :], k_ref, v_ref,
                   kc_ref, vc_ref, o_ref.at[rows, :], seq=seq, n_kv=n_kv, n_rep=n_rep)


def _swa_block(n, sink_ref, q, k_ref, v_ref, kc_ref, vc_ref, o_ref, *, seq, n_kv, n_rep):
    nk = 3 * SW_BLOCK
    scale = HEAD_DIM ** -0.5
    ws = jnp.clip((n - 1) * SW_BLOCK, 0, seq - nk)
    ws = pl.multiple_of(ws, SW_BLOCK)
    n_all = nk + kc_ref.shape[0]
    k_all = jnp.concatenate([k_ref[pl.ds(ws, nk), :], kc_ref[...]], axis=0)
    v_all = jnp.concatenate([v_ref[pl.ds(ws, nk), :], vc_ref[...]], axis=0)
    qpos = n * SW_BLOCK + lax.broadcasted_iota(jnp.int32, (SW_BLOCK, n_all), 0)
    col = lax.broadcasted_iota(jnp.int32, (SW_BLOCK, n_all), 1)
    visible = (col >= nk) | (jnp.abs(ws + col - qpos) <= SW_WINDOW)
    maskb = jnp.where(visible, 0.0, NEG_INF).astype(F32)
    lane = lax.broadcasted_iota(jnp.int32, v_all.shape, 1)
    one = jnp.ones((), BF16)
    outs = [None] * (n_kv * n_rep)
    for g in range(n_kv):
        gl = slice(g * HEAD_DIM, (g + 1) * HEAD_DIM)
        own = (lane >= g * HEAD_DIM) & (lane < (g + 1) * HEAD_DIM)
        qg = jnp.concatenate([q[:, (g * n_rep + r) * HEAD_DIM:(g * n_rep + r + 1) * HEAD_DIM]
                              for r in range(n_rep)], axis=0)
        s = _dot_nt(qg, k_all[:, gl]) * scale
        s = (s.reshape(n_rep, SW_BLOCK, n_all) + maskb[None]).reshape(n_rep * SW_BLOCK, n_all)
        sink = jnp.concatenate([jnp.full((SW_BLOCK, 1), sink_ref[g * n_rep + r], F32)
                                for r in range(n_rep)], axis=0)
        m = jnp.maximum(jnp.max(s, axis=-1, keepdims=True), sink)
        p = jnp.exp(s - m).astype(BF16)
        ext = _dot(p, jnp.where(own, v_all, one))
        den = pltpu.roll(ext, HEAD_DIM, 1) + jnp.exp(sink - m)
        o = ext / den
        for r in range(n_rep):
            outs[g * n_rep + r] = o[r * SW_BLOCK:(r + 1) * SW_BLOCK, gl]
    o_ref[...] = jnp.concatenate(outs, axis=-1).astype(o_ref.dtype)


def _sw_attention(q, k, v, kc, vc, sink):
    s, bw = q.shape
    l = kc.shape[0]
    kvw = k.shape[1]
    n_kv = kvw // HEAD_DIM
    n_rep = (bw // HEAD_DIM) // n_kv
    assert kvw == LANES
    return pl.pallas_call(
        functools.partial(_swa_kernel, seq=s, n_kv=n_kv, n_rep=n_rep),
        grid=(s // (SW_BLOCK * SW_STEP_BLOCKS),),
        in_specs=[pl.BlockSpec(memory_space=pltpu.SMEM),
                  pl.BlockSpec((SW_BLOCK * SW_STEP_BLOCKS, bw), lambda n: (n, 0)),
                  pl.BlockSpec((s, kvw), lambda n: (0, 0)),
                  pl.BlockSpec((s, kvw), lambda n: (0, 0)),
                  pl.BlockSpec((l, kvw), lambda n: (0, 0)),
                  pl.BlockSpec((l, kvw), lambda n: (0, 0))],
        out_specs=pl.BlockSpec((SW_BLOCK * SW_STEP_BLOCKS, bw), lambda n: (n, 0)),
        out_shape=jax.ShapeDtypeStruct((s, bw), BF16),
        compiler_params=_cparams(("arbitrary",)),
        name="sw_attention",
    )(sink, q, k, v, kc, vc)


def _s5_prepare(lam_re, lam_im, b_re, b_im, c_re, c_im, log_step):
    tc = S5_CHUNK
    lam = lax.complex(lam_re.astype(F32), lam_im.astype(F32))
    dt = jnp.exp(log_step.astype(F32))[..., None]
    lam_dt = lam * dt
    lam_bar = jnp.exp(lam_dt)
    b_bar = ((lam_bar - 1.0) / lam)[..., None] * lax.complex(b_re.astype(F32), b_im.astype(F32))
    c_mat = lax.complex(c_re.astype(F32), c_im.astype(F32))
    kk = jnp.arange(tc + 1, dtype=F32)
    pw = jnp.exp(lam_dt[None] * kk[:, None, None, None])
    n_g, n_p, n_h = b_bar.shape[1:]

    bt = jnp.transpose(b_bar, (1, 0, 3, 2))
    bt = jnp.stack([bt.real, bt.imag], axis=2)
    cw = c_mat[None] * pw[:tc, :, :, None, :]
    cw = jnp.stack([cw[:, 0], cw[::-1, 1]], axis=1)
    cwt = jnp.transpose(cw, (2, 1, 4, 0, 3)).reshape(n_g, 2, n_p, tc * n_h)
    cwt = jnp.stack([cwt.real, cwt.imag], axis=2)
    t_i = np.arange(tc)

    def cat_ri(z):
        return jnp.concatenate([z.real, z.imag], axis=-1)

    w_f = pw[tc - 1 - t_i, 0][:, :, None, :] * jnp.transpose(b_bar[0], (0, 2, 1))[None]
    w_r = pw[t_i, 1][:, :, None, :] * jnp.transpose(b_bar[1], (0, 2, 1))[None]
    w_f = jnp.transpose(cat_ri(w_f), (1, 0, 2, 3)).reshape(n_g, tc * n_h, 2 * n_p)
    w_r = jnp.transpose(cat_ri(w_r), (1, 0, 2, 3)).reshape(n_g, tc * n_h, 2 * n_p)
    w_cat = jnp.concatenate([w_f, w_r], axis=-1).astype(BF16)

    def v_of(c_dir, pw_sel):
        cp = c_dir[None] * pw_sel[:, :, None, :]
        v = jnp.concatenate([cp.real, -cp.imag], axis=-1)
        return jnp.transpose(v, (1, 3, 0, 2)).reshape(n_g, 2 * n_p, tc * n_h)
    v_cat = jnp.concatenate([v_of(c_mat[0], pw[t_i + 1, 0]), v_of(c_mat[1], pw[tc - t_i, 1])],
                            axis=1).astype(BF16)

    def coef(k):
        a = jnp.exp(lam_dt * (tc * k))
        return jnp.stack([jnp.concatenate([a.real, a.real], -1),
                          jnp.concatenate([-a.imag, a.imag], -1)], axis=2)
    a_log = jnp.stack([coef(1.0), coef(2.0), coef(4.0)], axis=2)
    a_car = jnp.stack([coef(float(j)) for j in range(1, SUBLANES + 1)], axis=3)
    a_car = jnp.stack([a_car[0], a_car[1][:, :, ::-1]], axis=0)
    return _s5_expand(bt, cwt, w_cat, v_cat, jnp.transpose(a_log, (1, 0, 2, 3, 4)),
                      jnp.transpose(a_car, (1, 0, 2, 3, 4)))


def _s5_expand(bt, cwt, w_cat, v_cat, a_log, a_car):
    n_g = bt.shape[0]
    n_t = n_g // S5_LG
    n_state = 2 * S5_STATE
    by_tile = lambda a: a.reshape((n_t, S5_LG) + a.shape[1:])

    def lanes_of(a):
        a = jnp.moveaxis(by_tile(a), 1, -2)
        return a.reshape(a.shape[:-2] + (S5_LG * n_state,))
    return by_tile(bt), by_tile(cwt), by_tile(w_cat), by_tile(v_cat), lanes_of(a_log), lanes_of(a_car)


def _s5_lane_perm():
    r = np.arange(S5_CHUNK * LANES)
    t, a, h = r // LANES, (r % LANES) // S5_GROUP, r % S5_GROUP
    dest = a * (S5_CHUNK * S5_GROUP) + t * S5_GROUP + h
    return (jnp.asarray(dest)[:, None] == jnp.arange(S5_CHUNK * LANES)[None, :]).astype(BF16)


def _chunk_rows(tok_ref):
    n_rows = tok_ref.shape[1] // S5_CHUNK
    return jnp.concatenate([tok_ref[0, pl.ds(t, n_rows, stride=S5_CHUNK), :] for t in range(S5_CHUNK)], axis=-1)


def _s5_in_kernel(u_ref, perm_ref, bt_ref, cwt_ref, w_ref, yp_ref, z_ref, m_scr):
    width = S5_CHUNK * S5_GROUP
    n_state = 2 * S5_STATE
    half = S5_LG * n_state

    @pl.when(pl.program_id(1) == 0)
    def _():
        lane = lax.broadcasted_iota(jnp.int32, (S5_GROUP, width), 1)
        def lag_kernels(a, direction):
            hp = lax.Precision.HIGHEST
            return (jnp.dot(bt_ref[0, a, direction, 0], cwt_ref[0, a, direction, 0], precision=hp,
                            preferred_element_type=F32)
                    - jnp.dot(bt_ref[0, a, direction, 1], cwt_ref[0, a, direction, 1], precision=hp,
                              preferred_element_type=F32))

        for a in range(S5_LG):
            k_fwd = lag_kernels(a, 0)
            k_rev = lag_kernels(a, 1)
            for tau in range(S5_CHUNK):
                right = tau * S5_GROUP
                left = (S5_CHUNK - 1 - tau) * S5_GROUP
                blk = (jnp.where(lane >= right, pltpu.roll(k_fwd, right, 1), 0.0)
                       + jnp.where(lane < width - left, pltpu.roll(k_rev, (width - left) % width, 1), 0.0))
                m_scr[a, tau * S5_GROUP:(tau + 1) * S5_GROUP, 0:width] = blk.astype(BF16)
            m_scr[a, :, width:] = w_ref[0, a]

    up = _dot(_chunk_rows(u_ref).astype(BF16), perm_ref[...]).astype(BF16)
    for a in range(S5_LG):
        zz = _dot(up[:, a * width:(a + 1) * width], m_scr[a])
        yp_ref[0, :, a * width:(a + 1) * width] = zz[:, :width]
        z_ref[0, :, a * n_state:(a + 1) * n_state] = zz[:, width:width + n_state]
        z_ref[0, :, half + a * n_state:half + (a + 1) * n_state] = zz[:, width + n_state:]


def _s5_scan_kernel(z_ref, alog_ref, acar_ref, e_ref, *, nc, nc_ctx):
    n_state = 2 * S5_STATE
    half = z_ref.shape[2] // 2
    nblk = nc // SUBLANES
    nblk_ctx = nc_ctx // SUBLANES
    row = lax.broadcasted_iota(jnp.int32, (SUBLANES, n_state), 0)

    def cmul(a1, a2, s):
        return a1 * s + a2 * pltpu.roll(s, S5_STATE, 1)

    def step(t, carry):
        blk = (t, jnp.where(t < nblk_ctx, nblk_ctx - 1 - t, nblk + nblk_ctx - 1 - t))
        new = []
        for d in range(2):
            r0 = pl.multiple_of(blk[d] * SUBLANES, SUBLANES)
            for j in range(half // n_state):
                tile = slice(j * n_state, (j + 1) * n_state)
                lanes = slice(d * half + j * n_state, d * half + (j + 1) * n_state)
                z = z_ref[0, pl.ds(r0, SUBLANES), lanes]
                c = carry[len(new)]
                for i, sh in enumerate((1, 2, 4)):
                    if d == 0:
                        zs = jnp.where(row >= sh, pltpu.roll(z, sh, 0), 0.0)
                    else:
                        zs = jnp.where(row < SUBLANES - sh, pltpu.roll(z, SUBLANES - sh, 0), 0.0)
                    z = z + cmul(alog_ref[0, d, i, 0:1, tile], alog_ref[0, d, i, 1:2, tile], zs)
                cb = jnp.broadcast_to(c, (SUBLANES, n_state))
                s = z + cmul(acar_ref[0, d, 0, :, tile], acar_ref[0, d, 1, :, tile], cb)
                if d == 0:
                    e = jnp.where(row >= 1, pltpu.roll(s, 1, 0), cb)
                    c_new = s[SUBLANES - 1:SUBLANES, :]
                else:
                    e = jnp.where(row < SUBLANES - 1, pltpu.roll(s, SUBLANES - 1, 0), cb)
                    c_new = s[0:1, :]
                e_ref[0, pl.ds(r0, SUBLANES), lanes] = e
                new.append(c_new)
        return tuple(new)

    init = tuple(jnp.zeros((1, n_state), F32) for _ in range(2 * (half // n_state)))
    lax.fori_loop(0, nblk, step, init, unroll=2)


def _s5_out_kernel(e_ref, yp_ref, u_ref, d_ref, perm_ref, v_ref, y_ref):
    width = S5_CHUNK * S5_GROUP
    n_state = 2 * S5_STATE
    half = S5_LG * n_state
    parts = []
    for a in range(S5_LG):
        e_a = jnp.concatenate([e_ref[0, :, a * n_state:(a + 1) * n_state],
                               e_ref[0, :, half + a * n_state:half + (a + 1) * n_state]], axis=-1)
        parts.append(yp_ref[0, :, a * width:(a + 1) * width] + _dot(e_a.astype(BF16), v_ref[0, a]))
    y = jnp.concatenate(parts, axis=-1)
    hi = y.astype(BF16)
    lo = (y - hi.astype(F32)).astype(BF16)
    perm = perm_ref[...]
    y = _dot_nt(hi, perm) + _dot_nt(lo, perm) + _chunk_rows(u_ref) * d_ref[0]
    n_rows = y.shape[0]
    for t in range(S5_CHUNK):
        y_ref[0, pl.ds(t, n_rows, stride=S5_CHUNK), :] = y[:, t * LANES:(t + 1) * LANES]


def _s5_rows(nc):
    return max(r for r in range(2 * SUBLANES, 257, 2 * SUBLANES) if nc % r == 0)


def _s5_mixer_pre_glu(u_ctx, u_lat, prep, d_skip):
    bt, cwt, w_cat, v_cat, a_log, a_car = prep
    n_t, l, _ = u_ctx.shape
    s = u_lat.shape[1]
    nc, nc_ctx = (l + s) // S5_CHUNK, l // S5_CHUNK
    width = S5_CHUNK * LANES
    n_z = 2 * S5_LG * 2 * S5_STATE
    u = jnp.concatenate([u_ctx, u_lat], axis=1)
    d_rows = jnp.tile(d_skip.astype(F32).reshape(n_t, 1, LANES), (1, 1, S5_CHUNK))
    perm = _s5_lane_perm()
    rb = _s5_rows(nc)
    rows = lambda w: pl.BlockSpec((1, rb, w), lambda g, r: (g, r, 0))
    per_tile = lambda a: pl.BlockSpec((1,) + a.shape[1:], lambda g, *_: (g,) + (0,) * (a.ndim - 1))
    whole = lambda a: pl.BlockSpec(a.shape, lambda g, r: (0,) * a.ndim)
    tokens = pl.BlockSpec((1, rb * S5_CHUNK, LANES), lambda g, r: (g, r, 0))
    width_g = S5_CHUNK * S5_GROUP
    yp, z = pl.pallas_call(
        _s5_in_kernel,
        grid=(n_t, nc // rb),
        in_specs=[tokens, whole(perm), per_tile(bt), per_tile(cwt), per_tile(w_cat)],
        out_specs=[rows(width), rows(n_z)],
        out_shape=[jax.ShapeDtypeStruct((n_t, nc, width), F32), jax.ShapeDtypeStruct((n_t, nc, n_z), F32)],
        scratch_shapes=[pltpu.VMEM((S5_LG, width_g, width_g + 4 * S5_STATE), BF16)],
        compiler_params=_cparams(("arbitrary", "arbitrary")),
        name="s5_in",
    )(u, perm, bt, cwt, w_cat)
    e = pl.pallas_call(
        functools.partial(_s5_scan_kernel, nc=nc, nc_ctx=nc_ctx),
        grid=(n_t,),
        in_specs=[per_tile(z), per_tile(a_log), per_tile(a_car)],
        out_specs=per_tile(z),
        out_shape=jax.ShapeDtypeStruct(z.shape, F32),
        compiler_params=_cparams(("arbitrary",)),
        name="s5_scan",
    )(z, a_log, a_car)
    y = pl.pallas_call(
        _s5_out_kernel,
        grid=(n_t, nc // rb),
        in_specs=[rows(n_z), rows(width), tokens, per_tile(d_rows), whole(perm), per_tile(v_cat)],
        out_specs=tokens,
        out_shape=jax.ShapeDtypeStruct((n_t, l + s, LANES), F32),
        compiler_params=_cparams(("arbitrary", "arbitrary")),
        name="s5_out",
    )(e, yp, u, d_rows, perm, v_cat)
    return y


def _merge_kernel(x_ref, ya_ref, y5_ref, ys_ref, ga_ref, gb_ref, gs_ref, wglu_ref, wb_ref, wo_ref,
                  gt_ref, g_ref, sh_ref, sc_ref, xo_ref, h_ref, ht_ref):
    y5 = jnp.concatenate([y5_ref[t] for t in range(y5_ref.shape[0])], axis=-1)
    z = _gelu(y5)
    yb = z * jax.nn.sigmoid(_dot(z.astype(BF16), wglu_ref[...]))
    m = (ga_ref[...].astype(F32) * _dot(ya_ref[...], wb_ref[0])
         + gb_ref[...].astype(F32) * _dot(yb.astype(BF16), wb_ref[1])
         + gs_ref[...].astype(F32) * _dot(ys_ref[...], wb_ref[2]))
    x = x_ref[...] + gt_ref[...] * _dot(m.astype(BF16), wo_ref[...])
    xo_ref[...] = x
    h = _norm_mod(x, g_ref[...], sh_ref[...], sc_ref[...])
    h_ref[...] = h.astype(BF16)
    ht_ref[...] = h.T.astype(BF16)


def _merge(x, ya, y5, y5_row0, ys, ga, gb, gs, w_glu, w_branch, w_out, gate, g, shift, scale):
    n, d = x.shape
    bw = ya.shape[1]
    tm = min(ROW_BLOCK, n)
    assert y5_row0 % tm == 0
    row = lambda w: pl.BlockSpec((tm, w), lambda i: (i, 0))
    full = lambda a: pl.BlockSpec(a.shape, lambda i: (0,) * a.ndim)
    vec = pl.BlockSpec((1, d), lambda i: (0, 0))
    y5_rows = pl.BlockSpec((y5.shape[0], tm, LANES), lambda i: (0, i + y5_row0 // tm, 0))
    return pl.pallas_call(
        _merge_kernel,
        grid=(n // tm,),
        in_specs=[row(d), row(bw), y5_rows, row(bw), row(d), row(d), row(d),
                  full(w_glu), full(w_branch), full(w_out), vec, vec, vec, vec],
        out_specs=[row(d), row(d), pl.BlockSpec((d, tm), lambda i: (0, i))],
        out_shape=[jax.ShapeDtypeStruct((n, d), F32), jax.ShapeDtypeStruct((n, d), BF16),
                   jax.ShapeDtypeStruct((d, n), BF16)],
        compiler_params=_cparams(("arbitrary",)),
        name="merge",
    )(x, ya, y5, ys, ga, gb, gs, w_glu, w_branch, w_out, gate, g, shift, scale)


def _knock_out_16(s, order, exact, want_rank=True):
    rank = jnp.full(s.shape, float(PEER_TOPK), F32) if want_rank else None
    live = s
    vals = []
    for r in range(PEER_TOPK):
        m = jnp.max(live, axis=0, keepdims=True)
        hit = live == m
        if exact:
            first = jnp.min(jnp.where(hit, order, float(PEER_TOPK * PEER_NKEYS)), axis=0, keepdims=True)
            hit = order == first
        if want_rank:
            rank = jnp.where(hit, float(r), rank)
        live = jnp.where(hit, -jnp.inf, live)
        vals.append(m)
    out = (rank < float(PEER_TOPK)) if want_rank else (live != s)
    n_out = jnp.sum(jnp.where(out, 1.0, 0.0), axis=0, keepdims=True)
    return rank, jnp.concatenate(vals, axis=0), n_out


def _bf16_pair_word(x):
    hi = pltpu.bitcast(x.astype(BF16).astype(F32), jnp.uint32)
    return hi | (hi >> 16)


def _pair_tiles():
    tiles = [(0, 0, 8), (0, 8, 8)]
    for a in range(1, 8):
        tiles.append((a, 0, PEER_TOPK // (a + 1)))
    return tiles


def _route_kernel(h_ref, wq_ref, k1_ref, k2_ref, cnt_ref, c1_ref, rk_ref, e2_ref, q_scr):
    tb = h_ref.shape[0]
    q_scr[...] = _dot(h_ref[...], wq_ref[...]).astype(BF16)
    half = PEER_QDIM // 2
    iota = lax.broadcasted_iota(jnp.int32, (PEER_NKEYS, tb), 0).astype(F32)
    row8 = lax.broadcasted_iota(jnp.int32, (SUBLANES, tb), 0).astype(F32)
    k1 = k1_ref[...]
    k2 = k2_ref[...]

    def emit(hd, s1, s2, exact):
        rank1, v1, n1 = _knock_out_16(s1, iota, exact, want_rank=exact)
        rank2, v2, n2 = _knock_out_16(s2, iota, exact)
        e1v = jnp.exp(v1 - v1[0:1])
        e2v = jnp.exp(v2 - v2[0:1])
        tiles, flats, gates = [], [], []
        for a, b0, nv in _pair_tiles():
            c = v1[a:a + 1] + v2[b0:b0 + SUBLANES]
            tiles.append(jnp.where(row8 < nv, c, -jnp.inf))
            flats.append(a * PEER_TOPK + b0 + row8)
            gates.append(e1v[a:a + 1] * e2v[b0:b0 + SUBLANES])
        tiles.append(v1[SUBLANES:] + v2[0:1])
        flats.append((row8 + SUBLANES) * PEER_TOPK)
        gates.append(e1v[SUBLANES:] * e2v[0:1])
        cand = jnp.concatenate(tiles, axis=0)
        flat = jnp.concatenate(flats, axis=0)
        gate = jnp.concatenate(gates, axis=0)
        rank_c, _, n_c = _knock_out_16(cand, flat, exact)
        self_ = jnp.where(rank_c < float(PEER_TOPK), 1.0, 0.0)
        z = jnp.sum(self_ * gate, axis=0, keepdims=True)
        cnt = [self_[0:8].sum(axis=0, keepdims=True) + self_[8:16].sum(axis=0, keepdims=True)]
        for t in range(2, 9):
            cnt.append(self_[t * SUBLANES:(t + 1) * SUBLANES].sum(axis=0, keepdims=True))
        cnt = jnp.concatenate(cnt + [self_[9 * SUBLANES:]], axis=0)
        cnt1 = jnp.zeros((PEER_NKEYS, tb), F32)
        for a in range(PEER_TOPK):
            is_a = (rank1 == float(a)) if exact else (s1 == v1[a:a + 1])
            cnt1 = jnp.where(is_a, cnt[a:a + 1], cnt1)
        cnt_ref[hd] = _bf16_pair_word(cnt1)
        c1_ref[hd] = _bf16_pair_word(jnp.exp(s1 - v1[0:1]) * (0.5 / z))
        rk_ref[hd] = rank2.astype(BF16)
        e2_ref[hd] = jnp.exp(s2 - v2[0:1]).astype(BF16)
        want = float(PEER_TOPK)
        return jnp.where((n1 != want) | (n2 != want) | (n_c != want), 1.0, 0.0)

    def head_group(hg, _):
        scores = []
        for k in range(PEER_ROUTE_HEADS):
            hd = hg * PEER_ROUTE_HEADS + k
            c0 = pl.multiple_of(hd * PEER_QDIM, PEER_QDIM)
            s1 = _dot_nt(k1, q_scr[:, pl.ds(c0, half)])
            s2 = _dot_nt(k2, q_scr[:, pl.ds(c0 + half, half)])
            scores.append((hd, s1, s2))
        tied = [jnp.max(emit(hd, s1, s2, exact=False)) for hd, s1, s2 in scores]
        for (hd, s1, s2), t in zip(scores, tied):
            @pl.when(t > 0.0)
            def _():
                emit(hd, s1, s2, exact=True)
        return 0

    lax.fori_loop(0, PEER_HEADS // PEER_ROUTE_HEADS, head_group, 0)


def _peer_route(h, w_q, k1, k2):
    n, d = h.shape
    tb = min(PEER_ROUTE_BLOCK, n)
    qw = w_q.shape[1]
    out_blk = pl.BlockSpec((PEER_HEADS, PEER_NKEYS, tb), lambda i: (0, 0, i))
    shp = lambda dt: jax.ShapeDtypeStruct((PEER_HEADS, PEER_NKEYS, n), dt)
    return pl.pallas_call(
        _route_kernel,
        grid=(n // tb,),
        in_specs=[pl.BlockSpec((tb, d), lambda i: (i, 0)),
                  pl.BlockSpec((d, qw), lambda i: (0, 0)),
                  pl.BlockSpec(k1.shape, lambda i: (0, 0)),
                  pl.BlockSpec(k2.shape, lambda i: (0, 0))],
        out_specs=[out_blk] * 4,
        out_shape=[shp(jnp.uint32), shp(jnp.uint32), shp(BF16), shp(BF16)],
        scratch_shapes=[pltpu.VMEM((tb, qw), BF16)],
        compiler_params=_cparams(("arbitrary",)),
        name="peer_route",
    )(h, w_q, k1, k2)


def _dense_kernel(ht_ref, x_ref, gt_ref, u_ref, v_ref, cnt_ref, c1_ref, rk_ref, e2_ref, *rest, final_norm):
    if final_norm:
        gfin_ref, o_ref, acc_ref, w_scr = rest
    else:
        o_ref, acc_ref, w_scr = rest
    eb = pl.program_id(1)
    tb = ht_ref.shape[1]
    n_exp = u_ref.shape[0]
    n_sub = n_exp // PEER_SUB
    i_per_sub = PEER_SUB // PEER_NKEYS

    @pl.when(eb == 0)
    def _():
        acc_ref[...] = jnp.zeros_like(acc_ref)

    def scores(sb):
        return _dot(u_ref[sb * PEER_SUB:(sb + 1) * PEER_SUB, :], ht_ref[...])

    pk_rows = 2 * SUBLANES
    n_pk = PEER_NKEYS // pk_rows

    def row_tile(ref, hd, i):
        words = jnp.broadcast_to(ref[hd, pl.ds(i, 1), :], (SUBLANES, tb))
        return pltpu.bitcast(words, BF16)[None]

    a_next = scores(0)
    for sb in range(n_sub):
        a_cur = a_next
        if sb + 1 < n_sub:
            a_next = scores(sb + 1)
        for il in range(i_per_sub):
            i = eb * (n_exp // PEER_NKEYS) + sb * i_per_sub + il
            gsum = jnp.zeros((n_pk, pk_rows, tb), BF16)
            for hd in range(PEER_HEADS):
                cnt = row_tile(cnt_ref, hd, i)
                c1 = row_tile(c1_ref, hd, i)
                rk = rk_ref[hd].reshape(n_pk, pk_rows, tb)
                e2 = e2_ref[hd].reshape(n_pk, pk_rows, tb)
                gsum = gsum + jnp.where(rk < cnt, e2 * c1, jnp.zeros((), BF16))
            r0 = sb * PEER_SUB + il * PEER_NKEYS
            a_i = a_cur[il * PEER_NKEYS:(il + 1) * PEER_NKEYS]
            half_gate = gsum.reshape(PEER_NKEYS, tb).astype(F32)
            w_i = a_i * (1.0 + lax.erf(a_i * (1.0 / math.sqrt(2.0)))) * half_gate
            w_scr[r0:r0 + PEER_NKEYS, :] = w_i.astype(BF16)
        done = (sb + 1) * PEER_SUB
        if done % PEER_ACC_CHUNK == 0:
            c0 = done - PEER_ACC_CHUNK
            acc_ref[...] += lax.dot_general(v_ref[c0:done, :], w_scr[c0:done, :], (((0,), (0,)), ((), ())),
                                            preferred_element_type=F32)

    @pl.when(eb == pl.num_programs(1) - 1)
    def _():
        x = x_ref[...] + gt_ref[...] * acc_ref[...].T
        if final_norm:
            x = x * lax.rsqrt(jnp.mean(x * x, axis=-1, keepdims=True) + EPS) * gfin_ref[...]
        o_ref[...] = x


def _peer_dense(h_t, x, gate, u, v, layer, cnt1, c1, rank2, e2, final_g=None):
    d, n = h_t.shape
    n_e = u.shape[1]
    tb = min(PEER_TOK_BLOCK, n)
    eb = PEER_EXP_BLOCK
    tab = pl.BlockSpec((PEER_HEADS, PEER_NKEYS, tb), lambda i, e: (0, 0, i))
    vec = pl.BlockSpec((1, d), lambda i, e: (0, 0))
    experts = pl.BlockSpec((None, eb, d), lambda i, e: (layer, e, 0))
    final_norm = final_g is not None
    return pl.pallas_call(
        functools.partial(_dense_kernel, final_norm=final_norm),
        grid=(n // tb, n_e // eb),
        in_specs=[pl.BlockSpec((d, tb), lambda i, e: (0, i)),
                  pl.BlockSpec((tb, d), lambda i, e: (i, 0)),
                  vec, experts, experts, tab, tab, tab, tab] + ([vec] if final_norm else []),
        out_specs=pl.BlockSpec((tb, d), lambda i, e: (i, 0)),
        out_shape=jax.ShapeDtypeStruct((n, d), F32),
        scratch_shapes=[pltpu.VMEM((d, tb), F32), pltpu.VMEM((eb, tb), BF16)],
        compiler_params=_cparams(("arbitrary", "arbitrary")),
        name="peer_dense",
    )(h_t, x, gate, u, v, cnt1, c1, rank2, e2, *([final_g] if final_norm else []))


def kernel(x, c, ctx, c_ctx, w_mod, b_mod, g_mix, g_ffn, w_in, na_rpb, s5_lam_re, s5_lam_im,
           s5_b_re, s5_b_im, s5_c_re, s5_c_im, s5_log_step, s5_d, s5_w_glu, sw_sink, w_branch,
           w_out, peer_w_q, peer_sub_keys, peer_u, peer_v, g_final):
    batch, seq, d = x.shape
    l_ctx = ctx.shape[1]
    depth = w_mod.shape[0]
    assert batch == 1 and seq % (NA_ROWS * GRID_W * NA_STEP_BLOCKS) == 0 and l_ctx % (S5_CHUNK * SUBLANES) == 0
    assert seq // GRID_W >= NA_KROWS and seq >= 3 * SW_BLOCK and seq % (SW_BLOCK * SW_STEP_BLOCKS) == 0

    cc = jnp.zeros((SUBLANES, d), F32).at[0].set(c[0]).at[1].set(c_ctx)
    mod = _mod_vectors(cc, w_mod, b_mod).reshape(depth, SUBLANES, 6, d)
    rope_tabs = _rope_tables(seq)
    row = lambda v: v.reshape(1, d)
    na_bias_lo, na_bias_hi = jax.vmap(_na_bias_tiles)(na_rpb)
    s5_prep = jax.vmap(_s5_prepare)(s5_lam_re, s5_lam_im, s5_b_re, s5_b_im, s5_c_re, s5_c_im, s5_log_step)
    u_all = peer_u.astype(BF16)
    v_all = peer_v.astype(BF16)

    xx, xc = x[0], ctx[0]
    for l in range(depth):
        need_ctx = l < depth - 1
        m_lat, m_ctx = mod[l, 0], mod[l, 1]
        w_in_l = w_in[l].astype(BF16)
        g_mix_l = row(g_mix[l])
        g_ffn_l = row(g_ffn[l])

        qa, ka, va, ub, qs, ks, vs, ga, gb, gs = _inproj(
            xx, g_mix_l, row(m_lat[0]), row(m_lat[1]), w_in_l, rope_tabs)
        qa_c, ka_c, va_c, ub_c, qs_c, ks_c, vs_c, ga_c, gb_c, gs_c = _inproj(
            xc, g_mix_l, row(m_ctx[0]), row(m_ctx[1]), w_in_l, None)

        ya = _na_attention(qa, ka, va, ka_c, va_c, na_bias_lo[l], na_bias_hi[l])
        ys = _sw_attention(qs, ks, vs, ks_c, vs_c, sw_sink[l])
        y5 = _s5_mixer_pre_glu(ub_c, ub, [a[l] for a in s5_prep], s5_d[l])

        w_glu_l = s5_w_glu[l].astype(BF16)
        w_branch_l = w_branch[l].astype(BF16)
        w_out_l = w_out[l].astype(BF16)
        xx, hx2, hx2_t = _merge(xx, ya, y5, l_ctx, ys, ga, gb, gs, w_glu_l, w_branch_l, w_out_l,
                                row(m_lat[2]), g_ffn_l, row(m_lat[3]), row(m_lat[4]))

        w_q_l = peer_w_q[l].astype(BF16)
        k1 = peer_sub_keys[l, 0].astype(BF16)
        k2 = peer_sub_keys[l, 1].astype(BF16)
        if need_ctx:
            ya_c = _ctx_attention(qa_c, ka_c, va_c, None)
            ys_c = _ctx_attention(qs_c, ks_c, vs_c, sw_sink[l])
            xc, hc2, hc2_t = _merge(xc, ya_c, y5, 0, ys_c, ga_c, gb_c, gs_c, w_glu_l, w_branch_l, w_out_l,
                                    row(m_ctx[2]), g_ffn_l, row(m_ctx[3]), row(m_ctx[4]))
            xc = _peer_dense(hc2_t, xc, row(m_ctx[5]), u_all, v_all, l, *_peer_route(hc2, w_q_l, k1, k2))
        xx = _peer_dense(hx2_t, xx, row(m_lat[5]), u_all, v_all, l, *_peer_route(hx2, w_q_l, k1, k2),
                         final_g=None if need_ctx else row(g_final))

    return xx[None]
```

```python
import functools
import math

import numpy as np
import jax
import jax.numpy as jnp
from jax import lax
from jax.experimental import pallas as pl
from jax.experimental.pallas import tpu as pltpu

F32 = jnp.float32
BF16 = jnp.bfloat16

GRID_W = 64
HEAD_DIM = 64
NA_WIN_H = 8
NA_WIN_W = 16
S5_GROUP = 16
S5_STATE = 64
SW_WINDOW = 128
ROPE_BASE = 10000.0
PEER_HEADS = 8
PEER_NKEYS = 128
PEER_QDIM = 256
PEER_TOPK = 16
EPS = 1e-6
NEG_INF = -1e30

LANES = 128
SUBLANES = 8
VMEM_LIMIT_BYTES = 56 * 1024 * 1024

ROW_BLOCK = 256
NA_ROWS = 4
NA_KROWS = NA_ROWS + NA_WIN_H - 1
NA_LANES = 256
NA_STEP_BLOCKS = 2
SW_BLOCK = 128
SW_STEP_BLOCKS = 4
S5_CHUNK = 16
S5_LG = LANES // S5_GROUP
PEER_ROUTE_BLOCK = 256
PEER_ROUTE_HEADS = 4
PEER_TOK_BLOCK = 512
PEER_EXP_BLOCK = 2048
PEER_SUB = 512
PEER_ACC_CHUNK = 1024


def _cparams(sem):
    return pltpu.CompilerParams(dimension_semantics=sem, vmem_limit_bytes=VMEM_LIMIT_BYTES)


def _dot(a, b):
    return jnp.dot(a, b, preferred_element_type=F32)


def _dot_nt(a, b):
    return lax.dot_general(a, b, (((1,), (1,)), ((), ())), preferred_element_type=F32)


def _gelu(x):
    return 0.5 * x * (1.0 + lax.erf(x * (1.0 / math.sqrt(2.0))))


def _mod_kernel(cc_ref, w_ref, b_ref, o_ref):
    a = cc_ref[...]
    a = a * jax.nn.sigmoid(a)
    o_ref[0] = _dot(a.astype(BF16), w_ref[0].astype(BF16)) + b_ref[0]


def _mod_vectors(cc, w_mod, b_mod):
    depth, d, n6 = w_mod.shape
    tn = 1024
    return pl.pallas_call(
        _mod_kernel,
        grid=(depth, n6 // tn),
        in_specs=[pl.BlockSpec((SUBLANES, d), lambda l, j: (0, 0)),
                  pl.BlockSpec((1, d, tn), lambda l, j: (l, 0, j)),
                  pl.BlockSpec((1, 1, tn), lambda l, j: (l, 0, j))],
        out_specs=pl.BlockSpec((1, SUBLANES, tn), lambda l, j: (l, 0, j)),
        out_shape=jax.ShapeDtypeStruct((depth, SUBLANES, n6), F32),
        compiler_params=_cparams(("arbitrary", "arbitrary")),
        name="mod_vectors",
    )(cc, w_mod, b_mod.reshape(depth, 1, n6))


def _norm_mod(x, g, shift, scale):
    y = x * lax.rsqrt(jnp.mean(x * x, axis=-1, keepdims=True) + EPS)
    return (y * g) * (1.0 + scale) + shift


def _rope(x, cos, sin, lane_lo):
    up = pltpu.roll(x, LANES - 16, 1)
    dn = pltpu.roll(x, 16, 1)
    return x * cos + jnp.where(lane_lo, up, dn) * sin


def _inproj_kernel(x_ref, g_ref, sh_ref, sc_ref, w_ref, *rest, bw, kvw, d_model, rope):
    if rope:
        cos_ref, sin_ref = rest[:2]
        rest = rest[2:]
    qa_ref, ka_ref, va_ref, ub_ref, qs_ref, ks_ref, vs_ref, ga_ref, gb_ref, gs_ref = rest
    h = _norm_mod(x_ref[...], g_ref[...], sh_ref[...], sc_ref[...]).astype(BF16)

    def proj(c0, width):
        return _dot(h, w_ref[:, c0:c0 + width])

    c = 0
    qa_ref[...] = proj(c, bw).astype(BF16); c += bw
    ka_ref[...] = proj(c, bw).astype(BF16); c += bw
    va_ref[...] = proj(c, bw).astype(BF16); c += bw
    ub = proj(c, bw); c += bw
    for t in range(bw // LANES):
        ub_ref[t] = ub[:, t * LANES:(t + 1) * LANES]
    qs = proj(c, bw); c += bw
    ks = proj(c, kvw); c += kvw
    vs_ref[...] = proj(c, kvw).astype(BF16); c += kvw
    if rope:
        cos = cos_ref[...]
        sin = sin_ref[...]
        lane = lax.broadcasted_iota(jnp.int32, cos.shape, 1)
        lane_lo = (lane % 32) < 16
        for p in range(bw // LANES):
            sl = slice(p * LANES, (p + 1) * LANES)
            qs_ref[:, sl] = _rope(qs[:, sl], cos, sin, lane_lo).astype(BF16)
        for p in range(kvw // LANES):
            sl = slice(p * LANES, (p + 1) * LANES)
            ks_ref[:, sl] = _rope(ks[:, sl], cos, sin, lane_lo).astype(BF16)
    else:
        qs_ref[...] = qs.astype(BF16)
        ks_ref[...] = ks.astype(BF16)
    ga_ref[...] = jax.nn.sigmoid(proj(c, d_model)).astype(BF16); c += d_model
    gb_ref[...] = jax.nn.sigmoid(proj(c, d_model)).astype(BF16); c += d_model
    gs_ref[...] = jax.nn.sigmoid(proj(c, d_model)).astype(BF16)


def _inproj(x, g, shift, scale, w_in, rope_tabs):
    n, d = x.shape
    bw = d // 2
    kvw = bw // 4
    tm = min(ROW_BLOCK, n)
    rope = rope_tabs is not None
    row = lambda i: (i, 0)
    fixed = lambda i: (0, 0)
    in_specs = [pl.BlockSpec((tm, d), row), pl.BlockSpec((1, d), fixed), pl.BlockSpec((1, d), fixed),
                pl.BlockSpec((1, d), fixed), pl.BlockSpec(w_in.shape, fixed)]
    args = [x, g, shift, scale, w_in]
    if rope:
        in_specs += [pl.BlockSpec((tm, LANES), row)] * 2
        args += list(rope_tabs)
    widths = [bw, bw, bw, bw, bw, kvw, kvw, d, d, d]
    dtypes = [BF16, BF16, BF16, F32, BF16, BF16, BF16, BF16, BF16, BF16]
    s5_slot = 3
    return pl.pallas_call(
        functools.partial(_inproj_kernel, bw=bw, kvw=kvw, d_model=d, rope=rope),
        grid=(n // tm,),
        in_specs=in_specs,
        out_specs=[pl.BlockSpec((bw // LANES, tm, LANES), lambda i: (0, i, 0)) if k == s5_slot
                   else pl.BlockSpec((tm, w), row) for k, w in enumerate(widths)],
        out_shape=[jax.ShapeDtypeStruct((bw // LANES, n, LANES) if k == s5_slot else (n, w), dt)
                   for k, (w, dt) in enumerate(zip(widths, dtypes))],
        compiler_params=_cparams(("arbitrary",)),
        name="inproj",
    )(*args)


def _rope_tables(seq):
    t = jnp.arange(seq)
    half = HEAD_DIM // 4
    inv = ROPE_BASE ** (-jnp.arange(half, dtype=F32) / half)
    sign = jnp.concatenate([-jnp.ones((half,), F32), jnp.ones((half,), F32)])

    def axis_tabs(pos):
        ang = pos.astype(F32)[:, None] * inv[None, :]
        c = jnp.cos(ang)
        s = jnp.sin(ang)
        return jnp.concatenate([c, c], -1), jnp.concatenate([s, s], -1) * sign

    cr, sr = axis_tabs(t // GRID_W)
    cc, sc = axis_tabs(t % GRID_W)
    cos = jnp.concatenate([cr, cc], -1)
    sin = jnp.concatenate([sr, sc], -1)
    return jnp.tile(cos, (1, LANES // HEAD_DIM)), jnp.tile(sin, (1, LANES // HEAD_DIM))


def _na_patterns(rows):
    kh = NA_WIN_H
    nb = rows // NA_ROWS
    kr0s, sigs = [], []
    for b in range(nb):
        r_lo = b * NA_ROWS
        kr0 = int(np.clip(r_lo - kh // 2, 0, rows - NA_KROWS))
        sig = tuple((int(np.clip(r - kh // 2, 0, rows - kh)) - kr0, r - kr0)
                    for r in range(r_lo, r_lo + NA_ROWS))
        kr0s.append(kr0)
        sigs.append(sig)
    uniq = sorted(set(sigs))
    ids = np.array([uniq.index(s) for s in sigs], np.int32)
    return np.array(kr0s, np.int32), ids, uniq


NA_TILE_MASKED = 2 * NA_WIN_H - 1
NA_TILE_ZERO = 2 * NA_WIN_H


def _na_tile_index(uniq):
    n_slots = 2 * ((NA_KROWS + 2) // 2)
    idx = np.full((len(uniq), NA_ROWS, n_slots), NA_TILE_ZERO, np.int32)
    for p, sig in enumerate(uniq):
        for rq, (r0_rel, r_rel) in enumerate(sig):
            for kk in range(NA_KROWS):
                inside = r0_rel <= kk < r0_rel + NA_WIN_H
                idx[p, rq, kk] = kk - r_rel + NA_WIN_H - 1 if inside else NA_TILE_MASKED
    return idx.reshape(-1), n_slots


def _na_bias_tiles(rpb):
    kw = NA_WIN_W
    cols = np.arange(GRID_W)
    col_start = np.clip(cols - kw // 2, 0, GRID_W - kw)
    col_sel = (cols[None, None, :] - cols[None, :, None] + kw - 1
               == np.arange(2 * kw - 1)[:, None, None])
    col_ok = ((cols[None, :] >= col_start[:, None]) & (cols[None, :] < col_start[:, None] + kw))
    by_col = jnp.einsum('hrv,vcj->hrcj', rpb.astype(F32), jnp.asarray(col_sel, F32),
                        precision=lax.Precision.HIGHEST)
    by_col = jnp.where(jnp.asarray(col_ok), by_col, NEG_INF)
    n_h = rpb.shape[0]
    tiles = jnp.concatenate([by_col, jnp.full((n_h, 1, GRID_W, GRID_W), NEG_INF, F32),
                             jnp.zeros((n_h, 1, GRID_W, GRID_W), F32)], axis=1)
    zero = jnp.zeros_like(tiles)
    return jnp.concatenate([tiles, zero], axis=-1), jnp.concatenate([zero, tiles], axis=-1)


def _na_kernel(kr0_ref, pat_ref, tile_ref, q_ref, k_ref, v_ref, kc_ref, vc_ref, lo_ref, hi_ref, o_ref, *,
               n_slots):
    tq = NA_ROWS * GRID_W
    for qb in range(NA_STEP_BLOCKS):
        rows = slice(qb * tq, (qb + 1) * tq)
        _na_block(pl.program_id(1) * NA_STEP_BLOCKS + qb, kr0_ref, pat_ref, tile_ref, q_ref.at[rows, :],
                  k_ref, v_ref, kc_ref, vc_ref, lo_ref, hi_ref, o_ref.at[rows, :], n_slots=n_slots)


def _na_block(b, kr0_ref, pat_ref, tile_ref, q_ref, k_ref, v_ref, kc_ref, vc_ref, lo_ref, hi_ref, o_ref, *,
              n_slots):
    start = pl.multiple_of(kr0_ref[b] * GRID_W, GRID_W)
    nk = NA_KROWS * GRID_W
    scale = HEAD_DIM ** -0.5
    one = jnp.ones((), BF16)
    hp = LANES // HEAD_DIM
    tile_base = pat_ref[b] * (NA_ROWS * n_slots)

    def bias_of(head):
        row_blocks = []
        for rq in range(NA_ROWS):
            pairs = []
            for j in range(n_slots // 2):
                even = tile_ref[tile_base + rq * n_slots + 2 * j]
                odd = tile_ref[tile_base + rq * n_slots + 2 * j + 1]
                pairs.append(lo_ref[head, even] + hi_ref[head, odd])
            row_blocks.append(jnp.concatenate(pairs, axis=-1))
        return jnp.concatenate(row_blocks, axis=0)

    for lt in range(q_ref.shape[1] // LANES):
        lanes = slice(lt * LANES, (lt + 1) * LANES)
        q = q_ref[:, lanes]
        k_all = jnp.concatenate([k_ref[pl.ds(start, nk), lanes], kc_ref[:, lanes]], axis=0)
        v_all = jnp.concatenate([v_ref[pl.ds(start, nk), lanes], vc_ref[:, lanes]], axis=0)
        lane = lax.broadcasted_iota(jnp.int32, v_all.shape, 1)
        ext = []
        for h in range(hp):
            sl = slice(h * HEAD_DIM, (h + 1) * HEAD_DIM)
            own = (lane >= h * HEAD_DIM) & (lane < (h + 1) * HEAD_DIM)
            s = _dot_nt(q[:, sl], k_all[:, sl]) * scale
            n_biased = n_slots * GRID_W
            s = jnp.concatenate([s[:, :n_biased] + bias_of(lt * hp + h), s[:, n_biased:]], axis=-1)
            p = jnp.exp(s - jnp.max(s, axis=-1, keepdims=True)).astype(BF16)
            ext.append(_dot(p, jnp.where(own, v_all, one)))
        out_lane = lax.broadcasted_iota(jnp.int32, ext[0].shape, 1)
        low = out_lane < HEAD_DIM
        num = jnp.where(low, ext[0], ext[1])
        den = pltpu.roll(jnp.where(low, ext[1], ext[0]), HEAD_DIM, 1)
        o_ref[:, lanes] = (num / den).astype(o_ref.dtype)


def _na_attention(q, k, v, kc, vc, bias_lo, bias_hi):
    s, bw = q.shape
    l = kc.shape[0]
    rows = s // GRID_W
    kr0s, ids, uniq = _na_patterns(rows)
    tile_idx, n_slots = _na_tile_index(uniq)
    assert n_slots * GRID_W <= NA_KROWS * GRID_W + l
    tq = NA_ROWS * GRID_W * NA_STEP_BLOCKS
    hp = NA_LANES // HEAD_DIM
    tiles = pl.BlockSpec((hp,) + bias_lo.shape[1:], lambda p, b, *_: (p, 0, 0, 0))
    grid_spec = pltpu.PrefetchScalarGridSpec(
        num_scalar_prefetch=3,
        grid=(bw // NA_LANES, rows // (NA_ROWS * NA_STEP_BLOCKS)),
        in_specs=[pl.BlockSpec((tq, NA_LANES), lambda p, b, *_: (b, p)),
                  pl.BlockSpec((s, NA_LANES), lambda p, b, *_: (0, p)),
                  pl.BlockSpec((s, NA_LANES), lambda p, b, *_: (0, p)),
                  pl.BlockSpec((l, NA_LANES), lambda p, b, *_: (0, p)),
                  pl.BlockSpec((l, NA_LANES), lambda p, b, *_: (0, p)),
                  tiles, tiles],
        out_specs=pl.BlockSpec((tq, NA_LANES), lambda p, b, *_: (b, p)),
    )
    return pl.pallas_call(
        functools.partial(_na_kernel, n_slots=n_slots),
        grid_spec=grid_spec,
        out_shape=jax.ShapeDtypeStruct((s, bw), BF16),
        compiler_params=_cparams(("arbitrary", "arbitrary")),
        name="na_attention",
    )(jnp.asarray(kr0s), jnp.asarray(ids), jnp.asarray(tile_idx), q, k, v, kc, vc, bias_lo, bias_hi)


def _ctx_attn_kernel(*refs, n_heads, n_rep, has_sink):
    if has_sink:
        sink_ref, q_ref, k_ref, v_ref, o_ref = refs
    else:
        q_ref, k_ref, v_ref, o_ref = refs
    scale = HEAD_DIM ** -0.5
    q = q_ref[...]
    k = k_ref[...]
    v = v_ref[...]
    outs = []
    for h in range(n_heads):
        g = h // n_rep
        sl = slice(h * HEAD_DIM, (h + 1) * HEAD_DIM)
        gl = slice(g * HEAD_DIM, (g + 1) * HEAD_DIM)
        s = _dot_nt(q[:, sl], k[:, gl]) * scale
        m = jnp.max(s, axis=-1, keepdims=True)
        if has_sink:
            m = jnp.maximum(m, sink_ref[h])
        p = jnp.exp(s - m)
        den = jnp.sum(p, axis=-1, keepdims=True)
        if has_sink:
            den = den + jnp.exp(sink_ref[h] - m)
        outs.append(_dot(p.astype(BF16), v[:, gl]) / den)
    o_ref[...] = jnp.concatenate(outs, axis=-1).astype(o_ref.dtype)


def _ctx_attention(q, k, v, sink):
    l, bw = q.shape
    n_heads = bw // HEAD_DIM
    n_rep = n_heads // (k.shape[1] // HEAD_DIM)
    has_sink = sink is not None
    full = lambda a: pl.BlockSpec(a.shape, lambda i: (0, 0))
    in_specs = [full(q), full(k), full(v)]
    args = [q, k, v]
    if has_sink:
        in_specs = [pl.BlockSpec(memory_space=pltpu.SMEM)] + in_specs
        args = [sink] + args
    return pl.pallas_call(
        functools.partial(_ctx_attn_kernel, n_heads=n_heads, n_rep=n_rep, has_sink=has_sink),
        grid=(1,),
        in_specs=in_specs,
        out_specs=pl.BlockSpec((l, bw), lambda i: (0, 0)),
        out_shape=jax.ShapeDtypeStruct((l, bw), BF16),
        compiler_params=_cparams(("arbitrary",)),
        name="ctx_attention",
    )(*args)


def _swa_kernel(sink_ref, q_ref, k_ref, v_ref, kc_ref, vc_ref, o_ref, *, seq, n_kv, n_rep):
    for qb in range(SW_STEP_BLOCKS):
        rows = slice(qb * SW_BLOCK, (qb + 1) * SW_BLOCK)
        _swa_block(pl.program_id(0) * SW_STEP_BLOCKS + qb, sink_ref, q_ref[rows, :], k_ref, v_ref,
                   kc_ref, vc_ref, o_ref.at[rows, :], seq=seq, n_kv=n_kv, n_rep=n_rep)


def _swa_block(n, sink_ref, q, k_ref, v_ref, kc_ref, vc_ref, o_ref, *, seq, n_kv, n_rep):
    nk = 3 * SW_BLOCK
    scale = HEAD_DIM ** -0.5
    ws = jnp.clip((n - 1) * SW_BLOCK, 0, seq - nk)
    ws = pl.multiple_of(ws, SW_BLOCK)
    n_all = nk + kc_ref.shape[0]
    k_all = jnp.concatenate([k_ref[pl.ds(ws, nk), :], kc_ref[...]], axis=0)
    v_all = jnp.concatenate([v_ref[pl.ds(ws, nk), :], vc_ref[...]], axis=0)
    qpos = n * SW_BLOCK + lax.broadcasted_iota(jnp.int32, (SW_BLOCK, n_all), 0)
    col = lax.broadcasted_iota(jnp.int32, (SW_BLOCK, n_all), 1)
    visible = (col >= nk) | (jnp.abs(ws + col - qpos) <= SW_WINDOW)
    maskb = jnp.where(visible, 0.0, NEG_INF).astype(F32)
    lane = lax.broadcasted_iota(jnp.int32, v_all.shape, 1)
    one = jnp.ones((), BF16)
    outs = [None] * (n_kv * n_rep)
    for g in range(n_kv):
        gl = slice(g * HEAD_DIM, (g + 1) * HEAD_DIM)
        own = (lane >= g * HEAD_DIM) & (lane < (g + 1) * HEAD_DIM)
        qg = jnp.concatenate([q[:, (g * n_rep + r) * HEAD_DIM:(g * n_rep + r + 1) * HEAD_DIM]
                              for r in range(n_rep)], axis=0)
        s = _dot_nt(qg, k_all[:, gl]) * scale
        s = (s.reshape(n_rep, SW_BLOCK, n_all) + maskb[None]).reshape(n_rep * SW_BLOCK, n_all)
        sink = jnp.concatenate([jnp.full((SW_BLOCK, 1), sink_ref[g * n_rep + r], F32)
                                for r in range(n_rep)], axis=0)
        m = jnp.maximum(jnp.max(s, axis=-1, keepdims=True), sink)
        p = jnp.exp(s - m).astype(BF16)
        ext = _dot(p, jnp.where(own, v_all, one))
        den = pltpu.roll(ext, HEAD_DIM, 1) + jnp.exp(sink - m)
        o = ext / den
        for r in range(n_rep):
            outs[g * n_rep + r] = o[r * SW_BLOCK:(r + 1) * SW_BLOCK, gl]
    o_ref[...] = jnp.concatenate(outs, axis=-1).astype(o_ref.dtype)


def _sw_attention(q, k, v, kc, vc, sink):
    s, bw = q.shape
    l = kc.shape[0]
    kvw = k.shape[1]
    n_kv = kvw // HEAD_DIM
    n_rep = (bw // HEAD_DIM) // n_kv
    assert kvw == LANES
    return pl.pallas_call(
        functools.partial(_swa_kernel, seq=s, n_kv=n_kv, n_rep=n_rep),
        grid=(s // (SW_BLOCK * SW_STEP_BLOCKS),),
        in_specs=[pl.BlockSpec(memory_space=pltpu.SMEM),
                  pl.BlockSpec((SW_BLOCK * SW_STEP_BLOCKS, bw), lambda n: (n, 0)),
                  pl.BlockSpec((s, kvw), lambda n: (0, 0)),
                  pl.BlockSpec((s, kvw), lambda n: (0, 0)),
                  pl.BlockSpec((l, kvw), lambda n: (0, 0)),
                  pl.BlockSpec((l, kvw), lambda n: (0, 0))],
        out_specs=pl.BlockSpec((SW_BLOCK * SW_STEP_BLOCKS, bw), lambda n: (n, 0)),
        out_shape=jax.ShapeDtypeStruct((s, bw), BF16),
        compiler_params=_cparams(("arbitrary",)),
        name="sw_attention",
    )(sink, q, k, v, kc, vc)


def _s5_prepare(lam_re, lam_im, b_re, b_im, c_re, c_im, log_step):
    tc = S5_CHUNK
    lam = lax.complex(lam_re.astype(F32), lam_im.astype(F32))
    dt = jnp.exp(log_step.astype(F32))[..., None]
    lam_dt = lam * dt
    lam_bar = jnp.exp(lam_dt)
    b_bar = ((lam_bar - 1.0) / lam)[..., None] * lax.complex(b_re.astype(F32), b_im.astype(F32))
    c_mat = lax.complex(c_re.astype(F32), c_im.astype(F32))
    kk = jnp.arange(tc + 1, dtype=F32)
    pw = jnp.exp(lam_dt[None] * kk[:, None, None, None])
    n_g, n_p, n_h = b_bar.shape[1:]

    bt = jnp.transpose(b_bar, (1, 0, 3, 2))
    bt = jnp.stack([bt.real, bt.imag], axis=2)
    cw = c_mat[None] * pw[:tc, :, :, None, :]
    cw = jnp.stack([cw[:, 0], cw[::-1, 1]], axis=1)
    cwt = jnp.transpose(cw, (2, 1, 4, 0, 3)).reshape(n_g, 2, n_p, tc * n_h)
    cwt = jnp.stack([cwt.real, cwt.imag], axis=2)
    t_i = np.arange(tc)

    def cat_ri(z):
        return jnp.concatenate([z.real, z.imag], axis=-1)

    w_f = pw[tc - 1 - t_i, 0][:, :, None, :] * jnp.transpose(b_bar[0], (0, 2, 1))[None]
    w_r = pw[t_i, 1][:, :, None, :] * jnp.transpose(b_bar[1], (0, 2, 1))[None]
    w_f = jnp.transpose(cat_ri(w_f), (1, 0, 2, 3)).reshape(n_g, tc * n_h, 2 * n_p)
    w_r = jnp.transpose(cat_ri(w_r), (1, 0, 2, 3)).reshape(n_g, tc * n_h, 2 * n_p)
    w_cat = jnp.concatenate([w_f, w_r], axis=-1).astype(BF16)

    def v_of(c_dir, pw_sel):
        cp = c_dir[None] * pw_sel[:, :, None, :]
        v = jnp.concatenate([cp.real, -cp.imag], axis=-1)
        return jnp.transpose(v, (1, 3, 0, 2)).reshape(n_g, 2 * n_p, tc * n_h)
    v_cat = jnp.concatenate([v_of(c_mat[0], pw[t_i + 1, 0]), v_of(c_mat[1], pw[tc - t_i, 1])],
                            axis=1).astype(BF16)

    def coef(k):
        a = jnp.exp(lam_dt * (tc * k))
        return jnp.stack([jnp.concatenate([a.real, a.real], -1),
                          jnp.concatenate([-a.imag, a.imag], -1)], axis=2)
    a_log = jnp.stack([coef(1.0), coef(2.0), coef(4.0)], axis=2)
    a_car = jnp.stack([coef(float(j)) for j in range(1, SUBLANES + 1)], axis=3)
    a_car = jnp.stack([a_car[0], a_car[1][:, :, ::-1]], axis=0)
    return _s5_expand(bt, cwt, w_cat, v_cat, jnp.transpose(a_log, (1, 0, 2, 3, 4)),
                      jnp.transpose(a_car, (1, 0, 2, 3, 4)))


def _s5_expand(bt, cwt, w_cat, v_cat, a_log, a_car):
    n_g = bt.shape[0]
    n_t = n_g // S5_LG
    n_state = 2 * S5_STATE
    by_tile = lambda a: a.reshape((n_t, S5_LG) + a.shape[1:])

    def lanes_of(a):
        a = jnp.moveaxis(by_tile(a), 1, -2)
        return a.reshape(a.shape[:-2] + (S5_LG * n_state,))
    return by_tile(bt), by_tile(cwt), by_tile(w_cat), by_tile(v_cat), lanes_of(a_log), lanes_of(a_car)


def _s5_lane_perm():
    r = np.arange(S5_CHUNK * LANES)
    t, a, h = r // LANES, (r % LANES) // S5_GROUP, r % S5_GROUP
    dest = a * (S5_CHUNK * S5_GROUP) + t * S5_GROUP + h
    return (jnp.asarray(dest)[:, None] == jnp.arange(S5_CHUNK * LANES)[None, :]).astype(BF16)


def _chunk_rows(tok_ref):
    n_rows = tok_ref.shape[1] // S5_CHUNK
    return jnp.concatenate([tok_ref[0, pl.ds(t, n_rows, stride=S5_CHUNK), :] for t in range(S5_CHUNK)], axis=-1)


def _s5_in_kernel(u_ref, perm_ref, bt_ref, cwt_ref, w_ref, yp_ref, z_ref, m_scr):
    width = S5_CHUNK * S5_GROUP
    n_state = 2 * S5_STATE
    half = S5_LG * n_state

    @pl.when(pl.program_id(1) == 0)
    def _():
        lane = lax.broadcasted_iota(jnp.int32, (S5_GROUP, width), 1)
        def lag_kernels(a, direction):
            hp = lax.Precision.HIGHEST
            return (jnp.dot(bt_ref[0, a, direction, 0], cwt_ref[0, a, direction, 0], precision=hp,
                            preferred_element_type=F32)
                    - jnp.dot(bt_ref[0, a, direction, 1], cwt_ref[0, a, direction, 1], precision=hp,
                              preferred_element_type=F32))

        for a in range(S5_LG):
            k_fwd = lag_kernels(a, 0)
            k_rev = lag_kernels(a, 1)
            for tau in range(S5_CHUNK):
                right = tau * S5_GROUP
                left = (S5_CHUNK - 1 - tau) * S5_GROUP
                blk = (jnp.where(lane >= right, pltpu.roll(k_fwd, right, 1), 0.0)
                       + jnp.where(lane < width - left, pltpu.roll(k_rev, (width - left) % width, 1), 0.0))
                m_scr[a, tau * S5_GROUP:(tau + 1) * S5_GROUP, 0:width] = blk.astype(BF16)
            m_scr[a, :, width:] = w_ref[0, a]

    up = _dot(_chunk_rows(u_ref).astype(BF16), perm_ref[...]).astype(BF16)
    for a in range(S5_LG):
        zz = _dot(up[:, a * width:(a + 1) * width], m_scr[a])
        yp_ref[0, :, a * width:(a + 1) * width] = zz[:, :width]
        z_ref[0, :, a * n_state:(a + 1) * n_state] = zz[:, width:width + n_state]
        z_ref[0, :, half + a * n_state:half + (a + 1) * n_state] = zz[:, width + n_state:]


def _s5_scan_kernel(z_ref, alog_ref, acar_ref, e_ref, *, nc, nc_ctx):
    n_state = 2 * S5_STATE
    half = z_ref.shape[2] // 2
    nblk = nc // SUBLANES
    nblk_ctx = nc_ctx // SUBLANES
    row = lax.broadcasted_iota(jnp.int32, (SUBLANES, n_state), 0)

    def cmul(a1, a2, s):
        return a1 * s + a2 * pltpu.roll(s, S5_STATE, 1)

    def step(t, carry):
        blk = (t, jnp.where(t < nblk_ctx, nblk_ctx - 1 - t, nblk + nblk_ctx - 1 - t))
        new = []
        for d in range(2):
            r0 = pl.multiple_of(blk[d] * SUBLANES, SUBLANES)
            for j in range(half // n_state):
                tile = slice(j * n_state, (j + 1) * n_state)
                lanes = slice(d * half + j * n_state, d * half + (j + 1) * n_state)
                z = z_ref[0, pl.ds(r0, SUBLANES), lanes]
                c = carry[len(new)]
                for i, sh in enumerate((1, 2, 4)):
                    if d == 0:
                        zs = jnp.where(row >= sh, pltpu.roll(z, sh, 0), 0.0)
                    else:
                        zs = jnp.where(row < SUBLANES - sh, pltpu.roll(z, SUBLANES - sh, 0), 0.0)
                    z = z + cmul(alog_ref[0, d, i, 0:1, tile], alog_ref[0, d, i, 1:2, tile], zs)
                cb = jnp.broadcast_to(c, (SUBLANES, n_state))
                s = z + cmul(acar_ref[0, d, 0, :, tile], acar_ref[0, d, 1, :, tile], cb)
                if d == 0:
                    e = jnp.where(row >= 1, pltpu.roll(s, 1, 0), cb)
                    c_new = s[SUBLANES - 1:SUBLANES, :]
                else:
                    e = jnp.where(row < SUBLANES - 1, pltpu.roll(s, SUBLANES - 1, 0), cb)
                    c_new = s[0:1, :]
                e_ref[0, pl.ds(r0, SUBLANES), lanes] = e
                new.append(c_new)
        return tuple(new)

    init = tuple(jnp.zeros((1, n_state), F32) for _ in range(2 * (half // n_state)))
    lax.fori_loop(0, nblk, step, init, unroll=2)


def _s5_out_kernel(e_ref, yp_ref, u_ref, d_ref, perm_ref, v_ref, y_ref):
    width = S5_CHUNK * S5_GROUP
    n_state = 2 * S5_STATE
    half = S5_LG * n_state
    parts = []
    for a in range(S5_LG):
        e_a = jnp.concatenate([e_ref[0, :, a * n_state:(a + 1) * n_state],
                               e_ref[0, :, half + a * n_state:half + (a + 1) * n_state]], axis=-1)
        parts.append(yp_ref[0, :, a * width:(a + 1) * width] + _dot(e_a.astype(BF16), v_ref[0, a]))
    y = jnp.concatenate(parts, axis=-1)
    hi = y.astype(BF16)
    lo = (y - hi.astype(F32)).astype(BF16)
    perm = perm_ref[...]
    y = _dot_nt(hi, perm) + _dot_nt(lo, perm) + _chunk_rows(u_ref) * d_ref[0]
    n_rows = y.shape[0]
    for t in range(S5_CHUNK):
        y_ref[0, pl.ds(t, n_rows, stride=S5_CHUNK), :] = y[:, t * LANES:(t + 1) * LANES]


def _s5_rows(nc):
    return max(r for r in range(2 * SUBLANES, 257, 2 * SUBLANES) if nc % r == 0)


def _s5_mixer_pre_glu(u_ctx, u_lat, prep, d_skip):
    bt, cwt, w_cat, v_cat, a_log, a_car = prep
    n_t, l, _ = u_ctx.shape
    s = u_lat.shape[1]
    nc, nc_ctx = (l + s) // S5_CHUNK, l // S5_CHUNK
    width = S5_CHUNK * LANES
    n_z = 2 * S5_LG * 2 * S5_STATE
    u = jnp.concatenate([u_ctx, u_lat], axis=1)
    d_rows = jnp.tile(d_skip.astype(F32).reshape(n_t, 1, LANES), (1, 1, S5_CHUNK))
    perm = _s5_lane_perm()
    rb = _s5_rows(nc)
    rows = lambda w: pl.BlockSpec((1, rb, w), lambda g, r: (g, r, 0))
    per_tile = lambda a: pl.BlockSpec((1,) + a.shape[1:], lambda g, *_: (g,) + (0,) * (a.ndim - 1))
    whole = lambda a: pl.BlockSpec(a.shape, lambda g, r: (0,) * a.ndim)
    tokens = pl.BlockSpec((1, rb * S5_CHUNK, LANES), lambda g, r: (g, r, 0))
    width_g = S5_CHUNK * S5_GROUP
    yp, z = pl.pallas_call(
        _s5_in_kernel,
        grid=(n_t, nc // rb),
        in_specs=[tokens, whole(perm), per_tile(bt), per_tile(cwt), per_tile(w_cat)],
        out_specs=[rows(width), rows(n_z)],
        out_shape=[jax.ShapeDtypeStruct((n_t, nc, width), F32), jax.ShapeDtypeStruct((n_t, nc, n_z), F32)],
        scratch_shapes=[pltpu.VMEM((S5_LG, width_g, width_g + 4 * S5_STATE), BF16)],
        compiler_params=_cparams(("arbitrary", "arbitrary")),
        name="s5_in",
    )(u, perm, bt, cwt, w_cat)
    e = pl.pallas_call(
        functools.partial(_s5_scan_kernel, nc=nc, nc_ctx=nc_ctx),
        grid=(n_t,),
        in_specs=[per_tile(z), per_tile(a_log), per_tile(a_car)],
        out_specs=per_tile(z),
        out_shape=jax.ShapeDtypeStruct(z.shape, F32),
        compiler_params=_cparams(("arbitrary",)),
        name="s5_scan",
    )(z, a_log, a_car)
    y = pl.pallas_call(
        _s5_out_kernel,
        grid=(n_t, nc // rb),
        in_specs=[rows(n_z), rows(width), tokens, per_tile(d_rows), whole(perm), per_tile(v_cat)],
        out_specs=tokens,
        out_shape=jax.ShapeDtypeStruct((n_t, l + s, LANES), F32),
        compiler_params=_cparams(("arbitrary", "arbitrary")),
        name="s5_out",
    )(e, yp, u, d_rows, perm, v_cat)
    return y


def _merge_kernel(x_ref, ya_ref, y5_ref, ys_ref, ga_ref, gb_ref, gs_ref, wglu_ref, wb_ref, wo_ref,
                  gt_ref, g_ref, sh_ref, sc_ref, xo_ref, h_ref, ht_ref):
    y5 = jnp.concatenate([y5_ref[t] for t in range(y5_ref.shape[0])], axis=-1)
    z = _gelu(y5)
    yb = z * jax.nn.sigmoid(_dot(z.astype(BF16), wglu_ref[...]))
    m = (ga_ref[...].astype(F32) * _dot(ya_ref[...], wb_ref[0])
         + gb_ref[...].astype(F32) * _dot(yb.astype(BF16), wb_ref[1])
         + gs_ref[...].astype(F32) * _dot(ys_ref[...], wb_ref[2]))
    x = x_ref[...] + gt_ref[...] * _dot(m.astype(BF16), wo_ref[...])
    xo_ref[...] = x
    h = _norm_mod(x, g_ref[...], sh_ref[...], sc_ref[...])
    h_ref[...] = h.astype(BF16)
    ht_ref[...] = h.T.astype(BF16)


def _merge(x, ya, y5, y5_row0, ys, ga, gb, gs, w_glu, w_branch, w_out, gate, g, shift, scale):
    n, d = x.shape
    bw = ya.shape[1]
    tm = min(ROW_BLOCK, n)
    assert y5_row0 % tm == 0
    row = lambda w: pl.BlockSpec((tm, w), lambda i: (i, 0))
    full = lambda a: pl.BlockSpec(a.shape, lambda i: (0,) * a.ndim)
    vec = pl.BlockSpec((1, d), lambda i: (0, 0))
    y5_rows = pl.BlockSpec((y5.shape[0], tm, LANES), lambda i: (0, i + y5_row0 // tm, 0))
    return pl.pallas_call(
        _merge_kernel,
        grid=(n // tm,),
        in_specs=[row(d), row(bw), y5_rows, row(bw), row(d), row(d), row(d),
                  full(w_glu), full(w_branch), full(w_out), vec, vec, vec, vec],
        out_specs=[row(d), row(d), pl.BlockSpec((d, tm), lambda i: (0, i))],
        out_shape=[jax.ShapeDtypeStruct((n, d), F32), jax.ShapeDtypeStruct((n, d), BF16),
                   jax.ShapeDtypeStruct((d, n), BF16)],
        compiler_params=_cparams(("arbitrary",)),
        name="merge",
    )(x, ya, y5, ys, ga, gb, gs, w_glu, w_branch, w_out, gate, g, shift, scale)


def _knock_out_16(s, order, exact, want_rank=True):
    rank = jnp.full(s.shape, float(PEER_TOPK), F32) if want_rank else None
    live = s
    vals = []
    for r in range(PEER_TOPK):
        m = jnp.max(live, axis=0, keepdims=True)
        hit = live == m
        if exact:
            first = jnp.min(jnp.where(hit, order, float(PEER_TOPK * PEER_NKEYS)), axis=0, keepdims=True)
            hit = order == first
        if want_rank:
            rank = jnp.where(hit, float(r), rank)
        live = jnp.where(hit, -jnp.inf, live)
        vals.append(m)
    out = (rank < float(PEER_TOPK)) if want_rank else (live != s)
    n_out = jnp.sum(jnp.where(out, 1.0, 0.0), axis=0, keepdims=True)
    return rank, jnp.concatenate(vals, axis=0), n_out


def _bf16_pair_word(x):
    hi = pltpu.bitcast(x.astype(BF16).astype(F32), jnp.uint32)
    return hi | (hi >> 16)


def _pair_tiles():
    tiles = [(0, 0, 8), (0, 8, 8)]
    for a in range(1, 8):
        tiles.append((a, 0, PEER_TOPK // (a + 1)))
    return tiles


def _route_kernel(h_ref, wq_ref, k1_ref, k2_ref, cnt_ref, c1_ref, rk_ref, e2_ref, q_scr):
    tb = h_ref.shape[0]
    q_scr[...] = _dot(h_ref[...], wq_ref[...]).astype(BF16)
    half = PEER_QDIM // 2
    iota = lax.broadcasted_iota(jnp.int32, (PEER_NKEYS, tb), 0).astype(F32)
    row8 = lax.broadcasted_iota(jnp.int32, (SUBLANES, tb), 0).astype(F32)
    k1 = k1_ref[...]
    k2 = k2_ref[...]

    def emit(hd, s1, s2, exact):
        rank1, v1, n1 = _knock_out_16(s1, iota, exact, want_rank=exact)
        rank2, v2, n2 = _knock_out_16(s2, iota, exact)
        e1v = jnp.exp(v1 - v1[0:1])
        e2v = jnp.exp(v2 - v2[0:1])
        tiles, flats, gates = [], [], []
        for a, b0, nv in _pair_tiles():
            c = v1[a:a + 1] + v2[b0:b0 + SUBLANES]
            tiles.append(jnp.where(row8 < nv, c, -jnp.inf))
            flats.append(a * PEER_TOPK + b0 + row8)
            gates.append(e1v[a:a + 1] * e2v[b0:b0 + SUBLANES])
        tiles.append(v1[SUBLANES:] + v2[0:1])
        flats.append((row8 + SUBLANES) * PEER_TOPK)
        gates.append(e1v[SUBLANES:] * e2v[0:1])
        cand = jnp.concatenate(tiles, axis=0)
        flat = jnp.concatenate(flats, axis=0)
        gate = jnp.concatenate(gates, axis=0)
        rank_c, _, n_c = _knock_out_16(cand, flat, exact)
        self_ = jnp.where(rank_c < float(PEER_TOPK), 1.0, 0.0)
        z = jnp.sum(self_ * gate, axis=0, keepdims=True)
        cnt = [self_[0:8].sum(axis=0, keepdims=True) + self_[8:16].sum(axis=0, keepdims=True)]
        for t in range(2, 9):
            cnt.append(self_[t * SUBLANES:(t + 1) * SUBLANES].sum(axis=0, keepdims=True))
        cnt = jnp.concatenate(cnt + [self_[9 * SUBLANES:]], axis=0)
        cnt1 = jnp.zeros((PEER_NKEYS, tb), F32)
        for a in range(PEER_TOPK):
            is_a = (rank1 == float(a)) if exact else (s1 == v1[a:a + 1])
            cnt1 = jnp.where(is_a, cnt[a:a + 1], cnt1)
        cnt_ref[hd] = _bf16_pair_word(cnt1)
        c1_ref[hd] = _bf16_pair_word(jnp.exp(s1 - v1[0:1]) * (0.5 / z))
        rk_ref[hd] = rank2.astype(BF16)
        e2_ref[hd] = jnp.exp(s2 - v2[0:1]).astype(BF16)
        want = float(PEER_TOPK)
        return jnp.where((n1 != want) | (n2 != want) | (n_c != want), 1.0, 0.0)

    def head_group(hg, _):
        scores = []
        for k in range(PEER_ROUTE_HEADS):
            hd = hg * PEER_ROUTE_HEADS + k
            c0 = pl.multiple_of(hd * PEER_QDIM, PEER_QDIM)
            s1 = _dot_nt(k1, q_scr[:, pl.ds(c0, half)])
            s2 = _dot_nt(k2, q_scr[:, pl.ds(c0 + half, half)])
            scores.append((hd, s1, s2))
        tied = [jnp.max(emit(hd, s1, s2, exact=False)) for hd, s1, s2 in scores]
        for (hd, s1, s2), t in zip(scores, tied):
            @pl.when(t > 0.0)
            def _():
                emit(hd, s1, s2, exact=True)
        return 0

    lax.fori_loop(0, PEER_HEADS // PEER_ROUTE_HEADS, head_group, 0)


def _peer_route(h, w_q, k1, k2):
    n, d = h.shape
    tb = min(PEER_ROUTE_BLOCK, n)
    qw = w_q.shape[1]
    out_blk = pl.BlockSpec((PEER_HEADS, PEER_NKEYS, tb), lambda i: (0, 0, i))
    shp = lambda dt: jax.ShapeDtypeStruct((PEER_HEADS, PEER_NKEYS, n), dt)
    return pl.pallas_call(
        _route_kernel,
        grid=(n // tb,),
        in_specs=[pl.BlockSpec((tb, d), lambda i: (i, 0)),
                  pl.BlockSpec((d, qw), lambda i: (0, 0)),
                  pl.BlockSpec(k1.shape, lambda i: (0, 0)),
                  pl.BlockSpec(k2.shape, lambda i: (0, 0))],
        out_specs=[out_blk] * 4,
        out_shape=[shp(jnp.uint32), shp(jnp.uint32), shp(BF16), shp(BF16)],
        scratch_shapes=[pltpu.VMEM((tb, qw), BF16)],
        compiler_params=_cparams(("arbitrary",)),
        name="peer_route",
    )(h, w_q, k1, k2)


def _dense_kernel(ht_ref, x_ref, gt_ref, u_ref, vt_ref, cnt_ref, c1_ref, rk_ref, e2_ref, o_ref,
                  acc_ref, w_scr):
    eb = pl.program_id(1)
    tb = ht_ref.shape[1]
    n_exp = u_ref.shape[0]
    n_sub = n_exp // PEER_SUB
    i_per_sub = PEER_SUB // PEER_NKEYS

    @pl.when(eb == 0)
    def _():
        acc_ref[...] = jnp.zeros_like(acc_ref)

    def scores(sb):
        return _dot(u_ref[sb * PEER_SUB:(sb + 1) * PEER_SUB, :], ht_ref[...])

    pk_rows = 2 * SUBLANES
    n_pk = PEER_NKEYS // pk_rows

    def row_tile(ref, hd, i):
        words = jnp.broadcast_to(ref[hd, pl.ds(i, 1), :], (SUBLANES, tb))
        return pltpu.bitcast(words, BF16)[None]

    def accumulate(c0, c1):
        acc_ref[...] += _dot(vt_ref[:, c0:c1], w_scr[c0:c1, :])

    a_next = scores(0)
    pending = None
    for sb in range(n_sub):
        a_cur = a_next
        if sb + 1 < n_sub:
            a_next = scores(sb + 1)
        if pending is not None:
            accumulate(*pending)
            pending = None
        for il in range(i_per_sub):
            i = eb * (n_exp // PEER_NKEYS) + sb * i_per_sub + il
            gsum = jnp.zeros((n_pk, pk_rows, tb), BF16)
            for hd in range(PEER_HEADS):
                cnt = row_tile(cnt_ref, hd, i)
                c1 = row_tile(c1_ref, hd, i)
                rk = rk_ref[hd].reshape(n_pk, pk_rows, tb)
                e2 = e2_ref[hd].reshape(n_pk, pk_rows, tb)
                gsum = gsum + jnp.where(rk < cnt, e2 * c1, jnp.zeros((), BF16))
            r0 = sb * PEER_SUB + il * PEER_NKEYS
            a_i = a_cur[il * PEER_NKEYS:(il + 1) * PEER_NKEYS]
            half_gate = gsum.reshape(PEER_NKEYS, tb).astype(F32)
            w_i = a_i * (1.0 + lax.erf(a_i * (1.0 / math.sqrt(2.0)))) * half_gate
            w_scr[r0:r0 + PEER_NKEYS, :] = w_i.astype(BF16)
        done = (sb + 1) * PEER_SUB
        if done % PEER_ACC_CHUNK == 0:
            pending = (done - PEER_ACC_CHUNK, done)
    if pending is not None:
        accumulate(*pending)

    @pl.when(eb == pl.num_programs(1) - 1)
    def _():
        o_ref[...] = x_ref[...] + gt_ref[...] * acc_ref[...].T


def _peer_dense(h_t, x, gate, u, v_t, layer, cnt1, c1, rank2, e2):
    d, n = h_t.shape
    n_e = u.shape[1]
    tb = min(PEER_TOK_BLOCK, n)
    eb = PEER_EXP_BLOCK
    tab = pl.BlockSpec((PEER_HEADS, PEER_NKEYS, tb), lambda i, e: (0, 0, i))
    return pl.pallas_call(
        _dense_kernel,
        grid=(n // tb, n_e // eb),
        in_specs=[pl.BlockSpec((d, tb), lambda i, e: (0, i)),
                  pl.BlockSpec((tb, d), lambda i, e: (i, 0)),
                  pl.BlockSpec((1, d), lambda i, e: (0, 0)),
                  pl.BlockSpec((None, eb, d), lambda i, e: (layer, e, 0)),
                  pl.BlockSpec((None, d, eb), lambda i, e: (layer, 0, e)),
                  tab, tab, tab, tab],
        out_specs=pl.BlockSpec((tb, d), lambda i, e: (i, 0)),
        out_shape=jax.ShapeDtypeStruct((n, d), F32),
        scratch_shapes=[pltpu.VMEM((d, tb), F32), pltpu.VMEM((eb, tb), BF16)],
        compiler_params=_cparams(("arbitrary", "arbitrary")),
        name="peer_dense",
    )(h_t, x, gate, u, v_t, cnt1, c1, rank2, e2)


def _final_norm_kernel(x_ref, g_ref, o_ref):
    x = x_ref[...]
    o_ref[...] = x * lax.rsqrt(jnp.mean(x * x, axis=-1, keepdims=True) + EPS) * g_ref[...]


def _final_norm(x, g):
    n, d = x.shape
    tm = min(2 * ROW_BLOCK, n)
    return pl.pallas_call(
        _final_norm_kernel,
        grid=(n // tm,),
        in_specs=[pl.BlockSpec((tm, d), lambda i: (i, 0)), pl.BlockSpec((1, d), lambda i: (0, 0))],
        out_specs=pl.BlockSpec((tm, d), lambda i: (i, 0)),
        out_shape=jax.ShapeDtypeStruct((n, d), F32),
        compiler_params=_cparams(("arbitrary",)),
        name="final_norm",
    )(x, g)


def kernel(x, c, ctx, c_ctx, w_mod, b_mod, g_mix, g_ffn, w_in, na_rpb, s5_lam_re, s5_lam_im,
           s5_b_re, s5_b_im, s5_c_re, s5_c_im, s5_log_step, s5_d, s5_w_glu, sw_sink, w_branch,
           w_out, peer_w_q, peer_sub_keys, peer_u, peer_v, g_final):
    batch, seq, d = x.shape
    l_ctx = ctx.shape[1]
    depth = w_mod.shape[0]
    assert batch == 1 and seq % (NA_ROWS * GRID_W * NA_STEP_BLOCKS) == 0 and l_ctx % (S5_CHUNK * SUBLANES) == 0
    assert seq // GRID_W >= NA_KROWS and seq >= 3 * SW_BLOCK and seq % (SW_BLOCK * SW_STEP_BLOCKS) == 0

    cc = jnp.zeros((SUBLANES, d), F32).at[0].set(c[0]).at[1].set(c_ctx)
    mod = _mod_vectors(cc, w_mod, b_mod).reshape(depth, SUBLANES, 6, d)
    rope_tabs = _rope_tables(seq)
    row = lambda v: v.reshape(1, d)
    na_bias_lo, na_bias_hi = jax.vmap(_na_bias_tiles)(na_rpb)
    s5_prep = jax.vmap(_s5_prepare)(s5_lam_re, s5_lam_im, s5_b_re, s5_b_im, s5_c_re, s5_c_im, s5_log_step)
    u_all = peer_u.astype(BF16)
    vt_all = jnp.transpose(peer_v.astype(BF16), (0, 2, 1))

    xx, xc = x[0], ctx[0]
    for l in range(depth):
        need_ctx = l < depth - 1
        m_lat, m_ctx = mod[l, 0], mod[l, 1]
        w_in_l = w_in[l].astype(BF16)
        g_mix_l = row(g_mix[l])
        g_ffn_l = row(g_ffn[l])

        qa, ka, va, ub, qs, ks, vs, ga, gb, gs = _inproj(
            xx, g_mix_l, row(m_lat[0]), row(m_lat[1]), w_in_l, rope_tabs)
        qa_c, ka_c, va_c, ub_c, qs_c, ks_c, vs_c, ga_c, gb_c, gs_c = _inproj(
            xc, g_mix_l, row(m_ctx[0]), row(m_ctx[1]), w_in_l, None)

        ya = _na_attention(qa, ka, va, ka_c, va_c, na_bias_lo[l], na_bias_hi[l])
        ys = _sw_attention(qs, ks, vs, ks_c, vs_c, sw_sink[l])
        y5 = _s5_mixer_pre_glu(ub_c, ub, [a[l] for a in s5_prep], s5_d[l])

        w_glu_l = s5_w_glu[l].astype(BF16)
        w_branch_l = w_branch[l].astype(BF16)
        w_out_l = w_out[l].astype(BF16)
        xx, hx2, hx2_t = _merge(xx, ya, y5, l_ctx, ys, ga, gb, gs, w_glu_l, w_branch_l, w_out_l,
                                row(m_lat[2]), g_ffn_l, row(m_lat[3]), row(m_lat[4]))

        w_q_l = peer_w_q[l].astype(BF16)
        k1 = peer_sub_keys[l, 0].astype(BF16)
        k2 = peer_sub_keys[l, 1].astype(BF16)
        if need_ctx:
            ya_c = _ctx_attention(qa_c, ka_c, va_c, None)
            ys_c = _ctx_attention(qs_c, ks_c, vs_c, sw_sink[l])
            xc, hc2, hc2_t = _merge(xc, ya_c, y5, 0, ys_c, ga_c, gb_c, gs_c, w_glu_l, w_branch_l, w_out_l,
                                    row(m_ctx[2]), g_ffn_l, row(m_ctx[3]), row(m_ctx[4]))
            xc = _peer_dense(hc2_t, xc, row(m_ctx[5]), u_all, vt_all, l, *_peer_route(hc2, w_q_l, k1, k2))
        xx = _peer_dense(hx2_t, xx, row(m_lat[5]), u_all, vt_all, l, *_peer_route(hx2, w_q_l, k1, k2))

    return _final_norm(xx, row(g_final))[None]
```

```python
import functools
import math

import numpy as np
import jax
import jax.numpy as jnp
from jax import lax
from jax.experimental import pallas as pl
from jax.experimental.pallas import tpu as pltpu

F32 = jnp.float32
BF16 = jnp.bfloat16

GRID_W = 64
HEAD_DIM = 64
NA_WIN_H = 8
NA_WIN_W = 16
S5_GROUP = 16
S5_STATE = 64
SW_WINDOW = 128
ROPE_BASE = 10000.0
PEER_HEADS = 8
PEER_NKEYS = 128
PEER_QDIM = 256
PEER_TOPK = 16
EPS = 1e-6
NEG_INF = -1e30

LANES = 128
SUBLANES = 8
VMEM_LIMIT_BYTES = 56 * 1024 * 1024

ROW_BLOCK = 256
NA_ROWS = 4
NA_KROWS = NA_ROWS + NA_WIN_H - 1
NA_LANES = 256
NA_STEP_BLOCKS = 2
SW_BLOCK = 128
SW_STEP_BLOCKS = 4
S5_CHUNK = 16
S5_LG = LANES // S5_GROUP
PEER_ROUTE_BLOCK = 256
PEER_ROUTE_HEADS = 4
PEER_TOK_BLOCK = 512
PEER_EXP_BLOCK = 2048
PEER_SUB = 512
PEER_ACC_CHUNK = 1024


def _cparams(sem):
    return pltpu.CompilerParams(dimension_semantics=sem, vmem_limit_bytes=VMEM_LIMIT_BYTES)


def _dot(a, b):
    return jnp.dot(a, b, preferred_element_type=F32)


def _dot_nt(a, b):
    return lax.dot_general(a, b, (((1,), (1,)), ((), ())), preferred_element_type=F32)


def _gelu(x):
    return 0.5 * x * (1.0 + lax.erf(x * (1.0 / math.sqrt(2.0))))


def _mod_kernel(cc_ref, w_ref, b_ref, o_ref):
    a = cc_ref[...]
    a = a * jax.nn.sigmoid(a)
    o_ref[0] = _dot(a.astype(BF16), w_ref[0].astype(BF16)) + b_ref[0]


def _mod_vectors(cc, w_mod, b_mod):
    depth, d, n6 = w_mod.shape
    tn = 1024
    return pl.pallas_call(
        _mod_kernel,
        grid=(depth, n6 // tn),
        in_specs=[pl.BlockSpec((SUBLANES, d), lambda l, j: (0, 0)),
                  pl.BlockSpec((1, d, tn), lambda l, j: (l, 0, j)),
                  pl.BlockSpec((1, 1, tn), lambda l, j: (l, 0, j))],
        out_specs=pl.BlockSpec((1, SUBLANES, tn), lambda l, j: (l, 0, j)),
        out_shape=jax.ShapeDtypeStruct((depth, SUBLANES, n6), F32),
        compiler_params=_cparams(("arbitrary", "arbitrary")),
        name="mod_vectors",
    )(cc, w_mod, b_mod.reshape(depth, 1, n6))


def _norm_mod(x, g, shift, scale):
    y = x * lax.rsqrt(jnp.mean(x * x, axis=-1, keepdims=True) + EPS)
    return (y * g) * (1.0 + scale) + shift


def _rope(x, cos, sin, lane_lo):
    up = pltpu.roll(x, LANES - 16, 1)
    dn = pltpu.roll(x, 16, 1)
    return x * cos + jnp.where(lane_lo, up, dn) * sin


def _inproj_kernel(x_ref, g_ref, sh_ref, sc_ref, w_ref, *rest, bw, kvw, d_model, rope):
    if rope:
        cos_ref, sin_ref = rest[:2]
        rest = rest[2:]
    qa_ref, ka_ref, va_ref, ub_ref, qs_ref, ks_ref, vs_ref, ga_ref, gb_ref, gs_ref = rest
    h = _norm_mod(x_ref[...], g_ref[...], sh_ref[...], sc_ref[...]).astype(BF16)

    def proj(c0, width):
        return _dot(h, w_ref[:, c0:c0 + width])

    c = 0
    qa_ref[...] = proj(c, bw).astype(BF16); c += bw
    ka_ref[...] = proj(c, bw).astype(BF16); c += bw
    va_ref[...] = proj(c, bw).astype(BF16); c += bw
    ub = proj(c, bw); c += bw
    for t in range(bw // LANES):
        ub_ref[t] = ub[:, t * LANES:(t + 1) * LANES]
    qs = proj(c, bw); c += bw
    ks = proj(c, kvw); c += kvw
    vs_ref[...] = proj(c, kvw).astype(BF16); c += kvw
    if rope:
        cos = cos_ref[...]
        sin = sin_ref[...]
        lane = lax.broadcasted_iota(jnp.int32, cos.shape, 1)
        lane_lo = (lane % 32) < 16
        for p in range(bw // LANES):
            sl = slice(p * LANES, (p + 1) * LANES)
            qs_ref[:, sl] = _rope(qs[:, sl], cos, sin, lane_lo).astype(BF16)
        for p in range(kvw // LANES):
            sl = slice(p * LANES, (p + 1) * LANES)
            ks_ref[:, sl] = _rope(ks[:, sl], cos, sin, lane_lo).astype(BF16)
    else:
        qs_ref[...] = qs.astype(BF16)
        ks_ref[...] = ks.astype(BF16)
    ga_ref[...] = jax.nn.sigmoid(proj(c, d_model)).astype(BF16); c += d_model
    gb_ref[...] = jax.nn.sigmoid(proj(c, d_model)).astype(BF16); c += d_model
    gs_ref[...] = jax.nn.sigmoid(proj(c, d_model)).astype(BF16)


def _inproj(x, g, shift, scale, w_in, rope_tabs):
    n, d = x.shape
    bw = d // 2
    kvw = bw // 4
    tm = min(ROW_BLOCK, n)
    rope = rope_tabs is not None
    row = lambda i: (i, 0)
    fixed = lambda i: (0, 0)
    in_specs = [pl.BlockSpec((tm, d), row), pl.BlockSpec((1, d), fixed), pl.BlockSpec((1, d), fixed),
                pl.BlockSpec((1, d), fixed), pl.BlockSpec(w_in.shape, fixed)]
    args = [x, g, shift, scale, w_in]
    if rope:
        in_specs += [pl.BlockSpec((tm, LANES), row)] * 2
        args += list(rope_tabs)
    widths = [bw, bw, bw, bw, bw, kvw, kvw, d, d, d]
    dtypes = [BF16, BF16, BF16, F32, BF16, BF16, BF16, BF16, BF16, BF16]
    s5_slot = 3
    return pl.pallas_call(
        functools.partial(_inproj_kernel, bw=bw, kvw=kvw, d_model=d, rope=rope),
        grid=(n // tm,),
        in_specs=in_specs,
        out_specs=[pl.BlockSpec((bw // LANES, tm, LANES), lambda i: (0, i, 0)) if k == s5_slot
                   else pl.BlockSpec((tm, w), row) for k, w in enumerate(widths)],
        out_shape=[jax.ShapeDtypeStruct((bw // LANES, n, LANES) if k == s5_slot else (n, w), dt)
                   for k, (w, dt) in enumerate(zip(widths, dtypes))],
        compiler_params=_cparams(("arbitrary",)),
        name="inproj",
    )(*args)


def _rope_tables(seq):
    t = jnp.arange(seq)
    half = HEAD_DIM // 4
    inv = ROPE_BASE ** (-jnp.arange(half, dtype=F32) / half)
    sign = jnp.concatenate([-jnp.ones((half,), F32), jnp.ones((half,), F32)])

    def axis_tabs(pos):
        ang = pos.astype(F32)[:, None] * inv[None, :]
        c = jnp.cos(ang)
        s = jnp.sin(ang)
        return jnp.concatenate([c, c], -1), jnp.concatenate([s, s], -1) * sign

    cr, sr = axis_tabs(t // GRID_W)
    cc, sc = axis_tabs(t % GRID_W)
    cos = jnp.concatenate([cr, cc], -1)
    sin = jnp.concatenate([sr, sc], -1)
    return jnp.tile(cos, (1, LANES // HEAD_DIM)), jnp.tile(sin, (1, LANES // HEAD_DIM))


def _na_patterns(rows):
    kh = NA_WIN_H
    nb = rows // NA_ROWS
    kr0s, sigs = [], []
    for b in range(nb):
        r_lo = b * NA_ROWS
        kr0 = int(np.clip(r_lo - kh // 2, 0, rows - NA_KROWS))
        sig = tuple((int(np.clip(r - kh // 2, 0, rows - kh)) - kr0, r - kr0)
                    for r in range(r_lo, r_lo + NA_ROWS))
        kr0s.append(kr0)
        sigs.append(sig)
    uniq = sorted(set(sigs))
    ids = np.array([uniq.index(s) for s in sigs], np.int32)
    return np.array(kr0s, np.int32), ids, uniq


NA_TILE_MASKED = 2 * NA_WIN_H - 1
NA_TILE_ZERO = 2 * NA_WIN_H


def _na_tile_index(uniq):
    n_slots = 2 * ((NA_KROWS + 2) // 2)
    idx = np.full((len(uniq), NA_ROWS, n_slots), NA_TILE_ZERO, np.int32)
    for p, sig in enumerate(uniq):
        for rq, (r0_rel, r_rel) in enumerate(sig):
            for kk in range(NA_KROWS):
                inside = r0_rel <= kk < r0_rel + NA_WIN_H
                idx[p, rq, kk] = kk - r_rel + NA_WIN_H - 1 if inside else NA_TILE_MASKED
    return idx.reshape(-1), n_slots


def _na_bias_tiles(rpb):
    kw = NA_WIN_W
    cols = np.arange(GRID_W)
    col_start = np.clip(cols - kw // 2, 0, GRID_W - kw)
    col_sel = (cols[None, None, :] - cols[None, :, None] + kw - 1
               == np.arange(2 * kw - 1)[:, None, None])
    col_ok = ((cols[None, :] >= col_start[:, None]) & (cols[None, :] < col_start[:, None] + kw))
    by_col = jnp.einsum('hrv,vcj->hrcj', rpb.astype(F32), jnp.asarray(col_sel, F32),
                        precision=lax.Precision.HIGHEST)
    by_col = jnp.where(jnp.asarray(col_ok), by_col, NEG_INF)
    n_h = rpb.shape[0]
    tiles = jnp.concatenate([by_col, jnp.full((n_h, 1, GRID_W, GRID_W), NEG_INF, F32),
                             jnp.zeros((n_h, 1, GRID_W, GRID_W), F32)], axis=1)
    zero = jnp.zeros_like(tiles)
    return jnp.concatenate([tiles, zero], axis=-1), jnp.concatenate([zero, tiles], axis=-1)


def _na_kernel(kr0_ref, pat_ref, tile_ref, q_ref, k_ref, v_ref, kc_ref, vc_ref, lo_ref, hi_ref, o_ref, *,
               n_slots):
    tq = NA_ROWS * GRID_W
    for qb in range(NA_STEP_BLOCKS):
        rows = slice(qb * tq, (qb + 1) * tq)
        _na_block(pl.program_id(1) * NA_STEP_BLOCKS + qb, kr0_ref, pat_ref, tile_ref, q_ref.at[rows, :],
                  k_ref, v_ref, kc_ref, vc_ref, lo_ref, hi_ref, o_ref.at[rows, :], n_slots=n_slots)


def _na_block(b, kr0_ref, pat_ref, tile_ref, q_ref, k_ref, v_ref, kc_ref, vc_ref, lo_ref, hi_ref, o_ref, *,
              n_slots):
    start = pl.multiple_of(kr0_ref[b] * GRID_W, GRID_W)
    nk = NA_KROWS * GRID_W
    scale = HEAD_DIM ** -0.5
    one = jnp.ones((), BF16)
    hp = LANES // HEAD_DIM
    tile_base = pat_ref[b] * (NA_ROWS * n_slots)

    def bias_of(head):
        row_blocks = []
        for rq in range(NA_ROWS):
            pairs = []
            for j in range(n_slots // 2):
                even = tile_ref[tile_base + rq * n_slots + 2 * j]
                odd = tile_ref[tile_base + rq * n_slots + 2 * j + 1]
                pairs.append(lo_ref[head, even] + hi_ref[head, odd])
            row_blocks.append(jnp.concatenate(pairs, axis=-1))
        return jnp.concatenate(row_blocks, axis=0)

    for lt in range(q_ref.shape[1] // LANES):
        lanes = slice(lt * LANES, (lt + 1) * LANES)
        q = q_ref[:, lanes]
        k_all = jnp.concatenate([k_ref[pl.ds(start, nk), lanes], kc_ref[:, lanes]], axis=0)
        v_all = jnp.concatenate([v_ref[pl.ds(start, nk), lanes], vc_ref[:, lanes]], axis=0)
        lane = lax.broadcasted_iota(jnp.int32, v_all.shape, 1)
        ext = []
        raw = [_dot_nt(q[:, h * HEAD_DIM:(h + 1) * HEAD_DIM], k_all[:, h * HEAD_DIM:(h + 1) * HEAD_DIM])
               for h in range(hp)]
        for h in range(hp):
            own = (lane >= h * HEAD_DIM) & (lane < (h + 1) * HEAD_DIM)
            s = raw[h] * scale
            n_biased = n_slots * GRID_W
            s = jnp.concatenate([s[:, :n_biased] + bias_of(lt * hp + h), s[:, n_biased:]], axis=-1)
            p = jnp.exp(s - jnp.max(s, axis=-1, keepdims=True)).astype(BF16)
            ext.append(_dot(p, jnp.where(own, v_all, one)))
        out_lane = lax.broadcasted_iota(jnp.int32, ext[0].shape, 1)
        low = out_lane < HEAD_DIM
        num = jnp.where(low, ext[0], ext[1])
        den = pltpu.roll(jnp.where(low, ext[1], ext[0]), HEAD_DIM, 1)
        o_ref[:, lanes] = (num / den).astype(o_ref.dtype)


def _na_attention(q, k, v, kc, vc, bias_lo, bias_hi):
    s, bw = q.shape
    l = kc.shape[0]
    rows = s // GRID_W
    kr0s, ids, uniq = _na_patterns(rows)
    tile_idx, n_slots = _na_tile_index(uniq)
    assert n_slots * GRID_W <= NA_KROWS * GRID_W + l
    tq = NA_ROWS * GRID_W * NA_STEP_BLOCKS
    hp = NA_LANES // HEAD_DIM
    tiles = pl.BlockSpec((hp,) + bias_lo.shape[1:], lambda p, b, *_: (p, 0, 0, 0))
    grid_spec = pltpu.PrefetchScalarGridSpec(
        num_scalar_prefetch=3,
        grid=(bw // NA_LANES, rows // (NA_ROWS * NA_STEP_BLOCKS)),
        in_specs=[pl.BlockSpec((tq, NA_LANES), lambda p, b, *_: (b, p)),
                  pl.BlockSpec((s, NA_LANES), lambda p, b, *_: (0, p)),
                  pl.BlockSpec((s, NA_LANES), lambda p, b, *_: (0, p)),
                  pl.BlockSpec((l, NA_LANES), lambda p, b, *_: (0, p)),
                  pl.BlockSpec((l, NA_LANES), lambda p, b, *_: (0, p)),
                  tiles, tiles],
        out_specs=pl.BlockSpec((tq, NA_LANES), lambda p, b, *_: (b, p)),
    )
    return pl.pallas_call(
        functools.partial(_na_kernel, n_slots=n_slots),
        grid_spec=grid_spec,
        out_shape=jax.ShapeDtypeStruct((s, bw), BF16),
        compiler_params=_cparams(("arbitrary", "arbitrary")),
        name="na_attention",
    )(jnp.asarray(kr0s), jnp.asarray(ids), jnp.asarray(tile_idx), q, k, v, kc, vc, bias_lo, bias_hi)


def _ctx_attn_kernel(*refs, n_heads, n_rep, has_sink):
    if has_sink:
        sink_ref, q_ref, k_ref, v_ref, o_ref = refs
    else:
        q_ref, k_ref, v_ref, o_ref = refs
    scale = HEAD_DIM ** -0.5
    q = q_ref[...]
    k = k_ref[...]
    v = v_ref[...]
    outs = []
    for h in range(n_heads):
        g = h // n_rep
        sl = slice(h * HEAD_DIM, (h + 1) * HEAD_DIM)
        gl = slice(g * HEAD_DIM, (g + 1) * HEAD_DIM)
        s = _dot_nt(q[:, sl], k[:, gl]) * scale
        m = jnp.max(s, axis=-1, keepdims=True)
        if has_sink:
            m = jnp.maximum(m, sink_ref[h])
        p = jnp.exp(s - m)
        den = jnp.sum(p, axis=-1, keepdims=True)
        if has_sink:
            den = den + jnp.exp(sink_ref[h] - m)
        outs.append(_dot(p.astype(BF16), v[:, gl]) / den)
    o_ref[...] = jnp.concatenate(outs, axis=-1).astype(o_ref.dtype)


def _ctx_attention(q, k, v, sink):
    l, bw = q.shape
    n_heads = bw // HEAD_DIM
    n_rep = n_heads // (k.shape[1] // HEAD_DIM)
    has_sink = sink is not None
    full = lambda a: pl.BlockSpec(a.shape, lambda i: (0, 0))
    in_specs = [full(q), full(k), full(v)]
    args = [q, k, v]
    if has_sink:
        in_specs = [pl.BlockSpec(memory_space=pltpu.SMEM)] + in_specs
        args = [sink] + args
    return pl.pallas_call(
        functools.partial(_ctx_attn_kernel, n_heads=n_heads, n_rep=n_rep, has_sink=has_sink),
        grid=(1,),
        in_specs=in_specs,
        out_specs=pl.BlockSpec((l, bw), lambda i: (0, 0)),
        out_shape=jax.ShapeDtypeStruct((l, bw), BF16),
        compiler_params=_cparams(("arbitrary",)),
        name="ctx_attention",
    )(*args)


def _swa_kernel(sink_ref, q_ref, k_ref, v_ref, kc_ref, vc_ref, o_ref, *, seq, n_kv, n_rep):
    for qb in range(SW_STEP_BLOCKS):
        rows = slice(qb * SW_BLOCK, (qb + 1) * SW_BLOCK)
        _swa_block(pl.program_id(0) * SW_STEP_BLOCKS + qb, sink_ref, q_ref[rows, :], k_ref, v_ref,
                   kc_ref, vc_ref, o_ref.at[rows, :], seq=seq, n_kv=n_kv, n_rep=n_rep)


def _swa_block(n, sink_ref, q, k_ref, v_ref, kc_ref, vc_ref, o_ref, *, seq, n_kv, n_rep):
    nk = 3 * SW_BLOCK
    scale = HEAD_DIM ** -0.5
    ws = jnp.clip((n - 1) * SW_BLOCK, 0, seq - nk)
    ws = pl.multiple_of(ws, SW_BLOCK)
    n_all = nk + kc_ref.shape[0]
    k_all = jnp.concatenate([k_ref[pl.ds(ws, nk), :], kc_ref[...]], axis=0)
    v_all = jnp.concatenate([v_ref[pl.ds(ws, nk), :], vc_ref[...]], axis=0)
    qpos = n * SW_BLOCK + lax.broadcasted_iota(jnp.int32, (SW_BLOCK, n_all), 0)
    col = lax.broadcasted_iota(jnp.int32, (SW_BLOCK, n_all), 1)
    visible = (col >= nk) | (jnp.abs(ws + col - qpos) <= SW_WINDOW)
    maskb = jnp.where(visible, 0.0, NEG_INF).astype(F32)
    lane = lax.broadcasted_iota(jnp.int32, v_all.shape, 1)
    one = jnp.ones((), BF16)
    outs = [None] * (n_kv * n_rep)
    raw = []
    for g in range(n_kv):
        qg = jnp.concatenate([q[:, (g * n_rep + r) * HEAD_DIM:(g * n_rep + r + 1) * HEAD_DIM]
                              for r in range(n_rep)], axis=0)
        raw.append(_dot_nt(qg, k_all[:, g * HEAD_DIM:(g + 1) * HEAD_DIM]))
    for g in range(n_kv):
        gl = slice(g * HEAD_DIM, (g + 1) * HEAD_DIM)
        own = (lane >= g * HEAD_DIM) & (lane < (g + 1) * HEAD_DIM)
        s = raw[g] * scale
        s = (s.reshape(n_rep, SW_BLOCK, n_all) + maskb[None]).reshape(n_rep * SW_BLOCK, n_all)
        sink = jnp.concatenate([jnp.full((SW_BLOCK, 1), sink_ref[g * n_rep + r], F32)
                                for r in range(n_rep)], axis=0)
        m = jnp.maximum(jnp.max(s, axis=-1, keepdims=True), sink)
        p = jnp.exp(s - m).astype(BF16)
        ext = _dot(p, jnp.where(own, v_all, one))
        den = pltpu.roll(ext, HEAD_DIM, 1) + jnp.exp(sink - m)
        o = ext / den
        for r in range(n_rep):
            outs[g * n_rep + r] = o[r * SW_BLOCK:(r + 1) * SW_BLOCK, gl]
    o_ref[...] = jnp.concatenate(outs, axis=-1).astype(o_ref.dtype)


def _sw_attention(q, k, v, kc, vc, sink):
    s, bw = q.shape
    l = kc.shape[0]
    kvw = k.shape[1]
    n_kv = kvw // HEAD_DIM
    n_rep = (bw // HEAD_DIM) // n_kv
    assert kvw == LANES
    return pl.pallas_call(
        functools.partial(_swa_kernel, seq=s, n_kv=n_kv, n_rep=n_rep),
        grid=(s // (SW_BLOCK * SW_STEP_BLOCKS),),
        in_specs=[pl.BlockSpec(memory_space=pltpu.SMEM),
                  pl.BlockSpec((SW_BLOCK * SW_STEP_BLOCKS, bw), lambda n: (n, 0)),
                  pl.BlockSpec((s, kvw), lambda n: (0, 0)),
                  pl.BlockSpec((s, kvw), lambda n: (0, 0)),
                  pl.BlockSpec((l, kvw), lambda n: (0, 0)),
                  pl.BlockSpec((l, kvw), lambda n: (0, 0))],
        out_specs=pl.BlockSpec((SW_BLOCK * SW_STEP_BLOCKS, bw), lambda n: (n, 0)),
        out_shape=jax.ShapeDtypeStruct((s, bw), BF16),
        compiler_params=_cparams(("arbitrary",)),
        name="sw_attention",
    )(sink, q, k, v, kc, vc)


def _s5_prepare(lam_re, lam_im, b_re, b_im, c_re, c_im, log_step):
    tc = S5_CHUNK
    lam = lax.complex(lam_re.astype(F32), lam_im.astype(F32))
    dt = jnp.exp(log_step.astype(F32))[..., None]
    lam_dt = lam * dt
    lam_bar = jnp.exp(lam_dt)
    b_bar = ((lam_bar - 1.0) / lam)[..., None] * lax.complex(b_re.astype(F32), b_im.astype(F32))
    c_mat = lax.complex(c_re.astype(F32), c_im.astype(F32))
    kk = jnp.arange(tc + 1, dtype=F32)
    pw = jnp.exp(lam_dt[None] * kk[:, None, None, None])
    n_g, n_p, n_h = b_bar.shape[1:]

    bt = jnp.transpose(b_bar, (1, 0, 3, 2))
    bt = jnp.stack([bt.real, bt.imag], axis=2)
    cw = c_mat[None] * pw[:tc, :, :, None, :]
    cw = jnp.stack([cw[:, 0], cw[::-1, 1]], axis=1)
    cwt = jnp.transpose(cw, (2, 1, 4, 0, 3)).reshape(n_g, 2, n_p, tc * n_h)
    cwt = jnp.stack([cwt.real, cwt.imag], axis=2)
    t_i = np.arange(tc)

    def cat_ri(z):
        return jnp.concatenate([z.real, z.imag], axis=-1)

    w_f = pw[tc - 1 - t_i, 0][:, :, None, :] * jnp.transpose(b_bar[0], (0, 2, 1))[None]
    w_r = pw[t_i, 1][:, :, None, :] * jnp.transpose(b_bar[1], (0, 2, 1))[None]
    w_f = jnp.transpose(cat_ri(w_f), (1, 0, 2, 3)).reshape(n_g, tc * n_h, 2 * n_p)
    w_r = jnp.transpose(cat_ri(w_r), (1, 0, 2, 3)).reshape(n_g, tc * n_h, 2 * n_p)
    w_cat = jnp.concatenate([w_f, w_r], axis=-1).astype(BF16)

    def v_of(c_dir, pw_sel):
        cp = c_dir[None] * pw_sel[:, :, None, :]
        v = jnp.concatenate([cp.real, -cp.imag], axis=-1)
        return jnp.transpose(v, (1, 3, 0, 2)).reshape(n_g, 2 * n_p, tc * n_h)
    v_cat = jnp.concatenate([v_of(c_mat[0], pw[t_i + 1, 0]), v_of(c_mat[1], pw[tc - t_i, 1])],
                            axis=1).astype(BF16)

    def coef(k):
        a = jnp.exp(lam_dt * (tc * k))
        return jnp.stack([jnp.concatenate([a.real, a.real], -1),
                          jnp.concatenate([-a.imag, a.imag], -1)], axis=2)
    a_log = jnp.stack([coef(1.0), coef(2.0), coef(4.0)], axis=2)
    a_car = jnp.stack([coef(float(j)) for j in range(1, SUBLANES + 1)], axis=3)
    a_car = jnp.stack([a_car[0], a_car[1][:, :, ::-1]], axis=0)
    return _s5_expand(bt, cwt, w_cat, v_cat, jnp.transpose(a_log, (1, 0, 2, 3, 4)),
                      jnp.transpose(a_car, (1, 0, 2, 3, 4)))


def _s5_expand(bt, cwt, w_cat, v_cat, a_log, a_car):
    n_g = bt.shape[0]
    n_t = n_g // S5_LG
    n_state = 2 * S5_STATE
    by_tile = lambda a: a.reshape((n_t, S5_LG) + a.shape[1:])

    def lanes_of(a):
        a = jnp.moveaxis(by_tile(a), 1, -2)
        return a.reshape(a.shape[:-2] + (S5_LG * n_state,))
    return by_tile(bt), by_tile(cwt), by_tile(w_cat), by_tile(v_cat), lanes_of(a_log), lanes_of(a_car)


def _s5_lane_perm():
    r = np.arange(S5_CHUNK * LANES)
    t, a, h = r // LANES, (r % LANES) // S5_GROUP, r % S5_GROUP
    dest = a * (S5_CHUNK * S5_GROUP) + t * S5_GROUP + h
    return (jnp.asarray(dest)[:, None] == jnp.arange(S5_CHUNK * LANES)[None, :]).astype(BF16)


def _chunk_rows(tok_ref):
    n_rows = tok_ref.shape[1] // S5_CHUNK
    return jnp.concatenate([tok_ref[0, pl.ds(t, n_rows, stride=S5_CHUNK), :] for t in range(S5_CHUNK)], axis=-1)


def _s5_in_kernel(u_ref, perm_ref, bt_ref, cwt_ref, w_ref, yp_ref, z_ref, m_scr):
    width = S5_CHUNK * S5_GROUP
    n_state = 2 * S5_STATE
    half = S5_LG * n_state

    @pl.when(pl.program_id(1) == 0)
    def _():
        lane = lax.broadcasted_iota(jnp.int32, (S5_GROUP, width), 1)
        def lag_kernels(a, direction):
            hp = lax.Precision.HIGHEST
            return (jnp.dot(bt_ref[0, a, direction, 0], cwt_ref[0, a, direction, 0], precision=hp,
                            preferred_element_type=F32)
                    - jnp.dot(bt_ref[0, a, direction, 1], cwt_ref[0, a, direction, 1], precision=hp,
                              preferred_element_type=F32))

        for a in range(S5_LG):
            k_fwd = lag_kernels(a, 0)
            k_rev = lag_kernels(a, 1)
            for tau in range(S5_CHUNK):
                right = tau * S5_GROUP
                left = (S5_CHUNK - 1 - tau) * S5_GROUP
                blk = (jnp.where(lane >= right, pltpu.roll(k_fwd, right, 1), 0.0)
                       + jnp.where(lane < width - left, pltpu.roll(k_rev, (width - left) % width, 1), 0.0))
                m_scr[a, tau * S5_GROUP:(tau + 1) * S5_GROUP, 0:width] = blk.astype(BF16)
            m_scr[a, :, width:] = w_ref[0, a]

    up = _dot(_chunk_rows(u_ref).astype(BF16), perm_ref[...]).astype(BF16)
    for a in range(S5_LG):
        zz = _dot(up[:, a * width:(a + 1) * width], m_scr[a])
        yp_ref[0, :, a * width:(a + 1) * width] = zz[:, :width]
        z_ref[0, :, a * n_state:(a + 1) * n_state] = zz[:, width:width + n_state]
        z_ref[0, :, half + a * n_state:half + (a + 1) * n_state] = zz[:, width + n_state:]


def _s5_scan_kernel(z_ref, alog_ref, acar_ref, e_ref, *, nc, nc_ctx):
    n_state = 2 * S5_STATE
    half = z_ref.shape[2] // 2
    nblk = nc // SUBLANES
    nblk_ctx = nc_ctx // SUBLANES
    row = lax.broadcasted_iota(jnp.int32, (SUBLANES, n_state), 0)

    def cmul(a1, a2, s):
        return a1 * s + a2 * pltpu.roll(s, S5_STATE, 1)

    def step(t, carry):
        blk = (t, jnp.where(t < nblk_ctx, nblk_ctx - 1 - t, nblk + nblk_ctx - 1 - t))
        new = []
        for d in range(2):
            r0 = pl.multiple_of(blk[d] * SUBLANES, SUBLANES)
            for j in range(half // n_state):
                tile = slice(j * n_state, (j + 1) * n_state)
                lanes = slice(d * half + j * n_state, d * half + (j + 1) * n_state)
                z = z_ref[0, pl.ds(r0, SUBLANES), lanes]
                c = carry[len(new)]
                for i, sh in enumerate((1, 2, 4)):
                    if d == 0:
                        zs = jnp.where(row >= sh, pltpu.roll(z, sh, 0), 0.0)
                    else:
                        zs = jnp.where(row < SUBLANES - sh, pltpu.roll(z, SUBLANES - sh, 0), 0.0)
                    z = z + cmul(alog_ref[0, d, i, 0:1, tile], alog_ref[0, d, i, 1:2, tile], zs)
                cb = jnp.broadcast_to(c, (SUBLANES, n_state))
                s = z + cmul(acar_ref[0, d, 0, :, tile], acar_ref[0, d, 1, :, tile], cb)
                if d == 0:
                    e = jnp.where(row >= 1, pltpu.roll(s, 1, 0), cb)
                    c_new = s[SUBLANES - 1:SUBLANES, :]
                else:
                    e = jnp.where(row < SUBLANES - 1, pltpu.roll(s, SUBLANES - 1, 0), cb)
                    c_new = s[0:1, :]
                e_ref[0, pl.ds(r0, SUBLANES), lanes] = e
                new.append(c_new)
        return tuple(new)

    init = tuple(jnp.zeros((1, n_state), F32) for _ in range(2 * (half // n_state)))
    lax.fori_loop(0, nblk, step, init, unroll=2)


def _s5_out_kernel(e_ref, yp_ref, u_ref, d_ref, perm_ref, v_ref, y_ref):
    width = S5_CHUNK * S5_GROUP
    n_state = 2 * S5_STATE
    half = S5_LG * n_state
    parts = []
    for a in range(S5_LG):
        e_a = jnp.concatenate([e_ref[0, :, a * n_state:(a + 1) * n_state],
                               e_ref[0, :, half + a * n_state:half + (a + 1) * n_state]], axis=-1)
        parts.append(yp_ref[0, :, a * width:(a + 1) * width] + _dot(e_a.astype(BF16), v_ref[0, a]))
    y = jnp.concatenate(parts, axis=-1)
    hi = y.astype(BF16)
    lo = (y - hi.astype(F32)).astype(BF16)
    perm = perm_ref[...]
    y = _dot_nt(hi, perm) + _dot_nt(lo, perm) + _chunk_rows(u_ref) * d_ref[0]
    n_rows = y.shape[0]
    for t in range(S5_CHUNK):
        y_ref[0, pl.ds(t, n_rows, stride=S5_CHUNK), :] = y[:, t * LANES:(t + 1) * LANES]


def _s5_rows(nc):
    return max(r for r in range(2 * SUBLANES, 257, 2 * SUBLANES) if nc % r == 0)


def _s5_mixer_pre_glu(u_ctx, u_lat, prep, d_skip):
    bt, cwt, w_cat, v_cat, a_log, a_car = prep
    n_t, l, _ = u_ctx.shape
    s = u_lat.shape[1]
    nc, nc_ctx = (l + s) // S5_CHUNK, l // S5_CHUNK
    width = S5_CHUNK * LANES
    n_z = 2 * S5_LG * 2 * S5_STATE
    u = jnp.concatenate([u_ctx, u_lat], axis=1)
    d_rows = jnp.tile(d_skip.astype(F32).reshape(n_t, 1, LANES), (1, 1, S5_CHUNK))
    perm = _s5_lane_perm()
    rb = _s5_rows(nc)
    rows = lambda w: pl.BlockSpec((1, rb, w), lambda g, r: (g, r, 0))
    per_tile = lambda a: pl.BlockSpec((1,) + a.shape[1:], lambda g, *_: (g,) + (0,) * (a.ndim - 1))
    whole = lambda a: pl.BlockSpec(a.shape, lambda g, r: (0,) * a.ndim)
    tokens = pl.BlockSpec((1, rb * S5_CHUNK, LANES), lambda g, r: (g, r, 0))
    width_g = S5_CHUNK * S5_GROUP
    yp, z = pl.pallas_call(
        _s5_in_kernel,
        grid=(n_t, nc // rb),
        in_specs=[tokens, whole(perm), per_tile(bt), per_tile(cwt), per_tile(w_cat)],
        out_specs=[rows(width), rows(n_z)],
        out_shape=[jax.ShapeDtypeStruct((n_t, nc, width), F32), jax.ShapeDtypeStruct((n_t, nc, n_z), F32)],
        scratch_shapes=[pltpu.VMEM((S5_LG, width_g, width_g + 4 * S5_STATE), BF16)],
        compiler_params=_cparams(("arbitrary", "arbitrary")),
        name="s5_in",
    )(u, perm, bt, cwt, w_cat)
    e = pl.pallas_call(
        functools.partial(_s5_scan_kernel, nc=nc, nc_ctx=nc_ctx),
        grid=(n_t,),
        in_specs=[per_tile(z), per_tile(a_log), per_tile(a_car)],
        out_specs=per_tile(z),
        out_shape=jax.ShapeDtypeStruct(z.shape, F32),
        compiler_params=_cparams(("arbitrary",)),
        name="s5_scan",
    )(z, a_log, a_car)
    y = pl.pallas_call(
        _s5_out_kernel,
        grid=(n_t, nc // rb),
        in_specs=[rows(n_z), rows(width), tokens, per_tile(d_rows), whole(perm), per_tile(v_cat)],
        out_specs=tokens,
        out_shape=jax.ShapeDtypeStruct((n_t, l + s, LANES), F32),
        compiler_params=_cparams(("arbitrary", "arbitrary")),
        name="s5_out",
    )(e, yp, u, d_rows, perm, v_cat)
    return y


def _merge_kernel(x_ref, ya_ref, y5_ref, ys_ref, ga_ref, gb_ref, gs_ref, wglu_ref, wb_ref, wo_ref,
                  gt_ref, g_ref, sh_ref, sc_ref, xo_ref, h_ref, ht_ref):
    y5 = jnp.concatenate([y5_ref[t] for t in range(y5_ref.shape[0])], axis=-1)
    z = _gelu(y5)
    yb = z * jax.nn.sigmoid(_dot(z.astype(BF16), wglu_ref[...]))
    m = (ga_ref[...].astype(F32) * _dot(ya_ref[...], wb_ref[0])
         + gb_ref[...].astype(F32) * _dot(yb.astype(BF16), wb_ref[1])
         + gs_ref[...].astype(F32) * _dot(ys_ref[...], wb_ref[2]))
    x = x_ref[...] + gt_ref[...] * _dot(m.astype(BF16), wo_ref[...])
    xo_ref[...] = x
    h = _norm_mod(x, g_ref[...], sh_ref[...], sc_ref[...])
    h_ref[...] = h.astype(BF16)
    ht_ref[...] = h.T.astype(BF16)


def _merge(x, ya, y5, y5_row0, ys, ga, gb, gs, w_glu, w_branch, w_out, gate, g, shift, scale):
    n, d = x.shape
    bw = ya.shape[1]
    tm = min(ROW_BLOCK, n)
    assert y5_row0 % tm == 0
    row = lambda w: pl.BlockSpec((tm, w), lambda i: (i, 0))
    full = lambda a: pl.BlockSpec(a.shape, lambda i: (0,) * a.ndim)
    vec = pl.BlockSpec((1, d), lambda i: (0, 0))
    y5_rows = pl.BlockSpec((y5.shape[0], tm, LANES), lambda i: (0, i + y5_row0 // tm, 0))
    return pl.pallas_call(
        _merge_kernel,
        grid=(n // tm,),
        in_specs=[row(d), row(bw), y5_rows, row(bw), row(d), row(d), row(d),
                  full(w_glu), full(w_branch), full(w_out), vec, vec, vec, vec],
        out_specs=[row(d), row(d), pl.BlockSpec((d, tm), lambda i: (0, i))],
        out_shape=[jax.ShapeDtypeStruct((n, d), F32), jax.ShapeDtypeStruct((n, d), BF16),
                   jax.ShapeDtypeStruct((d, n), BF16)],
        compiler_params=_cparams(("arbitrary",)),
        name="merge",
    )(x, ya, y5, ys, ga, gb, gs, w_glu, w_branch, w_out, gate, g, shift, scale)


def _knock_out_16(s, order, exact, want_rank=True):
    rank = jnp.full(s.shape, float(PEER_TOPK), F32) if want_rank else None
    live = s
    vals = []
    for r in range(PEER_TOPK):
        m = jnp.max(live, axis=0, keepdims=True)
        hit = live == m
        if exact:
            first = jnp.min(jnp.where(hit, order, float(PEER_TOPK * PEER_NKEYS)), axis=0, keepdims=True)
            hit = order == first
        if want_rank:
            rank = jnp.where(hit, float(r), rank)
        live = jnp.where(hit, -jnp.inf, live)
        vals.append(m)
    out = (rank < float(PEER_TOPK)) if want_rank else (live != s)
    n_out = jnp.sum(jnp.where(out, 1.0, 0.0), axis=0, keepdims=True)
    return rank, jnp.concatenate(vals, axis=0), n_out


def _bf16_pair_word(x):
    hi = pltpu.bitcast(x.astype(BF16).astype(F32), jnp.uint32)
    return hi | (hi >> 16)


def _pair_tiles():
    tiles = [(0, 0, 8), (0, 8, 8)]
    for a in range(1, 8):
        tiles.append((a, 0, PEER_TOPK // (a + 1)))
    return tiles


def _route_kernel(h_ref, wq_ref, k1_ref, k2_ref, cnt_ref, c1_ref, rk_ref, e2_ref, q_scr):
    tb = h_ref.shape[0]
    q_scr[...] = _dot(h_ref[...], wq_ref[...]).astype(BF16)
    half = PEER_QDIM // 2
    iota = lax.broadcasted_iota(jnp.int32, (PEER_NKEYS, tb), 0).astype(F32)
    row8 = lax.broadcasted_iota(jnp.int32, (SUBLANES, tb), 0).astype(F32)
    k1 = k1_ref[...]
    k2 = k2_ref[...]

    def emit(hd, s1, s2, exact):
        rank1, v1, n1 = _knock_out_16(s1, iota, exact, want_rank=exact)
        rank2, v2, n2 = _knock_out_16(s2, iota, exact)
        e1v = jnp.exp(v1 - v1[0:1])
        e2v = jnp.exp(v2 - v2[0:1])
        tiles, flats, gates = [], [], []
        for a, b0, nv in _pair_tiles():
            c = v1[a:a + 1] + v2[b0:b0 + SUBLANES]
            tiles.append(jnp.where(row8 < nv, c, -jnp.inf))
            flats.append(a * PEER_TOPK + b0 + row8)
            gates.append(e1v[a:a + 1] * e2v[b0:b0 + SUBLANES])
        tiles.append(v1[SUBLANES:] + v2[0:1])
        flats.append((row8 + SUBLANES) * PEER_TOPK)
        gates.append(e1v[SUBLANES:] * e2v[0:1])
        cand = jnp.concatenate(tiles, axis=0)
        flat = jnp.concatenate(flats, axis=0)
        gate = jnp.concatenate(gates, axis=0)
        rank_c, _, n_c = _knock_out_16(cand, flat, exact)
        self_ = jnp.where(rank_c < float(PEER_TOPK), 1.0, 0.0)
        z = jnp.sum(self_ * gate, axis=0, keepdims=True)
        cnt = [self_[0:8].sum(axis=0, keepdims=True) + self_[8:16].sum(axis=0, keepdims=True)]
        for t in range(2, 9):
            cnt.append(self_[t * SUBLANES:(t + 1) * SUBLANES].sum(axis=0, keepdims=True))
        cnt = jnp.concatenate(cnt + [self_[9 * SUBLANES:]], axis=0)
        cnt1 = jnp.zeros((PEER_NKEYS, tb), F32)
        for a in range(PEER_TOPK):
            is_a = (rank1 == float(a)) if exact else (s1 == v1[a:a + 1])
            cnt1 = jnp.where(is_a, cnt[a:a + 1], cnt1)
        cnt_ref[hd] = _bf16_pair_word(cnt1)
        c1_ref[hd] = _bf16_pair_word(jnp.exp(s1 - v1[0:1]) * (0.5 / z))
        rk_ref[hd] = rank2.astype(BF16)
        e2_ref[hd] = jnp.exp(s2 - v2[0:1]).astype(BF16)
        want = float(PEER_TOPK)
        return jnp.where((n1 != want) | (n2 != want) | (n_c != want), 1.0, 0.0)

    def head_group(hg, _):
        scores = []
        for k in range(PEER_ROUTE_HEADS):
            hd = hg * PEER_ROUTE_HEADS + k
            c0 = pl.multiple_of(hd * PEER_QDIM, PEER_QDIM)
            s1 = _dot_nt(k1, q_scr[:, pl.ds(c0, half)])
            s2 = _dot_nt(k2, q_scr[:, pl.ds(c0 + half, half)])
            scores.append((hd, s1, s2))
        tied = [jnp.max(emit(hd, s1, s2, exact=False)) for hd, s1, s2 in scores]
        for (hd, s1, s2), t in zip(scores, tied):
            @pl.when(t > 0.0)
            def _():
                emit(hd, s1, s2, exact=True)
        return 0

    lax.fori_loop(0, PEER_HEADS // PEER_ROUTE_HEADS, head_group, 0)


def _peer_route(h, w_q, k1, k2):
    n, d = h.shape
    tb = min(PEER_ROUTE_BLOCK, n)
    qw = w_q.shape[1]
    out_blk = pl.BlockSpec((PEER_HEADS, PEER_NKEYS, tb), lambda i: (0, 0, i))
    shp = lambda dt: jax.ShapeDtypeStruct((PEER_HEADS, PEER_NKEYS, n), dt)
    return pl.pallas_call(
        _route_kernel,
        grid=(n // tb,),
        in_specs=[pl.BlockSpec((tb, d), lambda i: (i, 0)),
                  pl.BlockSpec((d, qw), lambda i: (0, 0)),
                  pl.BlockSpec(k1.shape, lambda i: (0, 0)),
                  pl.BlockSpec(k2.shape, lambda i: (0, 0))],
        out_specs=[out_blk] * 4,
        out_shape=[shp(jnp.uint32), shp(jnp.uint32), shp(BF16), shp(BF16)],
        scratch_shapes=[pltpu.VMEM((tb, qw), BF16)],
        compiler_params=_cparams(("arbitrary",)),
        name="peer_route",
    )(h, w_q, k1, k2)


def _dense_kernel(ht_ref, x_ref, gt_ref, u_ref, vt_ref, cnt_ref, c1_ref, rk_ref, e2_ref, o_ref,
                  acc_ref, w_scr):
    eb = pl.program_id(1)
    tb = ht_ref.shape[1]
    n_exp = u_ref.shape[0]
    n_sub = n_exp // PEER_SUB
    i_per_sub = PEER_SUB // PEER_NKEYS

    @pl.when(eb == 0)
    def _():
        acc_ref[...] = jnp.zeros_like(acc_ref)

    def scores(sb):
        return _dot(u_ref[sb * PEER_SUB:(sb + 1) * PEER_SUB, :], ht_ref[...])

    pk_rows = 2 * SUBLANES
    n_pk = PEER_NKEYS // pk_rows

    def row_tile(ref, hd, i):
        words = jnp.broadcast_to(ref[hd, pl.ds(i, 1), :], (SUBLANES, tb))
        return pltpu.bitcast(words, BF16)[None]

    def accumulate(c0, c1):
        acc_ref[...] += _dot(vt_ref[:, c0:c1], w_scr[c0:c1, :])

    a_next = scores(0)
    pending = None
    for sb in range(n_sub):
        a_cur = a_next
        if sb + 1 < n_sub:
            a_next = scores(sb + 1)
        if pending is not None:
            accumulate(*pending)
            pending = None
        for il in range(i_per_sub):
            i = eb * (n_exp // PEER_NKEYS) + sb * i_per_sub + il
            gsum = jnp.zeros((n_pk, pk_rows, tb), BF16)
            for hd in range(PEER_HEADS):
                cnt = row_tile(cnt_ref, hd, i)
                c1 = row_tile(c1_ref, hd, i)
                rk = rk_ref[hd].reshape(n_pk, pk_rows, tb)
                e2 = e2_ref[hd].reshape(n_pk, pk_rows, tb)
                gsum = gsum + jnp.where(rk < cnt, e2 * c1, jnp.zeros((), BF16))
            r0 = sb * PEER_SUB + il * PEER_NKEYS
            a_i = a_cur[il * PEER_NKEYS:(il + 1) * PEER_NKEYS]
            half_gate = gsum.reshape(PEER_NKEYS, tb).astype(F32)
            w_i = a_i * (1.0 + lax.erf(a_i * (1.0 / math.sqrt(2.0)))) * half_gate
            w_scr[r0:r0 + PEER_NKEYS, :] = w_i.astype(BF16)
        done = (sb + 1) * PEER_SUB
        if done % PEER_ACC_CHUNK == 0:
            pending = (done - PEER_ACC_CHUNK, done)
    if pending is not None:
        accumulate(*pending)

    @pl.when(eb == pl.num_programs(1) - 1)
    def _():
        o_ref[...] = x_ref[...] + gt_ref[...] * acc_ref[...].T


def _peer_dense(h_t, x, gate, u, v_t, layer, cnt1, c1, rank2, e2):
    d, n = h_t.shape
    n_e = u.shape[1]
    tb = min(PEER_TOK_BLOCK, n)
    eb = PEER_EXP_BLOCK
    tab = pl.BlockSpec((PEER_HEADS, PEER_NKEYS, tb), lambda i, e: (0, 0, i))
    return pl.pallas_call(
        _dense_kernel,
        grid=(n // tb, n_e // eb),
        in_specs=[pl.BlockSpec((d, tb), lambda i, e: (0, i)),
                  pl.BlockSpec((tb, d), lambda i, e: (i, 0)),
                  pl.BlockSpec((1, d), lambda i, e: (0, 0)),
                  pl.BlockSpec((None, eb, d), lambda i, e: (layer, e, 0)),
                  pl.BlockSpec((None, d, eb), lambda i, e: (layer, 0, e)),
                  tab, tab, tab, tab],
        out_specs=pl.BlockSpec((tb, d), lambda i, e: (i, 0)),
        out_shape=jax.ShapeDtypeStruct((n, d), F32),
        scratch_shapes=[pltpu.VMEM((d, tb), F32), pltpu.VMEM((eb, tb), BF16)],
        compiler_params=_cparams(("arbitrary", "arbitrary")),
        name="peer_dense",
    )(h_t, x, gate, u, v_t, cnt1, c1, rank2, e2)


def _final_norm_kernel(x_ref, g_ref, o_ref):
    x = x_ref[...]
    o_ref[...] = x * lax.rsqrt(jnp.mean(x * x, axis=-1, keepdims=True) + EPS) * g_ref[...]


def _final_norm(x, g):
    n, d = x.shape
    tm = min(2 * ROW_BLOCK, n)
    return pl.pallas_call(
        _final_norm_kernel,
        grid=(n // tm,),
        in_specs=[pl.BlockSpec((tm, d), lambda i: (i, 0)), pl.BlockSpec((1, d), lambda i: (0, 0))],
        out_specs=pl.BlockSpec((tm, d), lambda i: (i, 0)),
        out_shape=jax.ShapeDtypeStruct((n, d), F32),
        compiler_params=_cparams(("arbitrary",)),
        name="final_norm",
    )(x, g)


def kernel(x, c, ctx, c_ctx, w_mod, b_mod, g_mix, g_ffn, w_in, na_rpb, s5_lam_re, s5_lam_im,
           s5_b_re, s5_b_im, s5_c_re, s5_c_im, s5_log_step, s5_d, s5_w_glu, sw_sink, w_branch,
           w_out, peer_w_q, peer_sub_keys, peer_u, peer_v, g_final):
    batch, seq, d = x.shape
    l_ctx = ctx.shape[1]
    depth = w_mod.shape[0]
    assert batch == 1 and seq % (NA_ROWS * GRID_W * NA_STEP_BLOCKS) == 0 and l_ctx % (S5_CHUNK * SUBLANES) == 0
    assert seq // GRID_W >= NA_KROWS and seq >= 3 * SW_BLOCK and seq % (SW_BLOCK * SW_STEP_BLOCKS) == 0

    cc = jnp.zeros((SUBLANES, d), F32).at[0].set(c[0]).at[1].set(c_ctx)
    mod = _mod_vectors(cc, w_mod, b_mod).reshape(depth, SUBLANES, 6, d)
    rope_tabs = _rope_tables(seq)
    row = lambda v: v.reshape(1, d)
    na_bias_lo, na_bias_hi = jax.vmap(_na_bias_tiles)(na_rpb)
    s5_prep = jax.vmap(_s5_prepare)(s5_lam_re, s5_lam_im, s5_b_re, s5_b_im, s5_c_re, s5_c_im, s5_log_step)
    u_all = peer_u.astype(BF16)
    vt_all = jnp.transpose(peer_v.astype(BF16), (0, 2, 1))

    xx, xc = x[0], ctx[0]
    for l in range(depth):
        need_ctx = l < depth - 1
        m_lat, m_ctx = mod[l, 0], mod[l, 1]
        w_in_l = w_in[l].astype(BF16)
        g_mix_l = row(g_mix[l])
        g_ffn_l = row(g_ffn[l])

        qa, ka, va, ub, qs, ks, vs, ga, gb, gs = _inproj(
            xx, g_mix_l, row(m_lat[0]), row(m_lat[1]), w_in_l, rope_tabs)
        qa_c, ka_c, va_c, ub_c, qs_c, ks_c, vs_c, ga_c, gb_c, gs_c = _inproj(
            xc, g_mix_l, row(m_ctx[0]), row(m_ctx[1]), w_in_l, None)

        ya = _na_attention(qa, ka, va, ka_c, va_c, na_bias_lo[l], na_bias_hi[l])
        ys = _sw_attention(qs, ks, vs, ks_c, vs_c, sw_sink[l])
        y5 = _s5_mixer_pre_glu(ub_c, ub, [a[l] for a in s5_prep], s5_d[l])

        w_glu_l = s5_w_glu[l].astype(BF16)
        w_branch_l = w_branch[l].astype(BF16)
        w_out_l = w_out[l].astype(BF16)
        xx, hx2, hx2_t = _merge(xx, ya, y5, l_ctx, ys, ga, gb, gs, w_glu_l, w_branch_l, w_out_l,
                                row(m_lat[2]), g_ffn_l, row(m_lat[3]), row(m_lat[4]))

        w_q_l = peer_w_q[l].astype(BF16)
        k1 = peer_sub_keys[l, 0].astype(BF16)
        k2 = peer_sub_keys[l, 1].astype(BF16)
        if need_ctx:
            ya_c = _ctx_attention(qa_c, ka_c, va_c, None)
            ys_c = _ctx_attention(qs_c, ks_c, vs_c, sw_sink[l])
            xc, hc2, hc2_t = _merge(xc, ya_c, y5, 0, ys_c, ga_c, gb_c, gs_c, w_glu_l, w_branch_l, w_out_l,
                                    row(m_ctx[2]), g_ffn_l, row(m_ctx[3]), row(m_ctx[4]))
            xc = _peer_dense(hc2_t, xc, row(m_ctx[5]), u_all, vt_all, l, *_peer_route(hc2, w_q_l, k1, k2))
        xx = _peer_dense(hx2_t, xx, row(m_lat[5]), u_all, vt_all, l, *_peer_route(hx2, w_q_l, k1, k2))

    return _final_norm(xx, row(g_final))[None]
```
